```python
import jax, jax.numpy as jnp
from jax import lax
import numpy as np

D_MODEL = 1024
BATCH = 4
SEQ = 4096
DEPTH = 2
DEC_BATCH = 128
DEC_SEQ = 4
PAST_LEN = 2048
PAGE_SIZE = 128

N_MIXERS = 2
N_ATTN_LAYERS = (DEPTH + 1) // 2
N_CONV_LAYERS = DEPTH // 2
N_HEADS = 16
HEAD_DIM = D_MODEL // N_HEADS
N_KV_HEADS = 4
ROT_DIM = HEAD_DIM // 4
ROPE_THETA = 500000.0
IDX_HEADS = 8
IDX_DIM = 64
TOPK_MAX = 256
Q_BLOCK = 128
CONV_WIDTH = 3
D_FF = -(-8 * D_MODEL // (3 * 256)) * 256
EPS = 1e-6
Q_COLS = N_HEADS * HEAD_DIM
KV_COLS = N_KV_HEADS * HEAD_DIM
IQ_COLS = IDX_HEADS * IDX_DIM
ATTN_IN = Q_COLS + 2 * KV_COLS + IQ_COLS + IDX_DIM + IDX_HEADS

kernel_name = "dsa_shortconv_adaln_hybrid_step"


def rms_norm(x, g):
    xf = x.astype(jnp.float32)
    y = xf * lax.rsqrt(jnp.mean(xf * xf, axis=-1, keepdims=True) + EPS)
    return y.astype(x.dtype) * g


def adaln_params(c, w, b):
    mod = jax.nn.silu(c) @ w + b
    return [m[:, None, :] for m in jnp.split(mod, 6, axis=-1)]


def rope_tables(pos):
    inv = ROPE_THETA ** (-jnp.arange(0, ROT_DIM, 2, dtype=jnp.float32) / ROT_DIM)
    ang = pos[:, None] * inv[None, :]
    return jnp.cos(ang), jnp.sin(ang)


def partial_rope(x, cos, sin):
    half = ROT_DIM // 2
    c = cos[:, None, :].astype(x.dtype)
    s = sin[:, None, :].astype(x.dtype)
    x1 = x[..., :half]
    x2 = x[..., half:ROT_DIM]
    return jnp.concatenate([x1 * c - x2 * s, x2 * c + x1 * s, x[..., ROT_DIM:]], axis=-1)


def attn_project(h, w_in, pos):
    B, T, _ = h.shape
    p = h @ w_in
    q, k, v, qi, ki, wi = jnp.split(
        p, [Q_COLS, Q_COLS + KV_COLS, Q_COLS + 2 * KV_COLS, Q_COLS + 2 * KV_COLS + IQ_COLS,
            Q_COLS + 2 * KV_COLS + IQ_COLS + IDX_DIM], axis=-1)
    cos, sin = rope_tables(pos)
    q = partial_rope(q.reshape(B, T, N_HEADS, HEAD_DIM), cos, sin)
    k = partial_rope(k.reshape(B, T, N_KV_HEADS, HEAD_DIM), cos, sin)
    v = v.reshape(B, T, N_KV_HEADS, HEAD_DIM)
    qi = partial_rope(qi.reshape(B, T, IDX_HEADS, IDX_DIM), cos, sin)
    ki = partial_rope(ki.reshape(B, T, 1, IDX_DIM), cos, sin)[:, :, 0]
    return q, k, v, qi, ki, wi


def indexer_scores(qi, wi, ki, valid):
    dots = jnp.einsum('bqhd,bsd->bqsh', qi, ki).astype(jnp.float32) * (IDX_DIM ** -0.5)
    sc = jnp.einsum('bqsh,bqh->bqs', jax.nn.relu(dots), wi.astype(jnp.float32)) * (IDX_HEADS ** -0.5)
    return jnp.where(valid, sc, -jnp.inf)


def sparse_attend(q, k_sel, v_sel, valid):
    B, Q = q.shape[:2]
    qg = q.reshape(B, Q, N_KV_HEADS, N_HEADS // N_KV_HEADS, HEAD_DIM)
    s = jnp.einsum('bqgrd,bqngd->bqgrn', qg, k_sel).astype(jnp.float32) * (HEAD_DIM ** -0.5)
    s = jnp.where(valid[:, :, None, None, :], s, -jnp.inf)
    p = jax.nn.softmax(s, axis=-1).astype(v_sel.dtype)
    o = jnp.einsum('bqgrn,bqngd->bqgrd', p, v_sel)
    return o.reshape(B, Q, N_HEADS * HEAD_DIM)


def take_rows(a, idx):
    return jax.vmap(lambda ab, ib: ab[ib])(a, idx)


def attn_prompt(h, w_in, w_o):
    B, T, _ = h.shape
    q, k, v, qi, ki, wi = attn_project(h, w_in, jnp.arange(T, dtype=jnp.float32))
    topk = min(TOPK_MAX, T // 4)
    key_pos = jnp.arange(T)

    def block(i):
        s0 = i * Q_BLOCK
        qb = lax.dynamic_slice_in_dim(q, s0, Q_BLOCK, axis=1)
        qib = lax.dynamic_slice_in_dim(qi, s0, Q_BLOCK, axis=1)
        wib = lax.dynamic_slice_in_dim(wi, s0, Q_BLOCK, axis=1)
        qpos = s0 + jnp.arange(Q_BLOCK)
        valid = key_pos[None, :] <= qpos[:, None]
        sc = indexer_scores(qib, wib, ki, valid[None])
        _, sel = lax.top_k(sc, topk)
        return sparse_attend(qb, take_rows(k, sel), take_rows(v, sel), sel <= qpos[None, :, None])

    out = lax.map(block, jnp.arange(T // Q_BLOCK))
    out = jnp.moveaxis(out, 0, 1).reshape(B, T, N_HEADS * HEAD_DIM)
    return out @ w_o, k, v, ki


def attn_sample(h, w_in, w_o, cache_k, cache_v, cache_ki, page_table):
    Bd, Tn, _ = h.shape
    past = page_table.shape[1] * PAGE_SIZE
    qpos = past + jnp.arange(Tn)
    q, k, v, qi, ki, wi = attn_project(h, w_in, qpos.astype(jnp.float32))
    ki_past = cache_ki[page_table].reshape(Bd, past, IDX_DIM)
    ki_all = jnp.concatenate([ki_past, ki.astype(ki_past.dtype)], axis=1)
    L = past + Tn
    valid = jnp.arange(L)[None, :] <= qpos[:, None]
    sc = indexer_scores(qi, wi, ki_all, valid[None])
    topk = min(TOPK_MAX, L // 4)
    _, sel = lax.top_k(sc, topk)
    in_past = (sel < past)[..., None, None]
    ps = jnp.minimum(sel, past - 1)
    phys = jax.vmap(lambda pt, i: pt[i])(page_table, ps // PAGE_SIZE)
    slot = ps % PAGE_SIZE
    ns = jnp.clip(sel - past, 0, Tn - 1)
    k_sel = jnp.where(in_past, cache_k[phys, slot], take_rows(k, ns).astype(cache_k.dtype))
    v_sel = jnp.where(in_past, cache_v[phys, slot], take_rows(v, ns).astype(cache_v.dtype))
    o = sparse_attend(q, k_sel, v_sel, sel <= qpos[None, :, None])
    return o @ w_o, k, v, ki


def conv_mixer(h, prefix, w_in, conv_k, w_out):
    T = h.shape[1]
    bg, cg, hv = jnp.split(h @ w_in, 3, axis=-1)
    u = cg * hv
    up = jnp.concatenate([prefix.astype(u.dtype), u], axis=1)
    y = sum(conv_k[j] * up[:, j:j + T] for j in range(CONV_WIDTH))
    return (bg * y) @ w_out, up[:, up.shape[1] - (CONV_WIDTH - 1):]


def swiglu(h, w_gu, w_down):
    g, u = jnp.split(h @ w_gu, 2, axis=-1)
    return (jax.nn.silu(g) * u) @ w_down


def setup_inputs(seed: int = 0) -> dict:
    key = jax.random.key(seed)
    ks = jax.random.split(key, 24)
    n_pages = PAST_LEN // PAGE_SIZE
    n_phys = (5 * DEC_BATCH * n_pages + 3) // 4

    def nrm(k, shape, scale=1.0):
        return scale * jax.random.normal(k, shape, jnp.float32)

    page_table = jax.random.permutation(ks[8], n_phys)[:DEC_BATCH * n_pages]
    page_table = page_table.reshape(DEC_BATCH, n_pages).astype(jnp.int32)
    return {
        "x_prompt": nrm(ks[0], (BATCH, SEQ, D_MODEL)),
        "x_sample": nrm(ks[1], (DEC_BATCH, DEC_SEQ, D_MODEL)),
        "c_prompt": nrm(ks[2], (BATCH, D_MODEL)),
        "c_sample": nrm(ks[3], (DEC_BATCH, D_MODEL)),
        "cache_k": nrm(ks[4], (N_ATTN_LAYERS, n_phys, PAGE_SIZE, N_KV_HEADS, HEAD_DIM)),
        "cache_v": nrm(ks[5], (N_ATTN_LAYERS, n_phys, PAGE_SIZE, N_KV_HEADS, HEAD_DIM)),
        "cache_kidx": nrm(ks[6], (N_ATTN_LAYERS, n_phys, PAGE_SIZE, IDX_DIM)),
        "state_conv": nrm(ks[7], (N_CONV_LAYERS, DEC_BATCH, CONV_WIDTH - 1, D_MODEL)),
        "page_table": page_table,
        "ada_w": nrm(ks[9], (DEPTH, D_MODEL, 6 * D_MODEL), 0.5 * D_MODEL ** -0.5),
        "ada_b": nrm(ks[10], (DEPTH, 6 * D_MODEL), 0.02),
        "norm1_g": 1.0 + nrm(ks[11], (DEPTH, D_MODEL), 0.02),
        "norm2_g": 1.0 + nrm(ks[12], (DEPTH, D_MODEL), 0.02),
        "final_g": 1.0 + nrm(ks[13], (D_MODEL,), 0.02),
        "attn_w_in": nrm(ks[14], (N_ATTN_LAYERS, D_MODEL, ATTN_IN), D_MODEL ** -0.5),
        "attn_w_out": nrm(ks[15], (N_ATTN_LAYERS, Q_COLS, D_MODEL), Q_COLS ** -0.5),
        "conv_w_in": nrm(ks[16], (N_CONV_LAYERS, D_MODEL, 3 * D_MODEL), D_MODEL ** -0.5),
        "conv_k": nrm(ks[17], (N_CONV_LAYERS, CONV_WIDTH, D_MODEL), CONV_WIDTH ** -0.5),
        "conv_w_out": nrm(ks[18], (N_CONV_LAYERS, D_MODEL, D_MODEL), D_MODEL ** -0.5),
        "ffn_w_gu": nrm(ks[19], (DEPTH, D_MODEL, 2 * D_FF), D_MODEL ** -0.5),
        "ffn_w_down": nrm(ks[20], (DEPTH, D_FF, D_MODEL), D_FF ** -0.5),
    }


def reference(x_prompt, x_sample, c_prompt, c_sample, cache_k, cache_v, cache_kidx, state_conv, page_table,
              ada_w, ada_b, norm1_g, norm2_g, final_g, attn_w_in, attn_w_out, conv_w_in, conv_k, conv_w_out,
              ffn_w_gu, ffn_w_down):
    yp, ys = x_prompt, x_sample
    kp_l, vp_l, kip_l, cp_l = [], [], [], []
    ks_l, vs_l, kis_l, cs_l = [], [], [], []
    for i in range(DEPTH):
        j = i // N_MIXERS
        sh1p, sc1p, g1p, sh2p, sc2p, g2p = adaln_params(c_prompt, ada_w[i], ada_b[i])
        sh1s, sc1s, g1s, sh2s, sc2s, g2s = adaln_params(c_sample, ada_w[i], ada_b[i])
        hp = rms_norm(yp, norm1_g[i]) * (1.0 + sc1p) + sh1p
        hs = rms_norm(ys, norm1_g[i]) * (1.0 + sc1s) + sh1s
        if i % N_MIXERS == 0:
            op, kp, vp, kip = attn_prompt(hp, attn_w_in[j], attn_w_out[j])
            os_, ks_, vs_, kis = attn_sample(hs, attn_w_in[j], attn_w_out[j], cache_k[j], cache_v[j],
                                             cache_kidx[j], page_table)
            kp_l.append(kp); vp_l.append(vp); kip_l.append(kip)
            ks_l.append(ks_); vs_l.append(vs_); kis_l.append(kis)
        else:
            zero_prefix = jnp.zeros((hp.shape[0], CONV_WIDTH - 1, D_MODEL), hp.dtype)
            op, cp = conv_mixer(hp, zero_prefix, conv_w_in[j], conv_k[j], conv_w_out[j])
            os_, cs = conv_mixer(hs, state_conv[j], conv_w_in[j], conv_k[j], conv_w_out[j])
            cp_l.append(cp); cs_l.append(cs)
        yp = yp + g1p * op
        ys = ys + g1s * os_
        hp = rms_norm(yp, norm2_g[i]) * (1.0 + sc2p) + sh2p
        hs = rms_norm(ys, norm2_g[i]) * (1.0 + sc2s) + sh2s
        yp = yp + g2p * swiglu(hp, ffn_w_gu[i], ffn_w_down[i])
        ys = ys + g2s * swiglu(hs, ffn_w_gu[i], ffn_w_down[i])
    yp = rms_norm(yp, final_g)
    ys = rms_norm(ys, final_g)
    return (yp, ys, jnp.stack(kp_l), jnp.stack(vp_l), jnp.stack(kip_l), jnp.stack(cp_l),
            jnp.stack(ks_l), jnp.stack(vs_l), jnp.stack(kis_l), jnp.stack(cs_l))
```

```python
import functools

import jax
import jax.numpy as jnp
from jax import lax
from jax.experimental import pallas as pl
from jax.experimental.pallas import tpu as pltpu

F32 = jnp.float32
BF16 = jnp.bfloat16

D_MODEL = 1024
N_HEADS = 16
HEAD_DIM = 64
N_KV_HEADS = 4
HEADS_PER_KV = N_HEADS // N_KV_HEADS
ROT_DIM = 16
ROPE_THETA = 500000.0
IDX_HEADS = 8
IDX_DIM = 64
TOPK = 256
PAGE_SIZE = 128
CONV_WIDTH = 3
D_FF = 2816
EPS = 1e-6
Q_COLS = N_HEADS * HEAD_DIM
KV_COLS = N_KV_HEADS * HEAD_DIM
IQ_COLS = IDX_HEADS * IDX_DIM
ATTN_IN = Q_COLS + 2 * KV_COLS + IQ_COLS + IDX_DIM + IDX_HEADS
ATTN_IN_PAD = 2176
LANES = 128
SUBLANES = 8
MASK_BIAS = -1e30
F32_MAX = 3.4028234663852886e38
VMEM_LIMIT = 60 * 1024 * 1024

ROW_TILE = 512
Q_TILE = 256
SCORE_CHUNK = 512
ATTN_CHUNK = 256
FFN_CHUNK = 256


def _dot(a, b):
    return jnp.dot(a, b, preferred_element_type=F32)


def _dot_nt(a, b):
    return lax.dot_general(a, b, (((1,), (1,)), ((), ())), preferred_element_type=F32)


def _silu(x):
    return x / (1.0 + jnp.exp(-x))


def _norm_mod(x, g, sc, sh):
    ms = jnp.mean(x * x, axis=-1, keepdims=True)
    y = x * lax.rsqrt(ms + EPS)
    return (y * g) * (1.0 + sc) + sh


def _params(*sem):
    return pltpu.CompilerParams(dimension_semantics=sem, vmem_limit_bytes=VMEM_LIMIT)


def _const_spec(shape):
    nd = len(shape)
    return pl.BlockSpec(shape, lambda *_: (0,) * nd, pipeline_mode=pl.Buffered(1))


def _mod_kernel(c_ref, w_ref, b_ref, o_ref):
    s = _silu(c_ref[...]).astype(BF16)
    o_ref[...] = _dot(s, w_ref[...].astype(BF16)) + b_ref[...]


def _modulation(c_all, ada_w, ada_b):
    depth, _, n = ada_w.shape
    rows = c_all.shape[0]
    tn = 512
    return pl.pallas_call(
        _mod_kernel,
        grid=(depth, n // tn),
        in_specs=[
            pl.BlockSpec((rows, D_MODEL), lambda l, j: (0, 0)),
            pl.BlockSpec((None, D_MODEL, tn), lambda l, j: (l, 0, j)),
            pl.BlockSpec((None, 1, tn), lambda l, j: (l, 0, j)),
        ],
        out_specs=pl.BlockSpec((None, rows, tn), lambda l, j: (l, 0, j)),
        out_shape=jax.ShapeDtypeStruct((depth, rows, n), F32),
        compiler_params=_params("arbitrary", "arbitrary"),
        name="adaln_modulation",
    )(c_all, ada_w, ada_b.reshape(depth, 1, n))


def _rope_block(x, c, s1, s2):
    return x * c + pltpu.roll(x, LANES - ROT_DIM // 2, 1) * s1 + pltpu.roll(x, ROT_DIM // 2, 1) * s2


def _rope_cols(p, c, s1, s2):
    nblk = p.shape[1] // LANES
    return [_rope_block(p[:, j * LANES:(j + 1) * LANES], c, s1, s2) for j in range(nblk)]


def _last_block_tables(c, s1, s2):
    lane = lax.broadcasted_iota(jnp.int32, c.shape, 1)
    is_key = lane < IDX_DIM
    return jnp.where(is_key, c, 1.0), jnp.where(is_key, s1, 0.0), jnp.where(is_key, s2, 0.0)


def _proj_prompt_kernel(x_ref, sc_ref, sh_ref, g_ref, w_ref, c_ref, s1_ref, s2_ref,
                        k_ref, v_ref, ki_ref, qt_ref, qit_ref, kg_ref, vt_ref, kib_ref, wit_ref):
    h = _norm_mod(x_ref[...], g_ref[...], sc_ref[...], sh_ref[...]).astype(BF16)
    c, s1, s2 = c_ref[...], s1_ref[...], s2_ref[...]

    pq = _dot(h, w_ref[:, 0:Q_COLS])
    for j, blk in enumerate(_rope_cols(pq, c, s1, s2)):
        qt_ref[j * LANES:(j + 1) * LANES, :] = (blk * HEAD_DIM ** -0.5).T.astype(BF16)

    pk = _dot(h, w_ref[:, Q_COLS:Q_COLS + KV_COLS])
    k = jnp.concatenate(_rope_cols(pk, c, s1, s2), axis=1)
    k_ref[...] = k
    for g in range(N_KV_HEADS):
        kg_ref[g] = k[:, g * HEAD_DIM:(g + 1) * HEAD_DIM].astype(BF16)

    pv = _dot(h, w_ref[:, Q_COLS + KV_COLS:Q_COLS + 2 * KV_COLS])
    v_ref[...] = pv
    for cc in range(pv.shape[0] // ATTN_CHUNK):
        vt_ref[cc] = pv[cc * ATTN_CHUNK:(cc + 1) * ATTN_CHUNK, :].T.astype(BF16)

    o = Q_COLS + 2 * KV_COLS
    pqi = _dot(h, w_ref[:, o:o + IQ_COLS])
    for j, blk in enumerate(_rope_cols(pqi, c, s1, s2)):
        qit_ref[j * LANES:(j + 1) * LANES, :] = blk.T.astype(BF16)

    pl_ = _dot(h, w_ref[:, o + IQ_COLS:ATTN_IN_PAD])
    last = _rope_block(pl_, *_last_block_tables(c, s1, s2))
    ki = last[:, 0:IDX_DIM]
    ki_ref[...] = ki
    kib_ref[...] = ki.astype(BF16)
    wit_ref[...] = last.T[IDX_DIM:IDX_DIM + IDX_HEADS, :]


def _proj_prompt(x, sc, sh, g, w, rope):
    nb, t, _ = x.shape
    tm = ROW_TILE
    row = lambda width: pl.BlockSpec((None, tm, width), lambda b, j: (b, j, 0))
    mod = pl.BlockSpec((None, 1, D_MODEL), lambda b, j: (b, 0, 0))
    tab = pl.BlockSpec((tm, LANES), lambda b, j: (j, 0))
    col = lambda height: pl.BlockSpec((None, height, tm), lambda b, j: (b, 0, j))
    out_shape = (
        jax.ShapeDtypeStruct((nb, t, KV_COLS), F32),
        jax.ShapeDtypeStruct((nb, t, KV_COLS), F32),
        jax.ShapeDtypeStruct((nb, t, IDX_DIM), F32),
        jax.ShapeDtypeStruct((nb, Q_COLS, t), BF16),
        jax.ShapeDtypeStruct((nb, IQ_COLS, t), BF16),
        jax.ShapeDtypeStruct((nb, N_KV_HEADS, t, HEAD_DIM), BF16),
        jax.ShapeDtypeStruct((nb, t // ATTN_CHUNK, KV_COLS, ATTN_CHUNK), BF16),
        jax.ShapeDtypeStruct((nb, t, IDX_DIM), BF16),
        jax.ShapeDtypeStruct((nb, IDX_HEADS, t), F32),
    )
    out_specs = (
        row(KV_COLS), row(KV_COLS), row(IDX_DIM), col(Q_COLS), col(IQ_COLS),
        pl.BlockSpec((None, N_KV_HEADS, tm, HEAD_DIM), lambda b, j: (b, 0, j, 0)),
        pl.BlockSpec((None, tm // ATTN_CHUNK, KV_COLS, ATTN_CHUNK), lambda b, j: (b, j, 0, 0)),
        row(IDX_DIM), col(IDX_HEADS),
    )
    return pl.pallas_call(
        _proj_prompt_kernel,
        grid=(nb, t // tm),
        in_specs=[row(D_MODEL), mod, mod, _const_spec((1, D_MODEL)),
                  _const_spec((D_MODEL, ATTN_IN_PAD)), tab, tab, tab],
        out_specs=out_specs,
        out_shape=out_shape,
        compiler_params=_params("arbitrary", "arbitrary"),
        name="attn_proj_prompt",
    )(x, sc, sh, g, w, *rope)


def _proj_sample_kernel(x_ref, sc_ref, sh_ref, g_ref, w_ref, c_ref, s1_ref, s2_ref, p_ref):
    h = _norm_mod(x_ref[...], g_ref[...], sc_ref[...], sh_ref[...]).astype(BF16)
    c, s1, s2 = c_ref[...], s1_ref[...], s2_ref[...]
    p = _dot(h, w_ref[...])
    v_lo = (Q_COLS + KV_COLS) // LANES
    v_hi = (Q_COLS + 2 * KV_COLS) // LANES
    nblk = ATTN_IN_PAD // LANES
    for j in range(nblk):
        blk = p[:, j * LANES:(j + 1) * LANES]
        if j == nblk - 1:
            blk = _rope_block(blk, *_last_block_tables(c, s1, s2))
        elif not (v_lo <= j < v_hi):
            blk = _rope_block(blk, c, s1, s2)
        p_ref[:, j * LANES:(j + 1) * LANES] = blk


def _proj_sample(x, sc, sh, g, w, rope):
    rows = x.shape[0]
    full = lambda width: pl.BlockSpec((rows, width), lambda i: (0, 0))
    return pl.pallas_call(
        _proj_sample_kernel,
        grid=(1,),
        in_specs=[full(D_MODEL), full(D_MODEL), full(D_MODEL),
                  pl.BlockSpec((1, D_MODEL), lambda i: (0, 0)),
                  pl.BlockSpec((D_MODEL, ATTN_IN_PAD), lambda i: (0, 0)),
                  full(LANES), full(LANES), full(LANES)],
        out_specs=full(ATTN_IN_PAD),
        out_shape=jax.ShapeDtypeStruct((rows, ATTN_IN_PAD), F32),
        compiler_params=_params("arbitrary"),
        name="attn_proj_sample",
    )(x, sc, sh, g, w, *rope)


def _fold_rows(x):
    rows, tq = x.shape
    return x.reshape(rows // SUBLANES, SUBLANES, tq).sum(axis=0)


def _select_to_bias(sc_ref, qpos, nk, kc):
    tq = sc_ref.shape[1]
    topk = float(TOPK)

    def chunk(c):
        return sc_ref[pl.ds(pl.multiple_of(c * kc, kc), kc), :]

    def key_index(c):
        return c * kc + lax.broadcasted_iota(jnp.int32, (kc, tq), 0)

    def count(pred):
        def body(c, acc):
            return acc + _fold_rows(jnp.where(pred(chunk(c), c), 1.0, 0.0))
        acc = lax.fori_loop(0, nk, body, jnp.zeros((SUBLANES, tq), F32))
        return acc.sum(axis=0, keepdims=True)

    def stats(c, carry):
        mx, mn, cnt = carry
        s = chunk(c)
        above = s > -jnp.inf
        mx = jnp.maximum(mx, s.reshape(kc // SUBLANES, SUBLANES, tq).max(axis=0))
        mn = jnp.minimum(mn, jnp.where(above, s, jnp.inf).reshape(kc // SUBLANES, SUBLANES, tq).min(axis=0))
        cnt = cnt + _fold_rows(jnp.where(above, 1.0, 0.0))
        return mx, mn, cnt

    mx, mn, cnt = lax.fori_loop(
        0, nk, stats,
        (jnp.full((SUBLANES, tq), -jnp.inf, F32), jnp.full((SUBLANES, tq), jnp.inf, F32),
         jnp.zeros((SUBLANES, tq), F32)))
    mx = mx.max(axis=0, keepdims=True)
    mn = mn.min(axis=0, keepdims=True)
    n_above = cnt.sum(axis=0, keepdims=True)
    n_max = count(lambda s, c: s >= mx)

    few = n_above < topk
    flat = n_max >= topk
    lo = jnp.where(few, -jnp.inf, jnp.where(flat, mx, mn))
    hi = jnp.where(few, mn, jnp.where(flat, jnp.inf, mx))
    n_lo = jnp.where(few, 2.0 * topk, jnp.where(flat, n_max, n_above))
    n_hi = jnp.where(few, n_above, jnp.where(flat, 0.0, n_max))
    done = jnp.where(few | flat | (n_lo == topk), 1.0, 0.0)

    def not_finished(d):
        return (jnp.min(d) < 0.5).astype(jnp.int32)

    def bisect(carry):
        lo, hi, n_lo, n_hi, done, _ = carry
        mid = jnp.clip(0.5 * lo + 0.5 * hi, -F32_MAX, F32_MAX)
        stuck = (mid <= lo) | (mid >= hi)
        n_mid = count(lambda s, c: s >= mid)
        move = (done < 0.5) & jnp.logical_not(stuck)
        up = move & (n_mid >= topk)
        down = move & (n_mid < topk)
        lo = jnp.where(up, mid, lo)
        n_lo = jnp.where(up, n_mid, n_lo)
        hi = jnp.where(down, mid, hi)
        n_hi = jnp.where(down, n_mid, n_hi)
        done = jnp.where(stuck | (n_lo == topk), 1.0, done)
        return lo, hi, n_lo, n_hi, done, not_finished(done)

    lo, hi, n_lo, n_hi, done, _ = lax.while_loop(
        lambda carry: carry[5] > 0, bisect, (lo, hi, n_lo, n_hi, done, not_finished(done)))

    thr = lo
    exact = n_lo == topk
    need = topk - n_hi
    all_keys = jnp.full((1, tq), 2 ** 30, jnp.int32)

    def tie_break():
        def step(i, j):
            cand = j + lax.shift_right_logical(jnp.int32(2048), i)
            n_before = count(lambda s, c: (s == thr) & (key_index(c) < cand))
            return jnp.where(n_before < need, cand, j)
        return lax.fori_loop(0, 12, step, jnp.zeros((1, tq), jnp.int32))

    any_tie = jnp.min(jnp.where(exact, 1.0, 0.0)) < 0.5
    last_tied = lax.cond(any_tie, tie_break, lambda: all_keys)
    last_tied = jnp.where(exact, all_keys, last_tied)

    def write(c, _):
        s = chunk(c)
        idx = key_index(c)
        keep = ((s > thr) | ((s == thr) & (idx <= last_tied))) & (idx <= qpos)
        sc_ref[pl.ds(pl.multiple_of(c * kc, kc), kc), :] = jnp.where(keep, 0.0, MASK_BIAS)
        return 0

    lax.fori_loop(0, nk, write, 0)


def _prompt_attn_kernel(qt_ref, qit_ref, wit_ref, kg_ref, vt_ref, kib_ref, o_ref,
                        sc_ref, acc_ref, m_ref, l_ref):
    i = pl.program_id(1)
    tq = qt_ref.shape[1]
    qpos = i * tq + lax.broadcasted_iota(jnp.int32, (1, tq), 1)
    n_keys = (i + 1) * tq

    n_score = (n_keys + SCORE_CHUNK - 1) // SCORE_CHUNK

    def score_chunk(c, _):
        start = pl.multiple_of(c * SCORE_CHUNK, SCORE_CHUNK)
        kc = kib_ref[pl.ds(start, SCORE_CHUNK), :]
        acc = jnp.zeros((SCORE_CHUNK, tq), F32)
        for h in range(IDX_HEADS):
            d = _dot(kc, qit_ref[h * IDX_DIM:(h + 1) * IDX_DIM, :])
            acc = acc + jnp.maximum(d, 0.0) * wit_ref[h:h + 1, :]
        idx = start + lax.broadcasted_iota(jnp.int32, (SCORE_CHUNK, tq), 0)
        sc_ref[pl.ds(start, SCORE_CHUNK), :] = jnp.where(idx <= qpos, acc, -jnp.inf)
        return 0

    lax.fori_loop(0, n_score, score_chunk, 0)

    _select_to_bias(sc_ref, qpos, n_score, SCORE_CHUNK)

    m_ref[...] = jnp.full(m_ref.shape, MASK_BIAS, F32)
    l_ref[...] = jnp.zeros(l_ref.shape, F32)
    acc_ref[...] = jnp.zeros(acc_ref.shape, F32)

    def attn_chunk(c, _):
        start = pl.multiple_of(c * ATTN_CHUNK, ATTN_CHUNK)
        bias = sc_ref[pl.ds(start, ATTN_CHUNK), :]
        bias = jnp.concatenate([bias] * HEADS_PER_KV, axis=1)
        for g in range(N_KV_HEADS):
            qg = jnp.concatenate(
                [qt_ref[(g * HEADS_PER_KV + r) * HEAD_DIM:(g * HEADS_PER_KV + r + 1) * HEAD_DIM, :]
                 for r in range(HEADS_PER_KV)], axis=1)
            s = _dot(kg_ref[g, pl.ds(start, ATTN_CHUNK), :], qg) + bias
            m_old = m_ref[g]
            m_new = jnp.maximum(m_old, s.max(axis=0, keepdims=True))
            alpha = jnp.exp(m_old - m_new)
            p = jnp.exp(s - m_new)
            l_ref[g] = alpha * l_ref[g] + p.sum(axis=0, keepdims=True)
            vt = vt_ref[c, g * HEAD_DIM:(g + 1) * HEAD_DIM, :]
            acc_ref[g] = alpha * acc_ref[g] + _dot(vt, p.astype(BF16))
            m_ref[g] = m_new
        return 0

    lax.fori_loop(0, (i + 1) * (tq // ATTN_CHUNK), attn_chunk, 0)

    for g in range(N_KV_HEADS):
        og = acc_ref[g] / l_ref[g]
        for pair in range(HEADS_PER_KV // 2):
            two = jnp.concatenate([og[:, (2 * pair) * tq:(2 * pair + 1) * tq],
                                   og[:, (2 * pair + 1) * tq:(2 * pair + 2) * tq]], axis=0)
            col = (g * HEADS_PER_KV + 2 * pair) * HEAD_DIM
            o_ref[:, col:col + 2 * HEAD_DIM] = two.T.astype(BF16)


def _prompt_attention(qt, qit, wit, kg, vt, kib):
    nb, _, t = qt.shape
    tq = Q_TILE
    col = lambda height: pl.BlockSpec((None, height, tq), lambda b, i: (b, 0, i))
    return pl.pallas_call(
        _prompt_attn_kernel,
        grid=(nb, t // tq),
        in_specs=[
            col(Q_COLS), col(IQ_COLS), col(IDX_HEADS),
            pl.BlockSpec((None, N_KV_HEADS, t, HEAD_DIM), lambda b, i: (b, 0, 0, 0)),
            pl.BlockSpec((None, t // ATTN_CHUNK, KV_COLS, ATTN_CHUNK), lambda b, i: (b, 0, 0, 0)),
            pl.BlockSpec((None, t, IDX_DIM), lambda b, i: (b, 0, 0)),
        ],
        out_specs=pl.BlockSpec((None, tq, Q_COLS), lambda b, i: (b, i, 0)),
        out_shape=jax.ShapeDtypeStruct((nb, t, Q_COLS), BF16),
        scratch_shapes=[
            pltpu.VMEM((t, tq), F32),
            pltpu.VMEM((N_KV_HEADS, HEAD_DIM, HEADS_PER_KV * tq), F32),
            pltpu.VMEM((N_KV_HEADS, 1, HEADS_PER_KV * tq), F32),
            pltpu.VMEM((N_KV_HEADS, 1, HEADS_PER_KV * tq), F32),
        ],
        compiler_params=_params("arbitrary", "arbitrary"),
        name="prompt_attention",
    )(qt, qit, wit, kg, vt, kib)


def _page_specs(n_pages, width):
    return [pl.BlockSpec((None, PAGE_SIZE, width), lambda b, pt, j=j: (pt[b, j], 0, 0))
            for j in range(n_pages)]


def _sample_score_kernel(pt_ref, qi_ref, w_ref, *rest):
    del pt_ref
    n_pages = len(rest) - 3
    pages, new_ref, o_ref, kall_ref = rest[:n_pages], rest[n_pages], rest[n_pages + 1], rest[n_pages + 2]
    for j in range(n_pages):
        kall_ref[j * PAGE_SIZE:(j + 1) * PAGE_SIZE, :] = pages[j][...].astype(BF16)
    kall_ref[n_pages * PAGE_SIZE:(n_pages + 1) * PAGE_SIZE, :] = new_ref[...].astype(BF16)
    d = _dot_nt(qi_ref[...], kall_ref[...])
    r = jnp.maximum(d, 0.0) * w_ref[...]
    o_ref[...] = r.reshape(IDX_HEADS, SUBLANES, r.shape[1]).sum(axis=0)


def _sample_scores(page_table, qi8, w8, cache_ki, ki_new):
    nb, n_pages = page_table.shape
    n_keys = (n_pages + 1) * PAGE_SIZE
    rows = IDX_HEADS * SUBLANES
    grid_spec = pltpu.PrefetchScalarGridSpec(
        num_scalar_prefetch=1,
        grid=(nb,),
        in_specs=[pl.BlockSpec((None, rows, IDX_DIM), lambda b, pt: (b, 0, 0)),
                  pl.BlockSpec((None, rows, 1), lambda b, pt: (b, 0, 0))]
        + _page_specs(n_pages, IDX_DIM)
        + [pl.BlockSpec((None, PAGE_SIZE, IDX_DIM), lambda b, pt: (b, 0, 0))],
        out_specs=pl.BlockSpec((None, SUBLANES, n_keys), lambda b, pt: (b, 0, 0)),
        scratch_shapes=[pltpu.VMEM((n_keys, IDX_DIM), BF16)],
    )
    return pl.pallas_call(
        _sample_score_kernel,
        grid_spec=grid_spec,
        out_shape=jax.ShapeDtypeStruct((nb, SUBLANES, n_keys), F32),
        compiler_params=_params("arbitrary"),
        name="sample_indexer_scores",
    )(page_table, qi8, w8, *([cache_ki] * n_pages), ki_new)


def _sample_select_kernel(s_ref, o_ref, sc_ref, *, past, period):
    tq = s_ref.shape[1]
    n_keys = s_ref.shape[0]
    lane = lax.broadcasted_iota(jnp.int32, (1, tq), 1)
    qpos = past + (lane & (period - 1))
    nk = n_keys // PAGE_SIZE
    for c in range(nk):
        idx = c * PAGE_SIZE + lax.broadcasted_iota(jnp.int32, (PAGE_SIZE, tq), 0)
        sc_ref[c * PAGE_SIZE:(c + 1) * PAGE_SIZE, :] = jnp.where(
            idx <= qpos, s_ref[c * PAGE_SIZE:(c + 1) * PAGE_SIZE, :], -jnp.inf)
    _select_to_bias(sc_ref, qpos, nk, PAGE_SIZE)
    o_ref[...] = sc_ref[...]


def _sample_select(scores_t, past, period):
    n_keys, nq = scores_t.shape
    tq = Q_TILE
    blk = pl.BlockSpec((n_keys, tq), lambda i: (0, i))
    return pl.pallas_call(
        functools.partial(_sample_select_kernel, past=past, period=period),
        grid=(nq // tq,),
        in_specs=[blk],
        out_specs=blk,
        out_shape=jax.ShapeDtypeStruct((n_keys, nq), F32),
        scratch_shapes=[pltpu.VMEM((n_keys, tq), F32)],
        compiler_params=_params("arbitrary"),
        name="sample_select",
    )(scores_t)


def _sample_attn_kernel(pt_ref, q_ref, bias_ref, *rest):
    del pt_ref
    n_pages = (len(rest) - 5) // 2
    kpages, vpages = rest[:n_pages], rest[n_pages:2 * n_pages]
    knew_ref, vnew_ref, o_ref, kall_ref, vall_ref = rest[2 * n_pages:]
    for j in range(n_pages):
        kall_ref[j * PAGE_SIZE:(j + 1) * PAGE_SIZE, :] = kpages[j][...].astype(BF16)
        vall_ref[j * PAGE_SIZE:(j + 1) * PAGE_SIZE, :] = vpages[j][...].astype(BF16)
    kall_ref[n_pages * PAGE_SIZE:(n_pages + 1) * PAGE_SIZE, :] = knew_ref[...].astype(BF16)
    vall_ref[n_pages * PAGE_SIZE:(n_pages + 1) * PAGE_SIZE, :] = vnew_ref[...].astype(BF16)
    s = _dot_nt(q_ref[...], kall_ref[...])
    bias = bias_ref[...]
    rows, n_keys = s.shape
    s = s + jnp.broadcast_to(bias[None], (rows // SUBLANES, SUBLANES, n_keys)).reshape(rows, n_keys)
    m = s.max(axis=1, keepdims=True)
    p = jnp.exp(s - m)
    l = p.sum(axis=1, keepdims=True)
    o_ref[...] = _dot(p.astype(BF16), vall_ref[...]) / l


def _sample_attention(page_table, q_bd, bias8, cache_k, cache_v, k_new, v_new):
    nb, n_pages = page_table.shape
    n_keys = (n_pages + 1) * PAGE_SIZE
    rows = N_HEADS * SUBLANES
    per_seq = lambda r, w: pl.BlockSpec((None, r, w), lambda b, pt: (b, 0, 0))
    grid_spec = pltpu.PrefetchScalarGridSpec(
        num_scalar_prefetch=1,
        grid=(nb,),
        in_specs=[per_seq(rows, KV_COLS), per_seq(SUBLANES, n_keys)]
        + _page_specs(n_pages, KV_COLS) + _page_specs(n_pages, KV_COLS)
        + [per_seq(PAGE_SIZE, KV_COLS), per_seq(PAGE_SIZE, KV_COLS)],
        out_specs=per_seq(rows, KV_COLS),
        scratch_shapes=[pltpu.VMEM((n_keys, KV_COLS), BF16), pltpu.VMEM((n_keys, KV_COLS), BF16)],
    )
    return pl.pallas_call(
        _sample_attn_kernel,
        grid_spec=grid_spec,
        out_shape=jax.ShapeDtypeStruct((nb, rows, KV_COLS), F32),
        compiler_params=_params("arbitrary"),
        name="sample_attention",
    )(page_table, q_bd, bias8, *([cache_k] * n_pages), *([cache_v] * n_pages), k_new, v_new)


def _ffn(h, wgu_ref, wd_ref):
    acc = jnp.zeros((h.shape[0], D_MODEL), F32)
    for c in range(D_FF // FFN_CHUNK):
        gu = _dot(h, wgu_ref[:, 2 * c * FFN_CHUNK:2 * (c + 1) * FFN_CHUNK])
        a = _silu(gu[:, :FFN_CHUNK]) * gu[:, FFN_CHUNK:]
        acc = acc + _dot(a.astype(BF16), wd_ref[c * FFN_CHUNK:(c + 1) * FFN_CHUNK, :])
    return acc


def _interleave_gate_up(w_gu):
    d = w_gu.shape[0]
    n = D_FF // FFN_CHUNK
    g = w_gu[:, :D_FF].reshape(d, n, 1, FFN_CHUNK)
    u = w_gu[:, D_FF:].reshape(d, n, 1, FFN_CHUNK)
    return jnp.concatenate([g, u], axis=2).reshape(d, 2 * D_FF)


def _row_specs(nb, rows, per_row_mod):
    tm = min(ROW_TILE, rows)
    row = pl.BlockSpec((None, tm, D_MODEL), lambda b, j: (b, j, 0))
    if per_row_mod:
        mod = row
    else:
        mod = pl.BlockSpec((None, 1, D_MODEL), lambda b, j: (b, 0, 0))
    return tm, row, mod


def _attn_out_ffn_kernel(x_ref, o_ref, g1_ref, sc2_ref, sh2_ref, g2_ref, n2_ref,
                         wo_ref, wgu_ref, wd_ref, y_ref):
    y1 = x_ref[...] + g1_ref[...] * _dot(o_ref[...], wo_ref[...])
    h2 = _norm_mod(y1, n2_ref[...], sc2_ref[...], sh2_ref[...]).astype(BF16)
    y_ref[...] = y1 + g2_ref[...] * _ffn(h2, wgu_ref, wd_ref)


def _attn_out_ffn(x, o, g1, sc2, sh2, g2, n2, wo, wgu, wd):
    nb, rows, _ = x.shape
    tm, row, mod = _row_specs(nb, rows, g1.shape[1] == rows)
    return pl.pallas_call(
        _attn_out_ffn_kernel,
        grid=(nb, rows // tm),
        in_specs=[row, row, mod, mod, mod, mod, _const_spec((1, D_MODEL)),
                  _const_spec(wo.shape), _const_spec(wgu.shape), _const_spec(wd.shape)],
        out_specs=row,
        out_shape=jax.ShapeDtypeStruct(x.shape, F32),
        compiler_params=_params("arbitrary", "arbitrary"),
        name="attn_out_ffn",
    )(x, o, g1, sc2, sh2, g2, n2, wo, wgu, wd)


def _conv_layer_kernel(x_ref, sc1_ref, sh1_ref, g1_ref, sc2_ref, sh2_ref, g2_ref, p1_ref, p2_ref,
                       n1_ref, n2_ref, nf_ref, win_ref, ck_ref, wout_ref, wgu_ref, wd_ref,
                       y_ref, tail_ref, ubuf_ref, z_ref, *, seg):
    j = pl.program_id(1)
    tm = x_ref.shape[0]
    x = x_ref[...]
    h = _norm_mod(x, n1_ref[...], sc1_ref[...], sh1_ref[...]).astype(BF16)

    @pl.when(j == 0)
    def _():
        ubuf_ref[0:SUBLANES, :] = jnp.zeros((SUBLANES, D_MODEL), F32)

    @pl.when(j > 0)
    def _():
        ubuf_ref[0:SUBLANES, :] = ubuf_ref[tm:tm + SUBLANES, :]

    t = (j * tm + lax.broadcasted_iota(jnp.int32, (tm, 1), 0)) & (seg - 1)
    ck = ck_ref[...]
    cw = FFN_CHUNK
    for c in range(D_MODEL // cw):
        cols = slice(c * cw, (c + 1) * cw)
        bg = _dot(h, win_ref[:, c * cw:(c + 1) * cw])
        cg = _dot(h, win_ref[:, D_MODEL + c * cw:D_MODEL + (c + 1) * cw])
        hv = _dot(h, win_ref[:, 2 * D_MODEL + c * cw:2 * D_MODEL + (c + 1) * cw])
        u = cg * hv
        ubuf_ref[SUBLANES:SUBLANES + tm, cols] = u
        um1 = jnp.where(t >= 1, ubuf_ref[SUBLANES - 1:SUBLANES - 1 + tm, cols], p1_ref[:, cols])
        um2 = jnp.where(t >= 2, ubuf_ref[SUBLANES - 2:SUBLANES - 2 + tm, cols], p2_ref[:, cols])
        conv = ck[0:1, cols] * um2 + ck[1:2, cols] * um1 + ck[2:3, cols] * u
        z_ref[:, cols] = (bg * conv).astype(BF16)
    r = tail_ref.shape[0]
    tail_ref[...] = ubuf_ref[SUBLANES + tm - r:SUBLANES + tm, :]

    y1 = x + g1_ref[...] * _dot(z_ref[...], wout_ref[...])
    h2 = _norm_mod(y1, n2_ref[...], sc2_ref[...], sh2_ref[...]).astype(BF16)
    y2 = y1 + g2_ref[...] * _ffn(h2, wgu_ref, wd_ref)
    ms = jnp.mean(y2 * y2, axis=-1, keepdims=True)
    y_ref[...] = (y2 * lax.rsqrt(ms + EPS)) * nf_ref[...]


def _conv_layer(x, mods, p1, p2, n1, n2, nf, win, ck, wout, wgu, wd, *, seg, full_tail):
    nb, rows, _ = x.shape
    per_row = mods[0].shape[1] == rows
    tm, row, mod = _row_specs(nb, rows, per_row)
    if per_row:
        prefix = row
    else:
        prefix = pl.BlockSpec((None, 1, D_MODEL), lambda b, j: (0, 0, 0))
    if full_tail:
        tail_spec, tail_rows = row, rows
    else:
        tail_spec, tail_rows = pl.BlockSpec((None, SUBLANES, D_MODEL), lambda b, j: (b, 0, 0)), SUBLANES
    vec = _const_spec((1, D_MODEL))
    return pl.pallas_call(
        functools.partial(_conv_layer_kernel, seg=seg),
        grid=(nb, rows // tm),
        in_specs=[row] + [mod] * 6 + [prefix, prefix, vec, vec, vec,
                                      _const_spec(win.shape), _const_spec(ck.shape), _const_spec(wout.shape),
                                      _const_spec(wgu.shape), _const_spec(wd.shape)],
        out_specs=(row, tail_spec),
        out_shape=(jax.ShapeDtypeStruct(x.shape, F32),
                   jax.ShapeDtypeStruct((nb, tail_rows, D_MODEL), F32)),
        scratch_shapes=[pltpu.VMEM((tm + SUBLANES, D_MODEL), F32), pltpu.VMEM((tm, D_MODEL), BF16)],
        compiler_params=_params("arbitrary", "arbitrary"),
        name="conv_layer",
    )(x, *mods, p1, p2, n1, n2, nf, win, ck, wout, wgu, wd)


def _rope_tables(pos):
    half = ROT_DIM // 2
    inv = ROPE_THETA ** (-jnp.arange(0, ROT_DIM, 2, dtype=F32) / ROT_DIM)
    ang = pos[:, None] * inv[None, :]
    cos, sin = jnp.cos(ang), jnp.sin(ang)
    n = pos.shape[0]
    rest = HEAD_DIM - ROT_DIM
    c = jnp.concatenate([cos, cos, jnp.ones((n, rest), F32)], axis=1)
    s1 = jnp.concatenate([-sin, jnp.zeros((n, half + rest), F32)], axis=1)
    s2 = jnp.concatenate([jnp.zeros((n, half), F32), sin, jnp.zeros((n, rest), F32)], axis=1)
    return tuple(jnp.tile(a, (1, LANES // HEAD_DIM)) for a in (c, s1, s2))


def kernel(x_prompt, x_sample, c_prompt, c_sample, cache_k, cache_v, cache_kidx, state_conv, page_table,
           ada_w, ada_b, norm1_g, norm2_g, final_g, attn_w_in, attn_w_out, conv_w_in, conv_k, conv_w_out,
           ffn_w_gu, ffn_w_down):
    nb, t, _ = x_prompt.shape
    ns, tn, _ = x_sample.shape
    n_pages = page_table.shape[1]
    past = n_pages * PAGE_SIZE
    n_phys = cache_k.shape[1]
    rows_s = ns * tn

    pad = (-(nb + ns)) % SUBLANES
    c_all = jnp.concatenate([c_prompt, c_sample, jnp.zeros((pad, D_MODEL), F32)], axis=0)
    mod = _modulation(c_all, ada_w, ada_b)

    def mods(layer):
        m = mod[layer].reshape(-1, 6, D_MODEL)
        prompt = [m[:nb, i][:, None, :] for i in range(6)]
        sample = [jnp.repeat(m[nb:nb + ns, i], tn, axis=0)[None] for i in range(6)]
        return prompt, sample

    vec = lambda a: a.reshape(1, D_MODEL)
    w_in = jnp.pad(attn_w_in[0], ((0, 0), (0, ATTN_IN_PAD - ATTN_IN))).astype(BF16)
    w_o = attn_w_out[0].astype(BF16)
    wgu = [_interleave_gate_up(ffn_w_gu[i]).astype(BF16) for i in range(2)]
    wd = [ffn_w_down[i].astype(BF16) for i in range(2)]

    (sh1p, sc1p, g1p, sh2p, sc2p, g2p), (sh1s, sc1s, g1s, sh2s, sc2s, g2s) = mods(0)
    rope_p = _rope_tables(jnp.arange(t, dtype=F32))
    k_p, v_p, ki_p, qt, qit, kg, vt, kib, wit = _proj_prompt(
        x_prompt, sc1p, sh1p, vec(norm1_g[0]), w_in, rope_p)
    o_p = _prompt_attention(qt, qit, wit, kg, vt, kib)
    y_p = _attn_out_ffn(x_prompt, o_p, g1p, sc2p, sh2p, g2p, vec(norm2_g[0]), w_o, wgu[0], wd[0])

    pos_s = jnp.tile(past + jnp.arange(tn, dtype=F32), ns)
    rope_s = _rope_tables(pos_s)
    xs = x_sample.reshape(rows_s, D_MODEL)
    proj = _proj_sample(xs, sc1s[0], sh1s[0], vec(norm1_g[0]), w_in, rope_s)
    o = 0
    q_s = proj[:, o:o + Q_COLS].reshape(ns, tn, N_KV_HEADS, HEADS_PER_KV, HEAD_DIM); o += Q_COLS
    k_s = proj[:, o:o + KV_COLS].reshape(ns, tn, KV_COLS); o += KV_COLS
    v_s = proj[:, o:o + KV_COLS].reshape(ns, tn, KV_COLS); o += KV_COLS
    qi_s = proj[:, o:o + IQ_COLS].reshape(ns, tn, IDX_HEADS, IDX_DIM); o += IQ_COLS
    ki_s = proj[:, o:o + IDX_DIM].reshape(ns, tn, IDX_DIM); o += IDX_DIM
    wi_s = proj[:, o:o + IDX_HEADS].reshape(ns, tn, IDX_HEADS)

    qpad = ((0, 0), (0, SUBLANES - tn))
    qi8 = jnp.pad(qi_s, qpad + ((0, 0), (0, 0))).transpose(0, 2, 1, 3)
    qi8 = qi8.reshape(ns, IDX_HEADS * SUBLANES, IDX_DIM).astype(BF16)
    w8 = jnp.pad(wi_s, qpad + ((0, 0),)).transpose(0, 2, 1).reshape(ns, IDX_HEADS * SUBLANES, 1)
    new_pad = ((0, 0), (0, PAGE_SIZE - tn), (0, 0))
    scores = _sample_scores(page_table, qi8, w8, cache_kidx[0], jnp.pad(ki_s, new_pad))
    n_keys = scores.shape[2]
    scores_t = scores[:, :tn, :].transpose(2, 0, 1).reshape(n_keys, rows_s)
    bias_t = _sample_select(scores_t, past, tn)
    bias8 = jnp.pad(bias_t.reshape(n_keys, ns, tn).transpose(1, 2, 0), qpad + ((0, 0),))

    q8 = jnp.pad(q_s * HEAD_DIM ** -0.5, qpad + ((0, 0), (0, 0), (0, 0))).transpose(0, 2, 3, 1, 4)
    eye = jnp.eye(N_KV_HEADS, dtype=F32)
    q_bd = (q8[:, :, :, :, None, :] * eye[None, :, None, None, :, None]).reshape(
        ns, N_HEADS * SUBLANES, KV_COLS).astype(BF16)
    o_bd = _sample_attention(page_table, q_bd, bias8,
                             cache_k[0].reshape(n_phys, PAGE_SIZE, KV_COLS),
                             cache_v[0].reshape(n_phys, PAGE_SIZE, KV_COLS),
                             jnp.pad(k_s, new_pad), jnp.pad(v_s, new_pad))
    o_bd = o_bd.reshape(ns, N_KV_HEADS, HEADS_PER_KV, SUBLANES, N_KV_HEADS, HEAD_DIM)
    o_s = jnp.stack([o_bd[:, g, :, :tn, g, :] for g in range(N_KV_HEADS)], axis=1)
    o_s = o_s.transpose(0, 3, 1, 2, 4).reshape(1, rows_s, Q_COLS).astype(BF16)
    y_s = _attn_out_ffn(xs[None], o_s, g1s, sc2s, sh2s, g2s, vec(norm2_g[0]), w_o, wgu[0], wd[0])

    (sh1p, sc1p, g1p, sh2p, sc2p, g2p), (sh1s, sc1s, g1s, sh2s, sc2s, g2s) = mods(1)
    win = conv_w_in[0].astype(BF16)
    wout = conv_w_out[0].astype(BF16)
    zero_prefix = jnp.zeros((1, 1, D_MODEL), F32)
    out_p, tail_p = _conv_layer(
        y_p, (sc1p, sh1p, g1p, sc2p, sh2p, g2p), zero_prefix, zero_prefix,
        vec(norm1_g[1]), vec(norm2_g[1]), vec(final_g), win, conv_k[0], wout, wgu[1], wd[1],
        seg=t, full_tail=False)
    st = state_conv[0]
    zeros_row = jnp.zeros((ns, 1, D_MODEL), F32)
    p1 = jnp.concatenate([st[:, 1:2], zeros_row, zeros_row, zeros_row], axis=1).reshape(1, rows_s, D_MODEL)
    p2 = jnp.concatenate([st[:, 0:1], st[:, 1:2], zeros_row, zeros_row], axis=1).reshape(1, rows_s, D_MODEL)
    out_s, u_s = _conv_layer(
        y_s, (sc1s, sh1s, g1s, sc2s, sh2s, g2s), p1, p2,
        vec(norm1_g[1]), vec(norm2_g[1]), vec(final_g), win, conv_k[0], wout, wgu[1], wd[1],
        seg=tn, full_tail=True)

    keep = CONV_WIDTH - 1
    return (
        out_p,
        out_s.reshape(ns, tn, D_MODEL),
        k_p.reshape(1, nb, t, N_KV_HEADS, HEAD_DIM),
        v_p.reshape(1, nb, t, N_KV_HEADS, HEAD_DIM),
        ki_p[None],
        tail_p[None, :, SUBLANES - keep:, :],
        k_s.reshape(1, ns, tn, N_KV_HEADS, HEAD_DIM),
        v_s.reshape(1, ns, tn, N_KV_HEADS, HEAD_DIM),
        ki_s[None],
        u_s.reshape(ns, tn, D_MODEL)[None, :, tn - keep:, :],
    )
```

```python
import functools

import jax
import jax.numpy as jnp
from jax import lax
from jax.experimental import pallas as pl
from jax.experimental.pallas import tpu as pltpu

F32 = jnp.float32
BF16 = jnp.bfloat16

D_MODEL = 1024
N_HEADS = 16
HEAD_DIM = 64
N_KV_HEADS = 4
HEADS_PER_KV = N_HEADS // N_KV_HEADS
ROT_DIM = 16
ROPE_THETA = 500000.0
IDX_HEADS = 8
IDX_DIM = 64
TOPK = 256
PAGE_SIZE = 128
CONV_WIDTH = 3
D_FF = 2816
EPS = 1e-6
Q_COLS = N_HEADS * HEAD_DIM
KV_COLS = N_KV_HEADS * HEAD_DIM
IQ_COLS = IDX_HEADS * IDX_DIM
ATTN_IN = Q_COLS + 2 * KV_COLS + IQ_COLS + IDX_DIM + IDX_HEADS
ATTN_IN_PAD = 2176
LANES = 128
SUBLANES = 8
MASK_BIAS = -1e30
F32_MAX = 3.4028234663852886e38
VMEM_LIMIT = 60 * 1024 * 1024

ROW_TILE = 512
Q_TILE = 256
SCORE_CHUNK = 512
ATTN_CHUNK = 128
V_ROWS = HEAD_DIM + 16
FFN_CHUNK = 256


def _dot(a, b):
    return jnp.dot(a, b, preferred_element_type=F32)


def _dot_nt(a, b):
    return lax.dot_general(a, b, (((1,), (1,)), ((), ())), preferred_element_type=F32)


def _silu(x):
    return x / (1.0 + jnp.exp(-x))


def _norm_mod(x, g, sc, sh):
    ms = jnp.mean(x * x, axis=-1, keepdims=True)
    y = x * lax.rsqrt(ms + EPS)
    return (y * g) * (1.0 + sc) + sh


def _params(*sem):
    return pltpu.CompilerParams(dimension_semantics=sem, vmem_limit_bytes=VMEM_LIMIT)


def _const_spec(shape):
    nd = len(shape)
    return pl.BlockSpec(shape, lambda *_: (0,) * nd, pipeline_mode=pl.Buffered(1))


def _mod_kernel(c_ref, w_ref, b_ref, o_ref):
    s = _silu(c_ref[...]).astype(BF16)
    o_ref[...] = _dot(s, w_ref[...].astype(BF16)) + b_ref[...]


def _modulation(c_all, ada_w, ada_b):
    depth, _, n = ada_w.shape
    rows = c_all.shape[0]
    tn = 512
    return pl.pallas_call(
        _mod_kernel,
        grid=(depth, n // tn),
        in_specs=[
            pl.BlockSpec((rows, D_MODEL), lambda l, j: (0, 0)),
            pl.BlockSpec((None, D_MODEL, tn), lambda l, j: (l, 0, j)),
            pl.BlockSpec((None, 1, tn), lambda l, j: (l, 0, j)),
        ],
        out_specs=pl.BlockSpec((None, rows, tn), lambda l, j: (l, 0, j)),
        out_shape=jax.ShapeDtypeStruct((depth, rows, n), F32),
        compiler_params=_params("arbitrary", "arbitrary"),
        name="adaln_modulation",
    )(c_all, ada_w, ada_b.reshape(depth, 1, n))


def _rope_block(x, c, s1, s2):
    return x * c + pltpu.roll(x, LANES - ROT_DIM // 2, 1) * s1 + pltpu.roll(x, ROT_DIM // 2, 1) * s2


def _rope_cols(p, c, s1, s2):
    nblk = p.shape[1] // LANES
    return [_rope_block(p[:, j * LANES:(j + 1) * LANES], c, s1, s2) for j in range(nblk)]


def _last_block_tables(c, s1, s2):
    lane = lax.broadcasted_iota(jnp.int32, c.shape, 1)
    is_key = lane < IDX_DIM
    return jnp.where(is_key, c, 1.0), jnp.where(is_key, s1, 0.0), jnp.where(is_key, s2, 0.0)


def _proj_prompt_kernel(x_ref, sc_ref, sh_ref, g_ref, w_ref, c_ref, s1_ref, s2_ref,
                        k_ref, v_ref, ki_ref, qt_ref, qit_ref, kg_ref, vt_ref, kib_ref, wit_ref):
    h = _norm_mod(x_ref[...], g_ref[...], sc_ref[...], sh_ref[...]).astype(BF16)
    c, s1, s2 = c_ref[...], s1_ref[...], s2_ref[...]

    pq = _dot(h, w_ref[:, 0:Q_COLS])
    for j, blk in enumerate(_rope_cols(pq, c, s1, s2)):
        qt_ref[j * LANES:(j + 1) * LANES, :] = (blk * HEAD_DIM ** -0.5).T.astype(BF16)

    pk = _dot(h, w_ref[:, Q_COLS:Q_COLS + KV_COLS])
    k = jnp.concatenate(_rope_cols(pk, c, s1, s2), axis=1)
    k_ref[...] = k
    for g in range(N_KV_HEADS):
        kg_ref[g] = k[:, g * HEAD_DIM:(g + 1) * HEAD_DIM].astype(BF16)

    pv = _dot(h, w_ref[:, Q_COLS + KV_COLS:Q_COLS + 2 * KV_COLS])
    v_ref[...] = pv
    ones = jnp.ones((V_ROWS - HEAD_DIM, ATTN_CHUNK), BF16)
    for cc in range(pv.shape[0] // ATTN_CHUNK):
        vt = pv[cc * ATTN_CHUNK:(cc + 1) * ATTN_CHUNK, :].T.astype(BF16)
        for g in range(N_KV_HEADS):
            vt_ref[cc, g * V_ROWS:g * V_ROWS + HEAD_DIM, :] = vt[g * HEAD_DIM:(g + 1) * HEAD_DIM, :]
            vt_ref[cc, g * V_ROWS + HEAD_DIM:(g + 1) * V_ROWS, :] = ones

    o = Q_COLS + 2 * KV_COLS
    pqi = _dot(h, w_ref[:, o:o + IQ_COLS])
    for j, blk in enumerate(_rope_cols(pqi, c, s1, s2)):
        qit_ref[j * LANES:(j + 1) * LANES, :] = blk.T.astype(BF16)

    pl_ = _dot(h, w_ref[:, o + IQ_COLS:ATTN_IN_PAD])
    last = _rope_block(pl_, *_last_block_tables(c, s1, s2))
    ki = last[:, 0:IDX_DIM]
    ki_ref[...] = ki
    kib_ref[...] = ki.astype(BF16)
    wit_ref[...] = last.T[IDX_DIM:IDX_DIM + IDX_HEADS, :]


def _proj_prompt(x, sc, sh, g, w, rope):
    nb, t, _ = x.shape
    tm = ROW_TILE
    row = lambda width: pl.BlockSpec((None, tm, width), lambda b, j: (b, j, 0))
    mod = pl.BlockSpec((None, 1, D_MODEL), lambda b, j: (b, 0, 0))
    tab = pl.BlockSpec((tm, LANES), lambda b, j: (j, 0))
    col = lambda height: pl.BlockSpec((None, height, tm), lambda b, j: (b, 0, j))
    out_shape = (
        jax.ShapeDtypeStruct((nb, t, KV_COLS), F32),
        jax.ShapeDtypeStruct((nb, t, KV_COLS), F32),
        jax.ShapeDtypeStruct((nb, t, IDX_DIM), F32),
        jax.ShapeDtypeStruct((nb, Q_COLS, t), BF16),
        jax.ShapeDtypeStruct((nb, IQ_COLS, t), BF16),
        jax.ShapeDtypeStruct((nb, N_KV_HEADS, t, HEAD_DIM), BF16),
        jax.ShapeDtypeStruct((nb, t // ATTN_CHUNK, N_KV_HEADS * V_ROWS, ATTN_CHUNK), BF16),
        jax.ShapeDtypeStruct((nb, t, IDX_DIM), BF16),
        jax.ShapeDtypeStruct((nb, IDX_HEADS, t), F32),
    )
    out_specs = (
        row(KV_COLS), row(KV_COLS), row(IDX_DIM), col(Q_COLS), col(IQ_COLS),
        pl.BlockSpec((None, N_KV_HEADS, tm, HEAD_DIM), lambda b, j: (b, 0, j, 0)),
        pl.BlockSpec((None, tm // ATTN_CHUNK, N_KV_HEADS * V_ROWS, ATTN_CHUNK), lambda b, j: (b, j, 0, 0)),
        row(IDX_DIM), col(IDX_HEADS),
    )
    return pl.pallas_call(
        _proj_prompt_kernel,
        grid=(nb, t // tm),
        in_specs=[row(D_MODEL), mod, mod, _const_spec((1, D_MODEL)),
                  _const_spec((D_MODEL, ATTN_IN_PAD)), tab, tab, tab],
        out_specs=out_specs,
        out_shape=out_shape,
        compiler_params=_params("arbitrary", "arbitrary"),
        name="attn_proj_prompt",
    )(x, sc, sh, g, w, *rope)


def _proj_sample_kernel(x_ref, sc_ref, sh_ref, g_ref, w_ref, c_ref, s1_ref, s2_ref, p_ref):
    h = _norm_mod(x_ref[...], g_ref[...], sc_ref[...], sh_ref[...]).astype(BF16)
    c, s1, s2 = c_ref[...], s1_ref[...], s2_ref[...]
    p = _dot(h, w_ref[...])
    v_lo = (Q_COLS + KV_COLS) // LANES
    v_hi = (Q_COLS + 2 * KV_COLS) // LANES
    nblk = ATTN_IN_PAD // LANES
    for j in range(nblk):
        blk = p[:, j * LANES:(j + 1) * LANES]
        if j == nblk - 1:
            blk = _rope_block(blk, *_last_block_tables(c, s1, s2))
        elif not (v_lo <= j < v_hi):
            blk = _rope_block(blk, c, s1, s2)
        p_ref[:, j * LANES:(j + 1) * LANES] = blk


def _proj_sample(x, sc, sh, g, w, rope):
    rows = x.shape[0]
    full = lambda width: pl.BlockSpec((rows, width), lambda i: (0, 0))
    return pl.pallas_call(
        _proj_sample_kernel,
        grid=(1,),
        in_specs=[full(D_MODEL), full(D_MODEL), full(D_MODEL),
                  pl.BlockSpec((1, D_MODEL), lambda i: (0, 0)),
                  pl.BlockSpec((D_MODEL, ATTN_IN_PAD), lambda i: (0, 0)),
                  full(LANES), full(LANES), full(LANES)],
        out_specs=full(ATTN_IN_PAD),
        out_shape=jax.ShapeDtypeStruct((rows, ATTN_IN_PAD), F32),
        compiler_params=_params("arbitrary"),
        name="attn_proj_sample",
    )(x, sc, sh, g, w, *rope)


FOLD_ROWS = 32


def _fold(x, op):
    rows, tq = x.shape
    x3 = x.reshape(rows // FOLD_ROWS, FOLD_ROWS, tq)
    return {"sum": x3.sum, "max": x3.max, "min": x3.min}[op](axis=0)


def _select_to_bias(sc_ref, qpos, nk, kc):
    tq = sc_ref.shape[1]
    topk = float(TOPK)

    def chunk(c):
        return sc_ref[pl.ds(pl.multiple_of(c * kc, kc), kc), :]

    def key_index(c):
        return c * kc + lax.broadcasted_iota(jnp.int32, (kc, tq), 0)

    def ones_where(m):
        return _fold(jnp.where(m, 1.0, 0.0), "sum")

    def count(pred):
        def body(c, acc):
            return acc + ones_where(pred(chunk(c), c))
        acc = lax.fori_loop(0, nk, body, jnp.zeros((FOLD_ROWS, tq), F32))
        return acc.sum(axis=0, keepdims=True)

    def stats(c, carry):
        mx, mn, n_above, n_ge0, n_gt0 = carry
        s = chunk(c)
        above = s > -jnp.inf
        mx = jnp.maximum(mx, _fold(s, "max"))
        mn = jnp.minimum(mn, _fold(jnp.where(above, s, jnp.inf), "min"))
        return (mx, mn, n_above + ones_where(above), n_ge0 + ones_where(s >= 0.0),
                n_gt0 + ones_where(s > 0.0))

    zeros = jnp.zeros((FOLD_ROWS, tq), F32)
    mx, mn, n_above, n_ge0, n_gt0 = lax.fori_loop(
        0, nk, stats, (jnp.full((FOLD_ROWS, tq), -jnp.inf, F32), jnp.full((FOLD_ROWS, tq), jnp.inf, F32),
                       zeros, zeros, zeros))
    mx = mx.max(axis=0, keepdims=True)
    mn = mn.min(axis=0, keepdims=True)
    n_above = n_above.sum(axis=0, keepdims=True)
    n_ge0 = n_ge0.sum(axis=0, keepdims=True)
    n_gt0 = n_gt0.sum(axis=0, keepdims=True)
    n_max = count(lambda s, c: s >= mx)

    few = n_above < topk
    flat = n_max >= topk
    zero = (n_gt0 < topk) & (n_ge0 >= topk)
    pos = n_gt0 >= topk
    lo = jnp.where(few, -jnp.inf, jnp.where(flat, mx, jnp.where(zero | pos, 0.0, mn)))
    hi = jnp.where(few, mn, jnp.where(flat, jnp.inf, jnp.where(pos, mx, 0.0)))
    n_lo = jnp.where(few, 2.0 * topk, jnp.where(flat, n_max, jnp.where(zero | pos, n_ge0, n_above)))
    n_hi = jnp.where(few, n_above, jnp.where(flat, 0.0, jnp.where(zero, n_gt0, jnp.where(pos, n_max, n_ge0))))
    done = jnp.where(few | flat | zero | (n_lo == topk), 1.0, 0.0)

    def not_finished(d):
        return (jnp.min(d) < 0.5).astype(jnp.int32)

    def bisect(carry):
        lo, hi, n_lo, n_hi, done, _ = carry
        mid = jnp.clip(0.5 * lo + 0.5 * hi, -F32_MAX, F32_MAX)
        stuck = (mid <= lo) | (mid >= hi)
        n_mid = count(lambda s, c: s >= mid)
        move = (done < 0.5) & jnp.logical_not(stuck)
        up = move & (n_mid >= topk)
        down = move & (n_mid < topk)
        lo = jnp.where(up, mid, lo)
        n_lo = jnp.where(up, n_mid, n_lo)
        hi = jnp.where(down, mid, hi)
        n_hi = jnp.where(down, n_mid, n_hi)
        done = jnp.where(stuck | (n_lo == topk), 1.0, done)
        return lo, hi, n_lo, n_hi, done, not_finished(done)

    lo, hi, n_lo, n_hi, done, _ = lax.while_loop(
        lambda carry: carry[5] > 0, bisect, (lo, hi, n_lo, n_hi, done, not_finished(done)))

    thr = lo
    exact = n_lo == topk
    need = topk - n_hi
    all_keys = jnp.full((1, tq), 2 ** 30, jnp.int32)

    def tie_break():
        def step(i, j):
            cand = j + lax.shift_right_logical(jnp.int32(2048), i)
            n_before = count(lambda s, c: (s == thr) & (key_index(c) < cand))
            return jnp.where(n_before < need, cand, j)
        return lax.fori_loop(0, 12, step, jnp.zeros((1, tq), jnp.int32))

    any_tie = jnp.min(jnp.where(exact, 1.0, 0.0)) < 0.5
    last_tied = lax.cond(any_tie, tie_break, lambda: all_keys)
    last_tied = jnp.where(exact, all_keys, last_tied)

    def write(c, _):
        s = chunk(c)
        idx = key_index(c)
        keep = ((s > thr) | ((s == thr) & (idx <= last_tied))) & (idx <= qpos)
        sc_ref[pl.ds(pl.multiple_of(c * kc, kc), kc), :] = jnp.where(keep, 0.0, MASK_BIAS)
        return 0

    lax.fori_loop(0, nk, write, 0)


def _prompt_attn_kernel(qt_ref, qit_ref, wit_ref, kg_ref, vt_ref, kib_ref, o_ref,
                        sc_ref, acc_ref, m_ref):
    i = pl.program_id(1)
    tq = qt_ref.shape[1]
    qpos = i * tq + lax.broadcasted_iota(jnp.int32, (1, tq), 1)
    n_keys = (i + 1) * tq

    n_score = (n_keys + SCORE_CHUNK - 1) // SCORE_CHUNK

    def score_chunk(c, _):
        start = pl.multiple_of(c * SCORE_CHUNK, SCORE_CHUNK)
        kc = kib_ref[pl.ds(start, SCORE_CHUNK), :]
        acc = jnp.zeros((SCORE_CHUNK, tq), F32)
        for h in range(IDX_HEADS):
            d = _dot(kc, qit_ref[h * IDX_DIM:(h + 1) * IDX_DIM, :])
            acc = acc + jnp.maximum(d, 0.0) * wit_ref[h:h + 1, :]
        idx = start + lax.broadcasted_iota(jnp.int32, (SCORE_CHUNK, tq), 0)
        sc_ref[pl.ds(start, SCORE_CHUNK), :] = jnp.where(idx <= qpos, acc, -jnp.inf)
        return 0

    lax.fori_loop(0, n_score, score_chunk, 0)

    _select_to_bias(sc_ref, qpos, n_score, SCORE_CHUNK)

    m_ref[...] = jnp.full(m_ref.shape, MASK_BIAS, F32)
    acc_ref[...] = jnp.zeros(acc_ref.shape, F32)

    def attn_chunk(c, _):
        start = pl.multiple_of(c * ATTN_CHUNK, ATTN_CHUNK)
        for h in range(N_HEADS):
            g = h // HEADS_PER_KV
            s = _dot(kg_ref[g, pl.ds(start, ATTN_CHUNK), :], qt_ref[h * HEAD_DIM:(h + 1) * HEAD_DIM, :])
            s = s + sc_ref[pl.ds(start, ATTN_CHUNK), :]
            m_old = m_ref[h]
            m_new = jnp.maximum(m_old, s.max(axis=0, keepdims=True))
            alpha = jnp.exp(m_old - m_new)
            p = jnp.exp(s - m_new).astype(BF16)
            vt = vt_ref[c, g * V_ROWS:(g + 1) * V_ROWS, :]
            acc_ref[h] = alpha * acc_ref[h] + _dot(vt, p)
            m_ref[h] = m_new
        return 0

    lax.fori_loop(0, (i + 1) * (tq // ATTN_CHUNK), attn_chunk, 0)

    for pair in range(N_HEADS // 2):
        two = []
        for h in (2 * pair, 2 * pair + 1):
            a = acc_ref[h]
            two.append(a[0:HEAD_DIM, :] / a[HEAD_DIM:HEAD_DIM + 1, :])
        col = 2 * pair * HEAD_DIM
        o_ref[:, col:col + 2 * HEAD_DIM] = jnp.concatenate(two, axis=0).T.astype(BF16)


def _prompt_attention(qt, qit, wit, kg, vt, kib):
    nb, _, t = qt.shape
    tq = Q_TILE
    col = lambda height: pl.BlockSpec((None, height, tq), lambda b, i: (b, 0, i))
    return pl.pallas_call(
        _prompt_attn_kernel,
        grid=(nb, t // tq),
        in_specs=[
            col(Q_COLS), col(IQ_COLS), col(IDX_HEADS),
            pl.BlockSpec((None, N_KV_HEADS, t, HEAD_DIM), lambda b, i: (b, 0, 0, 0)),
            pl.BlockSpec((None, t // ATTN_CHUNK, N_KV_HEADS * V_ROWS, ATTN_CHUNK), lambda b, i: (b, 0, 0, 0)),
            pl.BlockSpec((None, t, IDX_DIM), lambda b, i: (b, 0, 0)),
        ],
        out_specs=pl.BlockSpec((None, tq, Q_COLS), lambda b, i: (b, i, 0)),
        out_shape=jax.ShapeDtypeStruct((nb, t, Q_COLS), BF16),
        scratch_shapes=[
            pltpu.VMEM((t, tq), F32),
            pltpu.VMEM((N_HEADS, V_ROWS, tq), F32),
            pltpu.VMEM((N_HEADS, 1, tq), F32),
        ],
        compiler_params=_params("arbitrary", "arbitrary"),
        name="prompt_attention",
    )(qt, qit, wit, kg, vt, kib)


def _sample_score_kernel(pt_ref, qi_ref, w_ref, *rest):
    del pt_ref
    n_pages = len(rest) - 3
    pages, new_ref, o_ref, kall_ref = rest[:n_pages], rest[n_pages], rest[n_pages + 1], rest[n_pages + 2]
    for j in range(n_pages):
        kall_ref[:, j * PAGE_SIZE:(j + 1) * PAGE_SIZE] = pages[j][...].astype(BF16)
    kall_ref[:, n_pages * PAGE_SIZE:(n_pages + 1) * PAGE_SIZE] = new_ref[...].astype(BF16)
    d = _dot(qi_ref[...], kall_ref[...])
    r = jnp.maximum(d, 0.0) * w_ref[...]
    o_ref[...] = r.reshape(IDX_HEADS, SUBLANES, r.shape[1]).sum(axis=0)


def _sample_scores(page_table, qi8, w8, cache_ki_t, ki_new_t):
    nb, n_pages = page_table.shape
    n_keys = (n_pages + 1) * PAGE_SIZE
    rows = IDX_HEADS * SUBLANES
    grid_spec = pltpu.PrefetchScalarGridSpec(
        num_scalar_prefetch=1,
        grid=(nb,),
        in_specs=[pl.BlockSpec((None, rows, IDX_DIM), lambda b, pt: (b, 0, 0)),
                  pl.BlockSpec((None, rows, 1), lambda b, pt: (b, 0, 0))]
        + [pl.BlockSpec((None, IDX_DIM, PAGE_SIZE), lambda b, pt, j=j: (pt[b, j], 0, 0))
           for j in range(n_pages)]
        + [pl.BlockSpec((None, IDX_DIM, PAGE_SIZE), lambda b, pt: (b, 0, 0))],
        out_specs=pl.BlockSpec((None, SUBLANES, n_keys), lambda b, pt: (b, 0, 0)),
        scratch_shapes=[pltpu.VMEM((IDX_DIM, n_keys), BF16)],
    )
    return pl.pallas_call(
        _sample_score_kernel,
        grid_spec=grid_spec,
        out_shape=jax.ShapeDtypeStruct((nb, SUBLANES, n_keys), F32),
        compiler_params=_params("arbitrary"),
        name="sample_indexer_scores",
    )(page_table, qi8, w8, *([cache_ki_t] * n_pages), ki_new_t)


def _sample_select_kernel(s_ref, o_ref, sc_ref, *, past, period):
    tq = s_ref.shape[1]
    n_keys = s_ref.shape[0]
    lane = lax.broadcasted_iota(jnp.int32, (1, tq), 1)
    qpos = past + (lane & (period - 1))
    nk = n_keys // PAGE_SIZE
    for c in range(nk):
        idx = c * PAGE_SIZE + lax.broadcasted_iota(jnp.int32, (PAGE_SIZE, tq), 0)
        sc_ref[c * PAGE_SIZE:(c + 1) * PAGE_SIZE, :] = jnp.where(
            idx <= qpos, s_ref[c * PAGE_SIZE:(c + 1) * PAGE_SIZE, :], -jnp.inf)
    _select_to_bias(sc_ref, qpos, nk, PAGE_SIZE)
    o_ref[...] = sc_ref[...]


def _sample_select(scores_t, past, period):
    n_keys, nq = scores_t.shape
    tq = Q_TILE
    blk = pl.BlockSpec((n_keys, tq), lambda i: (0, i))
    return pl.pallas_call(
        functools.partial(_sample_select_kernel, past=past, period=period),
        grid=(nq // tq,),
        in_specs=[blk],
        out_specs=blk,
        out_shape=jax.ShapeDtypeStruct((n_keys, nq), F32),
        scratch_shapes=[pltpu.VMEM((n_keys, tq), F32)],
        compiler_params=_params("arbitrary"),
        name="sample_select",
    )(scores_t)


def _sample_attn_kernel(pt_ref, q_ref, bias_ref, *rest):
    del pt_ref
    n_pages = (len(rest) - 5) // 2
    kpages, vpages = rest[:n_pages], rest[n_pages:2 * n_pages]
    knew_ref, vnew_ref, o_ref, kall_ref, vall_ref = rest[2 * n_pages:]
    for j in range(n_pages):
        kall_ref[:, :, j * PAGE_SIZE:(j + 1) * PAGE_SIZE] = kpages[j][...].astype(BF16)
        vall_ref[:, :, j * PAGE_SIZE:(j + 1) * PAGE_SIZE] = vpages[j][...].astype(BF16)
    kall_ref[:, :, n_pages * PAGE_SIZE:(n_pages + 1) * PAGE_SIZE] = knew_ref[...].astype(BF16)
    vall_ref[:, :, n_pages * PAGE_SIZE:(n_pages + 1) * PAGE_SIZE] = vnew_ref[...].astype(BF16)
    bias = bias_ref[...]
    n_keys = bias.shape[1]
    bias = jnp.broadcast_to(bias[None], (HEADS_PER_KV, SUBLANES, n_keys)).reshape(
        HEADS_PER_KV * SUBLANES, n_keys)
    for g in range(N_KV_HEADS):
        s = _dot(q_ref[g], kall_ref[g]) + bias
        m = s.max(axis=1, keepdims=True)
        p = jnp.exp(s - m)
        l = p.sum(axis=1, keepdims=True)
        o_ref[g] = _dot_nt(p.astype(BF16), vall_ref[g]) / l


def _sample_attention(page_table, q8, bias8, cache_k_t, cache_v_t, k_new_t, v_new_t):
    nb, n_pages = page_table.shape
    n_keys = (n_pages + 1) * PAGE_SIZE
    rows = HEADS_PER_KV * SUBLANES
    page = [pl.BlockSpec((None, N_KV_HEADS, HEAD_DIM, PAGE_SIZE), lambda b, pt, j=j: (pt[b, j], 0, 0, 0))
            for j in range(n_pages)]
    new = pl.BlockSpec((None, N_KV_HEADS, HEAD_DIM, PAGE_SIZE), lambda b, pt: (b, 0, 0, 0))
    qo = pl.BlockSpec((None, N_KV_HEADS, rows, HEAD_DIM), lambda b, pt: (b, 0, 0, 0))
    grid_spec = pltpu.PrefetchScalarGridSpec(
        num_scalar_prefetch=1,
        grid=(nb,),
        in_specs=[qo, pl.BlockSpec((None, SUBLANES, n_keys), lambda b, pt: (b, 0, 0))]
        + page + page + [new, new],
        out_specs=qo,
        scratch_shapes=[pltpu.VMEM((N_KV_HEADS, HEAD_DIM, n_keys), BF16),
                        pltpu.VMEM((N_KV_HEADS, HEAD_DIM, n_keys), BF16)],
    )
    return pl.pallas_call(
        _sample_attn_kernel,
        grid_spec=grid_spec,
        out_shape=jax.ShapeDtypeStruct((nb, N_KV_HEADS, rows, HEAD_DIM), F32),
        compiler_params=_params("arbitrary"),
        name="sample_attention",
    )(page_table, q8, bias8, *([cache_k_t] * n_pages), *([cache_v_t] * n_pages), k_new_t, v_new_t)


def _ffn(h, wgu_ref, wd_ref):
    acc = jnp.zeros((h.shape[0], D_MODEL), F32)
    for c in range(D_FF // FFN_CHUNK):
        gate = _dot(h, wgu_ref[:, c * FFN_CHUNK:(c + 1) * FFN_CHUNK])
        up = _dot(h, wgu_ref[:, D_FF + c * FFN_CHUNK:D_FF + (c + 1) * FFN_CHUNK])
        a = _silu(gate) * up
        acc = acc + _dot(a.astype(BF16), wd_ref[c * FFN_CHUNK:(c + 1) * FFN_CHUNK, :])
    return acc


def _row_specs(nb, rows, per_row_mod):
    tm = min(ROW_TILE, rows)
    row = pl.BlockSpec((None, tm, D_MODEL), lambda b, j: (b, j, 0))
    if per_row_mod:
        mod = row
    else:
        mod = pl.BlockSpec((None, 1, D_MODEL), lambda b, j: (b, 0, 0))
    return tm, row, mod


def _attn_out_ffn_kernel(x_ref, o_ref, g1_ref, sc2_ref, sh2_ref, g2_ref, n2_ref,
                         wo_ref, wgu_ref, wd_ref, y_ref):
    y1 = x_ref[...] + g1_ref[...] * _dot(o_ref[...], wo_ref[...])
    h2 = _norm_mod(y1, n2_ref[...], sc2_ref[...], sh2_ref[...]).astype(BF16)
    y_ref[...] = y1 + g2_ref[...] * _ffn(h2, wgu_ref, wd_ref)


def _attn_out_ffn(x, o, g1, sc2, sh2, g2, n2, wo, wgu, wd):
    nb, rows, _ = x.shape
    tm, row, mod = _row_specs(nb, rows, g1.shape[1] == rows)
    return pl.pallas_call(
        _attn_out_ffn_kernel,
        grid=(nb, rows // tm),
        in_specs=[row, row, mod, mod, mod, mod, _const_spec((1, D_MODEL)),
                  _const_spec(wo.shape), _const_spec(wgu.shape), _const_spec(wd.shape)],
        out_specs=row,
        out_shape=jax.ShapeDtypeStruct(x.shape, F32),
        compiler_params=_params("arbitrary", "arbitrary"),
        name="attn_out_ffn",
    )(x, o, g1, sc2, sh2, g2, n2, wo, wgu, wd)


def _conv_layer_kernel(x_ref, sc1_ref, sh1_ref, g1_ref, sc2_ref, sh2_ref, g2_ref, p1_ref, p2_ref,
                       n1_ref, n2_ref, nf_ref, win_ref, ck_ref, wout_ref, wgu_ref, wd_ref,
                       y_ref, tail_ref, ubuf_ref, z_ref, *, seg):
    j = pl.program_id(1)
    tm = x_ref.shape[0]
    x = x_ref[...]
    h = _norm_mod(x, n1_ref[...], sc1_ref[...], sh1_ref[...]).astype(BF16)

    @pl.when(j == 0)
    def _():
        ubuf_ref[0:SUBLANES, :] = jnp.zeros((SUBLANES, D_MODEL), F32)

    @pl.when(j > 0)
    def _():
        ubuf_ref[0:SUBLANES, :] = ubuf_ref[tm:tm + SUBLANES, :]

    t = (j * tm + lax.broadcasted_iota(jnp.int32, (tm, 1), 0)) & (seg - 1)
    ck = ck_ref[...]
    cw = FFN_CHUNK
    for c in range(D_MODEL // cw):
        cols = slice(c * cw, (c + 1) * cw)
        bg = _dot(h, win_ref[:, c * cw:(c + 1) * cw])
        cg = _dot(h, win_ref[:, D_MODEL + c * cw:D_MODEL + (c + 1) * cw])
        hv = _dot(h, win_ref[:, 2 * D_MODEL + c * cw:2 * D_MODEL + (c + 1) * cw])
        u = cg * hv
        ubuf_ref[SUBLANES:SUBLANES + tm, cols] = u
        um1 = jnp.where(t >= 1, ubuf_ref[SUBLANES - 1:SUBLANES - 1 + tm, cols], p1_ref[:, cols])
        um2 = jnp.where(t >= 2, ubuf_ref[SUBLANES - 2:SUBLANES - 2 + tm, cols], p2_ref[:, cols])
        conv = ck[0:1, cols] * um2 + ck[1:2, cols] * um1 + ck[2:3, cols] * u
        z_ref[:, cols] = (bg * conv).astype(BF16)
    r = tail_ref.shape[0]
    tail_ref[...] = ubuf_ref[SUBLANES + tm - r:SUBLANES + tm, :]

    y1 = x + g1_ref[...] * _dot(z_ref[...], wout_ref[...])
    h2 = _norm_mod(y1, n2_ref[...], sc2_ref[...], sh2_ref[...]).astype(BF16)
    y2 = y1 + g2_ref[...] * _ffn(h2, wgu_ref, wd_ref)
    ms = jnp.mean(y2 * y2, axis=-1, keepdims=True)
    y_ref[...] = (y2 * lax.rsqrt(ms + EPS)) * nf_ref[...]


def _conv_layer(x, mods, p1, p2, n1, n2, nf, win, ck, wout, wgu, wd, *, seg, full_tail):
    nb, rows, _ = x.shape
    per_row = mods[0].shape[1] == rows
    tm, row, mod = _row_specs(nb, rows, per_row)
    if per_row:
        prefix = row
    else:
        prefix = pl.BlockSpec((None, 1, D_MODEL), lambda b, j: (0, 0, 0))
    if full_tail:
        tail_spec, tail_rows = row, rows
    else:
        tail_spec, tail_rows = pl.BlockSpec((None, SUBLANES, D_MODEL), lambda b, j: (b, 0, 0)), SUBLANES
    vec = _const_spec((1, D_MODEL))
    return pl.pallas_call(
        functools.partial(_conv_layer_kernel, seg=seg),
        grid=(nb, rows // tm),
        in_specs=[row] + [mod] * 6 + [prefix, prefix, vec, vec, vec,
                                      _const_spec(win.shape), _const_spec(ck.shape), _const_spec(wout.shape),
                                      _const_spec(wgu.shape), _const_spec(wd.shape)],
        out_specs=(row, tail_spec),
        out_shape=(jax.ShapeDtypeStruct(x.shape, F32),
                   jax.ShapeDtypeStruct((nb, tail_rows, D_MODEL), F32)),
        scratch_shapes=[pltpu.VMEM((tm + SUBLANES, D_MODEL), F32), pltpu.VMEM((tm, D_MODEL), BF16)],
        compiler_params=_params("arbitrary", "arbitrary"),
        name="conv_layer",
    )(x, *mods, p1, p2, n1, n2, nf, win, ck, wout, wgu, wd)


def _rope_tables(pos):
    half = ROT_DIM // 2
    inv = ROPE_THETA ** (-jnp.arange(0, ROT_DIM, 2, dtype=F32) / ROT_DIM)
    ang = pos[:, None] * inv[None, :]
    cos, sin = jnp.cos(ang), jnp.sin(ang)
    n = pos.shape[0]
    rest = HEAD_DIM - ROT_DIM
    c = jnp.concatenate([cos, cos, jnp.ones((n, rest), F32)], axis=1)
    s1 = jnp.concatenate([-sin, jnp.zeros((n, half + rest), F32)], axis=1)
    s2 = jnp.concatenate([jnp.zeros((n, half), F32), sin, jnp.zeros((n, rest), F32)], axis=1)
    return tuple(jnp.tile(a, (1, LANES // HEAD_DIM)) for a in (c, s1, s2))


def kernel(x_prompt, x_sample, c_prompt, c_sample, cache_k, cache_v, cache_kidx, state_conv, page_table,
           ada_w, ada_b, norm1_g, norm2_g, final_g, attn_w_in, attn_w_out, conv_w_in, conv_k, conv_w_out,
           ffn_w_gu, ffn_w_down):
    nb, t, _ = x_prompt.shape
    ns, tn, _ = x_sample.shape
    n_pages = page_table.shape[1]
    past = n_pages * PAGE_SIZE
    n_phys = cache_k.shape[1]
    rows_s = ns * tn

    pad = (-(nb + ns)) % SUBLANES
    c_all = jnp.concatenate([c_prompt, c_sample, jnp.zeros((pad, D_MODEL), F32)], axis=0)
    mod = _modulation(c_all, ada_w, ada_b)

    def mods(layer):
        m = mod[layer].reshape(-1, 6, D_MODEL)
        prompt = [m[:nb, i][:, None, :] for i in range(6)]
        sample = [jnp.repeat(m[nb:nb + ns, i], tn, axis=0)[None] for i in range(6)]
        return prompt, sample

    vec = lambda a: a.reshape(1, D_MODEL)
    w_in = jnp.pad(attn_w_in[0], ((0, 0), (0, ATTN_IN_PAD - ATTN_IN))).astype(BF16)
    w_o = attn_w_out[0].astype(BF16)
    wgu = [ffn_w_gu[i].astype(BF16) for i in range(2)]
    wd = [ffn_w_down[i].astype(BF16) for i in range(2)]

    (sh1p, sc1p, g1p, sh2p, sc2p, g2p), (sh1s, sc1s, g1s, sh2s, sc2s, g2s) = mods(0)
    rope_p = _rope_tables(jnp.arange(t, dtype=F32))
    k_p, v_p, ki_p, qt, qit, kg, vt, kib, wit = _proj_prompt(
        x_prompt, sc1p, sh1p, vec(norm1_g[0]), w_in, rope_p)
    o_p = _prompt_attention(qt, qit, wit, kg, vt, kib)
    y_p = _attn_out_ffn(x_prompt, o_p, g1p, sc2p, sh2p, g2p, vec(norm2_g[0]), w_o, wgu[0], wd[0])

    pos_s = jnp.tile(past + jnp.arange(tn, dtype=F32), ns)
    rope_s = _rope_tables(pos_s)
    xs = x_sample.reshape(rows_s, D_MODEL)
    proj = _proj_sample(xs, sc1s[0], sh1s[0], vec(norm1_g[0]), w_in, rope_s)
    o = 0
    q_s = proj[:, o:o + Q_COLS].reshape(ns, tn, N_KV_HEADS, HEADS_PER_KV, HEAD_DIM); o += Q_COLS
    k_s = proj[:, o:o + KV_COLS].reshape(ns, tn, KV_COLS); o += KV_COLS
    v_s = proj[:, o:o + KV_COLS].reshape(ns, tn, KV_COLS); o += KV_COLS
    qi_s = proj[:, o:o + IQ_COLS].reshape(ns, tn, IDX_HEADS, IDX_DIM); o += IQ_COLS
    ki_s = proj[:, o:o + IDX_DIM].reshape(ns, tn, IDX_DIM); o += IDX_DIM
    wi_s = proj[:, o:o + IDX_HEADS].reshape(ns, tn, IDX_HEADS)

    qpad = ((0, 0), (0, SUBLANES - tn))
    qi8 = jnp.pad(qi_s, qpad + ((0, 0), (0, 0))).transpose(0, 2, 1, 3)
    qi8 = qi8.reshape(ns, IDX_HEADS * SUBLANES, IDX_DIM).astype(BF16)
    w8 = jnp.pad(wi_s, qpad + ((0, 0),)).transpose(0, 2, 1).reshape(ns, IDX_HEADS * SUBLANES, 1)
    slot_pad = (0, PAGE_SIZE - tn)
    ki_new_t = jnp.pad(ki_s.transpose(0, 2, 1), ((0, 0), (0, 0), slot_pad))
    scores = _sample_scores(page_table, qi8, w8, cache_kidx[0].transpose(0, 2, 1), ki_new_t)
    n_keys = scores.shape[2]
    scores_t = scores[:, :tn, :].transpose(2, 0, 1).reshape(n_keys, rows_s)
    bias_t = _sample_select(scores_t, past, tn)
    bias8 = jnp.pad(bias_t.reshape(n_keys, ns, tn).transpose(1, 2, 0), qpad + ((0, 0),))

    q8 = jnp.pad(q_s * HEAD_DIM ** -0.5, qpad + ((0, 0), (0, 0), (0, 0))).transpose(0, 2, 3, 1, 4)
    q8 = q8.reshape(ns, N_KV_HEADS, HEADS_PER_KV * SUBLANES, HEAD_DIM).astype(BF16)
    new_t = lambda a: jnp.pad(a.reshape(ns, tn, N_KV_HEADS, HEAD_DIM).transpose(0, 2, 3, 1),
                              ((0, 0), (0, 0), (0, 0), slot_pad))
    o_g = _sample_attention(page_table, q8, bias8,
                            cache_k[0].transpose(0, 2, 3, 1), cache_v[0].transpose(0, 2, 3, 1),
                            new_t(k_s), new_t(v_s))
    o_g = o_g.reshape(ns, N_KV_HEADS, HEADS_PER_KV, SUBLANES, HEAD_DIM)[:, :, :, :tn]
    o_s = o_g.transpose(0, 3, 1, 2, 4).reshape(1, rows_s, Q_COLS).astype(BF16)
    y_s = _attn_out_ffn(xs[None], o_s, g1s, sc2s, sh2s, g2s, vec(norm2_g[0]), w_o, wgu[0], wd[0])

    (sh1p, sc1p, g1p, sh2p, sc2p, g2p), (sh1s, sc1s, g1s, sh2s, sc2s, g2s) = mods(1)
    win = conv_w_in[0].astype(BF16)
    wout = conv_w_out[0].astype(BF16)
    zero_prefix = jnp.zeros((1, 1, D_MODEL), F32)
    out_p, tail_p = _conv_layer(
        y_p, (sc1p, sh1p, g1p, sc2p, sh2p, g2p), zero_prefix, zero_prefix,
        vec(norm1_g[1]), vec(norm2_g[1]), vec(final_g), win, conv_k[0], wout, wgu[1], wd[1],
        seg=t, full_tail=False)
    st = state_conv[0]
    zeros_row = jnp.zeros((ns, 1, D_MODEL), F32)
    p1 = jnp.concatenate([st[:, 1:2], zeros_row, zeros_row, zeros_row], axis=1).reshape(1, rows_s, D_MODEL)
    p2 = jnp.concatenate([st[:, 0:1], st[:, 1:2], zeros_row, zeros_row], axis=1).reshape(1, rows_s, D_MODEL)
    out_s, u_s = _conv_layer(
        y_s, (sc1s, sh1s, g1s, sc2s, sh2s, g2s), p1, p2,
        vec(norm1_g[1]), vec(norm2_g[1]), vec(final_g), win, conv_k[0], wout, wgu[1], wd[1],
        seg=tn, full_tail=True)

    keep = CONV_WIDTH - 1
    return (
        out_p,
        out_s.reshape(ns, tn, D_MODEL),
        k_p.reshape(1, nb, t, N_KV_HEADS, HEAD_DIM),
        v_p.reshape(1, nb, t, N_KV_HEADS, HEAD_DIM),
        ki_p[None],
        tail_p[None, :, SUBLANES - keep:, :],
        k_s.reshape(1, ns, tn, N_KV_HEADS, HEAD_DIM),
        v_s.reshape(1, ns, tn, N_KV_HEADS, HEAD_DIM),
        ki_s[None],
        u_s.reshape(ns, tn, D_MODEL)[None, :, tn - keep:, :],
    )
```

```python
import functools

import jax
import jax.numpy as jnp
from jax import lax
from jax.experimental import pallas as pl
from jax.experimental.pallas import tpu as pltpu

F32 = jnp.float32
BF16 = jnp.bfloat16

D_MODEL = 1024
N_HEADS = 16
HEAD_DIM = 64
N_KV_HEADS = 4
HEADS_PER_KV = N_HEADS // N_KV_HEADS
ROT_DIM = 16
ROPE_THETA = 500000.0
IDX_HEADS = 8
IDX_DIM = 64
TOPK = 256
PAGE_SIZE = 128
CONV_WIDTH = 3
D_FF = 2816
EPS = 1e-6
Q_COLS = N_HEADS * HEAD_DIM
KV_COLS = N_KV_HEADS * HEAD_DIM
IQ_COLS = IDX_HEADS * IDX_DIM
ATTN_IN = Q_COLS + 2 * KV_COLS + IQ_COLS + IDX_DIM + IDX_HEADS
ATTN_IN_PAD = 2176
LANES = 128
SUBLANES = 8
MASK_BIAS = -1e30
F32_MAX = 3.4028234663852886e38
VMEM_LIMIT = 60 * 1024 * 1024

ROW_TILE = 512
Q_TILE = 256
SCORE_CHUNK = 512
ATTN_CHUNK = 128
V_ROWS = HEAD_DIM + 16
QK_AHEAD = 15
FFN_CHUNK = 256


def _dot(a, b):
    return jnp.dot(a, b, preferred_element_type=F32)


def _dot_nt(a, b):
    return lax.dot_general(a, b, (((1,), (1,)), ((), ())), preferred_element_type=F32)


def _silu(x):
    return x / (1.0 + jnp.exp(-x))


def _norm_mod(x, g, sc, sh):
    ms = jnp.mean(x * x, axis=-1, keepdims=True)
    y = x * lax.rsqrt(ms + EPS)
    return (y * g) * (1.0 + sc) + sh


def _params(*sem, flags=None):
    return pltpu.CompilerParams(dimension_semantics=sem, vmem_limit_bytes=VMEM_LIMIT, flags=flags)


def _const_spec(shape):
    nd = len(shape)
    return pl.BlockSpec(shape, lambda *_: (0,) * nd, pipeline_mode=pl.Buffered(1))


def _mod_kernel(c_ref, w_ref, b_ref, o_ref):
    s = _silu(c_ref[...]).astype(BF16)
    o_ref[...] = _dot(s, w_ref[...].astype(BF16)) + b_ref[...]


def _modulation(c_all, ada_w, ada_b):
    depth, _, n = ada_w.shape
    rows = c_all.shape[0]
    tn = 512
    return pl.pallas_call(
        _mod_kernel,
        grid=(depth, n // tn),
        in_specs=[
            pl.BlockSpec((rows, D_MODEL), lambda l, j: (0, 0)),
            pl.BlockSpec((None, D_MODEL, tn), lambda l, j: (l, 0, j)),
            pl.BlockSpec((None, 1, tn), lambda l, j: (l, 0, j)),
        ],
        out_specs=pl.BlockSpec((None, rows, tn), lambda l, j: (l, 0, j)),
        out_shape=jax.ShapeDtypeStruct((depth, rows, n), F32),
        compiler_params=_params("arbitrary", "arbitrary"),
        name="adaln_modulation",
    )(c_all, ada_w, ada_b.reshape(depth, 1, n))


def _rope_block(x, c, s1, s2):
    return x * c + pltpu.roll(x, LANES - ROT_DIM // 2, 1) * s1 + pltpu.roll(x, ROT_DIM // 2, 1) * s2


def _rope_cols(p, c, s1, s2):
    nblk = p.shape[1] // LANES
    return [_rope_block(p[:, j * LANES:(j + 1) * LANES], c, s1, s2) for j in range(nblk)]


def _last_block_tables(c, s1, s2):
    lane = lax.broadcasted_iota(jnp.int32, c.shape, 1)
    is_key = lane < IDX_DIM
    return jnp.where(is_key, c, 1.0), jnp.where(is_key, s1, 0.0), jnp.where(is_key, s2, 0.0)


def _proj_prompt_kernel(x_ref, sc_ref, sh_ref, g_ref, w_ref, c_ref, s1_ref, s2_ref,
                        k_ref, v_ref, ki_ref, qt_ref, qit_ref, kg_ref, vt_ref, kib_ref, wit_ref):
    h = _norm_mod(x_ref[...], g_ref[...], sc_ref[...], sh_ref[...]).astype(BF16)
    c, s1, s2 = c_ref[...], s1_ref[...], s2_ref[...]

    pq = _dot(h, w_ref[:, 0:Q_COLS])
    for j, blk in enumerate(_rope_cols(pq, c, s1, s2)):
        qt_ref[j * LANES:(j + 1) * LANES, :] = (blk * HEAD_DIM ** -0.5).T.astype(BF16)

    pk = _dot(h, w_ref[:, Q_COLS:Q_COLS + KV_COLS])
    k = jnp.concatenate(_rope_cols(pk, c, s1, s2), axis=1)
    k_ref[...] = k
    for g in range(N_KV_HEADS):
        kg_ref[g] = k[:, g * HEAD_DIM:(g + 1) * HEAD_DIM].astype(BF16)

    pv = _dot(h, w_ref[:, Q_COLS + KV_COLS:Q_COLS + 2 * KV_COLS])
    v_ref[...] = pv
    ones = jnp.ones((V_ROWS - HEAD_DIM, ATTN_CHUNK), BF16)
    for cc in range(pv.shape[0] // ATTN_CHUNK):
        vt = pv[cc * ATTN_CHUNK:(cc + 1) * ATTN_CHUNK, :].T.astype(BF16)
        for g in range(N_KV_HEADS):
            vt_ref[cc, g * V_ROWS:g * V_ROWS + HEAD_DIM, :] = vt[g * HEAD_DIM:(g + 1) * HEAD_DIM, :]
            vt_ref[cc, g * V_ROWS + HEAD_DIM:(g + 1) * V_ROWS, :] = ones

    o = Q_COLS + 2 * KV_COLS
    pqi = _dot(h, w_ref[:, o:o + IQ_COLS])
    for j, blk in enumerate(_rope_cols(pqi, c, s1, s2)):
        qit_ref[j * LANES:(j + 1) * LANES, :] = blk.T.astype(BF16)

    pl_ = _dot(h, w_ref[:, o + IQ_COLS:ATTN_IN_PAD])
    last = _rope_block(pl_, *_last_block_tables(c, s1, s2))
    ki = last[:, 0:IDX_DIM]
    ki_ref[...] = ki
    kib_ref[...] = ki.astype(BF16)
    wit_ref[...] = last.T[IDX_DIM:IDX_DIM + IDX_HEADS, :]


def _proj_prompt(x, sc, sh, g, w, rope):
    nb, t, _ = x.shape
    tm = ROW_TILE
    row = lambda width: pl.BlockSpec((None, tm, width), lambda b, j: (b, j, 0))
    mod = pl.BlockSpec((None, 1, D_MODEL), lambda b, j: (b, 0, 0))
    tab = pl.BlockSpec((tm, LANES), lambda b, j: (j, 0))
    col = lambda height: pl.BlockSpec((None, height, tm), lambda b, j: (b, 0, j))
    out_shape = (
        jax.ShapeDtypeStruct((nb, t, KV_COLS), F32),
        jax.ShapeDtypeStruct((nb, t, KV_COLS), F32),
        jax.ShapeDtypeStruct((nb, t, IDX_DIM), F32),
        jax.ShapeDtypeStruct((nb, Q_COLS, t), BF16),
        jax.ShapeDtypeStruct((nb, IQ_COLS, t), BF16),
        jax.ShapeDtypeStruct((nb, N_KV_HEADS, t, HEAD_DIM), BF16),
        jax.ShapeDtypeStruct((nb, t // ATTN_CHUNK, N_KV_HEADS * V_ROWS, ATTN_CHUNK), BF16),
        jax.ShapeDtypeStruct((nb, t, IDX_DIM), BF16),
        jax.ShapeDtypeStruct((nb, IDX_HEADS, t), F32),
    )
    out_specs = (
        row(KV_COLS), row(KV_COLS), row(IDX_DIM), col(Q_COLS), col(IQ_COLS),
        pl.BlockSpec((None, N_KV_HEADS, tm, HEAD_DIM), lambda b, j: (b, 0, j, 0)),
        pl.BlockSpec((None, tm // ATTN_CHUNK, N_KV_HEADS * V_ROWS, ATTN_CHUNK), lambda b, j: (b, j, 0, 0)),
        row(IDX_DIM), col(IDX_HEADS),
    )
    return pl.pallas_call(
        _proj_prompt_kernel,
        grid=(nb, t // tm),
        in_specs=[row(D_MODEL), mod, mod, _const_spec((1, D_MODEL)),
                  _const_spec((D_MODEL, ATTN_IN_PAD)), tab, tab, tab],
        out_specs=out_specs,
        out_shape=out_shape,
        compiler_params=_params("arbitrary", "arbitrary"),
        name="attn_proj_prompt",
    )(x, sc, sh, g, w, *rope)


def _proj_sample_kernel(x_ref, sc_ref, sh_ref, g_ref, w_ref, c_ref, s1_ref, s2_ref, p_ref):
    h = _norm_mod(x_ref[...], g_ref[...], sc_ref[...], sh_ref[...]).astype(BF16)
    c, s1, s2 = c_ref[...], s1_ref[...], s2_ref[...]
    p = _dot(h, w_ref[...])
    v_lo = (Q_COLS + KV_COLS) // LANES
    v_hi = (Q_COLS + 2 * KV_COLS) // LANES
    nblk = ATTN_IN_PAD // LANES
    for j in range(nblk):
        blk = p[:, j * LANES:(j + 1) * LANES]
        if j == nblk - 1:
            blk = _rope_block(blk, *_last_block_tables(c, s1, s2))
        elif not (v_lo <= j < v_hi):
            blk = _rope_block(blk, c, s1, s2)
        p_ref[:, j * LANES:(j + 1) * LANES] = blk


def _proj_sample(x, sc, sh, g, w, rope):
    rows = x.shape[0]
    full = lambda width: pl.BlockSpec((rows, width), lambda i: (0, 0))
    return pl.pallas_call(
        _proj_sample_kernel,
        grid=(1,),
        in_specs=[full(D_MODEL), full(D_MODEL), full(D_MODEL),
                  pl.BlockSpec((1, D_MODEL), lambda i: (0, 0)),
                  pl.BlockSpec((D_MODEL, ATTN_IN_PAD), lambda i: (0, 0)),
                  full(LANES), full(LANES), full(LANES)],
        out_specs=full(ATTN_IN_PAD),
        out_shape=jax.ShapeDtypeStruct((rows, ATTN_IN_PAD), F32),
        compiler_params=_params("arbitrary"),
        name="attn_proj_sample",
    )(x, sc, sh, g, w, *rope)


FOLD_ROWS = 32


def _fold(x, op):
    rows, tq = x.shape
    x3 = x.reshape(rows // FOLD_ROWS, FOLD_ROWS, tq)
    return {"sum": x3.sum, "max": x3.max, "min": x3.min}[op](axis=0)


def _select_to_bias(sc_ref, qpos, nk, kc):
    tq = sc_ref.shape[1]
    topk = float(TOPK)

    def chunk(c):
        return sc_ref[pl.ds(pl.multiple_of(c * kc, kc), kc), :]

    def key_index(c):
        return c * kc + lax.broadcasted_iota(jnp.int32, (kc, tq), 0)

    def ones_where(m):
        return _fold(jnp.where(m, 1.0, 0.0), "sum")

    def count(pred):
        def body(c, acc):
            return acc + ones_where(pred(chunk(c), c))
        acc = lax.fori_loop(0, nk, body, jnp.zeros((FOLD_ROWS, tq), F32))
        return acc.sum(axis=0, keepdims=True)

    def stats(c, carry):
        mx, mn, n_above, n_ge0, n_gt0 = carry
        s = chunk(c)
        above = s > -jnp.inf
        mx = jnp.maximum(mx, _fold(s, "max"))
        mn = jnp.minimum(mn, _fold(jnp.where(above, s, jnp.inf), "min"))
        return (mx, mn, n_above + ones_where(above), n_ge0 + ones_where(s >= 0.0),
                n_gt0 + ones_where(s > 0.0))

    zeros = jnp.zeros((FOLD_ROWS, tq), F32)
    mx, mn, n_above, n_ge0, n_gt0 = lax.fori_loop(
        0, nk, stats, (jnp.full((FOLD_ROWS, tq), -jnp.inf, F32), jnp.full((FOLD_ROWS, tq), jnp.inf, F32),
                       zeros, zeros, zeros))
    mx = mx.max(axis=0, keepdims=True)
    mn = mn.min(axis=0, keepdims=True)
    n_above = n_above.sum(axis=0, keepdims=True)
    n_ge0 = n_ge0.sum(axis=0, keepdims=True)
    n_gt0 = n_gt0.sum(axis=0, keepdims=True)
    n_max = count(lambda s, c: s >= mx)

    few = n_above < topk
    flat = n_max >= topk
    zero = (n_gt0 < topk) & (n_ge0 >= topk)
    pos = n_gt0 >= topk
    lo = jnp.where(few, -jnp.inf, jnp.where(flat, mx, jnp.where(zero | pos, 0.0, mn)))
    hi = jnp.where(few, mn, jnp.where(flat, jnp.inf, jnp.where(pos, mx, 0.0)))
    n_lo = jnp.where(few, 2.0 * topk, jnp.where(flat, n_max, jnp.where(zero | pos, n_ge0, n_above)))
    n_hi = jnp.where(few, n_above, jnp.where(flat, 0.0, jnp.where(zero, n_gt0, jnp.where(pos, n_max, n_ge0))))
    done = jnp.where(few | flat | zero | (n_lo == topk), 1.0, 0.0)

    def not_finished(d):
        return (jnp.min(d) < 0.5).astype(jnp.int32)

    def bisect(carry):
        lo, hi, n_lo, n_hi, done, _ = carry
        mid = jnp.clip(0.5 * lo + 0.5 * hi, -F32_MAX, F32_MAX)
        stuck = (mid <= lo) | (mid >= hi)
        n_mid = count(lambda s, c: s >= mid)
        move = (done < 0.5) & jnp.logical_not(stuck)
        up = move & (n_mid >= topk)
        down = move & (n_mid < topk)
        lo = jnp.where(up, mid, lo)
        n_lo = jnp.where(up, n_mid, n_lo)
        hi = jnp.where(down, mid, hi)
        n_hi = jnp.where(down, n_mid, n_hi)
        done = jnp.where(stuck | (n_lo == topk), 1.0, done)
        return lo, hi, n_lo, n_hi, done, not_finished(done)

    lo, hi, n_lo, n_hi, done, _ = lax.while_loop(
        lambda carry: carry[5] > 0, bisect, (lo, hi, n_lo, n_hi, done, not_finished(done)))

    thr = lo
    exact = n_lo == topk
    need = topk - n_hi
    all_keys = jnp.full((1, tq), 2 ** 30, jnp.int32)

    def tie_break():
        def step(i, j):
            cand = j + lax.shift_right_logical(jnp.int32(2048), i)
            n_before = count(lambda s, c: (s == thr) & (key_index(c) < cand))
            return jnp.where(n_before < need, cand, j)
        return lax.fori_loop(0, 12, step, jnp.zeros((1, tq), jnp.int32))

    any_tie = jnp.min(jnp.where(exact, 1.0, 0.0)) < 0.5
    last_tied = lax.cond(any_tie, tie_break, lambda: all_keys)
    last_tied = jnp.where(exact, all_keys, last_tied)

    def write(c, _):
        s = chunk(c)
        idx = key_index(c)
        keep = ((s > thr) | ((s == thr) & (idx <= last_tied))) & (idx <= qpos)
        sc_ref[pl.ds(pl.multiple_of(c * kc, kc), kc), :] = jnp.where(keep, 0.0, MASK_BIAS)
        return 0

    lax.fori_loop(0, nk, write, 0)


def _prompt_attn_kernel(qt_ref, qit_ref, wit_ref, kg_ref, vt_ref, kib_ref, o_ref,
                        sc_ref, acc_ref, m_ref, s_ref):
    i = pl.program_id(1)
    tq = qt_ref.shape[1]
    qpos = i * tq + lax.broadcasted_iota(jnp.int32, (1, tq), 1)
    n_keys = (i + 1) * tq

    n_score = (n_keys + SCORE_CHUNK - 1) // SCORE_CHUNK

    def score_chunk(c, _):
        start = pl.multiple_of(c * SCORE_CHUNK, SCORE_CHUNK)
        kc = kib_ref[pl.ds(start, SCORE_CHUNK), :]
        acc = jnp.zeros((SCORE_CHUNK, tq), F32)
        for h in range(IDX_HEADS):
            d = _dot(kc, qit_ref[h * IDX_DIM:(h + 1) * IDX_DIM, :])
            acc = acc + jnp.maximum(d, 0.0) * wit_ref[h:h + 1, :]
        idx = start + lax.broadcasted_iota(jnp.int32, (SCORE_CHUNK, tq), 0)
        sc_ref[pl.ds(start, SCORE_CHUNK), :] = jnp.where(idx <= qpos, acc, -jnp.inf)
        return 0

    lax.fori_loop(0, n_score, score_chunk, 0)

    _select_to_bias(sc_ref, qpos, n_score, SCORE_CHUNK)

    m_ref[...] = jnp.full(m_ref.shape, MASK_BIAS, F32)
    acc_ref[...] = jnp.zeros(acc_ref.shape, F32)

    def attn_chunk(c, _):
        start = pl.multiple_of(c * ATTN_CHUNK, ATTN_CHUNK)

        def qk(h):
            return _dot(kg_ref[h // HEADS_PER_KV, pl.ds(start, ATTN_CHUNK), :],
                        qt_ref[h * HEAD_DIM:(h + 1) * HEAD_DIM, :])

        for h in range(QK_AHEAD):
            s_ref[h] = qk(h)
        for h in range(N_HEADS):
            if h + QK_AHEAD < N_HEADS:
                s_ref[(h + QK_AHEAD) % (QK_AHEAD + 1)] = qk(h + QK_AHEAD)
            g = h // HEADS_PER_KV
            s = s_ref[h % (QK_AHEAD + 1)] + sc_ref[pl.ds(start, ATTN_CHUNK), :]
            m_old = m_ref[h]
            m_new = jnp.maximum(m_old, s.max(axis=0, keepdims=True))
            alpha = jnp.exp(m_old - m_new)
            p = jnp.exp(s - m_new).astype(BF16)
            vt = vt_ref[c, g * V_ROWS:(g + 1) * V_ROWS, :]
            acc_ref[h] = alpha * acc_ref[h] + _dot(vt, p)
            m_ref[h] = m_new
        return 0

    lax.fori_loop(0, (i + 1) * (tq // ATTN_CHUNK), attn_chunk, 0)

    for pair in range(N_HEADS // 2):
        two = []
        for h in (2 * pair, 2 * pair + 1):
            a = acc_ref[h]
            two.append(a[0:HEAD_DIM, :] / a[HEAD_DIM:HEAD_DIM + 1, :])
        col = 2 * pair * HEAD_DIM
        o_ref[:, col:col + 2 * HEAD_DIM] = jnp.concatenate(two, axis=0).T.astype(BF16)


def _prompt_attention(qt, qit, wit, kg, vt, kib):
    nb, _, t = qt.shape
    tq = Q_TILE
    col = lambda height: pl.BlockSpec((None, height, tq), lambda b, i: (b, 0, i))
    return pl.pallas_call(
        _prompt_attn_kernel,
        grid=(nb, t // tq),
        in_specs=[
            col(Q_COLS), col(IQ_COLS), col(IDX_HEADS),
            pl.BlockSpec((None, N_KV_HEADS, t, HEAD_DIM), lambda b, i: (b, 0, 0, 0)),
            pl.BlockSpec((None, t // ATTN_CHUNK, N_KV_HEADS * V_ROWS, ATTN_CHUNK), lambda b, i: (b, 0, 0, 0)),
            pl.BlockSpec((None, t, IDX_DIM), lambda b, i: (b, 0, 0)),
        ],
        out_specs=pl.BlockSpec((None, tq, Q_COLS), lambda b, i: (b, i, 0)),
        out_shape=jax.ShapeDtypeStruct((nb, t, Q_COLS), BF16),
        scratch_shapes=[
            pltpu.VMEM((t, tq), F32),
            pltpu.VMEM((N_HEADS, V_ROWS, tq), F32),
            pltpu.VMEM((N_HEADS, 1, tq), F32),
            pltpu.VMEM((QK_AHEAD + 1, ATTN_CHUNK, tq), F32),
        ],
        compiler_params=_params("arbitrary", "arbitrary"),
        name="prompt_attention",
    )(qt, qit, wit, kg, vt, kib)


def _sample_score_kernel(pt_ref, qi_ref, w_ref, *rest):
    del pt_ref
    n_pages = len(rest) - 3
    pages, new_ref, o_ref, kall_ref = rest[:n_pages], rest[n_pages], rest[n_pages + 1], rest[n_pages + 2]
    for j in range(n_pages):
        kall_ref[:, j * PAGE_SIZE:(j + 1) * PAGE_SIZE] = pages[j][...].astype(BF16)
    kall_ref[:, n_pages * PAGE_SIZE:(n_pages + 1) * PAGE_SIZE] = new_ref[...].astype(BF16)
    d = _dot(qi_ref[...], kall_ref[...])
    r = jnp.maximum(d, 0.0) * w_ref[...]
    o_ref[...] = r.reshape(IDX_HEADS, SUBLANES, r.shape[1]).sum(axis=0)


def _sample_scores(page_table, qi8, w8, cache_ki_t, ki_new_t):
    nb, n_pages = page_table.shape
    n_keys = (n_pages + 1) * PAGE_SIZE
    rows = IDX_HEADS * SUBLANES
    grid_spec = pltpu.PrefetchScalarGridSpec(
        num_scalar_prefetch=1,
        grid=(nb,),
        in_specs=[pl.BlockSpec((None, rows, IDX_DIM), lambda b, pt: (b, 0, 0)),
                  pl.BlockSpec((None, rows, 1), lambda b, pt: (b, 0, 0))]
        + [pl.BlockSpec((None, IDX_DIM, PAGE_SIZE), lambda b, pt, j=j: (pt[b, j], 0, 0))
           for j in range(n_pages)]
        + [pl.BlockSpec((None, IDX_DIM, PAGE_SIZE), lambda b, pt: (b, 0, 0))],
        out_specs=pl.BlockSpec((None, SUBLANES, n_keys), lambda b, pt: (b, 0, 0)),
        scratch_shapes=[pltpu.VMEM((IDX_DIM, n_keys), BF16)],
    )
    return pl.pallas_call(
        _sample_score_kernel,
        grid_spec=grid_spec,
        out_shape=jax.ShapeDtypeStruct((nb, SUBLANES, n_keys), F32),
        compiler_params=_params("arbitrary"),
        name="sample_indexer_scores",
    )(page_table, qi8, w8, *([cache_ki_t] * n_pages), ki_new_t)


def _sample_select_kernel(s_ref, o_ref, sc_ref, *, past, period):
    tq = s_ref.shape[1]
    n_keys = s_ref.shape[0]
    lane = lax.broadcasted_iota(jnp.int32, (1, tq), 1)
    qpos = past + (lane & (period - 1))
    nk = n_keys // PAGE_SIZE
    for c in range(nk):
        idx = c * PAGE_SIZE + lax.broadcasted_iota(jnp.int32, (PAGE_SIZE, tq), 0)
        sc_ref[c * PAGE_SIZE:(c + 1) * PAGE_SIZE, :] = jnp.where(
            idx <= qpos, s_ref[c * PAGE_SIZE:(c + 1) * PAGE_SIZE, :], -jnp.inf)
    _select_to_bias(sc_ref, qpos, nk, PAGE_SIZE)
    o_ref[...] = sc_ref[...]


def _sample_select(scores_t, past, period):
    n_keys, nq = scores_t.shape
    tq = Q_TILE
    blk = pl.BlockSpec((n_keys, tq), lambda i: (0, i))
    return pl.pallas_call(
        functools.partial(_sample_select_kernel, past=past, period=period),
        grid=(nq // tq,),
        in_specs=[blk],
        out_specs=blk,
        out_shape=jax.ShapeDtypeStruct((n_keys, nq), F32),
        scratch_shapes=[pltpu.VMEM((n_keys, tq), F32)],
        compiler_params=_params("arbitrary"),
        name="sample_select",
    )(scores_t)


def _sample_attn_kernel(pt_ref, q_ref, bias_ref, *rest):
    del pt_ref
    n_pages = (len(rest) - 5) // 2
    kpages, vpages = rest[:n_pages], rest[n_pages:2 * n_pages]
    knew_ref, vnew_ref, o_ref, kall_ref, vall_ref = rest[2 * n_pages:]
    for j in range(n_pages):
        kall_ref[:, :, j * PAGE_SIZE:(j + 1) * PAGE_SIZE] = kpages[j][...].astype(BF16)
        vall_ref[:, :, j * PAGE_SIZE:(j + 1) * PAGE_SIZE] = vpages[j][...].astype(BF16)
    kall_ref[:, :, n_pages * PAGE_SIZE:(n_pages + 1) * PAGE_SIZE] = knew_ref[...].astype(BF16)
    vall_ref[:, :, n_pages * PAGE_SIZE:(n_pages + 1) * PAGE_SIZE] = vnew_ref[...].astype(BF16)
    bias = bias_ref[...]
    n_keys = bias.shape[1]
    bias = jnp.broadcast_to(bias[None], (HEADS_PER_KV, SUBLANES, n_keys)).reshape(
        HEADS_PER_KV * SUBLANES, n_keys)
    for g in range(N_KV_HEADS):
        s = _dot(q_ref[g], kall_ref[g]) + bias
        m = s.max(axis=1, keepdims=True)
        p = jnp.exp(s - m)
        l = p.sum(axis=1, keepdims=True)
        o_ref[g] = _dot_nt(p.astype(BF16), vall_ref[g]) / l


def _sample_attention(page_table, q8, bias8, cache_k_t, cache_v_t, k_new_t, v_new_t):
    nb, n_pages = page_table.shape
    n_keys = (n_pages + 1) * PAGE_SIZE
    rows = HEADS_PER_KV * SUBLANES
    page = [pl.BlockSpec((None, N_KV_HEADS, HEAD_DIM, PAGE_SIZE), lambda b, pt, j=j: (pt[b, j], 0, 0, 0))
            for j in range(n_pages)]
    new = pl.BlockSpec((None, N_KV_HEADS, HEAD_DIM, PAGE_SIZE), lambda b, pt: (b, 0, 0, 0))
    qo = pl.BlockSpec((None, N_KV_HEADS, rows, HEAD_DIM), lambda b, pt: (b, 0, 0, 0))
    grid_spec = pltpu.PrefetchScalarGridSpec(
        num_scalar_prefetch=1,
        grid=(nb,),
        in_specs=[qo, pl.BlockSpec((None, SUBLANES, n_keys), lambda b, pt: (b, 0, 0))]
        + page + page + [new, new],
        out_specs=qo,
        scratch_shapes=[pltpu.VMEM((N_KV_HEADS, HEAD_DIM, n_keys), BF16),
                        pltpu.VMEM((N_KV_HEADS, HEAD_DIM, n_keys), BF16)],
    )
    return pl.pallas_call(
        _sample_attn_kernel,
        grid_spec=grid_spec,
        out_shape=jax.ShapeDtypeStruct((nb, N_KV_HEADS, rows, HEAD_DIM), F32),
        compiler_params=_params("arbitrary"),
        name="sample_attention",
    )(page_table, q8, bias8, *([cache_k_t] * n_pages), *([cache_v_t] * n_pages), k_new_t, v_new_t)


def _ffn(h, wgu_ref, wd_ref):
    acc = jnp.zeros((h.shape[0], D_MODEL), F32)
    for c in range(D_FF // FFN_CHUNK):
        gate = _dot(h, wgu_ref[:, c * FFN_CHUNK:(c + 1) * FFN_CHUNK])
        up = _dot(h, wgu_ref[:, D_FF + c * FFN_CHUNK:D_FF + (c + 1) * FFN_CHUNK])
        a = _silu(gate) * up
        acc = acc + _dot(a.astype(BF16), wd_ref[c * FFN_CHUNK:(c + 1) * FFN_CHUNK, :])
    return acc


def _row_specs(nb, rows, per_row_mod):
    tm = min(ROW_TILE, rows)
    row = pl.BlockSpec((None, tm, D_MODEL), lambda b, j: (b, j, 0))
    if per_row_mod:
        mod = row
    else:
        mod = pl.BlockSpec((None, 1, D_MODEL), lambda b, j: (b, 0, 0))
    return tm, row, mod


def _attn_out_ffn_kernel(x_ref, o_ref, g1_ref, sc2_ref, sh2_ref, g2_ref, n2_ref,
                         wo_ref, wgu_ref, wd_ref, y_ref):
    y1 = x_ref[...] + g1_ref[...] * _dot(o_ref[...], wo_ref[...])
    h2 = _norm_mod(y1, n2_ref[...], sc2_ref[...], sh2_ref[...]).astype(BF16)
    y_ref[...] = y1 + g2_ref[...] * _ffn(h2, wgu_ref, wd_ref)


def _attn_out_ffn(x, o, g1, sc2, sh2, g2, n2, wo, wgu, wd):
    nb, rows, _ = x.shape
    tm, row, mod = _row_specs(nb, rows, g1.shape[1] == rows)
    return pl.pallas_call(
        _attn_out_ffn_kernel,
        grid=(nb, rows // tm),
        in_specs=[row, row, mod, mod, mod, mod, _const_spec((1, D_MODEL)),
                  _const_spec(wo.shape), _const_spec(wgu.shape), _const_spec(wd.shape)],
        out_specs=row,
        out_shape=jax.ShapeDtypeStruct(x.shape, F32),
        compiler_params=_params("arbitrary", "arbitrary"),
        name="attn_out_ffn",
    )(x, o, g1, sc2, sh2, g2, n2, wo, wgu, wd)


def _conv_layer_kernel(x_ref, sc1_ref, sh1_ref, g1_ref, sc2_ref, sh2_ref, g2_ref, p1_ref, p2_ref,
                       n1_ref, n2_ref, nf_ref, win_ref, ck_ref, wout_ref, wgu_ref, wd_ref,
                       y_ref, tail_ref, ubuf_ref, z_ref, *, seg):
    j = pl.program_id(1)
    tm = x_ref.shape[0]
    x = x_ref[...]
    h = _norm_mod(x, n1_ref[...], sc1_ref[...], sh1_ref[...]).astype(BF16)

    @pl.when(j == 0)
    def _():
        ubuf_ref[0:SUBLANES, :] = jnp.zeros((SUBLANES, D_MODEL), F32)

    @pl.when(j > 0)
    def _():
        ubuf_ref[0:SUBLANES, :] = ubuf_ref[tm:tm + SUBLANES, :]

    t = (j * tm + lax.broadcasted_iota(jnp.int32, (tm, 1), 0)) & (seg - 1)
    ck = ck_ref[...]
    cw = FFN_CHUNK
    for c in range(D_MODEL // cw):
        cols = slice(c * cw, (c + 1) * cw)
        bg = _dot(h, win_ref[:, c * cw:(c + 1) * cw])
        cg = _dot(h, win_ref[:, D_MODEL + c * cw:D_MODEL + (c + 1) * cw])
        hv = _dot(h, win_ref[:, 2 * D_MODEL + c * cw:2 * D_MODEL + (c + 1) * cw])
        u = cg * hv
        ubuf_ref[SUBLANES:SUBLANES + tm, cols] = u
        um1 = jnp.where(t >= 1, ubuf_ref[SUBLANES - 1:SUBLANES - 1 + tm, cols], p1_ref[:, cols])
        um2 = jnp.where(t >= 2, ubuf_ref[SUBLANES - 2:SUBLANES - 2 + tm, cols], p2_ref[:, cols])
        conv = ck[0:1, cols] * um2 + ck[1:2, cols] * um1 + ck[2:3, cols] * u
        z_ref[:, cols] = (bg * conv).astype(BF16)
    r = tail_ref.shape[0]
    tail_ref[...] = ubuf_ref[SUBLANES + tm - r:SUBLANES + tm, :]

    y1 = x + g1_ref[...] * _dot(z_ref[...], wout_ref[...])
    h2 = _norm_mod(y1, n2_ref[...], sc2_ref[...], sh2_ref[...]).astype(BF16)
    y2 = y1 + g2_ref[...] * _ffn(h2, wgu_ref, wd_ref)
    ms = jnp.mean(y2 * y2, axis=-1, keepdims=True)
    y_ref[...] = (y2 * lax.rsqrt(ms + EPS)) * nf_ref[...]


def _conv_layer(x, mods, p1, p2, n1, n2, nf, win, ck, wout, wgu, wd, *, seg, full_tail):
    nb, rows, _ = x.shape
    per_row = mods[0].shape[1] == rows
    tm, row, mod = _row_specs(nb, rows, per_row)
    if per_row:
        prefix = row
    else:
        prefix = pl.BlockSpec((None, 1, D_MODEL), lambda b, j: (0, 0, 0))
    if full_tail:
        tail_spec, tail_rows = row, rows
    else:
        tail_spec, tail_rows = pl.BlockSpec((None, SUBLANES, D_MODEL), lambda b, j: (b, 0, 0)), SUBLANES
    vec = _const_spec((1, D_MODEL))
    return pl.pallas_call(
        functools.partial(_conv_layer_kernel, seg=seg),
        grid=(nb, rows // tm),
        in_specs=[row] + [mod] * 6 + [prefix, prefix, vec, vec, vec,
                                      _const_spec(win.shape), _const_spec(ck.shape), _const_spec(wout.shape),
                                      _const_spec(wgu.shape), _const_spec(wd.shape)],
        out_specs=(row, tail_spec),
        out_shape=(jax.ShapeDtypeStruct(x.shape, F32),
                   jax.ShapeDtypeStruct((nb, tail_rows, D_MODEL), F32)),
        scratch_shapes=[pltpu.VMEM((tm + SUBLANES, D_MODEL), F32), pltpu.VMEM((tm, D_MODEL), BF16)],
        compiler_params=_params("arbitrary", "arbitrary"),
        name="conv_layer",
    )(x, *mods, p1, p2, n1, n2, nf, win, ck, wout, wgu, wd)


def _rope_tables(pos):
    half = ROT_DIM // 2
    inv = ROPE_THETA ** (-jnp.arange(0, ROT_DIM, 2, dtype=F32) / ROT_DIM)
    ang = pos[:, None] * inv[None, :]
    cos, sin = jnp.cos(ang), jnp.sin(ang)
    n = pos.shape[0]
    rest = HEAD_DIM - ROT_DIM
    c = jnp.concatenate([cos, cos, jnp.ones((n, rest), F32)], axis=1)
    s1 = jnp.concatenate([-sin, jnp.zeros((n, half + rest), F32)], axis=1)
    s2 = jnp.concatenate([jnp.zeros((n, half), F32), sin, jnp.zeros((n, rest), F32)], axis=1)
    return tuple(jnp.tile(a, (1, LANES // HEAD_DIM)) for a in (c, s1, s2))


def kernel(x_prompt, x_sample, c_prompt, c_sample, cache_k, cache_v, cache_kidx, state_conv, page_table,
           ada_w, ada_b, norm1_g, norm2_g, final_g, attn_w_in, attn_w_out, conv_w_in, conv_k, conv_w_out,
           ffn_w_gu, ffn_w_down):
    nb, t, _ = x_prompt.shape
    ns, tn, _ = x_sample.shape
    n_pages = page_table.shape[1]
    past = n_pages * PAGE_SIZE
    n_phys = cache_k.shape[1]
    rows_s = ns * tn

    pad = (-(nb + ns)) % SUBLANES
    c_all = jnp.concatenate([c_prompt, c_sample, jnp.zeros((pad, D_MODEL), F32)], axis=0)
    mod = _modulation(c_all, ada_w, ada_b)

    def mods(layer):
        m = mod[layer].reshape(-1, 6, D_MODEL)
        prompt = [m[:nb, i][:, None, :] for i in range(6)]
        sample = [jnp.repeat(m[nb:nb + ns, i], tn, axis=0)[None] for i in range(6)]
        return prompt, sample

    vec = lambda a: a.reshape(1, D_MODEL)
    w_in = jnp.pad(attn_w_in[0], ((0, 0), (0, ATTN_IN_PAD - ATTN_IN))).astype(BF16)
    w_o = attn_w_out[0].astype(BF16)
    wgu = [ffn_w_gu[i].astype(BF16) for i in range(2)]
    wd = [ffn_w_down[i].astype(BF16) for i in range(2)]

    (sh1p, sc1p, g1p, sh2p, sc2p, g2p), (sh1s, sc1s, g1s, sh2s, sc2s, g2s) = mods(0)
    rope_p = _rope_tables(jnp.arange(t, dtype=F32))
    k_p, v_p, ki_p, qt, qit, kg, vt, kib, wit = _proj_prompt(
        x_prompt, sc1p, sh1p, vec(norm1_g[0]), w_in, rope_p)
    o_p = _prompt_attention(qt, qit, wit, kg, vt, kib)
    y_p = _attn_out_ffn(x_prompt, o_p, g1p, sc2p, sh2p, g2p, vec(norm2_g[0]), w_o, wgu[0], wd[0])

    pos_s = jnp.tile(past + jnp.arange(tn, dtype=F32), ns)
    rope_s = _rope_tables(pos_s)
    xs = x_sample.reshape(rows_s, D_MODEL)
    proj = _proj_sample(xs, sc1s[0], sh1s[0], vec(norm1_g[0]), w_in, rope_s)
    o = 0
    q_s = proj[:, o:o + Q_COLS].reshape(ns, tn, N_KV_HEADS, HEADS_PER_KV, HEAD_DIM); o += Q_COLS
    k_s = proj[:, o:o + KV_COLS].reshape(ns, tn, KV_COLS); o += KV_COLS
    v_s = proj[:, o:o + KV_COLS].reshape(ns, tn, KV_COLS); o += KV_COLS
    qi_s = proj[:, o:o + IQ_COLS].reshape(ns, tn, IDX_HEADS, IDX_DIM); o += IQ_COLS
    ki_s = proj[:, o:o + IDX_DIM].reshape(ns, tn, IDX_DIM); o += IDX_DIM
    wi_s = proj[:, o:o + IDX_HEADS].reshape(ns, tn, IDX_HEADS)

    qpad = ((0, 0), (0, SUBLANES - tn))
    qi8 = jnp.pad(qi_s, qpad + ((0, 0), (0, 0))).transpose(0, 2, 1, 3)
    qi8 = qi8.reshape(ns, IDX_HEADS * SUBLANES, IDX_DIM).astype(BF16)
    w8 = jnp.pad(wi_s, qpad + ((0, 0),)).transpose(0, 2, 1).reshape(ns, IDX_HEADS * SUBLANES, 1)
    slot_pad = (0, PAGE_SIZE - tn)
    ki_new_t = jnp.pad(ki_s.transpose(0, 2, 1), ((0, 0), (0, 0), slot_pad))
    scores = _sample_scores(page_table, qi8, w8, cache_kidx[0].transpose(0, 2, 1), ki_new_t)
    n_keys = scores.shape[2]
    scores_t = scores[:, :tn, :].transpose(2, 0, 1).reshape(n_keys, rows_s)
    bias_t = _sample_select(scores_t, past, tn)
    bias8 = jnp.pad(bias_t.reshape(n_keys, ns, tn).transpose(1, 2, 0), qpad + ((0, 0),))

    q8 = jnp.pad(q_s * HEAD_DIM ** -0.5, qpad + ((0, 0), (0, 0), (0, 0))).transpose(0, 2, 3, 1, 4)
    q8 = q8.reshape(ns, N_KV_HEADS, HEADS_PER_KV * SUBLANES, HEAD_DIM).astype(BF16)
    new_t = lambda a: jnp.pad(a.reshape(ns, tn, N_KV_HEADS, HEAD_DIM).transpose(0, 2, 3, 1),
                              ((0, 0), (0, 0), (0, 0), slot_pad))
    o_g = _sample_attention(page_table, q8, bias8,
                            cache_k[0].transpose(0, 2, 3, 1), cache_v[0].transpose(0, 2, 3, 1),
                            new_t(k_s), new_t(v_s))
    o_g = o_g.reshape(ns, N_KV_HEADS, HEADS_PER_KV, SUBLANES, HEAD_DIM)[:, :, :, :tn]
    o_s = o_g.transpose(0, 3, 1, 2, 4).reshape(1, rows_s, Q_COLS).astype(BF16)
    y_s = _attn_out_ffn(xs[None], o_s, g1s, sc2s, sh2s, g2s, vec(norm2_g[0]), w_o, wgu[0], wd[0])

    (sh1p, sc1p, g1p, sh2p, sc2p, g2p), (sh1s, sc1s, g1s, sh2s, sc2s, g2s) = mods(1)
    win = conv_w_in[0].astype(BF16)
    wout = conv_w_out[0].astype(BF16)
    zero_prefix = jnp.zeros((1, 1, D_MODEL), F32)
    out_p, tail_p = _conv_layer(
        y_p, (sc1p, sh1p, g1p, sc2p, sh2p, g2p), zero_prefix, zero_prefix,
        vec(norm1_g[1]), vec(norm2_g[1]), vec(final_g), win, conv_k[0], wout, wgu[1], wd[1],
        seg=t, full_tail=False)
    st = state_conv[0]
    zeros_row = jnp.zeros((ns, 1, D_MODEL), F32)
    p1 = jnp.concatenate([st[:, 1:2], zeros_row, zeros_row, zeros_row], axis=1).reshape(1, rows_s, D_MODEL)
    p2 = jnp.concatenate([st[:, 0:1], st[:, 1:2], zeros_row, zeros_row], axis=1).reshape(1, rows_s, D_MODEL)
    out_s, u_s = _conv_layer(
        y_s, (sc1s, sh1s, g1s, sc2s, sh2s, g2s), p1, p2,
        vec(norm1_g[1]), vec(norm2_g[1]), vec(final_g), win, conv_k[0], wout, wgu[1], wd[1],
        seg=tn, full_tail=True)

    keep = CONV_WIDTH - 1
    return (
        out_p,
        out_s.reshape(ns, tn, D_MODEL),
        k_p.reshape(1, nb, t, N_KV_HEADS, HEAD_DIM),
        v_p.reshape(1, nb, t, N_KV_HEADS, HEAD_DIM),
        ki_p[None],
        tail_p[None, :, SUBLANES - keep:, :],
        k_s.reshape(1, ns, tn, N_KV_HEADS, HEAD_DIM),
        v_s.reshape(1, ns, tn, N_KV_HEADS, HEAD_DIM),
        ki_s[None],
        u_s.reshape(ns, tn, D_MODEL)[None, :, tn - keep:, :],
    )
```

```python
import functools

import jax
import jax.numpy as jnp
from jax import lax
from jax.experimental import pallas as pl
from jax.experimental.pallas import tpu as pltpu

F32 = jnp.float32
BF16 = jnp.bfloat16

D_MODEL = 1024
N_HEADS = 16
HEAD_DIM = 64
N_KV_HEADS = 4
HEADS_PER_KV = N_HEADS // N_KV_HEADS
ROT_DIM = 16
ROPE_THETA = 500000.0
IDX_HEADS = 8
IDX_DIM = 64
TOPK = 256
PAGE_SIZE = 128
CONV_WIDTH = 3
D_FF = 2816
EPS = 1e-6
Q_COLS = N_HEADS * HEAD_DIM
KV_COLS = N_KV_HEADS * HEAD_DIM
IQ_COLS = IDX_HEADS * IDX_DIM
ATTN_IN = Q_COLS + 2 * KV_COLS + IQ_COLS + IDX_DIM + IDX_HEADS
ATTN_IN_PAD = 2176
LANES = 128
SUBLANES = 8
MASK_BIAS = -1e30
F32_MAX = 3.4028234663852886e38
LOG2_E = 1.4426950408889634
VMEM_LIMIT = 60 * 1024 * 1024

ROW_TILE = 512
Q_TILE = 256
SCORE_CHUNK = 512
ATTN_CHUNK = 256
V_ROWS = HEAD_DIM + 16
FFN_CHUNK = 256


def _dot(a, b):
    return jnp.dot(a, b, preferred_element_type=F32)


def _dot_nt(a, b):
    return lax.dot_general(a, b, (((1,), (1,)), ((), ())), preferred_element_type=F32)


def _silu(x):
    return x / (1.0 + jnp.exp(-x))


def _norm_mod(x, g, sc, sh):
    ms = jnp.mean(x * x, axis=-1, keepdims=True)
    y = x * lax.rsqrt(ms + EPS)
    return (y * g) * (1.0 + sc) + sh


def _params(*sem, flags=None):
    return pltpu.CompilerParams(dimension_semantics=sem, vmem_limit_bytes=VMEM_LIMIT, flags=flags)


def _const_spec(shape):
    nd = len(shape)
    return pl.BlockSpec(shape, lambda *_: (0,) * nd, pipeline_mode=pl.Buffered(1))


def _mod_kernel(c_ref, w_ref, b_ref, o_ref):
    s = _silu(c_ref[...]).astype(BF16)
    o_ref[...] = _dot(s, w_ref[...].astype(BF16)) + b_ref[...]


def _modulation(c_all, ada_w, ada_b):
    depth, _, n = ada_w.shape
    rows = c_all.shape[0]
    tn = 512
    return pl.pallas_call(
        _mod_kernel,
        grid=(depth, n // tn),
        in_specs=[
            pl.BlockSpec((rows, D_MODEL), lambda l, j: (0, 0)),
            pl.BlockSpec((None, D_MODEL, tn), lambda l, j: (l, 0, j)),
            pl.BlockSpec((None, 1, tn), lambda l, j: (l, 0, j)),
        ],
        out_specs=pl.BlockSpec((None, rows, tn), lambda l, j: (l, 0, j)),
        out_shape=jax.ShapeDtypeStruct((depth, rows, n), F32),
        compiler_params=_params("arbitrary", "arbitrary"),
        name="adaln_modulation",
    )(c_all, ada_w, ada_b.reshape(depth, 1, n))


def _rope_block(x, c, s1, s2):
    return x * c + pltpu.roll(x, LANES - ROT_DIM // 2, 1) * s1 + pltpu.roll(x, ROT_DIM // 2, 1) * s2


def _rope_cols(p, c, s1, s2):
    nblk = p.shape[1] // LANES
    return [_rope_block(p[:, j * LANES:(j + 1) * LANES], c, s1, s2) for j in range(nblk)]


def _last_block_tables(c, s1, s2):
    lane = lax.broadcasted_iota(jnp.int32, c.shape, 1)
    is_key = lane < IDX_DIM
    return jnp.where(is_key, c, 1.0), jnp.where(is_key, s1, 0.0), jnp.where(is_key, s2, 0.0)


def _proj_prompt_kernel(x_ref, sc_ref, sh_ref, g_ref, w_ref, c_ref, s1_ref, s2_ref,
                        k_ref, v_ref, ki_ref, qt_ref, qit_ref, kg_ref, vt_ref, kib_ref, wit_ref):
    h = _norm_mod(x_ref[...], g_ref[...], sc_ref[...], sh_ref[...]).astype(BF16)
    c, s1, s2 = c_ref[...], s1_ref[...], s2_ref[...]

    pq = _dot(h, w_ref[:, 0:Q_COLS])
    for j, blk in enumerate(_rope_cols(pq, c, s1, s2)):
        qt_ref[j * LANES:(j + 1) * LANES, :] = (blk * (HEAD_DIM ** -0.5 * LOG2_E)).T.astype(BF16)

    pk = _dot(h, w_ref[:, Q_COLS:Q_COLS + KV_COLS])
    k = jnp.concatenate(_rope_cols(pk, c, s1, s2), axis=1)
    k_ref[...] = k
    for g in range(N_KV_HEADS):
        kg_ref[g] = k[:, g * HEAD_DIM:(g + 1) * HEAD_DIM].astype(BF16)

    pv = _dot(h, w_ref[:, Q_COLS + KV_COLS:Q_COLS + 2 * KV_COLS])
    v_ref[...] = pv
    ones = jnp.ones((V_ROWS - HEAD_DIM, ATTN_CHUNK), BF16)
    for cc in range(pv.shape[0] // ATTN_CHUNK):
        vt = pv[cc * ATTN_CHUNK:(cc + 1) * ATTN_CHUNK, :].T.astype(BF16)
        for g in range(N_KV_HEADS):
            vt_ref[cc, g * V_ROWS:g * V_ROWS + HEAD_DIM, :] = vt[g * HEAD_DIM:(g + 1) * HEAD_DIM, :]
            vt_ref[cc, g * V_ROWS + HEAD_DIM:(g + 1) * V_ROWS, :] = ones

    o = Q_COLS + 2 * KV_COLS
    pqi = _dot(h, w_ref[:, o:o + IQ_COLS])
    for j, blk in enumerate(_rope_cols(pqi, c, s1, s2)):
        qit_ref[j * LANES:(j + 1) * LANES, :] = blk.T.astype(BF16)

    pl_ = _dot(h, w_ref[:, o + IQ_COLS:ATTN_IN_PAD])
    last = _rope_block(pl_, *_last_block_tables(c, s1, s2))
    ki = last[:, 0:IDX_DIM]
    ki_ref[...] = ki
    kib_ref[...] = ki.astype(BF16)
    wit_ref[...] = last.T[IDX_DIM:IDX_DIM + IDX_HEADS, :]


def _proj_prompt(x, sc, sh, g, w, rope):
    nb, t, _ = x.shape
    tm = ROW_TILE
    row = lambda width: pl.BlockSpec((None, tm, width), lambda b, j: (b, j, 0))
    mod = pl.BlockSpec((None, 1, D_MODEL), lambda b, j: (b, 0, 0))
    tab = pl.BlockSpec((tm, LANES), lambda b, j: (j, 0))
    col = lambda height: pl.BlockSpec((None, height, tm), lambda b, j: (b, 0, j))
    out_shape = (
        jax.ShapeDtypeStruct((nb, t, KV_COLS), F32),
        jax.ShapeDtypeStruct((nb, t, KV_COLS), F32),
        jax.ShapeDtypeStruct((nb, t, IDX_DIM), F32),
        jax.ShapeDtypeStruct((nb, Q_COLS, t), BF16),
        jax.ShapeDtypeStruct((nb, IQ_COLS, t), BF16),
        jax.ShapeDtypeStruct((nb, N_KV_HEADS, t, HEAD_DIM), BF16),
        jax.ShapeDtypeStruct((nb, t // ATTN_CHUNK, N_KV_HEADS * V_ROWS, ATTN_CHUNK), BF16),
        jax.ShapeDtypeStruct((nb, t, IDX_DIM), BF16),
        jax.ShapeDtypeStruct((nb, IDX_HEADS, t), F32),
    )
    out_specs = (
        row(KV_COLS), row(KV_COLS), row(IDX_DIM), col(Q_COLS), col(IQ_COLS),
        pl.BlockSpec((None, N_KV_HEADS, tm, HEAD_DIM), lambda b, j: (b, 0, j, 0)),
        pl.BlockSpec((None, tm // ATTN_CHUNK, N_KV_HEADS * V_ROWS, ATTN_CHUNK), lambda b, j: (b, j, 0, 0)),
        row(IDX_DIM), col(IDX_HEADS),
    )
    return pl.pallas_call(
        _proj_prompt_kernel,
        grid=(nb, t // tm),
        in_specs=[row(D_MODEL), mod, mod, _const_spec((1, D_MODEL)),
                  _const_spec((D_MODEL, ATTN_IN_PAD)), tab, tab, tab],
        out_specs=out_specs,
        out_shape=out_shape,
        compiler_params=_params("arbitrary", "arbitrary"),
        name="attn_proj_prompt",
    )(x, sc, sh, g, w, *rope)


def _proj_sample_kernel(x_ref, sc_ref, sh_ref, g_ref, w_ref, c_ref, s1_ref, s2_ref, p_ref):
    h = _norm_mod(x_ref[...], g_ref[...], sc_ref[...], sh_ref[...]).astype(BF16)
    c, s1, s2 = c_ref[...], s1_ref[...], s2_ref[...]
    p = _dot(h, w_ref[...])
    v_lo = (Q_COLS + KV_COLS) // LANES
    v_hi = (Q_COLS + 2 * KV_COLS) // LANES
    nblk = ATTN_IN_PAD // LANES
    for j in range(nblk):
        blk = p[:, j * LANES:(j + 1) * LANES]
        if j == nblk - 1:
            blk = _rope_block(blk, *_last_block_tables(c, s1, s2))
        elif not (v_lo <= j < v_hi):
            blk = _rope_block(blk, c, s1, s2)
        p_ref[:, j * LANES:(j + 1) * LANES] = blk


def _proj_sample(x, sc, sh, g, w, rope):
    rows = x.shape[0]
    full = lambda width: pl.BlockSpec((rows, width), lambda i: (0, 0))
    return pl.pallas_call(
        _proj_sample_kernel,
        grid=(1,),
        in_specs=[full(D_MODEL), full(D_MODEL), full(D_MODEL),
                  pl.BlockSpec((1, D_MODEL), lambda i: (0, 0)),
                  pl.BlockSpec((D_MODEL, ATTN_IN_PAD), lambda i: (0, 0)),
                  full(LANES), full(LANES), full(LANES)],
        out_specs=full(ATTN_IN_PAD),
        out_shape=jax.ShapeDtypeStruct((rows, ATTN_IN_PAD), F32),
        compiler_params=_params("arbitrary"),
        name="attn_proj_sample",
    )(x, sc, sh, g, w, *rope)


FOLD_ROWS = 32


def _fold(x, op):
    rows, tq = x.shape
    x3 = x.reshape(rows // FOLD_ROWS, FOLD_ROWS, tq)
    return {"sum": x3.sum, "max": x3.max, "min": x3.min}[op](axis=0)


def _select_to_bias(sc_ref, qpos, nk, kc):
    tq = sc_ref.shape[1]
    topk = float(TOPK)

    def chunk(c):
        return sc_ref[pl.ds(pl.multiple_of(c * kc, kc), kc), :]

    def key_index(c):
        return c * kc + lax.broadcasted_iota(jnp.int32, (kc, tq), 0)

    def ones_where(m):
        return _fold(jnp.where(m, 1.0, 0.0), "sum")

    def count(pred):
        def body(c, acc):
            return acc + ones_where(pred(chunk(c), c))
        acc = lax.fori_loop(0, nk, body, jnp.zeros((FOLD_ROWS, tq), F32))
        return acc.sum(axis=0, keepdims=True)

    def stats(c, carry):
        mx, mn, n_above, n_ge0, n_gt0 = carry
        s = chunk(c)
        above = s > -jnp.inf
        mx = jnp.maximum(mx, _fold(s, "max"))
        mn = jnp.minimum(mn, _fold(jnp.where(above, s, jnp.inf), "min"))
        return (mx, mn, n_above + ones_where(above), n_ge0 + ones_where(s >= 0.0),
                n_gt0 + ones_where(s > 0.0))

    zeros = jnp.zeros((FOLD_ROWS, tq), F32)
    mx, mn, n_above, n_ge0, n_gt0 = lax.fori_loop(
        0, nk, stats, (jnp.full((FOLD_ROWS, tq), -jnp.inf, F32), jnp.full((FOLD_ROWS, tq), jnp.inf, F32),
                       zeros, zeros, zeros))
    mx = mx.max(axis=0, keepdims=True)
    mn = mn.min(axis=0, keepdims=True)
    n_above = n_above.sum(axis=0, keepdims=True)
    n_ge0 = n_ge0.sum(axis=0, keepdims=True)
    n_gt0 = n_gt0.sum(axis=0, keepdims=True)
    n_max = count(lambda s, c: s >= mx)

    few = n_above < topk
    flat = n_max >= topk
    zero = (n_gt0 < topk) & (n_ge0 >= topk)
    pos = n_gt0 >= topk
    lo = jnp.where(few, -jnp.inf, jnp.where(flat, mx, jnp.where(zero | pos, 0.0, mn)))
    hi = jnp.where(few, mn, jnp.where(flat, jnp.inf, jnp.where(pos, mx, 0.0)))
    n_lo = jnp.where(few, 2.0 * topk, jnp.where(flat, n_max, jnp.where(zero | pos, n_ge0, n_above)))
    n_hi = jnp.where(few, n_above, jnp.where(flat, 0.0, jnp.where(zero, n_gt0, jnp.where(pos, n_max, n_ge0))))
    done = jnp.where(few | flat | zero | (n_lo == topk), 1.0, 0.0)

    def not_finished(d):
        return (jnp.min(d) < 0.5).astype(jnp.int32)

    def bisect(carry):
        lo, hi, n_lo, n_hi, done, _ = carry
        mid = jnp.clip(0.5 * lo + 0.5 * hi, -F32_MAX, F32_MAX)
        stuck = (mid <= lo) | (mid >= hi)
        n_mid = count(lambda s, c: s >= mid)
        move = (done < 0.5) & jnp.logical_not(stuck)
        up = move & (n_mid >= topk)
        down = move & (n_mid < topk)
        lo = jnp.where(up, mid, lo)
        n_lo = jnp.where(up, n_mid, n_lo)
        hi = jnp.where(down, mid, hi)
        n_hi = jnp.where(down, n_mid, n_hi)
        done = jnp.where(stuck | (n_lo == topk), 1.0, done)
        return lo, hi, n_lo, n_hi, done, not_finished(done)

    lo, hi, n_lo, n_hi, done, _ = lax.while_loop(
        lambda carry: carry[5] > 0, bisect, (lo, hi, n_lo, n_hi, done, not_finished(done)))

    thr = lo
    exact = n_lo == topk
    need = topk - n_hi
    all_keys = jnp.full((1, tq), 2 ** 30, jnp.int32)

    def tie_break():
        def step(i, j):
            cand = j + lax.shift_right_logical(jnp.int32(2048), i)
            n_before = count(lambda s, c: (s == thr) & (key_index(c) < cand))
            return jnp.where(n_before < need, cand, j)
        return lax.fori_loop(0, 12, step, jnp.zeros((1, tq), jnp.int32))

    any_tie = jnp.min(jnp.where(exact, 1.0, 0.0)) < 0.5
    last_tied = lax.cond(any_tie, tie_break, lambda: all_keys)
    last_tied = jnp.where(exact, all_keys, last_tied)

    def write(c, _):
        s = chunk(c)
        idx = key_index(c)
        keep = ((s > thr) | ((s == thr) & (idx <= last_tied))) & (idx <= qpos)
        sc_ref[pl.ds(pl.multiple_of(c * kc, kc), kc), :] = jnp.where(keep, 0.0, MASK_BIAS)
        return 0

    lax.fori_loop(0, nk, write, 0)


def _prompt_attn_kernel(qt_ref, qit_ref, wit_ref, kg_ref, vt_ref, kib_ref, o_ref,
                        sc_ref, acc_ref, m_ref, mx_ref, s_ref):
    i = pl.program_id(1)
    tq = qt_ref.shape[1]
    qpos = i * tq + lax.broadcasted_iota(jnp.int32, (1, tq), 1)
    n_keys = (i + 1) * tq

    n_score = (n_keys + SCORE_CHUNK - 1) // SCORE_CHUNK

    def score_chunk(c, _):
        start = pl.multiple_of(c * SCORE_CHUNK, SCORE_CHUNK)
        kc = kib_ref[pl.ds(start, SCORE_CHUNK), :]
        acc = jnp.zeros((SCORE_CHUNK, tq), F32)
        for h in range(IDX_HEADS):
            d = _dot(kc, qit_ref[h * IDX_DIM:(h + 1) * IDX_DIM, :])
            acc = acc + jnp.maximum(d, 0.0) * wit_ref[h:h + 1, :]
        idx = start + lax.broadcasted_iota(jnp.int32, (SCORE_CHUNK, tq), 0)
        sc_ref[pl.ds(start, SCORE_CHUNK), :] = jnp.where(idx <= qpos, acc, -jnp.inf)
        return 0

    lax.fori_loop(0, n_score, score_chunk, 0)

    _select_to_bias(sc_ref, qpos, n_score, SCORE_CHUNK)

    m_ref[...] = jnp.full(m_ref.shape, MASK_BIAS, F32)
    acc_ref[...] = jnp.zeros(acc_ref.shape, F32)

    def attn_chunk(c, _):
        start = pl.multiple_of(c * ATTN_CHUNK, ATTN_CHUNK)
        for h in range(N_HEADS):
            s = _dot(kg_ref[h // HEADS_PER_KV, pl.ds(start, ATTN_CHUNK), :],
                     qt_ref[h * HEAD_DIM:(h + 1) * HEAD_DIM, :])
            s = s + sc_ref[pl.ds(start, ATTN_CHUNK), :]
            s_ref[h] = s
            mx_ref[h] = s.max(axis=0, keepdims=True)
        for h in range(N_HEADS):
            g = h // HEADS_PER_KV
            m_old = m_ref[h]
            m_new = jnp.maximum(m_old, mx_ref[h])
            alpha = jnp.exp2(m_old - m_new)
            p = jnp.exp2(s_ref[h] - m_new).astype(BF16)
            vt = vt_ref[c, g * V_ROWS:(g + 1) * V_ROWS, :]
            acc_ref[h] = alpha * acc_ref[h] + _dot(vt, p)
            m_ref[h] = m_new
        return 0

    lax.fori_loop(0, (i + 1) * (tq // ATTN_CHUNK), attn_chunk, 0)

    for pair in range(N_HEADS // 2):
        two = []
        for h in (2 * pair, 2 * pair + 1):
            a = acc_ref[h]
            two.append(a[0:HEAD_DIM, :] / a[HEAD_DIM:HEAD_DIM + 1, :])
        col = 2 * pair * HEAD_DIM
        o_ref[:, col:col + 2 * HEAD_DIM] = jnp.concatenate(two, axis=0).T.astype(BF16)


def _prompt_attention(qt, qit, wit, kg, vt, kib):
    nb, _, t = qt.shape
    tq = Q_TILE
    col = lambda height: pl.BlockSpec((None, height, tq), lambda b, i: (b, 0, i))
    return pl.pallas_call(
        _prompt_attn_kernel,
        grid=(nb, t // tq),
        in_specs=[
            col(Q_COLS), col(IQ_COLS), col(IDX_HEADS),
            pl.BlockSpec((None, N_KV_HEADS, t, HEAD_DIM), lambda b, i: (b, 0, 0, 0)),
            pl.BlockSpec((None, t // ATTN_CHUNK, N_KV_HEADS * V_ROWS, ATTN_CHUNK), lambda b, i: (b, 0, 0, 0)),
            pl.BlockSpec((None, t, IDX_DIM), lambda b, i: (b, 0, 0)),
        ],
        out_specs=pl.BlockSpec((None, tq, Q_COLS), lambda b, i: (b, i, 0)),
        out_shape=jax.ShapeDtypeStruct((nb, t, Q_COLS), BF16),
        scratch_shapes=[
            pltpu.VMEM((t, tq), F32),
            pltpu.VMEM((N_HEADS, V_ROWS, tq), F32),
            pltpu.VMEM((N_HEADS, 1, tq), F32),
            pltpu.VMEM((N_HEADS, 1, tq), F32),
            pltpu.VMEM((N_HEADS, ATTN_CHUNK, tq), F32),
        ],
        compiler_params=_params("arbitrary", "arbitrary"),
        name="prompt_attention",
    )(qt, qit, wit, kg, vt, kib)


def _sample_score_kernel(pt_ref, qi_ref, w_ref, *rest):
    del pt_ref
    n_pages = len(rest) - 3
    pages, new_ref, o_ref, kall_ref = rest[:n_pages], rest[n_pages], rest[n_pages + 1], rest[n_pages + 2]
    for j in range(n_pages):
        kall_ref[:, j * PAGE_SIZE:(j + 1) * PAGE_SIZE] = pages[j][...].astype(BF16)
    kall_ref[:, n_pages * PAGE_SIZE:(n_pages + 1) * PAGE_SIZE] = new_ref[...].astype(BF16)
    d = _dot(qi_ref[...], kall_ref[...])
    r = jnp.maximum(d, 0.0) * w_ref[...]
    o_ref[...] = r.reshape(IDX_HEADS, SUBLANES, r.shape[1]).sum(axis=0)


def _sample_scores(page_table, qi8, w8, cache_ki_t, ki_new_t):
    nb, n_pages = page_table.shape
    n_keys = (n_pages + 1) * PAGE_SIZE
    rows = IDX_HEADS * SUBLANES
    grid_spec = pltpu.PrefetchScalarGridSpec(
        num_scalar_prefetch=1,
        grid=(nb,),
        in_specs=[pl.BlockSpec((None, rows, IDX_DIM), lambda b, pt: (b, 0, 0)),
                  pl.BlockSpec((None, rows, 1), lambda b, pt: (b, 0, 0))]
        + [pl.BlockSpec((None, IDX_DIM, PAGE_SIZE), lambda b, pt, j=j: (pt[b, j], 0, 0))
           for j in range(n_pages)]
        + [pl.BlockSpec((None, IDX_DIM, PAGE_SIZE), lambda b, pt: (b, 0, 0))],
        out_specs=pl.BlockSpec((None, SUBLANES, n_keys), lambda b, pt: (b, 0, 0)),
        scratch_shapes=[pltpu.VMEM((IDX_DIM, n_keys), BF16)],
    )
    return pl.pallas_call(
        _sample_score_kernel,
        grid_spec=grid_spec,
        out_shape=jax.ShapeDtypeStruct((nb, SUBLANES, n_keys), F32),
        compiler_params=_params("arbitrary"),
        name="sample_indexer_scores",
    )(page_table, qi8, w8, *([cache_ki_t] * n_pages), ki_new_t)


def _sample_select_kernel(s_ref, o_ref, sc_ref, *, past, period):
    tq = s_ref.shape[1]
    n_keys = s_ref.shape[0]
    lane = lax.broadcasted_iota(jnp.int32, (1, tq), 1)
    qpos = past + (lane & (period - 1))
    nk = n_keys // PAGE_SIZE
    for c in range(nk):
        idx = c * PAGE_SIZE + lax.broadcasted_iota(jnp.int32, (PAGE_SIZE, tq), 0)
        sc_ref[c * PAGE_SIZE:(c + 1) * PAGE_SIZE, :] = jnp.where(
            idx <= qpos, s_ref[c * PAGE_SIZE:(c + 1) * PAGE_SIZE, :], -jnp.inf)
    _select_to_bias(sc_ref, qpos, nk, PAGE_SIZE)
    o_ref[...] = sc_ref[...]


def _sample_select(scores_t, past, period):
    n_keys, nq = scores_t.shape
    tq = Q_TILE
    blk = pl.BlockSpec((n_keys, tq), lambda i: (0, i))
    return pl.pallas_call(
        functools.partial(_sample_select_kernel, past=past, period=period),
        grid=(nq // tq,),
        in_specs=[blk],
        out_specs=blk,
        out_shape=jax.ShapeDtypeStruct((n_keys, nq), F32),
        scratch_shapes=[pltpu.VMEM((n_keys, tq), F32)],
        compiler_params=_params("arbitrary"),
        name="sample_select",
    )(scores_t)


def _sample_attn_kernel(pt_ref, q_ref, bias_ref, *rest):
    del pt_ref
    n_pages = (len(rest) - 5) // 2
    kpages, vpages = rest[:n_pages], rest[n_pages:2 * n_pages]
    knew_ref, vnew_ref, o_ref, kall_ref, vall_ref = rest[2 * n_pages:]
    for j in range(n_pages):
        kall_ref[:, :, j * PAGE_SIZE:(j + 1) * PAGE_SIZE] = kpages[j][...].astype(BF16)
        vall_ref[:, :, j * PAGE_SIZE:(j + 1) * PAGE_SIZE] = vpages[j][...].astype(BF16)
    kall_ref[:, :, n_pages * PAGE_SIZE:(n_pages + 1) * PAGE_SIZE] = knew_ref[...].astype(BF16)
    vall_ref[:, :, n_pages * PAGE_SIZE:(n_pages + 1) * PAGE_SIZE] = vnew_ref[...].astype(BF16)
    bias = bias_ref[...]
    n_keys = bias.shape[1]
    bias = jnp.broadcast_to(bias[None], (HEADS_PER_KV, SUBLANES, n_keys)).reshape(
        HEADS_PER_KV * SUBLANES, n_keys)
    for g in range(N_KV_HEADS):
        s = _dot(q_ref[g], kall_ref[g]) + bias
        m = s.max(axis=1, keepdims=True)
        p = jnp.exp(s - m)
        l = p.sum(axis=1, keepdims=True)
        o_ref[g] = _dot_nt(p.astype(BF16), vall_ref[g]) / l


def _sample_attention(page_table, q8, bias8, cache_k_t, cache_v_t, k_new_t, v_new_t):
    nb, n_pages = page_table.shape
    n_keys = (n_pages + 1) * PAGE_SIZE
    rows = HEADS_PER_KV * SUBLANES
    page = [pl.BlockSpec((None, N_KV_HEADS, HEAD_DIM, PAGE_SIZE), lambda b, pt, j=j: (pt[b, j], 0, 0, 0))
            for j in range(n_pages)]
    new = pl.BlockSpec((None, N_KV_HEADS, HEAD_DIM, PAGE_SIZE), lambda b, pt: (b, 0, 0, 0))
    qo = pl.BlockSpec((None, N_KV_HEADS, rows, HEAD_DIM), lambda b, pt: (b, 0, 0, 0))
    grid_spec = pltpu.PrefetchScalarGridSpec(
        num_scalar_prefetch=1,
        grid=(nb,),
        in_specs=[qo, pl.BlockSpec((None, SUBLANES, n_keys), lambda b, pt: (b, 0, 0))]
        + page + page + [new, new],
        out_specs=qo,
        scratch_shapes=[pltpu.VMEM((N_KV_HEADS, HEAD_DIM, n_keys), BF16),
                        pltpu.VMEM((N_KV_HEADS, HEAD_DIM, n_keys), BF16)],
    )
    return pl.pallas_call(
        _sample_attn_kernel,
        grid_spec=grid_spec,
        out_shape=jax.ShapeDtypeStruct((nb, N_KV_HEADS, rows, HEAD_DIM), F32),
        compiler_params=_params("arbitrary"),
        name="sample_attention",
    )(page_table, q8, bias8, *([cache_k_t] * n_pages), *([cache_v_t] * n_pages), k_new_t, v_new_t)


def _ffn(h, wgu_ref, wd_ref):
    acc = jnp.zeros((h.shape[0], D_MODEL), F32)
    for c in range(D_FF // FFN_CHUNK):
        gate = _dot(h, wgu_ref[:, c * FFN_CHUNK:(c + 1) * FFN_CHUNK])
        up = _dot(h, wgu_ref[:, D_FF + c * FFN_CHUNK:D_FF + (c + 1) * FFN_CHUNK])
        a = _silu(gate) * up
        acc = acc + _dot(a.astype(BF16), wd_ref[c * FFN_CHUNK:(c + 1) * FFN_CHUNK, :])
    return acc


def _row_specs(nb, rows, per_row_mod):
    tm = min(ROW_TILE, rows)
    row = pl.BlockSpec((None, tm, D_MODEL), lambda b, j: (b, j, 0))
    if per_row_mod:
        mod = row
    else:
        mod = pl.BlockSpec((None, 1, D_MODEL), lambda b, j: (b, 0, 0))
    return tm, row, mod


def _attn_out_ffn_kernel(x_ref, o_ref, g1_ref, sc2_ref, sh2_ref, g2_ref, n2_ref,
                         wo_ref, wgu_ref, wd_ref, y_ref):
    y1 = x_ref[...] + g1_ref[...] * _dot(o_ref[...], wo_ref[...])
    h2 = _norm_mod(y1, n2_ref[...], sc2_ref[...], sh2_ref[...]).astype(BF16)
    y_ref[...] = y1 + g2_ref[...] * _ffn(h2, wgu_ref, wd_ref)


def _attn_out_ffn(x, o, g1, sc2, sh2, g2, n2, wo, wgu, wd):
    nb, rows, _ = x.shape
    tm, row, mod = _row_specs(nb, rows, g1.shape[1] == rows)
    return pl.pallas_call(
        _attn_out_ffn_kernel,
        grid=(nb, rows // tm),
        in_specs=[row, row, mod, mod, mod, mod, _const_spec((1, D_MODEL)),
                  _const_spec(wo.shape), _const_spec(wgu.shape), _const_spec(wd.shape)],
        out_specs=row,
        out_shape=jax.ShapeDtypeStruct(x.shape, F32),
        compiler_params=_params("arbitrary", "arbitrary"),
        name="attn_out_ffn",
    )(x, o, g1, sc2, sh2, g2, n2, wo, wgu, wd)


def _conv_layer_kernel(x_ref, sc1_ref, sh1_ref, g1_ref, sc2_ref, sh2_ref, g2_ref, p1_ref, p2_ref,
                       n1_ref, n2_ref, nf_ref, win_ref, ck_ref, wout_ref, wgu_ref, wd_ref,
                       y_ref, tail_ref, ubuf_ref, z_ref, *, seg):
    j = pl.program_id(1)
    tm = x_ref.shape[0]
    x = x_ref[...]
    h = _norm_mod(x, n1_ref[...], sc1_ref[...], sh1_ref[...]).astype(BF16)

    @pl.when(j == 0)
    def _():
        ubuf_ref[0:SUBLANES, :] = jnp.zeros((SUBLANES, D_MODEL), F32)

    @pl.when(j > 0)
    def _():
        ubuf_ref[0:SUBLANES, :] = ubuf_ref[tm:tm + SUBLANES, :]

    t = (j * tm + lax.broadcasted_iota(jnp.int32, (tm, 1), 0)) & (seg - 1)
    ck = ck_ref[...]
    cw = FFN_CHUNK
    for c in range(D_MODEL // cw):
        cols = slice(c * cw, (c + 1) * cw)
        bg = _dot(h, win_ref[:, c * cw:(c + 1) * cw])
        cg = _dot(h, win_ref[:, D_MODEL + c * cw:D_MODEL + (c + 1) * cw])
        hv = _dot(h, win_ref[:, 2 * D_MODEL + c * cw:2 * D_MODEL + (c + 1) * cw])
        u = cg * hv
        ubuf_ref[SUBLANES:SUBLANES + tm, cols] = u
        um1 = jnp.where(t >= 1, ubuf_ref[SUBLANES - 1:SUBLANES - 1 + tm, cols], p1_ref[:, cols])
        um2 = jnp.where(t >= 2, ubuf_ref[SUBLANES - 2:SUBLANES - 2 + tm, cols], p2_ref[:, cols])
        conv = ck[0:1, cols] * um2 + ck[1:2, cols] * um1 + ck[2:3, cols] * u
        z_ref[:, cols] = (bg * conv).astype(BF16)
    r = tail_ref.shape[0]
    tail_ref[...] = ubuf_ref[SUBLANES + tm - r:SUBLANES + tm, :]

    y1 = x + g1_ref[...] * _dot(z_ref[...], wout_ref[...])
    h2 = _norm_mod(y1, n2_ref[...], sc2_ref[...], sh2_ref[...]).astype(BF16)
    y2 = y1 + g2_ref[...] * _ffn(h2, wgu_ref, wd_ref)
    ms = jnp.mean(y2 * y2, axis=-1, keepdims=True)
    y_ref[...] = (y2 * lax.rsqrt(ms + EPS)) * nf_ref[...]


def _conv_layer(x, mods, p1, p2, n1, n2, nf, win, ck, wout, wgu, wd, *, seg, full_tail):
    nb, rows, _ = x.shape
    per_row = mods[0].shape[1] == rows
    tm, row, mod = _row_specs(nb, rows, per_row)
    if per_row:
        prefix = row
    else:
        prefix = pl.BlockSpec((None, 1, D_MODEL), lambda b, j: (0, 0, 0))
    if full_tail:
        tail_spec, tail_rows = row, rows
    else:
        tail_spec, tail_rows = pl.BlockSpec((None, SUBLANES, D_MODEL), lambda b, j: (b, 0, 0)), SUBLANES
    vec = _const_spec((1, D_MODEL))
    return pl.pallas_call(
        functools.partial(_conv_layer_kernel, seg=seg),
        grid=(nb, rows // tm),
        in_specs=[row] + [mod] * 6 + [prefix, prefix, vec, vec, vec,
                                      _const_spec(win.shape), _const_spec(ck.shape), _const_spec(wout.shape),
                                      _const_spec(wgu.shape), _const_spec(wd.shape)],
        out_specs=(row, tail_spec),
        out_shape=(jax.ShapeDtypeStruct(x.shape, F32),
                   jax.ShapeDtypeStruct((nb, tail_rows, D_MODEL), F32)),
        scratch_shapes=[pltpu.VMEM((tm + SUBLANES, D_MODEL), F32), pltpu.VMEM((tm, D_MODEL), BF16)],
        compiler_params=_params("arbitrary", "arbitrary"),
        name="conv_layer",
    )(x, *mods, p1, p2, n1, n2, nf, win, ck, wout, wgu, wd)


def _rope_tables(pos):
    half = ROT_DIM // 2
    inv = ROPE_THETA ** (-jnp.arange(0, ROT_DIM, 2, dtype=F32) / ROT_DIM)
    ang = pos[:, None] * inv[None, :]
    cos, sin = jnp.cos(ang), jnp.sin(ang)
    n = pos.shape[0]
    rest = HEAD_DIM - ROT_DIM
    c = jnp.concatenate([cos, cos, jnp.ones((n, rest), F32)], axis=1)
    s1 = jnp.concatenate([-sin, jnp.zeros((n, half + rest), F32)], axis=1)
    s2 = jnp.concatenate([jnp.zeros((n, half), F32), sin, jnp.zeros((n, rest), F32)], axis=1)
    return tuple(jnp.tile(a, (1, LANES // HEAD_DIM)) for a in (c, s1, s2))


def kernel(x_prompt, x_sample, c_prompt, c_sample, cache_k, cache_v, cache_kidx, state_conv, page_table,
           ada_w, ada_b, norm1_g, norm2_g, final_g, attn_w_in, attn_w_out, conv_w_in, conv_k, conv_w_out,
           ffn_w_gu, ffn_w_down):
    nb, t, _ = x_prompt.shape
    ns, tn, _ = x_sample.shape
    n_pages = page_table.shape[1]
    past = n_pages * PAGE_SIZE
    n_phys = cache_k.shape[1]
    rows_s = ns * tn

    pad = (-(nb + ns)) % SUBLANES
    c_all = jnp.concatenate([c_prompt, c_sample, jnp.zeros((pad, D_MODEL), F32)], axis=0)
    mod = _modulation(c_all, ada_w, ada_b)

    def mods(layer):
        m = mod[layer].reshape(-1, 6, D_MODEL)
        prompt = [m[:nb, i][:, None, :] for i in range(6)]
        sample = [jnp.repeat(m[nb:nb + ns, i], tn, axis=0)[None] for i in range(6)]
        return prompt, sample

    vec = lambda a: a.reshape(1, D_MODEL)
    w_in = jnp.pad(attn_w_in[0], ((0, 0), (0, ATTN_IN_PAD - ATTN_IN))).astype(BF16)
    w_o = attn_w_out[0].astype(BF16)
    wgu = [ffn_w_gu[i].astype(BF16) for i in range(2)]
    wd = [ffn_w_down[i].astype(BF16) for i in range(2)]

    (sh1p, sc1p, g1p, sh2p, sc2p, g2p), (sh1s, sc1s, g1s, sh2s, sc2s, g2s) = mods(0)
    rope_p = _rope_tables(jnp.arange(t, dtype=F32))
    k_p, v_p, ki_p, qt, qit, kg, vt, kib, wit = _proj_prompt(
        x_prompt, sc1p, sh1p, vec(norm1_g[0]), w_in, rope_p)
    o_p = _prompt_attention(qt, qit, wit, kg, vt, kib)
    y_p = _attn_out_ffn(x_prompt, o_p, g1p, sc2p, sh2p, g2p, vec(norm2_g[0]), w_o, wgu[0], wd[0])

    pos_s = jnp.tile(past + jnp.arange(tn, dtype=F32), ns)
    rope_s = _rope_tables(pos_s)
    xs = x_sample.reshape(rows_s, D_MODEL)
    proj = _proj_sample(xs, sc1s[0], sh1s[0], vec(norm1_g[0]), w_in, rope_s)
    o = 0
    q_s = proj[:, o:o + Q_COLS].reshape(ns, tn, N_KV_HEADS, HEADS_PER_KV, HEAD_DIM); o += Q_COLS
    k_s = proj[:, o:o + KV_COLS].reshape(ns, tn, KV_COLS); o += KV_COLS
    v_s = proj[:, o:o + KV_COLS].reshape(ns, tn, KV_COLS); o += KV_COLS
    qi_s = proj[:, o:o + IQ_COLS].reshape(ns, tn, IDX_HEADS, IDX_DIM); o += IQ_COLS
    ki_s = proj[:, o:o + IDX_DIM].reshape(ns, tn, IDX_DIM); o += IDX_DIM
    wi_s = proj[:, o:o + IDX_HEADS].reshape(ns, tn, IDX_HEADS)

    qpad = ((0, 0), (0, SUBLANES - tn))
    qi8 = jnp.pad(qi_s, qpad + ((0, 0), (0, 0))).transpose(0, 2, 1, 3)
    qi8 = qi8.reshape(ns, IDX_HEADS * SUBLANES, IDX_DIM).astype(BF16)
    w8 = jnp.pad(wi_s, qpad + ((0, 0),)).transpose(0, 2, 1).reshape(ns, IDX_HEADS * SUBLANES, 1)
    slot_pad = (0, PAGE_SIZE - tn)
    ki_new_t = jnp.pad(ki_s.transpose(0, 2, 1), ((0, 0), (0, 0), slot_pad))
    scores = _sample_scores(page_table, qi8, w8, cache_kidx[0].transpose(0, 2, 1), ki_new_t)
    n_keys = scores.shape[2]
    scores_t = scores[:, :tn, :].transpose(2, 0, 1).reshape(n_keys, rows_s)
    bias_t = _sample_select(scores_t, past, tn)
    bias8 = jnp.pad(bias_t.reshape(n_keys, ns, tn).transpose(1, 2, 0), qpad + ((0, 0),))

    q8 = jnp.pad(q_s * HEAD_DIM ** -0.5, qpad + ((0, 0), (0, 0), (0, 0))).transpose(0, 2, 3, 1, 4)
    q8 = q8.reshape(ns, N_KV_HEADS, HEADS_PER_KV * SUBLANES, HEAD_DIM).astype(BF16)
    new_t = lambda a: jnp.pad(a.reshape(ns, tn, N_KV_HEADS, HEAD_DIM).transpose(0, 2, 3, 1),
                              ((0, 0), (0, 0), (0, 0), slot_pad))
    o_g = _sample_attention(page_table, q8, bias8,
                            cache_k[0].transpose(0, 2, 3, 1), cache_v[0].transpose(0, 2, 3, 1),
                            new_t(k_s), new_t(v_s))
    o_g = o_g.reshape(ns, N_KV_HEADS, HEADS_PER_KV, SUBLANES, HEAD_DIM)[:, :, :, :tn]
    o_s = o_g.transpose(0, 3, 1, 2, 4).reshape(1, rows_s, Q_COLS).astype(BF16)
    y_s = _attn_out_ffn(xs[None], o_s, g1s, sc2s, sh2s, g2s, vec(norm2_g[0]), w_o, wgu[0], wd[0])

    (sh1p, sc1p, g1p, sh2p, sc2p, g2p), (sh1s, sc1s, g1s, sh2s, sc2s, g2s) = mods(1)
    win = conv_w_in[0].astype(BF16)
    wout = conv_w_out[0].astype(BF16)
    zero_prefix = jnp.zeros((1, 1, D_MODEL), F32)
    out_p, tail_p = _conv_layer(
        y_p, (sc1p, sh1p, g1p, sc2p, sh2p, g2p), zero_prefix, zero_prefix,
        vec(norm1_g[1]), vec(norm2_g[1]), vec(final_g), win, conv_k[0], wout, wgu[1], wd[1],
        seg=t, full_tail=False)
    st = state_conv[0]
    zeros_row = jnp.zeros((ns, 1, D_MODEL), F32)
    p1 = jnp.concatenate([st[:, 1:2], zeros_row, zeros_row, zeros_row], axis=1).reshape(1, rows_s, D_MODEL)
    p2 = jnp.concatenate([st[:, 0:1], st[:, 1:2], zeros_row, zeros_row], axis=1).reshape(1, rows_s, D_MODEL)
    out_s, u_s = _conv_layer(
        y_s, (sc1s, sh1s, g1s, sc2s, sh2s, g2s), p1, p2,
        vec(norm1_g[1]), vec(norm2_g[1]), vec(final_g), win, conv_k[0], wout, wgu[1], wd[1],
        seg=tn, full_tail=True)

    keep = CONV_WIDTH - 1
    return (
        out_p,
        out_s.reshape(ns, tn, D_MODEL),
        k_p.reshape(1, nb, t, N_KV_HEADS, HEAD_DIM),
        v_p.reshape(1, nb, t, N_KV_HEADS, HEAD_DIM),
        ki_p[None],
        tail_p[None, :, SUBLANES - keep:, :],
        k_s.reshape(1, ns, tn, N_KV_HEADS, HEAD_DIM),
        v_s.reshape(1, ns, tn, N_KV_HEADS, HEAD_DIM),
        ki_s[None],
        u_s.reshape(ns, tn, D_MODEL)[None, :, tn - keep:, :],
    )
```

```python
import functools

import jax
import jax.numpy as jnp
from jax import lax
from jax.experimental import pallas as pl
from jax.experimental.pallas import tpu as pltpu

F32 = jnp.float32
BF16 = jnp.bfloat16

D_MODEL = 1024
N_HEADS = 16
HEAD_DIM = 64
N_KV_HEADS = 4
HEADS_PER_KV = N_HEADS // N_KV_HEADS
ROT_DIM = 16
ROPE_THETA = 500000.0
IDX_HEADS = 8
IDX_DIM = 64
TOPK = 256
PAGE_SIZE = 128
CONV_WIDTH = 3
D_FF = 2816
EPS = 1e-6
Q_COLS = N_HEADS * HEAD_DIM
KV_COLS = N_KV_HEADS * HEAD_DIM
IQ_COLS = IDX_HEADS * IDX_DIM
ATTN_IN = Q_COLS + 2 * KV_COLS + IQ_COLS + IDX_DIM + IDX_HEADS
ATTN_IN_PAD = 2176
LANES = 128
SUBLANES = 8
MASK_BIAS = -1e30
F32_MAX = 3.4028234663852886e38
LOG2_E = 1.4426950408889634
VMEM_LIMIT = 60 * 1024 * 1024

ROW_TILE = 512
Q_TILE = 256
SCORE_CHUNK = 512
ATTN_CHUNK = 256
V_ROWS = HEAD_DIM + 16
FFN_CHUNK = 256


def _dot(a, b):
    return jnp.dot(a, b, preferred_element_type=F32)


def _dot_nt(a, b):
    return lax.dot_general(a, b, (((1,), (1,)), ((), ())), preferred_element_type=F32)


def _silu(x):
    return x / (1.0 + jnp.exp(-x))


def _norm_mod(x, g, sc, sh):
    ms = jnp.mean(x * x, axis=-1, keepdims=True)
    y = x * lax.rsqrt(ms + EPS)
    return (y * g) * (1.0 + sc) + sh


def _params(*sem, flags=None):
    return pltpu.CompilerParams(dimension_semantics=sem, vmem_limit_bytes=VMEM_LIMIT, flags=flags)


def _const_spec(shape):
    nd = len(shape)
    return pl.BlockSpec(shape, lambda *_: (0,) * nd, pipeline_mode=pl.Buffered(1))


def _mod_kernel(c_ref, w_ref, b_ref, o_ref):
    s = _silu(c_ref[...]).astype(BF16)
    o_ref[...] = _dot(s, w_ref[...].astype(BF16)) + b_ref[...]


def _modulation(c_all, ada_w, ada_b):
    depth, _, n = ada_w.shape
    rows = c_all.shape[0]
    tn = 512
    return pl.pallas_call(
        _mod_kernel,
        grid=(depth, n // tn),
        in_specs=[
            pl.BlockSpec((rows, D_MODEL), lambda l, j: (0, 0)),
            pl.BlockSpec((None, D_MODEL, tn), lambda l, j: (l, 0, j)),
            pl.BlockSpec((None, 1, tn), lambda l, j: (l, 0, j)),
        ],
        out_specs=pl.BlockSpec((None, rows, tn), lambda l, j: (l, 0, j)),
        out_shape=jax.ShapeDtypeStruct((depth, rows, n), F32),
        compiler_params=_params("arbitrary", "arbitrary"),
        name="adaln_modulation",
    )(c_all, ada_w, ada_b.reshape(depth, 1, n))


def _rope_block(x, c, s1, s2):
    return x * c + pltpu.roll(x, LANES - ROT_DIM // 2, 1) * s1 + pltpu.roll(x, ROT_DIM // 2, 1) * s2


def _rope_cols(p, c, s1, s2):
    nblk = p.shape[1] // LANES
    return [_rope_block(p[:, j * LANES:(j + 1) * LANES], c, s1, s2) for j in range(nblk)]


def _last_block_tables(c, s1, s2):
    lane = lax.broadcasted_iota(jnp.int32, c.shape, 1)
    is_key = lane < IDX_DIM
    return jnp.where(is_key, c, 1.0), jnp.where(is_key, s1, 0.0), jnp.where(is_key, s2, 0.0)


def _proj_prompt_kernel(x_ref, sc_ref, sh_ref, g_ref, w_ref, c_ref, s1_ref, s2_ref,
                        k_ref, v_ref, ki_ref, qt_ref, qit_ref, kg_ref, vt_ref, kib_ref, wit_ref):
    h = _norm_mod(x_ref[...], g_ref[...], sc_ref[...], sh_ref[...]).astype(BF16)
    c, s1, s2 = c_ref[...], s1_ref[...], s2_ref[...]

    pq = _dot(h, w_ref[:, 0:Q_COLS])
    for j, blk in enumerate(_rope_cols(pq, c, s1, s2)):
        qt_ref[j * LANES:(j + 1) * LANES, :] = (blk * (HEAD_DIM ** -0.5 * LOG2_E)).T.astype(BF16)

    pk = _dot(h, w_ref[:, Q_COLS:Q_COLS + KV_COLS])
    k = jnp.concatenate(_rope_cols(pk, c, s1, s2), axis=1)
    k_ref[...] = k
    for g in range(N_KV_HEADS):
        kg_ref[g] = k[:, g * HEAD_DIM:(g + 1) * HEAD_DIM].astype(BF16)

    pv = _dot(h, w_ref[:, Q_COLS + KV_COLS:Q_COLS + 2 * KV_COLS])
    v_ref[...] = pv
    ones = jnp.ones((V_ROWS - HEAD_DIM, ATTN_CHUNK), BF16)
    for cc in range(pv.shape[0] // ATTN_CHUNK):
        vt = pv[cc * ATTN_CHUNK:(cc + 1) * ATTN_CHUNK, :].T.astype(BF16)
        for g in range(N_KV_HEADS):
            vt_ref[cc, g * V_ROWS:g * V_ROWS + HEAD_DIM, :] = vt[g * HEAD_DIM:(g + 1) * HEAD_DIM, :]
            vt_ref[cc, g * V_ROWS + HEAD_DIM:(g + 1) * V_ROWS, :] = ones

    o = Q_COLS + 2 * KV_COLS
    pqi = _dot(h, w_ref[:, o:o + IQ_COLS])
    for j, blk in enumerate(_rope_cols(pqi, c, s1, s2)):
        qit_ref[j * LANES:(j + 1) * LANES, :] = blk.T.astype(BF16)

    pl_ = _dot(h, w_ref[:, o + IQ_COLS:ATTN_IN_PAD])
    last = _rope_block(pl_, *_last_block_tables(c, s1, s2))
    ki = last[:, 0:IDX_DIM]
    ki_ref[...] = ki
    kib_ref[...] = ki.astype(BF16)
    wit_ref[...] = last.T[IDX_DIM:IDX_DIM + IDX_HEADS, :]


def _proj_prompt(x, sc, sh, g, w, rope):
    nb, t, _ = x.shape
    tm = ROW_TILE
    row = lambda width: pl.BlockSpec((None, tm, width), lambda b, j: (b, j, 0))
    mod = pl.BlockSpec((None, 1, D_MODEL), lambda b, j: (b, 0, 0))
    tab = pl.BlockSpec((tm, LANES), lambda b, j: (j, 0))
    col = lambda height: pl.BlockSpec((None, height, tm), lambda b, j: (b, 0, j))
    out_shape = (
        jax.ShapeDtypeStruct((nb, t, KV_COLS), F32),
        jax.ShapeDtypeStruct((nb, t, KV_COLS), F32),
        jax.ShapeDtypeStruct((nb, t, IDX_DIM), F32),
        jax.ShapeDtypeStruct((nb, Q_COLS, t), BF16),
        jax.ShapeDtypeStruct((nb, IQ_COLS, t), BF16),
        jax.ShapeDtypeStruct((nb, N_KV_HEADS, t, HEAD_DIM), BF16),
        jax.ShapeDtypeStruct((nb, t // ATTN_CHUNK, N_KV_HEADS * V_ROWS, ATTN_CHUNK), BF16),
        jax.ShapeDtypeStruct((nb, t, IDX_DIM), BF16),
        jax.ShapeDtypeStruct((nb, IDX_HEADS, t), F32),
    )
    out_specs = (
        row(KV_COLS), row(KV_COLS), row(IDX_DIM), col(Q_COLS), col(IQ_COLS),
        pl.BlockSpec((None, N_KV_HEADS, tm, HEAD_DIM), lambda b, j: (b, 0, j, 0)),
        pl.BlockSpec((None, tm // ATTN_CHUNK, N_KV_HEADS * V_ROWS, ATTN_CHUNK), lambda b, j: (b, j, 0, 0)),
        row(IDX_DIM), col(IDX_HEADS),
    )
    return pl.pallas_call(
        _proj_prompt_kernel,
        grid=(nb, t // tm),
        in_specs=[row(D_MODEL), mod, mod, _const_spec((1, D_MODEL)),
                  _const_spec((D_MODEL, ATTN_IN_PAD)), tab, tab, tab],
        out_specs=out_specs,
        out_shape=out_shape,
        compiler_params=_params("arbitrary", "arbitrary"),
        name="attn_proj_prompt",
    )(x, sc, sh, g, w, *rope)


def _proj_sample_kernel(x_ref, sc_ref, sh_ref, g_ref, w_ref, c_ref, s1_ref, s2_ref, p_ref):
    h = _norm_mod(x_ref[...], g_ref[...], sc_ref[...], sh_ref[...]).astype(BF16)
    c, s1, s2 = c_ref[...], s1_ref[...], s2_ref[...]
    p = _dot(h, w_ref[...])
    v_lo = (Q_COLS + KV_COLS) // LANES
    v_hi = (Q_COLS + 2 * KV_COLS) // LANES
    nblk = ATTN_IN_PAD // LANES
    for j in range(nblk):
        blk = p[:, j * LANES:(j + 1) * LANES]
        if j == nblk - 1:
            blk = _rope_block(blk, *_last_block_tables(c, s1, s2))
        elif not (v_lo <= j < v_hi):
            blk = _rope_block(blk, c, s1, s2)
        p_ref[:, j * LANES:(j + 1) * LANES] = blk


def _proj_sample(x, sc, sh, g, w, rope):
    rows = x.shape[0]
    full = lambda width: pl.BlockSpec((rows, width), lambda i: (0, 0))
    return pl.pallas_call(
        _proj_sample_kernel,
        grid=(1,),
        in_specs=[full(D_MODEL), full(D_MODEL), full(D_MODEL),
                  pl.BlockSpec((1, D_MODEL), lambda i: (0, 0)),
                  pl.BlockSpec((D_MODEL, ATTN_IN_PAD), lambda i: (0, 0)),
                  full(LANES), full(LANES), full(LANES)],
        out_specs=full(ATTN_IN_PAD),
        out_shape=jax.ShapeDtypeStruct((rows, ATTN_IN_PAD), F32),
        compiler_params=_params("arbitrary"),
        name="attn_proj_sample",
    )(x, sc, sh, g, w, *rope)


FOLD_ROWS = 32


def _fold(x, op):
    rows, tq = x.shape
    x3 = x.reshape(rows // FOLD_ROWS, FOLD_ROWS, tq)
    return {"sum": x3.sum, "max": x3.max, "min": x3.min}[op](axis=0)


def _stats_init(tq):
    zeros = jnp.zeros((FOLD_ROWS, tq), F32)
    return (jnp.full((FOLD_ROWS, tq), -jnp.inf, F32), jnp.full((FOLD_ROWS, tq), jnp.inf, F32),
            zeros, zeros, zeros)


def _stats_update(stats, s):
    mx, mn, n_above, n_ge0, n_gt0 = stats
    ones_where = lambda m: _fold(jnp.where(m, 1.0, 0.0), "sum")
    above = s > -jnp.inf
    return (jnp.maximum(mx, _fold(s, "max")),
            jnp.minimum(mn, _fold(jnp.where(above, s, jnp.inf), "min")),
            n_above + ones_where(above), n_ge0 + ones_where(s >= 0.0), n_gt0 + ones_where(s > 0.0))


def _select_to_bias(sc_ref, qpos, nk, kc, stats):
    tq = sc_ref.shape[1]
    topk = float(TOPK)

    def chunk(c):
        return sc_ref[pl.ds(pl.multiple_of(c * kc, kc), kc), :]

    def key_index(c):
        return c * kc + lax.broadcasted_iota(jnp.int32, (kc, tq), 0)

    def ones_where(m):
        return _fold(jnp.where(m, 1.0, 0.0), "sum")

    def count(pred):
        def body(c, acc):
            return acc + ones_where(pred(chunk(c), c))
        acc = lax.fori_loop(0, nk, body, jnp.zeros((FOLD_ROWS, tq), F32))
        return acc.sum(axis=0, keepdims=True)

    mx, mn, n_above, n_ge0, n_gt0 = stats
    mx = mx.max(axis=0, keepdims=True)
    mn = mn.min(axis=0, keepdims=True)
    n_above = n_above.sum(axis=0, keepdims=True)
    n_ge0 = n_ge0.sum(axis=0, keepdims=True)
    n_gt0 = n_gt0.sum(axis=0, keepdims=True)
    n_max = count(lambda s, c: s >= mx)

    few = n_above < topk
    flat = n_max >= topk
    zero = (n_gt0 < topk) & (n_ge0 >= topk)
    pos = n_gt0 >= topk
    lo = jnp.where(few, -jnp.inf, jnp.where(flat, mx, jnp.where(zero | pos, 0.0, mn)))
    hi = jnp.where(few, mn, jnp.where(flat, jnp.inf, jnp.where(pos, mx, 0.0)))
    n_lo = jnp.where(few, 2.0 * topk, jnp.where(flat, n_max, jnp.where(zero | pos, n_ge0, n_above)))
    n_hi = jnp.where(few, n_above, jnp.where(flat, 0.0, jnp.where(zero, n_gt0, jnp.where(pos, n_max, n_ge0))))
    done = jnp.where(few | flat | zero | (n_lo == topk), 1.0, 0.0)

    def not_finished(d):
        return (jnp.min(d) < 0.5).astype(jnp.int32)

    def bisect(carry):
        lo, hi, n_lo, n_hi, done, _ = carry
        again = not_finished(done)
        mid = jnp.clip(0.5 * lo + 0.5 * hi, -F32_MAX, F32_MAX)
        stuck = (mid <= lo) | (mid >= hi)
        n_mid = count(lambda s, c: s >= mid)
        move = (done < 0.5) & jnp.logical_not(stuck)
        up = move & (n_mid >= topk)
        down = move & (n_mid < topk)
        lo = jnp.where(up, mid, lo)
        n_lo = jnp.where(up, n_mid, n_lo)
        hi = jnp.where(down, mid, hi)
        n_hi = jnp.where(down, n_mid, n_hi)
        done = jnp.where(stuck | (n_lo == topk), 1.0, done)
        return lo, hi, n_lo, n_hi, done, again

    lo, hi, n_lo, n_hi, done, _ = lax.while_loop(
        lambda carry: carry[5] > 0, bisect, (lo, hi, n_lo, n_hi, done, not_finished(done)))

    thr = lo
    exact = n_lo == topk
    need = topk - n_hi
    all_keys = jnp.full((1, tq), 2 ** 30, jnp.int32)

    def tie_break():
        def step(i, j):
            cand = j + lax.shift_right_logical(jnp.int32(2048), i)
            n_before = count(lambda s, c: (s == thr) & (key_index(c) < cand))
            return jnp.where(n_before < need, cand, j)
        return lax.fori_loop(0, 12, step, jnp.zeros((1, tq), jnp.int32))

    any_tie = jnp.min(jnp.where(exact, 1.0, 0.0)) < 0.5
    last_tied = lax.cond(any_tie, tie_break, lambda: all_keys)
    last_tied = jnp.where(exact, all_keys, last_tied)

    def write(c, _):
        s = chunk(c)
        idx = key_index(c)
        keep = ((s > thr) | ((s == thr) & (idx <= last_tied))) & (idx <= qpos)
        sc_ref[pl.ds(pl.multiple_of(c * kc, kc), kc), :] = jnp.where(keep, 0.0, MASK_BIAS)
        return 0

    lax.fori_loop(0, nk, write, 0)


def _prompt_attn_kernel(qt_ref, qit_ref, wit_ref, kg_ref, vt_ref, kib_ref, o_ref,
                        sc_ref, acc_ref, m_ref, mx_ref, s_ref):
    i = pl.program_id(1)
    tq = qt_ref.shape[1]
    qpos = i * tq + lax.broadcasted_iota(jnp.int32, (1, tq), 1)
    n_keys = (i + 1) * tq

    n_score = (n_keys + SCORE_CHUNK - 1) // SCORE_CHUNK

    def score_chunk(c, stats):
        start = pl.multiple_of(c * SCORE_CHUNK, SCORE_CHUNK)
        kc = kib_ref[pl.ds(start, SCORE_CHUNK), :]
        acc = jnp.zeros((SCORE_CHUNK, tq), F32)
        for h in range(IDX_HEADS):
            d = _dot(kc, qit_ref[h * IDX_DIM:(h + 1) * IDX_DIM, :])
            acc = acc + jnp.maximum(d, 0.0) * wit_ref[h:h + 1, :]
        idx = start + lax.broadcasted_iota(jnp.int32, (SCORE_CHUNK, tq), 0)
        s = jnp.where(idx <= qpos, acc, -jnp.inf)
        sc_ref[pl.ds(start, SCORE_CHUNK), :] = s
        return _stats_update(stats, s)

    stats = lax.fori_loop(0, n_score, score_chunk, _stats_init(tq))

    _select_to_bias(sc_ref, qpos, n_score, SCORE_CHUNK, stats)

    m_ref[...] = jnp.full(m_ref.shape, MASK_BIAS, F32)
    acc_ref[...] = jnp.zeros(acc_ref.shape, F32)

    def attn_chunk(c, _):
        start = pl.multiple_of(c * ATTN_CHUNK, ATTN_CHUNK)
        for h in range(N_HEADS):
            s = _dot(kg_ref[h // HEADS_PER_KV, pl.ds(start, ATTN_CHUNK), :],
                     qt_ref[h * HEAD_DIM:(h + 1) * HEAD_DIM, :])
            s = s + sc_ref[pl.ds(start, ATTN_CHUNK), :]
            s_ref[h] = s
            mx_ref[h] = s.max(axis=0, keepdims=True)
        for h in range(N_HEADS):
            g = h // HEADS_PER_KV
            m_old = m_ref[h]
            m_new = jnp.maximum(m_old, mx_ref[h])
            alpha = jnp.exp2(m_old - m_new)
            p = jnp.exp2(s_ref[h] - m_new).astype(BF16)
            vt = vt_ref[c, g * V_ROWS:(g + 1) * V_ROWS, :]
            acc_ref[h] = alpha * acc_ref[h] + _dot(vt, p)
            m_ref[h] = m_new
        return 0

    lax.fori_loop(0, (i + 1) * (tq // ATTN_CHUNK), attn_chunk, 0)

    for pair in range(N_HEADS // 2):
        two = []
        for h in (2 * pair, 2 * pair + 1):
            a = acc_ref[h]
            two.append(a[0:HEAD_DIM, :] / a[HEAD_DIM:HEAD_DIM + 1, :])
        col = 2 * pair * HEAD_DIM
        o_ref[:, col:col + 2 * HEAD_DIM] = jnp.concatenate(two, axis=0).T.astype(BF16)


def _prompt_attention(qt, qit, wit, kg, vt, kib):
    nb, _, t = qt.shape
    tq = Q_TILE
    col = lambda height: pl.BlockSpec((None, height, tq), lambda b, i: (b, 0, i))
    return pl.pallas_call(
        _prompt_attn_kernel,
        grid=(nb, t // tq),
        in_specs=[
            col(Q_COLS), col(IQ_COLS), col(IDX_HEADS),
            pl.BlockSpec((None, N_KV_HEADS, t, HEAD_DIM), lambda b, i: (b, 0, 0, 0)),
            pl.BlockSpec((None, t // ATTN_CHUNK, N_KV_HEADS * V_ROWS, ATTN_CHUNK), lambda b, i: (b, 0, 0, 0)),
            pl.BlockSpec((None, t, IDX_DIM), lambda b, i: (b, 0, 0)),
        ],
        out_specs=pl.BlockSpec((None, tq, Q_COLS), lambda b, i: (b, i, 0)),
        out_shape=jax.ShapeDtypeStruct((nb, t, Q_COLS), BF16),
        scratch_shapes=[
            pltpu.VMEM((t, tq), F32),
            pltpu.VMEM((N_HEADS, V_ROWS, tq), F32),
            pltpu.VMEM((N_HEADS, 1, tq), F32),
            pltpu.VMEM((N_HEADS, 1, tq), F32),
            pltpu.VMEM((N_HEADS, ATTN_CHUNK, tq), F32),
        ],
        compiler_params=_params("arbitrary", "arbitrary"),
        name="prompt_attention",
    )(qt, qit, wit, kg, vt, kib)


def _sample_score_kernel(pt_ref, qi_ref, w_ref, *rest):
    del pt_ref
    n_pages = len(rest) - 3
    pages, new_ref, o_ref, kall_ref = rest[:n_pages], rest[n_pages], rest[n_pages + 1], rest[n_pages + 2]
    for j in range(n_pages):
        kall_ref[:, j * PAGE_SIZE:(j + 1) * PAGE_SIZE] = pages[j][...].astype(BF16)
    kall_ref[:, n_pages * PAGE_SIZE:(n_pages + 1) * PAGE_SIZE] = new_ref[...].astype(BF16)
    d = _dot(qi_ref[...], kall_ref[...])
    r = jnp.maximum(d, 0.0) * w_ref[...]
    o_ref[...] = r.reshape(IDX_HEADS, SUBLANES, r.shape[1]).sum(axis=0)


def _sample_scores(page_table, qi8, w8, cache_ki_t, ki_new_t):
    nb, n_pages = page_table.shape
    n_keys = (n_pages + 1) * PAGE_SIZE
    rows = IDX_HEADS * SUBLANES
    grid_spec = pltpu.PrefetchScalarGridSpec(
        num_scalar_prefetch=1,
        grid=(nb,),
        in_specs=[pl.BlockSpec((None, rows, IDX_DIM), lambda b, pt: (b, 0, 0)),
                  pl.BlockSpec((None, rows, 1), lambda b, pt: (b, 0, 0))]
        + [pl.BlockSpec((None, IDX_DIM, PAGE_SIZE), lambda b, pt, j=j: (pt[b, j], 0, 0))
           for j in range(n_pages)]
        + [pl.BlockSpec((None, IDX_DIM, PAGE_SIZE), lambda b, pt: (b, 0, 0))],
        out_specs=pl.BlockSpec((None, SUBLANES, n_keys), lambda b, pt: (b, 0, 0)),
        scratch_shapes=[pltpu.VMEM((IDX_DIM, n_keys), BF16)],
    )
    return pl.pallas_call(
        _sample_score_kernel,
        grid_spec=grid_spec,
        out_shape=jax.ShapeDtypeStruct((nb, SUBLANES, n_keys), F32),
        compiler_params=_params("arbitrary"),
        name="sample_indexer_scores",
    )(page_table, qi8, w8, *([cache_ki_t] * n_pages), ki_new_t)


def _sample_select_kernel(s_ref, o_ref, sc_ref, *, past, period):
    tq = s_ref.shape[1]
    n_keys = s_ref.shape[0]
    lane = lax.broadcasted_iota(jnp.int32, (1, tq), 1)
    qpos = past + (lane & (period - 1))
    nk = n_keys // PAGE_SIZE
    stats = _stats_init(tq)
    for c in range(nk):
        idx = c * PAGE_SIZE + lax.broadcasted_iota(jnp.int32, (PAGE_SIZE, tq), 0)
        s = jnp.where(idx <= qpos, s_ref[c * PAGE_SIZE:(c + 1) * PAGE_SIZE, :], -jnp.inf)
        sc_ref[c * PAGE_SIZE:(c + 1) * PAGE_SIZE, :] = s
        stats = _stats_update(stats, s)
    _select_to_bias(sc_ref, qpos, nk, PAGE_SIZE, stats)
    o_ref[...] = sc_ref[...]


def _sample_select(scores_t, past, period):
    n_keys, nq = scores_t.shape
    tq = Q_TILE
    blk = pl.BlockSpec((n_keys, tq), lambda i: (0, i))
    return pl.pallas_call(
        functools.partial(_sample_select_kernel, past=past, period=period),
        grid=(nq // tq,),
        in_specs=[blk],
        out_specs=blk,
        out_shape=jax.ShapeDtypeStruct((n_keys, nq), F32),
        scratch_shapes=[pltpu.VMEM((n_keys, tq), F32)],
        compiler_params=_params("arbitrary"),
        name="sample_select",
    )(scores_t)


def _sample_attn_kernel(pt_ref, q_ref, bias_ref, *rest):
    del pt_ref
    n_pages = (len(rest) - 8) // 2
    kpages, vpages = rest[:n_pages], rest[n_pages:2 * n_pages]
    knew_ref, vnew_ref, o_ref, kall_ref, vall_ref, s_ref, p_ref, l_ref = rest[2 * n_pages:]
    for j in range(n_pages):
        kall_ref[:, :, j * PAGE_SIZE:(j + 1) * PAGE_SIZE] = kpages[j][...].astype(BF16)
        vall_ref[:, :, j * PAGE_SIZE:(j + 1) * PAGE_SIZE] = vpages[j][...].astype(BF16)
    kall_ref[:, :, n_pages * PAGE_SIZE:(n_pages + 1) * PAGE_SIZE] = knew_ref[...].astype(BF16)
    vall_ref[:, :, n_pages * PAGE_SIZE:(n_pages + 1) * PAGE_SIZE] = vnew_ref[...].astype(BF16)
    bias = bias_ref[...]
    n_keys = bias.shape[1]
    bias = jnp.broadcast_to(bias[None], (HEADS_PER_KV, SUBLANES, n_keys)).reshape(
        HEADS_PER_KV * SUBLANES, n_keys)
    for g in range(N_KV_HEADS):
        s_ref[g] = _dot(q_ref[g], kall_ref[g]) + bias
    for g in range(N_KV_HEADS):
        s = s_ref[g]
        m = s.max(axis=1, keepdims=True)
        p = jnp.exp(s - m)
        l_ref[g] = p.sum(axis=1, keepdims=True)
        p_ref[g] = p.astype(BF16)
    for g in range(N_KV_HEADS):
        o_ref[g] = _dot_nt(p_ref[g], vall_ref[g]) / l_ref[g]


def _sample_attention(page_table, q8, bias8, cache_k_t, cache_v_t, k_new_t, v_new_t):
    nb, n_pages = page_table.shape
    n_keys = (n_pages + 1) * PAGE_SIZE
    rows = HEADS_PER_KV * SUBLANES
    page = [pl.BlockSpec((None, N_KV_HEADS, HEAD_DIM, PAGE_SIZE), lambda b, pt, j=j: (pt[b, j], 0, 0, 0))
            for j in range(n_pages)]
    new = pl.BlockSpec((None, N_KV_HEADS, HEAD_DIM, PAGE_SIZE), lambda b, pt: (b, 0, 0, 0))
    qo = pl.BlockSpec((None, N_KV_HEADS, rows, HEAD_DIM), lambda b, pt: (b, 0, 0, 0))
    grid_spec = pltpu.PrefetchScalarGridSpec(
        num_scalar_prefetch=1,
        grid=(nb,),
        in_specs=[qo, pl.BlockSpec((None, SUBLANES, n_keys), lambda b, pt: (b, 0, 0))]
        + page + page + [new, new],
        out_specs=qo,
        scratch_shapes=[pltpu.VMEM((N_KV_HEADS, HEAD_DIM, n_keys), BF16),
                        pltpu.VMEM((N_KV_HEADS, HEAD_DIM, n_keys), BF16),
                        pltpu.VMEM((N_KV_HEADS, rows, n_keys), F32),
                        pltpu.VMEM((N_KV_HEADS, rows, n_keys), BF16),
                        pltpu.VMEM((N_KV_HEADS, rows, 1), F32)],
    )
    return pl.pallas_call(
        _sample_attn_kernel,
        grid_spec=grid_spec,
        out_shape=jax.ShapeDtypeStruct((nb, N_KV_HEADS, rows, HEAD_DIM), F32),
        compiler_params=_params("arbitrary"),
        name="sample_attention",
    )(page_table, q8, bias8, *([cache_k_t] * n_pages), *([cache_v_t] * n_pages), k_new_t, v_new_t)


def _ffn(h, wgu_ref, wd_ref):
    acc = jnp.zeros((h.shape[0], D_MODEL), F32)
    for c in range(D_FF // FFN_CHUNK):
        gate = _dot(h, wgu_ref[:, c * FFN_CHUNK:(c + 1) * FFN_CHUNK])
        up = _dot(h, wgu_ref[:, D_FF + c * FFN_CHUNK:D_FF + (c + 1) * FFN_CHUNK])
        a = _silu(gate) * up
        acc = acc + _dot(a.astype(BF16), wd_ref[c * FFN_CHUNK:(c + 1) * FFN_CHUNK, :])
    return acc


def _row_specs(nb, rows, per_row_mod):
    tm = min(ROW_TILE, rows)
    row = pl.BlockSpec((None, tm, D_MODEL), lambda b, j: (b, j, 0))
    if per_row_mod:
        mod = row
    else:
        mod = pl.BlockSpec((None, 1, D_MODEL), lambda b, j: (b, 0, 0))
    return tm, row, mod


def _attn_out_ffn_kernel(x_ref, o_ref, g1_ref, sc2_ref, sh2_ref, g2_ref, n2_ref,
                         wo_ref, wgu_ref, wd_ref, y_ref):
    y1 = x_ref[...] + g1_ref[...] * _dot(o_ref[...], wo_ref[...])
    h2 = _norm_mod(y1, n2_ref[...], sc2_ref[...], sh2_ref[...]).astype(BF16)
    y_ref[...] = y1 + g2_ref[...] * _ffn(h2, wgu_ref, wd_ref)


def _attn_out_ffn(x, o, g1, sc2, sh2, g2, n2, wo, wgu, wd):
    nb, rows, _ = x.shape
    tm, row, mod = _row_specs(nb, rows, g1.shape[1] == rows)
    return pl.pallas_call(
        _attn_out_ffn_kernel,
        grid=(nb, rows // tm),
        in_specs=[row, row, mod, mod, mod, mod, _const_spec((1, D_MODEL)),
                  _const_spec(wo.shape), _const_spec(wgu.shape), _const_spec(wd.shape)],
        out_specs=row,
        out_shape=jax.ShapeDtypeStruct(x.shape, F32),
        compiler_params=_params("arbitrary", "arbitrary"),
        name="attn_out_ffn",
    )(x, o, g1, sc2, sh2, g2, n2, wo, wgu, wd)


def _conv_layer_kernel(x_ref, sc1_ref, sh1_ref, g1_ref, sc2_ref, sh2_ref, g2_ref, p1_ref, p2_ref,
                       n1_ref, n2_ref, nf_ref, win_ref, ck_ref, wout_ref, wgu_ref, wd_ref,
                       y_ref, tail_ref, ubuf_ref, z_ref, *, seg):
    j = pl.program_id(1)
    tm = x_ref.shape[0]
    x = x_ref[...]
    h = _norm_mod(x, n1_ref[...], sc1_ref[...], sh1_ref[...]).astype(BF16)

    @pl.when(j == 0)
    def _():
        ubuf_ref[0:SUBLANES, :] = jnp.zeros((SUBLANES, D_MODEL), F32)

    @pl.when(j > 0)
    def _():
        ubuf_ref[0:SUBLANES, :] = ubuf_ref[tm:tm + SUBLANES, :]

    t = (j * tm + lax.broadcasted_iota(jnp.int32, (tm, 1), 0)) & (seg - 1)
    ck = ck_ref[...]
    cw = FFN_CHUNK
    for c in range(D_MODEL // cw):
        cols = slice(c * cw, (c + 1) * cw)
        bg = _dot(h, win_ref[:, c * cw:(c + 1) * cw])
        cg = _dot(h, win_ref[:, D_MODEL + c * cw:D_MODEL + (c + 1) * cw])
        hv = _dot(h, win_ref[:, 2 * D_MODEL + c * cw:2 * D_MODEL + (c + 1) * cw])
        u = cg * hv
        ubuf_ref[SUBLANES:SUBLANES + tm, cols] = u
        um1 = jnp.where(t >= 1, ubuf_ref[SUBLANES - 1:SUBLANES - 1 + tm, cols], p1_ref[:, cols])
        um2 = jnp.where(t >= 2, ubuf_ref[SUBLANES - 2:SUBLANES - 2 + tm, cols], p2_ref[:, cols])
        conv = ck[0:1, cols] * um2 + ck[1:2, cols] * um1 + ck[2:3, cols] * u
        z_ref[:, cols] = (bg * conv).astype(BF16)
    r = tail_ref.shape[0]
    tail_ref[...] = ubuf_ref[SUBLANES + tm - r:SUBLANES + tm, :]

    y1 = x + g1_ref[...] * _dot(z_ref[...], wout_ref[...])
    h2 = _norm_mod(y1, n2_ref[...], sc2_ref[...], sh2_ref[...]).astype(BF16)
    y2 = y1 + g2_ref[...] * _ffn(h2, wgu_ref, wd_ref)
    ms = jnp.mean(y2 * y2, axis=-1, keepdims=True)
    y_ref[...] = (y2 * lax.rsqrt(ms + EPS)) * nf_ref[...]


def _conv_layer(x, mods, p1, p2, n1, n2, nf, win, ck, wout, wgu, wd, *, seg, full_tail):
    nb, rows, _ = x.shape
    per_row = mods[0].shape[1] == rows
    tm, row, mod = _row_specs(nb, rows, per_row)
    if per_row:
        prefix = row
    else:
        prefix = pl.BlockSpec((None, 1, D_MODEL), lambda b, j: (0, 0, 0))
    if full_tail:
        tail_spec, tail_rows = row, rows
    else:
        tail_spec, tail_rows = pl.BlockSpec((None, SUBLANES, D_MODEL), lambda b, j: (b, 0, 0)), SUBLANES
    vec = _const_spec((1, D_MODEL))
    return pl.pallas_call(
        functools.partial(_conv_layer_kernel, seg=seg),
        grid=(nb, rows // tm),
        in_specs=[row] + [mod] * 6 + [prefix, prefix, vec, vec, vec,
                                      _const_spec(win.shape), _const_spec(ck.shape), _const_spec(wout.shape),
                                      _const_spec(wgu.shape), _const_spec(wd.shape)],
        out_specs=(row, tail_spec),
        out_shape=(jax.ShapeDtypeStruct(x.shape, F32),
                   jax.ShapeDtypeStruct((nb, tail_rows, D_MODEL), F32)),
        scratch_shapes=[pltpu.VMEM((tm + SUBLANES, D_MODEL), F32), pltpu.VMEM((tm, D_MODEL), BF16)],
        compiler_params=_params("arbitrary", "arbitrary"),
        name="conv_layer",
    )(x, *mods, p1, p2, n1, n2, nf, win, ck, wout, wgu, wd)


def _rope_tables(pos):
    half = ROT_DIM // 2
    inv = ROPE_THETA ** (-jnp.arange(0, ROT_DIM, 2, dtype=F32) / ROT_DIM)
    ang = pos[:, None] * inv[None, :]
    cos, sin = jnp.cos(ang), jnp.sin(ang)
    n = pos.shape[0]
    rest = HEAD_DIM - ROT_DIM
    c = jnp.concatenate([cos, cos, jnp.ones((n, rest), F32)], axis=1)
    s1 = jnp.concatenate([-sin, jnp.zeros((n, half + rest), F32)], axis=1)
    s2 = jnp.concatenate([jnp.zeros((n, half), F32), sin, jnp.zeros((n, rest), F32)], axis=1)
    return tuple(jnp.tile(a, (1, LANES // HEAD_DIM)) for a in (c, s1, s2))


def kernel(x_prompt, x_sample, c_prompt, c_sample, cache_k, cache_v, cache_kidx, state_conv, page_table,
           ada_w, ada_b, norm1_g, norm2_g, final_g, attn_w_in, attn_w_out, conv_w_in, conv_k, conv_w_out,
           ffn_w_gu, ffn_w_down):
    nb, t, _ = x_prompt.shape
    ns, tn, _ = x_sample.shape
    n_pages = page_table.shape[1]
    past = n_pages * PAGE_SIZE
    n_phys = cache_k.shape[1]
    rows_s = ns * tn

    pad = (-(nb + ns)) % SUBLANES
    c_all = jnp.concatenate([c_prompt, c_sample, jnp.zeros((pad, D_MODEL), F32)], axis=0)
    mod = _modulation(c_all, ada_w, ada_b)

    def mods(layer):
        m = mod[layer].reshape(-1, 6, D_MODEL)
        prompt = [m[:nb, i][:, None, :] for i in range(6)]
        sample = [jnp.repeat(m[nb:nb + ns, i], tn, axis=0)[None] for i in range(6)]
        return prompt, sample

    vec = lambda a: a.reshape(1, D_MODEL)
    w_in = jnp.pad(attn_w_in[0], ((0, 0), (0, ATTN_IN_PAD - ATTN_IN))).astype(BF16)
    w_o = attn_w_out[0].astype(BF16)
    wgu = [ffn_w_gu[i].astype(BF16) for i in range(2)]
    wd = [ffn_w_down[i].astype(BF16) for i in range(2)]

    (sh1p, sc1p, g1p, sh2p, sc2p, g2p), (sh1s, sc1s, g1s, sh2s, sc2s, g2s) = mods(0)
    rope_p = _rope_tables(jnp.arange(t, dtype=F32))
    k_p, v_p, ki_p, qt, qit, kg, vt, kib, wit = _proj_prompt(
        x_prompt, sc1p, sh1p, vec(norm1_g[0]), w_in, rope_p)
    o_p = _prompt_attention(qt, qit, wit, kg, vt, kib)
    y_p = _attn_out_ffn(x_prompt, o_p, g1p, sc2p, sh2p, g2p, vec(norm2_g[0]), w_o, wgu[0], wd[0])

    pos_s = jnp.tile(past + jnp.arange(tn, dtype=F32), ns)
    rope_s = _rope_tables(pos_s)
    xs = x_sample.reshape(rows_s, D_MODEL)
    proj = _proj_sample(xs, sc1s[0], sh1s[0], vec(norm1_g[0]), w_in, rope_s)
    o = 0
    q_s = proj[:, o:o + Q_COLS].reshape(ns, tn, N_KV_HEADS, HEADS_PER_KV, HEAD_DIM); o += Q_COLS
    k_s = proj[:, o:o + KV_COLS].reshape(ns, tn, KV_COLS); o += KV_COLS
    v_s = proj[:, o:o + KV_COLS].reshape(ns, tn, KV_COLS); o += KV_COLS
    qi_s = proj[:, o:o + IQ_COLS].reshape(ns, tn, IDX_HEADS, IDX_DIM); o += IQ_COLS
    ki_s = proj[:, o:o + IDX_DIM].reshape(ns, tn, IDX_DIM); o += IDX_DIM
    wi_s = proj[:, o:o + IDX_HEADS].reshape(ns, tn, IDX_HEADS)

    qpad = ((0, 0), (0, SUBLANES - tn))
    qi8 = jnp.pad(qi_s, qpad + ((0, 0), (0, 0))).transpose(0, 2, 1, 3)
    qi8 = qi8.reshape(ns, IDX_HEADS * SUBLANES, IDX_DIM).astype(BF16)
    w8 = jnp.pad(wi_s, qpad + ((0, 0),)).transpose(0, 2, 1).reshape(ns, IDX_HEADS * SUBLANES, 1)
    slot_pad = (0, PAGE_SIZE - tn)
    ki_new_t = jnp.pad(ki_s.transpose(0, 2, 1), ((0, 0), (0, 0), slot_pad))
    scores = _sample_scores(page_table, qi8, w8, cache_kidx[0].transpose(0, 2, 1), ki_new_t)
    n_keys = scores.shape[2]
    scores_t = scores[:, :tn, :].transpose(2, 0, 1).reshape(n_keys, rows_s)
    bias_t = _sample_select(scores_t, past, tn)
    bias8 = jnp.pad(bias_t.reshape(n_keys, ns, tn).transpose(1, 2, 0), qpad + ((0, 0),))

    q8 = jnp.pad(q_s * HEAD_DIM ** -0.5, qpad + ((0, 0), (0, 0), (0, 0))).transpose(0, 2, 3, 1, 4)
    q8 = q8.reshape(ns, N_KV_HEADS, HEADS_PER_KV * SUBLANES, HEAD_DIM).astype(BF16)
    new_t = lambda a: jnp.pad(a.reshape(ns, tn, N_KV_HEADS, HEAD_DIM).transpose(0, 2, 3, 1),
                              ((0, 0), (0, 0), (0, 0), slot_pad))
    o_g = _sample_attention(page_table, q8, bias8,
                            cache_k[0].transpose(0, 2, 3, 1), cache_v[0].transpose(0, 2, 3, 1),
                            new_t(k_s), new_t(v_s))
    o_g = o_g.reshape(ns, N_KV_HEADS, HEADS_PER_KV, SUBLANES, HEAD_DIM)[:, :, :, :tn]
    o_s = o_g.transpose(0, 3, 1, 2, 4).reshape(1, rows_s, Q_COLS).astype(BF16)
    y_s = _attn_out_ffn(xs[None], o_s, g1s, sc2s, sh2s, g2s, vec(norm2_g[0]), w_o, wgu[0], wd[0])

    (sh1p, sc1p, g1p, sh2p, sc2p, g2p), (sh1s, sc1s, g1s, sh2s, sc2s, g2s) = mods(1)
    win = conv_w_in[0].astype(BF16)
    wout = conv_w_out[0].astype(BF16)
    zero_prefix = jnp.zeros((1, 1, D_MODEL), F32)
    out_p, tail_p = _conv_layer(
        y_p, (sc1p, sh1p, g1p, sc2p, sh2p, g2p), zero_prefix, zero_prefix,
        vec(norm1_g[1]), vec(norm2_g[1]), vec(final_g), win, conv_k[0], wout, wgu[1], wd[1],
        seg=t, full_tail=False)
    st = state_conv[0]
    zeros_row = jnp.zeros((ns, 1, D_MODEL), F32)
    p1 = jnp.concatenate([st[:, 1:2], zeros_row, zeros_row, zeros_row], axis=1).reshape(1, rows_s, D_MODEL)
    p2 = jnp.concatenate([st[:, 0:1], st[:, 1:2], zeros_row, zeros_row], axis=1).reshape(1, rows_s, D_MODEL)
    out_s, u_s = _conv_layer(
        y_s, (sc1s, sh1s, g1s, sc2s, sh2s, g2s), p1, p2,
        vec(norm1_g[1]), vec(norm2_g[1]), vec(final_g), win, conv_k[0], wout, wgu[1], wd[1],
        seg=tn, full_tail=True)

    keep = CONV_WIDTH - 1
    return (
        out_p,
        out_s.reshape(ns, tn, D_MODEL),
        k_p.reshape(1, nb, t, N_KV_HEADS, HEAD_DIM),
        v_p.reshape(1, nb, t, N_KV_HEADS, HEAD_DIM),
        ki_p[None],
        tail_p[None, :, SUBLANES - keep:, :],
        k_s.reshape(1, ns, tn, N_KV_HEADS, HEAD_DIM),
        v_s.reshape(1, ns, tn, N_KV_HEADS, HEAD_DIM),
        ki_s[None],
        u_s.reshape(ns, tn, D_MODEL)[None, :, tn - keep:, :],
    )
```

```python
import functools

import jax
import jax.numpy as jnp
from jax import lax
from jax.experimental import pallas as pl
from jax.experimental.pallas import tpu as pltpu

F32 = jnp.float32
BF16 = jnp.bfloat16

D_MODEL = 1024
N_HEADS = 16
HEAD_DIM = 64
N_KV_HEADS = 4
HEADS_PER_KV = N_HEADS // N_KV_HEADS
ROT_DIM = 16
ROPE_THETA = 500000.0
IDX_HEADS = 8
IDX_DIM = 64
TOPK = 256
PAGE_SIZE = 128
CONV_WIDTH = 3
D_FF = 2816
EPS = 1e-6
Q_COLS = N_HEADS * HEAD_DIM
KV_COLS = N_KV_HEADS * HEAD_DIM
IQ_COLS = IDX_HEADS * IDX_DIM
ATTN_IN = Q_COLS + 2 * KV_COLS + IQ_COLS + IDX_DIM + IDX_HEADS
ATTN_IN_PAD = 2176
LANES = 128
SUBLANES = 8
MASK_BIAS = -1e30
F32_MAX = 3.4028234663852886e38
LOG2_E = 1.4426950408889634
VMEM_LIMIT = 60 * 1024 * 1024

ROW_TILE = 512
Q_TILE = 256
SCORE_CHUNK = 512
ATTN_CHUNK = 256
V_ROWS = HEAD_DIM + 16
FFN_CHUNK = 256


def _dot(a, b):
    return jnp.dot(a, b, preferred_element_type=F32)


def _dot_nt(a, b):
    return lax.dot_general(a, b, (((1,), (1,)), ((), ())), preferred_element_type=F32)


def _silu(x):
    return x / (1.0 + jnp.exp(-x))


def _norm_mod(x, g, sc, sh):
    ms = jnp.mean(x * x, axis=-1, keepdims=True)
    y = x * lax.rsqrt(ms + EPS)
    return (y * g) * (1.0 + sc) + sh


def _params(*sem, flags=None):
    return pltpu.CompilerParams(dimension_semantics=sem, vmem_limit_bytes=VMEM_LIMIT, flags=flags)


def _const_spec(shape):
    nd = len(shape)
    return pl.BlockSpec(shape, lambda *_: (0,) * nd, pipeline_mode=pl.Buffered(1))


def _mod_kernel(c_ref, w_ref, b_ref, o_ref):
    s = _silu(c_ref[...]).astype(BF16)
    o_ref[...] = _dot(s, w_ref[...].astype(BF16)) + b_ref[...]


def _modulation(c_all, ada_w, ada_b):
    depth, _, n = ada_w.shape
    rows = c_all.shape[0]
    tn = 512
    return pl.pallas_call(
        _mod_kernel,
        grid=(depth, n // tn),
        in_specs=[
            pl.BlockSpec((rows, D_MODEL), lambda l, j: (0, 0)),
            pl.BlockSpec((None, D_MODEL, tn), lambda l, j: (l, 0, j)),
            pl.BlockSpec((None, 1, tn), lambda l, j: (l, 0, j)),
        ],
        out_specs=pl.BlockSpec((None, rows, tn), lambda l, j: (l, 0, j)),
        out_shape=jax.ShapeDtypeStruct((depth, rows, n), F32),
        compiler_params=_params("arbitrary", "arbitrary"),
        name="adaln_modulation",
    )(c_all, ada_w, ada_b.reshape(depth, 1, n))


def _rope_block(x, c, s1, s2):
    return x * c + pltpu.roll(x, LANES - ROT_DIM // 2, 1) * s1 + pltpu.roll(x, ROT_DIM // 2, 1) * s2


def _rope_cols(p, c, s1, s2):
    nblk = p.shape[1] // LANES
    return [_rope_block(p[:, j * LANES:(j + 1) * LANES], c, s1, s2) for j in range(nblk)]


def _last_block_tables(c, s1, s2):
    lane = lax.broadcasted_iota(jnp.int32, c.shape, 1)
    is_key = lane < IDX_DIM
    return jnp.where(is_key, c, 1.0), jnp.where(is_key, s1, 0.0), jnp.where(is_key, s2, 0.0)


def _proj_prompt_kernel(x_ref, sc_ref, sh_ref, g_ref, w_ref, c_ref, s1_ref, s2_ref,
                        k_ref, v_ref, ki_ref, qt_ref, qit_ref, kg_ref, vt_ref, kib_ref, wit_ref):
    h = _norm_mod(x_ref[...], g_ref[...], sc_ref[...], sh_ref[...]).astype(BF16)
    c, s1, s2 = c_ref[...], s1_ref[...], s2_ref[...]

    pq = _dot(h, w_ref[:, 0:Q_COLS])
    for j, blk in enumerate(_rope_cols(pq, c, s1, s2)):
        qt_ref[j * LANES:(j + 1) * LANES, :] = (blk * (HEAD_DIM ** -0.5 * LOG2_E)).T.astype(BF16)

    pk = _dot(h, w_ref[:, Q_COLS:Q_COLS + KV_COLS])
    k = jnp.concatenate(_rope_cols(pk, c, s1, s2), axis=1)
    k_ref[...] = k
    for g in range(N_KV_HEADS):
        kg_ref[g] = k[:, g * HEAD_DIM:(g + 1) * HEAD_DIM].astype(BF16)

    pv = _dot(h, w_ref[:, Q_COLS + KV_COLS:Q_COLS + 2 * KV_COLS])
    v_ref[...] = pv
    ones = jnp.ones((V_ROWS - HEAD_DIM, ATTN_CHUNK), BF16)
    for cc in range(pv.shape[0] // ATTN_CHUNK):
        vt = pv[cc * ATTN_CHUNK:(cc + 1) * ATTN_CHUNK, :].T.astype(BF16)
        for g in range(N_KV_HEADS):
            vt_ref[cc, g * V_ROWS:g * V_ROWS + HEAD_DIM, :] = vt[g * HEAD_DIM:(g + 1) * HEAD_DIM, :]
            vt_ref[cc, g * V_ROWS + HEAD_DIM:(g + 1) * V_ROWS, :] = ones

    o = Q_COLS + 2 * KV_COLS
    pqi = _dot(h, w_ref[:, o:o + IQ_COLS])
    for j, blk in enumerate(_rope_cols(pqi, c, s1, s2)):
        qit_ref[j * LANES:(j + 1) * LANES, :] = blk.T.astype(BF16)

    pl_ = _dot(h, w_ref[:, o + IQ_COLS:ATTN_IN_PAD])
    last = _rope_block(pl_, *_last_block_tables(c, s1, s2))
    ki = last[:, 0:IDX_DIM]
    ki_ref[...] = ki
    kib_ref[...] = ki.astype(BF16)
    wit_ref[...] = last.T[IDX_DIM:IDX_DIM + IDX_HEADS, :]


def _proj_prompt(x, sc, sh, g, w, rope):
    nb, t, _ = x.shape
    tm = ROW_TILE
    row = lambda width: pl.BlockSpec((None, tm, width), lambda b, j: (b, j, 0))
    mod = pl.BlockSpec((None, 1, D_MODEL), lambda b, j: (b, 0, 0))
    tab = pl.BlockSpec((tm, LANES), lambda b, j: (j, 0))
    col = lambda height: pl.BlockSpec((None, height, tm), lambda b, j: (b, 0, j))
    out_shape = (
        jax.ShapeDtypeStruct((nb, t, KV_COLS), F32),
        jax.ShapeDtypeStruct((nb, t, KV_COLS), F32),
        jax.ShapeDtypeStruct((nb, t, IDX_DIM), F32),
        jax.ShapeDtypeStruct((nb, Q_COLS, t), BF16),
        jax.ShapeDtypeStruct((nb, IQ_COLS, t), BF16),
        jax.ShapeDtypeStruct((nb, N_KV_HEADS, t, HEAD_DIM), BF16),
        jax.ShapeDtypeStruct((nb, t // ATTN_CHUNK, N_KV_HEADS * V_ROWS, ATTN_CHUNK), BF16),
        jax.ShapeDtypeStruct((nb, t, IDX_DIM), BF16),
        jax.ShapeDtypeStruct((nb, IDX_HEADS, t), F32),
    )
    out_specs = (
        row(KV_COLS), row(KV_COLS), row(IDX_DIM), col(Q_COLS), col(IQ_COLS),
        pl.BlockSpec((None, N_KV_HEADS, tm, HEAD_DIM), lambda b, j: (b, 0, j, 0)),
        pl.BlockSpec((None, tm // ATTN_CHUNK, N_KV_HEADS * V_ROWS, ATTN_CHUNK), lambda b, j: (b, j, 0, 0)),
        row(IDX_DIM), col(IDX_HEADS),
    )
    return pl.pallas_call(
        _proj_prompt_kernel,
        grid=(nb, t // tm),
        in_specs=[row(D_MODEL), mod, mod, _const_spec((1, D_MODEL)),
                  _const_spec((D_MODEL, ATTN_IN_PAD)), tab, tab, tab],
        out_specs=out_specs,
        out_shape=out_shape,
        compiler_params=_params("arbitrary", "arbitrary"),
        name="attn_proj_prompt",
    )(x, sc, sh, g, w, *rope)


def _proj_sample_kernel(x_ref, sc_ref, sh_ref, g_ref, w_ref, c_ref, s1_ref, s2_ref, p_ref):
    h = _norm_mod(x_ref[...], g_ref[...], sc_ref[...], sh_ref[...]).astype(BF16)
    c, s1, s2 = c_ref[...], s1_ref[...], s2_ref[...]
    p = _dot(h, w_ref[...])
    v_lo = (Q_COLS + KV_COLS) // LANES
    v_hi = (Q_COLS + 2 * KV_COLS) // LANES
    nblk = ATTN_IN_PAD // LANES
    for j in range(nblk):
        blk = p[:, j * LANES:(j + 1) * LANES]
        if j == nblk - 1:
            blk = _rope_block(blk, *_last_block_tables(c, s1, s2))
        elif not (v_lo <= j < v_hi):
            blk = _rope_block(blk, c, s1, s2)
        p_ref[:, j * LANES:(j + 1) * LANES] = blk


def _proj_sample(x, sc, sh, g, w, rope):
    rows = x.shape[0]
    full = lambda width: pl.BlockSpec((rows, width), lambda i: (0, 0))
    return pl.pallas_call(
        _proj_sample_kernel,
        grid=(1,),
        in_specs=[full(D_MODEL), full(D_MODEL), full(D_MODEL),
                  pl.BlockSpec((1, D_MODEL), lambda i: (0, 0)),
                  pl.BlockSpec((D_MODEL, ATTN_IN_PAD), lambda i: (0, 0)),
                  full(LANES), full(LANES), full(LANES)],
        out_specs=full(ATTN_IN_PAD),
        out_shape=jax.ShapeDtypeStruct((rows, ATTN_IN_PAD), F32),
        compiler_params=_params("arbitrary"),
        name="attn_proj_sample",
    )(x, sc, sh, g, w, *rope)


FOLD_ROWS = 32


def _fold(x, op):
    rows, tq = x.shape
    x3 = x.reshape(rows // FOLD_ROWS, FOLD_ROWS, tq)
    return {"sum": x3.sum, "max": x3.max, "min": x3.min}[op](axis=0)


def _stats_init(tq):
    zeros = jnp.zeros((FOLD_ROWS, tq), F32)
    return (jnp.full((FOLD_ROWS, tq), -jnp.inf, F32), jnp.full((FOLD_ROWS, tq), jnp.inf, F32),
            zeros, zeros, zeros)


def _stats_update(stats, s):
    mx, mn, n_above, n_ge0, n_gt0 = stats
    ones_where = lambda m: _fold(jnp.where(m, 1.0, 0.0), "sum")
    above = s > -jnp.inf
    return (jnp.maximum(mx, _fold(s, "max")),
            jnp.minimum(mn, _fold(jnp.where(above, s, jnp.inf), "min")),
            n_above + ones_where(above), n_ge0 + ones_where(s >= 0.0), n_gt0 + ones_where(s > 0.0))


def _select_to_bias(sc_ref, tie_ref, qpos, nk, kc, stats):
    tq = sc_ref.shape[1]
    topk = float(TOPK)

    def chunk(c):
        return sc_ref[pl.ds(pl.multiple_of(c * kc, kc), kc), :]

    def key_index(c):
        return c * kc + lax.broadcasted_iota(jnp.int32, (kc, tq), 0)

    def ones_where(m):
        return _fold(jnp.where(m, 1.0, 0.0), "sum")

    def count(pred):
        def body(c, acc):
            return acc + ones_where(pred(chunk(c), c))
        acc = lax.fori_loop(0, nk, body, jnp.zeros((FOLD_ROWS, tq), F32))
        return acc.sum(axis=0, keepdims=True)

    mx, mn, n_above, n_ge0, n_gt0 = stats
    mx = mx.max(axis=0, keepdims=True)
    mn = mn.min(axis=0, keepdims=True)
    n_above = n_above.sum(axis=0, keepdims=True)
    n_ge0 = n_ge0.sum(axis=0, keepdims=True)
    n_gt0 = n_gt0.sum(axis=0, keepdims=True)
    n_max = count(lambda s, c: s >= mx)

    few = n_above < topk
    flat = n_max >= topk
    zero = (n_gt0 < topk) & (n_ge0 >= topk)
    pos = n_gt0 >= topk
    lo = jnp.where(few, -jnp.inf, jnp.where(flat, mx, jnp.where(zero | pos, 0.0, mn)))
    hi = jnp.where(few, mn, jnp.where(flat, jnp.inf, jnp.where(pos, mx, 0.0)))
    n_lo = jnp.where(few, 2.0 * topk, jnp.where(flat, n_max, jnp.where(zero | pos, n_ge0, n_above)))
    n_hi = jnp.where(few, n_above, jnp.where(flat, 0.0, jnp.where(zero, n_gt0, jnp.where(pos, n_max, n_ge0))))
    done = jnp.where(few | flat | zero | (n_lo == topk), 1.0, 0.0)

    def not_finished(d):
        return (jnp.min(d) < 0.5).astype(jnp.int32)

    def bisect(carry):
        lo, hi, n_lo, n_hi, done, _ = carry
        again = not_finished(done)
        mid = jnp.clip(0.5 * lo + 0.5 * hi, -F32_MAX, F32_MAX)
        stuck = (mid <= lo) | (mid >= hi)
        n_mid = count(lambda s, c: s >= mid)
        move = (done < 0.5) & jnp.logical_not(stuck)
        up = move & (n_mid >= topk)
        down = move & (n_mid < topk)
        lo = jnp.where(up, mid, lo)
        n_lo = jnp.where(up, n_mid, n_lo)
        hi = jnp.where(down, mid, hi)
        n_hi = jnp.where(down, n_mid, n_hi)
        done = jnp.where(stuck | (n_lo == topk), 1.0, done)
        return lo, hi, n_lo, n_hi, done, again

    lo, hi, n_lo, n_hi, done, _ = lax.while_loop(
        lambda carry: carry[5] > 0, bisect, (lo, hi, n_lo, n_hi, done, not_finished(done)))

    thr = lo
    exact = n_lo == topk
    need = topk - n_hi
    all_keys = jnp.full((1, tq), 2 ** 30, jnp.int32)

    def tie_break():
        def mark(c, _):
            tie_ref[pl.ds(pl.multiple_of(c * kc, kc), kc), :] = jnp.where(
                chunk(c) == thr, key_index(c), 2 ** 30)
            return 0
        lax.fori_loop(0, nk, mark, 0)

        def step(i, j):
            cand = j + lax.shift_right_logical(jnp.int32(2048), i)

            def body(c, acc):
                tied = tie_ref[pl.ds(pl.multiple_of(c * kc, kc), kc), :]
                return acc + ones_where(tied < cand)
            n_before = lax.fori_loop(0, nk, body, jnp.zeros((FOLD_ROWS, tq), F32)).sum(axis=0, keepdims=True)
            return jnp.where(n_before < need, cand, j)
        return lax.fori_loop(0, 12, step, jnp.zeros((1, tq), jnp.int32))

    any_tie = jnp.min(jnp.where(exact, 1.0, 0.0)) < 0.5
    last_tied = lax.cond(any_tie, tie_break, lambda: all_keys)
    last_tied = jnp.where(exact, all_keys, last_tied)

    def write(c, _):
        s = chunk(c)
        idx = key_index(c)
        keep = ((s > thr) | ((s == thr) & (idx <= last_tied))) & (idx <= qpos)
        sc_ref[pl.ds(pl.multiple_of(c * kc, kc), kc), :] = jnp.where(keep, 0.0, MASK_BIAS)
        return 0

    lax.fori_loop(0, nk, write, 0)


def _prompt_attn_kernel(qt_ref, qit_ref, wit_ref, kg_ref, vt_ref, kib_ref, o_ref,
                        sc_ref, tie_ref, acc_ref, m_ref, mx_ref, s_ref):
    i = pl.program_id(1)
    tq = qt_ref.shape[1]
    qpos = i * tq + lax.broadcasted_iota(jnp.int32, (1, tq), 1)
    n_keys = (i + 1) * tq

    n_score = (n_keys + SCORE_CHUNK - 1) // SCORE_CHUNK

    def score_chunk(c, stats):
        start = pl.multiple_of(c * SCORE_CHUNK, SCORE_CHUNK)
        kc = kib_ref[pl.ds(start, SCORE_CHUNK), :]
        acc = jnp.zeros((SCORE_CHUNK, tq), F32)
        for h in range(IDX_HEADS):
            d = _dot(kc, qit_ref[h * IDX_DIM:(h + 1) * IDX_DIM, :])
            acc = acc + jnp.maximum(d, 0.0) * wit_ref[h:h + 1, :]
        idx = start + lax.broadcasted_iota(jnp.int32, (SCORE_CHUNK, tq), 0)
        s = jnp.where(idx <= qpos, acc, -jnp.inf)
        sc_ref[pl.ds(start, SCORE_CHUNK), :] = s
        return _stats_update(stats, s)

    stats = lax.fori_loop(0, n_score, score_chunk, _stats_init(tq))

    _select_to_bias(sc_ref, tie_ref, qpos, n_score, SCORE_CHUNK, stats)

    m_ref[...] = jnp.full(m_ref.shape, MASK_BIAS, F32)
    acc_ref[...] = jnp.zeros(acc_ref.shape, F32)

    def attn_chunk(c, _):
        start = pl.multiple_of(c * ATTN_CHUNK, ATTN_CHUNK)
        for h in range(N_HEADS):
            s = _dot(kg_ref[h // HEADS_PER_KV, pl.ds(start, ATTN_CHUNK), :],
                     qt_ref[h * HEAD_DIM:(h + 1) * HEAD_DIM, :])
            s = s + sc_ref[pl.ds(start, ATTN_CHUNK), :]
            s_ref[h] = s
            mx_ref[h] = s.max(axis=0, keepdims=True)
        for h in range(N_HEADS):
            g = h // HEADS_PER_KV
            m_old = m_ref[h]
            m_new = jnp.maximum(m_old, mx_ref[h])
            alpha = jnp.exp2(m_old - m_new)
            p = jnp.exp2(s_ref[h] - m_new).astype(BF16)
            vt = vt_ref[c, g * V_ROWS:(g + 1) * V_ROWS, :]
            acc_ref[h] = alpha * acc_ref[h] + _dot(vt, p)
            m_ref[h] = m_new
        return 0

    lax.fori_loop(0, (i + 1) * (tq // ATTN_CHUNK), attn_chunk, 0)

    for pair in range(N_HEADS // 2):
        two = []
        for h in (2 * pair, 2 * pair + 1):
            a = acc_ref[h]
            two.append(a[0:HEAD_DIM, :] / a[HEAD_DIM:HEAD_DIM + 1, :])
        col = 2 * pair * HEAD_DIM
        o_ref[:, col:col + 2 * HEAD_DIM] = jnp.concatenate(two, axis=0).T.astype(BF16)


def _prompt_attention(qt, qit, wit, kg, vt, kib):
    nb, _, t = qt.shape
    tq = Q_TILE
    col = lambda height: pl.BlockSpec((None, height, tq), lambda b, i: (b, 0, i))
    return pl.pallas_call(
        _prompt_attn_kernel,
        grid=(nb, t // tq),
        in_specs=[
            col(Q_COLS), col(IQ_COLS), col(IDX_HEADS),
            pl.BlockSpec((None, N_KV_HEADS, t, HEAD_DIM), lambda b, i: (b, 0, 0, 0)),
            pl.BlockSpec((None, t // ATTN_CHUNK, N_KV_HEADS * V_ROWS, ATTN_CHUNK), lambda b, i: (b, 0, 0, 0)),
            pl.BlockSpec((None, t, IDX_DIM), lambda b, i: (b, 0, 0)),
        ],
        out_specs=pl.BlockSpec((None, tq, Q_COLS), lambda b, i: (b, i, 0)),
        out_shape=jax.ShapeDtypeStruct((nb, t, Q_COLS), BF16),
        scratch_shapes=[
            pltpu.VMEM((t, tq), F32),
            pltpu.VMEM((t, tq), jnp.int32),
            pltpu.VMEM((N_HEADS, V_ROWS, tq), F32),
            pltpu.VMEM((N_HEADS, 1, tq), F32),
            pltpu.VMEM((N_HEADS, 1, tq), F32),
            pltpu.VMEM((N_HEADS, ATTN_CHUNK, tq), F32),
        ],
        compiler_params=_params("arbitrary", "arbitrary"),
        name="prompt_attention",
    )(qt, qit, wit, kg, vt, kib)


def _sample_score_kernel(pt_ref, qi_ref, w_ref, *rest):
    del pt_ref
    n_pages = len(rest) - 3
    pages, new_ref, o_ref, kall_ref = rest[:n_pages], rest[n_pages], rest[n_pages + 1], rest[n_pages + 2]
    for j in range(n_pages):
        kall_ref[:, j * PAGE_SIZE:(j + 1) * PAGE_SIZE] = pages[j][...].astype(BF16)
    kall_ref[:, n_pages * PAGE_SIZE:(n_pages + 1) * PAGE_SIZE] = new_ref[...].astype(BF16)
    d = _dot(qi_ref[...], kall_ref[...])
    r = jnp.maximum(d, 0.0) * w_ref[...]
    o_ref[...] = r.reshape(IDX_HEADS, SUBLANES, r.shape[1]).sum(axis=0)


def _sample_scores(page_table, qi8, w8, cache_ki_t, ki_new_t):
    nb, n_pages = page_table.shape
    n_keys = (n_pages + 1) * PAGE_SIZE
    rows = IDX_HEADS * SUBLANES
    grid_spec = pltpu.PrefetchScalarGridSpec(
        num_scalar_prefetch=1,
        grid=(nb,),
        in_specs=[pl.BlockSpec((None, rows, IDX_DIM), lambda b, pt: (b, 0, 0)),
                  pl.BlockSpec((None, rows, 1), lambda b, pt: (b, 0, 0))]
        + [pl.BlockSpec((None, IDX_DIM, PAGE_SIZE), lambda b, pt, j=j: (pt[b, j], 0, 0))
           for j in range(n_pages)]
        + [pl.BlockSpec((None, IDX_DIM, PAGE_SIZE), lambda b, pt: (b, 0, 0))],
        out_specs=pl.BlockSpec((None, SUBLANES, n_keys), lambda b, pt: (b, 0, 0)),
        scratch_shapes=[pltpu.VMEM((IDX_DIM, n_keys), BF16)],
    )
    return pl.pallas_call(
        _sample_score_kernel,
        grid_spec=grid_spec,
        out_shape=jax.ShapeDtypeStruct((nb, SUBLANES, n_keys), F32),
        compiler_params=_params("arbitrary"),
        name="sample_indexer_scores",
    )(page_table, qi8, w8, *([cache_ki_t] * n_pages), ki_new_t)


def _sample_select_kernel(s_ref, o_ref, sc_ref, tie_ref, *, past, period):
    tq = s_ref.shape[1]
    n_keys = s_ref.shape[0]
    lane = lax.broadcasted_iota(jnp.int32, (1, tq), 1)
    qpos = past + (lane & (period - 1))
    nk = n_keys // PAGE_SIZE
    stats = _stats_init(tq)
    for c in range(nk):
        idx = c * PAGE_SIZE + lax.broadcasted_iota(jnp.int32, (PAGE_SIZE, tq), 0)
        s = jnp.where(idx <= qpos, s_ref[c * PAGE_SIZE:(c + 1) * PAGE_SIZE, :], -jnp.inf)
        sc_ref[c * PAGE_SIZE:(c + 1) * PAGE_SIZE, :] = s
        stats = _stats_update(stats, s)
    _select_to_bias(sc_ref, tie_ref, qpos, nk, PAGE_SIZE, stats)
    o_ref[...] = sc_ref[...]


def _sample_select(scores_t, past, period):
    n_keys, nq = scores_t.shape
    tq = Q_TILE
    blk = pl.BlockSpec((n_keys, tq), lambda i: (0, i))
    return pl.pallas_call(
        functools.partial(_sample_select_kernel, past=past, period=period),
        grid=(nq // tq,),
        in_specs=[blk],
        out_specs=blk,
        out_shape=jax.ShapeDtypeStruct((n_keys, nq), F32),
        scratch_shapes=[pltpu.VMEM((n_keys, tq), F32), pltpu.VMEM((n_keys, tq), jnp.int32)],
        compiler_params=_params("arbitrary"),
        name="sample_select",
    )(scores_t)


def _sample_attn_kernel(pt_ref, q_ref, bias_ref, *rest):
    del pt_ref
    n_pages = (len(rest) - 8) // 2
    kpages, vpages = rest[:n_pages], rest[n_pages:2 * n_pages]
    knew_ref, vnew_ref, o_ref, kall_ref, vall_ref, s_ref, p_ref, l_ref = rest[2 * n_pages:]
    for j in range(n_pages):
        kall_ref[:, :, j * PAGE_SIZE:(j + 1) * PAGE_SIZE] = kpages[j][...].astype(BF16)
        vall_ref[:, :, j * PAGE_SIZE:(j + 1) * PAGE_SIZE] = vpages[j][...].astype(BF16)
    fill = jnp.zeros((PAGE_SIZE - SUBLANES, LANES), F32)
    for g in range(N_KV_HEADS):
        for new_ref, all_ref in ((knew_ref, kall_ref), (vnew_ref, vall_ref)):
            page = jnp.concatenate([new_ref[g], fill], axis=0).T
            all_ref[g, :, n_pages * PAGE_SIZE:(n_pages + 1) * PAGE_SIZE] = page[0:HEAD_DIM, :].astype(BF16)
    bias = bias_ref[...]
    n_keys = bias.shape[1]
    bias = jnp.broadcast_to(bias[None], (HEADS_PER_KV, SUBLANES, n_keys)).reshape(
        HEADS_PER_KV * SUBLANES, n_keys)
    for g in range(N_KV_HEADS):
        s_ref[g] = _dot(q_ref[g], kall_ref[g]) + bias
    for g in range(N_KV_HEADS):
        s = s_ref[g]
        m = s.max(axis=1, keepdims=True)
        p = jnp.exp(s - m)
        l_ref[g] = p.sum(axis=1, keepdims=True)
        p_ref[g] = p.astype(BF16)
    for g in range(N_KV_HEADS):
        o_ref[g] = _dot_nt(p_ref[g], vall_ref[g]) / l_ref[g]


def _sample_attention(page_table, q8, bias8, cache_k_t, cache_v_t, k_new_t, v_new_t):
    nb, n_pages = page_table.shape
    n_keys = (n_pages + 1) * PAGE_SIZE
    rows = HEADS_PER_KV * SUBLANES
    page = [pl.BlockSpec((None, N_KV_HEADS, HEAD_DIM, PAGE_SIZE), lambda b, pt, j=j: (pt[b, j], 0, 0, 0))
            for j in range(n_pages)]
    new = pl.BlockSpec((None, N_KV_HEADS, SUBLANES, LANES), lambda b, pt: (b, 0, 0, 0))
    qo = pl.BlockSpec((None, N_KV_HEADS, rows, HEAD_DIM), lambda b, pt: (b, 0, 0, 0))
    grid_spec = pltpu.PrefetchScalarGridSpec(
        num_scalar_prefetch=1,
        grid=(nb,),
        in_specs=[qo, pl.BlockSpec((None, SUBLANES, n_keys), lambda b, pt: (b, 0, 0))]
        + page + page + [new, new],
        out_specs=qo,
        scratch_shapes=[pltpu.VMEM((N_KV_HEADS, HEAD_DIM, n_keys), BF16),
                        pltpu.VMEM((N_KV_HEADS, HEAD_DIM, n_keys), BF16),
                        pltpu.VMEM((N_KV_HEADS, rows, n_keys), F32),
                        pltpu.VMEM((N_KV_HEADS, rows, n_keys), BF16),
                        pltpu.VMEM((N_KV_HEADS, rows, 1), F32)],
    )
    return pl.pallas_call(
        _sample_attn_kernel,
        grid_spec=grid_spec,
        out_shape=jax.ShapeDtypeStruct((nb, N_KV_HEADS, rows, HEAD_DIM), F32),
        compiler_params=_params("arbitrary"),
        name="sample_attention",
    )(page_table, q8, bias8, *([cache_k_t] * n_pages), *([cache_v_t] * n_pages), k_new_t, v_new_t)


def _ffn(h, wgu_ref, wd_ref):
    acc = jnp.zeros((h.shape[0], D_MODEL), F32)
    for c in range(D_FF // FFN_CHUNK):
        gate = _dot(h, wgu_ref[:, c * FFN_CHUNK:(c + 1) * FFN_CHUNK])
        up = _dot(h, wgu_ref[:, D_FF + c * FFN_CHUNK:D_FF + (c + 1) * FFN_CHUNK])
        a = _silu(gate) * up
        acc = acc + _dot(a.astype(BF16), wd_ref[c * FFN_CHUNK:(c + 1) * FFN_CHUNK, :])
    return acc


def _row_specs(nb, rows, per_row_mod):
    tm = min(ROW_TILE, rows)
    row = pl.BlockSpec((None, tm, D_MODEL), lambda b, j: (b, j, 0))
    if per_row_mod:
        mod = row
    else:
        mod = pl.BlockSpec((None, 1, D_MODEL), lambda b, j: (b, 0, 0))
    return tm, row, mod


def _attn_out_ffn_kernel(x_ref, o_ref, g1_ref, sc2_ref, sh2_ref, g2_ref, n2_ref,
                         wo_ref, wgu_ref, wd_ref, y_ref):
    y1 = x_ref[...] + g1_ref[...] * _dot(o_ref[...], wo_ref[...])
    h2 = _norm_mod(y1, n2_ref[...], sc2_ref[...], sh2_ref[...]).astype(BF16)
    y_ref[...] = y1 + g2_ref[...] * _ffn(h2, wgu_ref, wd_ref)


def _layer_spec(stacked, layer):
    _, rows, cols = stacked.shape
    return pl.BlockSpec((None, rows, cols), lambda *_: (layer, 0, 0), pipeline_mode=pl.Buffered(1))


def _attn_out_ffn(x, o, g1, sc2, sh2, g2, n2, wo, wgu, wd, layer):
    nb, rows, _ = x.shape
    tm, row, mod = _row_specs(nb, rows, g1.shape[1] == rows)
    return pl.pallas_call(
        _attn_out_ffn_kernel,
        grid=(nb, rows // tm),
        in_specs=[row, row, mod, mod, mod, mod, _const_spec((1, D_MODEL)),
                  _const_spec(wo.shape), _layer_spec(wgu, layer), _layer_spec(wd, layer)],
        out_specs=row,
        out_shape=jax.ShapeDtypeStruct(x.shape, F32),
        compiler_params=_params("arbitrary", "arbitrary"),
        name="attn_out_ffn",
    )(x, o, g1, sc2, sh2, g2, n2, wo, wgu, wd)


def _conv_layer_kernel(x_ref, sc1_ref, sh1_ref, g1_ref, sc2_ref, sh2_ref, g2_ref, p1_ref, p2_ref,
                       n1_ref, n2_ref, nf_ref, win_ref, ck_ref, wout_ref, wgu_ref, wd_ref,
                       y_ref, tail_ref, ubuf_ref, z_ref, *, seg):
    j = pl.program_id(1)
    tm = x_ref.shape[0]
    x = x_ref[...]
    h = _norm_mod(x, n1_ref[...], sc1_ref[...], sh1_ref[...]).astype(BF16)

    @pl.when(j == 0)
    def _():
        ubuf_ref[0:SUBLANES, :] = jnp.zeros((SUBLANES, D_MODEL), F32)

    @pl.when(j > 0)
    def _():
        ubuf_ref[0:SUBLANES, :] = ubuf_ref[tm:tm + SUBLANES, :]

    t = (j * tm + lax.broadcasted_iota(jnp.int32, (tm, 1), 0)) & (seg - 1)
    ck = ck_ref[...]
    cw = FFN_CHUNK
    for c in range(D_MODEL // cw):
        cols = slice(c * cw, (c + 1) * cw)
        bg = _dot(h, win_ref[:, c * cw:(c + 1) * cw])
        cg = _dot(h, win_ref[:, D_MODEL + c * cw:D_MODEL + (c + 1) * cw])
        hv = _dot(h, win_ref[:, 2 * D_MODEL + c * cw:2 * D_MODEL + (c + 1) * cw])
        u = cg * hv
        ubuf_ref[SUBLANES:SUBLANES + tm, cols] = u
        um1 = jnp.where(t >= 1, ubuf_ref[SUBLANES - 1:SUBLANES - 1 + tm, cols], p1_ref[:, cols])
        um2 = jnp.where(t >= 2, ubuf_ref[SUBLANES - 2:SUBLANES - 2 + tm, cols], p2_ref[:, cols])
        conv = ck[0:1, cols] * um2 + ck[1:2, cols] * um1 + ck[2:3, cols] * u
        z_ref[:, cols] = (bg * conv).astype(BF16)
    r = tail_ref.shape[0]
    tail_ref[...] = ubuf_ref[SUBLANES + tm - r:SUBLANES + tm, :]

    y1 = x + g1_ref[...] * _dot(z_ref[...], wout_ref[...])
    h2 = _norm_mod(y1, n2_ref[...], sc2_ref[...], sh2_ref[...]).astype(BF16)
    y2 = y1 + g2_ref[...] * _ffn(h2, wgu_ref, wd_ref)
    ms = jnp.mean(y2 * y2, axis=-1, keepdims=True)
    y_ref[...] = (y2 * lax.rsqrt(ms + EPS)) * nf_ref[...]


def _conv_layer(x, mods, p1, p2, n1, n2, nf, win, ck, wout, wgu, wd, *, layer, seg, full_tail):
    nb, rows, _ = x.shape
    per_row = mods[0].shape[1] == rows
    tm, row, mod = _row_specs(nb, rows, per_row)
    if per_row:
        prefix = row
    else:
        prefix = pl.BlockSpec((None, 1, D_MODEL), lambda b, j: (0, 0, 0))
    if full_tail:
        tail_spec, tail_rows = row, rows
    else:
        tail_spec, tail_rows = pl.BlockSpec((None, SUBLANES, D_MODEL), lambda b, j: (b, 0, 0)), SUBLANES
    vec = _const_spec((1, D_MODEL))
    return pl.pallas_call(
        functools.partial(_conv_layer_kernel, seg=seg),
        grid=(nb, rows // tm),
        in_specs=[row] + [mod] * 6 + [prefix, prefix, vec, vec, vec,
                                      _const_spec(win.shape), _const_spec(ck.shape), _const_spec(wout.shape),
                                      _layer_spec(wgu, layer), _layer_spec(wd, layer)],
        out_specs=(row, tail_spec),
        out_shape=(jax.ShapeDtypeStruct(x.shape, F32),
                   jax.ShapeDtypeStruct((nb, tail_rows, D_MODEL), F32)),
        scratch_shapes=[pltpu.VMEM((tm + SUBLANES, D_MODEL), F32), pltpu.VMEM((tm, D_MODEL), BF16)],
        compiler_params=_params("arbitrary", "arbitrary"),
        name="conv_layer",
    )(x, *mods, p1, p2, n1, n2, nf, win, ck, wout, wgu, wd)


def _rope_tables(pos):
    half = ROT_DIM // 2
    inv = ROPE_THETA ** (-jnp.arange(0, ROT_DIM, 2, dtype=F32) / ROT_DIM)
    ang = pos[:, None] * inv[None, :]
    cos, sin = jnp.cos(ang), jnp.sin(ang)
    n = pos.shape[0]
    rest = HEAD_DIM - ROT_DIM
    c = jnp.concatenate([cos, cos, jnp.ones((n, rest), F32)], axis=1)
    s1 = jnp.concatenate([-sin, jnp.zeros((n, half + rest), F32)], axis=1)
    s2 = jnp.concatenate([jnp.zeros((n, half), F32), sin, jnp.zeros((n, rest), F32)], axis=1)
    return tuple(jnp.tile(a, (1, LANES // HEAD_DIM)) for a in (c, s1, s2))


def kernel(x_prompt, x_sample, c_prompt, c_sample, cache_k, cache_v, cache_kidx, state_conv, page_table,
           ada_w, ada_b, norm1_g, norm2_g, final_g, attn_w_in, attn_w_out, conv_w_in, conv_k, conv_w_out,
           ffn_w_gu, ffn_w_down):
    nb, t, _ = x_prompt.shape
    ns, tn, _ = x_sample.shape
    n_pages = page_table.shape[1]
    past = n_pages * PAGE_SIZE
    n_phys = cache_k.shape[1]
    rows_s = ns * tn

    pad = (-(nb + ns)) % SUBLANES
    c_all = jnp.concatenate([c_prompt, c_sample, jnp.zeros((pad, D_MODEL), F32)], axis=0)
    mod = _modulation(c_all, ada_w, ada_b)

    def mods(layer):
        m = mod[layer].reshape(-1, 6, D_MODEL)
        prompt = [m[:nb, i][:, None, :] for i in range(6)]
        sample = [jnp.repeat(m[nb:nb + ns, i], tn, axis=0)[None] for i in range(6)]
        return prompt, sample

    vec = lambda a: a.reshape(1, D_MODEL)
    w_in = jnp.pad(attn_w_in[0], ((0, 0), (0, ATTN_IN_PAD - ATTN_IN))).astype(BF16)
    w_o = attn_w_out[0].astype(BF16)
    wgu = ffn_w_gu.astype(BF16)
    wd = ffn_w_down.astype(BF16)

    (sh1p, sc1p, g1p, sh2p, sc2p, g2p), (sh1s, sc1s, g1s, sh2s, sc2s, g2s) = mods(0)
    rope_p = _rope_tables(jnp.arange(t, dtype=F32))
    k_p, v_p, ki_p, qt, qit, kg, vt, kib, wit = _proj_prompt(
        x_prompt, sc1p, sh1p, vec(norm1_g[0]), w_in, rope_p)
    o_p = _prompt_attention(qt, qit, wit, kg, vt, kib)
    y_p = _attn_out_ffn(x_prompt, o_p, g1p, sc2p, sh2p, g2p, vec(norm2_g[0]), w_o, wgu, wd, 0)

    pos_s = jnp.tile(past + jnp.arange(tn, dtype=F32), ns)
    rope_s = _rope_tables(pos_s)
    xs = x_sample.reshape(rows_s, D_MODEL)
    proj = _proj_sample(xs, sc1s[0], sh1s[0], vec(norm1_g[0]), w_in, rope_s)
    o = 0
    q_s = proj[:, o:o + Q_COLS].reshape(ns, tn, N_KV_HEADS, HEADS_PER_KV, HEAD_DIM); o += Q_COLS
    k_s = proj[:, o:o + KV_COLS].reshape(ns, tn, KV_COLS); o += KV_COLS
    v_s = proj[:, o:o + KV_COLS].reshape(ns, tn, KV_COLS); o += KV_COLS
    qi_s = proj[:, o:o + IQ_COLS].reshape(ns, tn, IDX_HEADS, IDX_DIM); o += IQ_COLS
    ki_s = proj[:, o:o + IDX_DIM].reshape(ns, tn, IDX_DIM); o += IDX_DIM
    wi_s = proj[:, o:o + IDX_HEADS].reshape(ns, tn, IDX_HEADS)

    qpad = ((0, 0), (0, SUBLANES - tn))
    qi8 = jnp.pad(qi_s, qpad + ((0, 0), (0, 0))).transpose(0, 2, 1, 3)
    qi8 = qi8.reshape(ns, IDX_HEADS * SUBLANES, IDX_DIM).astype(BF16)
    w8 = jnp.pad(wi_s, qpad + ((0, 0),)).transpose(0, 2, 1).reshape(ns, IDX_HEADS * SUBLANES, 1)
    slot_pad = (0, PAGE_SIZE - tn)
    ki_new_t = jnp.pad(ki_s.transpose(0, 2, 1), ((0, 0), (0, 0), slot_pad))
    scores = _sample_scores(page_table, qi8, w8, cache_kidx[0].transpose(0, 2, 1), ki_new_t)
    n_keys = scores.shape[2]
    scores_t = scores[:, :tn, :].transpose(2, 0, 1).reshape(n_keys, rows_s)
    bias_t = _sample_select(scores_t, past, tn)
    bias8 = jnp.pad(bias_t.reshape(n_keys, ns, tn).transpose(1, 2, 0), qpad + ((0, 0),))

    q8 = jnp.pad(q_s * HEAD_DIM ** -0.5, qpad + ((0, 0), (0, 0), (0, 0))).transpose(0, 2, 3, 1, 4)
    q8 = q8.reshape(ns, N_KV_HEADS, HEADS_PER_KV * SUBLANES, HEAD_DIM).astype(BF16)
    new_t = lambda a: jnp.pad(a.reshape(ns, tn, N_KV_HEADS, HEAD_DIM).transpose(0, 2, 1, 3),
                              ((0, 0), (0, 0), (0, SUBLANES - tn), (0, LANES - HEAD_DIM)))
    o_g = _sample_attention(page_table, q8, bias8,
                            cache_k[0].transpose(0, 2, 3, 1), cache_v[0].transpose(0, 2, 3, 1),
                            new_t(k_s), new_t(v_s))
    o_g = o_g.reshape(ns, N_KV_HEADS, HEADS_PER_KV, SUBLANES, HEAD_DIM)[:, :, :, :tn]
    o_s = o_g.transpose(0, 3, 1, 2, 4).reshape(1, rows_s, Q_COLS).astype(BF16)
    y_s = _attn_out_ffn(xs[None], o_s, g1s, sc2s, sh2s, g2s, vec(norm2_g[0]), w_o, wgu, wd, 0)

    (sh1p, sc1p, g1p, sh2p, sc2p, g2p), (sh1s, sc1s, g1s, sh2s, sc2s, g2s) = mods(1)
    win = conv_w_in[0].astype(BF16)
    wout = conv_w_out[0].astype(BF16)
    zero_prefix = jnp.zeros((1, 1, D_MODEL), F32)
    out_p, tail_p = _conv_layer(
        y_p, (sc1p, sh1p, g1p, sc2p, sh2p, g2p), zero_prefix, zero_prefix,
        vec(norm1_g[1]), vec(norm2_g[1]), vec(final_g), win, conv_k[0], wout, wgu, wd,
        layer=1, seg=t, full_tail=False)
    st = state_conv[0]
    zeros_row = jnp.zeros((ns, 1, D_MODEL), F32)
    p1 = jnp.concatenate([st[:, 1:2], zeros_row, zeros_row, zeros_row], axis=1).reshape(1, rows_s, D_MODEL)
    p2 = jnp.concatenate([st[:, 0:1], st[:, 1:2], zeros_row, zeros_row], axis=1).reshape(1, rows_s, D_MODEL)
    out_s, u_s = _conv_layer(
        y_s, (sc1s, sh1s, g1s, sc2s, sh2s, g2s), p1, p2,
        vec(norm1_g[1]), vec(norm2_g[1]), vec(final_g), win, conv_k[0], wout, wgu, wd,
        layer=1, seg=tn, full_tail=True)

    keep = CONV_WIDTH - 1
    return (
        out_p,
        out_s.reshape(ns, tn, D_MODEL),
        k_p.reshape(1, nb, t, N_KV_HEADS, HEAD_DIM),
        v_p.reshape(1, nb, t, N_KV_HEADS, HEAD_DIM),
        ki_p[None],
        tail_p[None, :, SUBLANES - keep:, :],
        k_s.reshape(1, ns, tn, N_KV_HEADS, HEAD_DIM),
        v_s.reshape(1, ns, tn, N_KV_HEADS, HEAD_DIM),
        ki_s[None],
        u_s.reshape(ns, tn, D_MODEL)[None, :, tn - keep:, :],
    )
```

```python
import functools

import jax
import jax.numpy as jnp
from jax import lax
from jax.experimental import pallas as pl
from jax.experimental.pallas import tpu as pltpu

F32 = jnp.float32
BF16 = jnp.bfloat16

D_MODEL = 1024
N_HEADS = 16
HEAD_DIM = 64
N_KV_HEADS = 4
HEADS_PER_KV = N_HEADS // N_KV_HEADS
ROT_DIM = 16
ROPE_THETA = 500000.0
IDX_HEADS = 8
IDX_DIM = 64
TOPK = 256
PAGE_SIZE = 128
CONV_WIDTH = 3
D_FF = 2816
EPS = 1e-6
Q_COLS = N_HEADS * HEAD_DIM
KV_COLS = N_KV_HEADS * HEAD_DIM
IQ_COLS = IDX_HEADS * IDX_DIM
ATTN_IN = Q_COLS + 2 * KV_COLS + IQ_COLS + IDX_DIM + IDX_HEADS
ATTN_IN_PAD = 2176
LANES = 128
SUBLANES = 8
MASK_BIAS = -1e30
F32_MAX = 3.4028234663852886e38
LOG2_E = 1.4426950408889634
VMEM_LIMIT = 60 * 1024 * 1024

ROW_TILE = 512
Q_TILE = 256
SCORE_CHUNK = 512
ATTN_CHUNK = 256
V_ROWS = HEAD_DIM + 16
FFN_CHUNK = 256


def _dot(a, b):
    return jnp.dot(a, b, preferred_element_type=F32)


def _dot_nt(a, b):
    return lax.dot_general(a, b, (((1,), (1,)), ((), ())), preferred_element_type=F32)


def _silu(x):
    return x / (1.0 + jnp.exp(-x))


def _norm_mod(x, g, sc, sh):
    ms = jnp.mean(x * x, axis=-1, keepdims=True)
    y = x * lax.rsqrt(ms + EPS)
    return (y * g) * (1.0 + sc) + sh


def _params(*sem, flags=None):
    return pltpu.CompilerParams(dimension_semantics=sem, vmem_limit_bytes=VMEM_LIMIT, flags=flags)


def _const_spec(shape):
    nd = len(shape)
    return pl.BlockSpec(shape, lambda *_: (0,) * nd, pipeline_mode=pl.Buffered(1))


def _mod_kernel(c_ref, w_ref, b_ref, o_ref):
    s = _silu(c_ref[...]).astype(BF16)
    o_ref[...] = _dot(s, w_ref[...].astype(BF16)) + b_ref[...]


def _modulation(c_all, ada_w, ada_b):
    depth, _, n = ada_w.shape
    rows = c_all.shape[0]
    tn = 1024
    return pl.pallas_call(
        _mod_kernel,
        grid=(depth, n // tn),
        in_specs=[
            pl.BlockSpec((rows, D_MODEL), lambda l, j: (0, 0)),
            pl.BlockSpec((None, D_MODEL, tn), lambda l, j: (l, 0, j)),
            pl.BlockSpec((None, 1, tn), lambda l, j: (l, 0, j)),
        ],
        out_specs=pl.BlockSpec((None, rows, tn), lambda l, j: (l, 0, j)),
        out_shape=jax.ShapeDtypeStruct((depth, rows, n), F32),
        compiler_params=_params("arbitrary", "arbitrary"),
        name="adaln_modulation",
    )(c_all, ada_w, ada_b.reshape(depth, 1, n))


def _rope_block(x, c, s1, s2):
    return x * c + pltpu.roll(x, LANES - ROT_DIM // 2, 1) * s1 + pltpu.roll(x, ROT_DIM // 2, 1) * s2


def _rope_cols(p, c, s1, s2):
    nblk = p.shape[1] // LANES
    return [_rope_block(p[:, j * LANES:(j + 1) * LANES], c, s1, s2) for j in range(nblk)]


def _last_block_tables(c, s1, s2):
    lane = lax.broadcasted_iota(jnp.int32, c.shape, 1)
    is_key = lane < IDX_DIM
    return jnp.where(is_key, c, 1.0), jnp.where(is_key, s1, 0.0), jnp.where(is_key, s2, 0.0)


def _proj_prompt_kernel(x_ref, sc_ref, sh_ref, g_ref, w_ref, c_ref, s1_ref, s2_ref,
                        k_ref, v_ref, ki_ref, qt_ref, qit_ref, kg_ref, vt_ref, kib_ref, wit_ref):
    h = _norm_mod(x_ref[...], g_ref[...], sc_ref[...], sh_ref[...]).astype(BF16)
    c, s1, s2 = c_ref[...], s1_ref[...], s2_ref[...]

    pq = _dot(h, w_ref[:, 0:Q_COLS])
    for j, blk in enumerate(_rope_cols(pq, c, s1, s2)):
        qt_ref[j * LANES:(j + 1) * LANES, :] = (blk * (HEAD_DIM ** -0.5 * LOG2_E)).astype(BF16).T

    pk = _dot(h, w_ref[:, Q_COLS:Q_COLS + KV_COLS])
    k = jnp.concatenate(_rope_cols(pk, c, s1, s2), axis=1)
    k_ref[...] = k
    for g in range(N_KV_HEADS):
        kg_ref[g] = k[:, g * HEAD_DIM:(g + 1) * HEAD_DIM].astype(BF16)

    pv = _dot(h, w_ref[:, Q_COLS + KV_COLS:Q_COLS + 2 * KV_COLS])
    v_ref[...] = pv
    ones = jnp.ones((V_ROWS - HEAD_DIM, ATTN_CHUNK), BF16)
    for cc in range(pv.shape[0] // ATTN_CHUNK):
        vt = pv[cc * ATTN_CHUNK:(cc + 1) * ATTN_CHUNK, :].astype(BF16).T
        for g in range(N_KV_HEADS):
            vt_ref[cc, g * V_ROWS:g * V_ROWS + HEAD_DIM, :] = vt[g * HEAD_DIM:(g + 1) * HEAD_DIM, :]
            vt_ref[cc, g * V_ROWS + HEAD_DIM:(g + 1) * V_ROWS, :] = ones

    o = Q_COLS + 2 * KV_COLS
    pqi = _dot(h, w_ref[:, o:o + IQ_COLS])
    for j, blk in enumerate(_rope_cols(pqi, c, s1, s2)):
        qit_ref[j * LANES:(j + 1) * LANES, :] = blk.astype(BF16).T

    pl_ = _dot(h, w_ref[:, o + IQ_COLS:ATTN_IN_PAD])
    last = _rope_block(pl_, *_last_block_tables(c, s1, s2))
    ki = last[:, 0:IDX_DIM]
    ki_ref[...] = ki
    kib_ref[...] = ki.astype(BF16)
    wit_ref[...] = last.T[IDX_DIM:IDX_DIM + IDX_HEADS, :]


def _proj_prompt(x, sc, sh, g, w, rope):
    nb, t, _ = x.shape
    tm = ROW_TILE
    row = lambda width: pl.BlockSpec((None, tm, width), lambda b, j: (b, j, 0))
    mod = pl.BlockSpec((None, 1, D_MODEL), lambda b, j: (b, 0, 0))
    tab = pl.BlockSpec((tm, LANES), lambda b, j: (j, 0))
    col = lambda height: pl.BlockSpec((None, height, tm), lambda b, j: (b, 0, j))
    out_shape = (
        jax.ShapeDtypeStruct((nb, t, KV_COLS), F32),
        jax.ShapeDtypeStruct((nb, t, KV_COLS), F32),
        jax.ShapeDtypeStruct((nb, t, IDX_DIM), F32),
        jax.ShapeDtypeStruct((nb, Q_COLS, t), BF16),
        jax.ShapeDtypeStruct((nb, IQ_COLS, t), BF16),
        jax.ShapeDtypeStruct((nb, N_KV_HEADS, t, HEAD_DIM), BF16),
        jax.ShapeDtypeStruct((nb, t // ATTN_CHUNK, N_KV_HEADS * V_ROWS, ATTN_CHUNK), BF16),
        jax.ShapeDtypeStruct((nb, t, IDX_DIM), BF16),
        jax.ShapeDtypeStruct((nb, IDX_HEADS, t), F32),
    )
    out_specs = (
        row(KV_COLS), row(KV_COLS), row(IDX_DIM), col(Q_COLS), col(IQ_COLS),
        pl.BlockSpec((None, N_KV_HEADS, tm, HEAD_DIM), lambda b, j: (b, 0, j, 0)),
        pl.BlockSpec((None, tm // ATTN_CHUNK, N_KV_HEADS * V_ROWS, ATTN_CHUNK), lambda b, j: (b, j, 0, 0)),
        row(IDX_DIM), col(IDX_HEADS),
    )
    return pl.pallas_call(
        _proj_prompt_kernel,
        grid=(nb, t // tm),
        in_specs=[row(D_MODEL), mod, mod, _const_spec((1, D_MODEL)),
                  _const_spec((D_MODEL, ATTN_IN_PAD)), tab, tab, tab],
        out_specs=out_specs,
        out_shape=out_shape,
        compiler_params=_params("arbitrary", "arbitrary"),
        name="attn_proj_prompt",
    )(x, sc, sh, g, w, *rope)


def _proj_sample_kernel(x_ref, sc_ref, sh_ref, g_ref, w_ref, c_ref, s1_ref, s2_ref, p_ref):
    h = _norm_mod(x_ref[...], g_ref[...], sc_ref[...], sh_ref[...]).astype(BF16)
    c, s1, s2 = c_ref[...], s1_ref[...], s2_ref[...]
    p = _dot(h, w_ref[...])
    v_lo = (Q_COLS + KV_COLS) // LANES
    v_hi = (Q_COLS + 2 * KV_COLS) // LANES
    nblk = ATTN_IN_PAD // LANES
    for j in range(nblk):
        blk = p[:, j * LANES:(j + 1) * LANES]
        if j == nblk - 1:
            blk = _rope_block(blk, *_last_block_tables(c, s1, s2))
        elif not (v_lo <= j < v_hi):
            blk = _rope_block(blk, c, s1, s2)
        p_ref[:, j * LANES:(j + 1) * LANES] = blk


def _proj_sample(x, sc, sh, g, w, rope):
    rows = x.shape[0]
    full = lambda width: pl.BlockSpec((rows, width), lambda i: (0, 0))
    return pl.pallas_call(
        _proj_sample_kernel,
        grid=(1,),
        in_specs=[full(D_MODEL), full(D_MODEL), full(D_MODEL),
                  pl.BlockSpec((1, D_MODEL), lambda i: (0, 0)),
                  pl.BlockSpec((D_MODEL, ATTN_IN_PAD), lambda i: (0, 0)),
                  full(LANES), full(LANES), full(LANES)],
        out_specs=full(ATTN_IN_PAD),
        out_shape=jax.ShapeDtypeStruct((rows, ATTN_IN_PAD), F32),
        compiler_params=_params("arbitrary"),
        name="attn_proj_sample",
    )(x, sc, sh, g, w, *rope)


FOLD_ROWS = 32


def _fold(x, op):
    rows, tq = x.shape
    x3 = x.reshape(rows // FOLD_ROWS, FOLD_ROWS, tq)
    return {"sum": x3.sum, "max": x3.max, "min": x3.min}[op](axis=0)


def _stats_init(tq):
    zeros = jnp.zeros((FOLD_ROWS, tq), F32)
    return (jnp.full((FOLD_ROWS, tq), -jnp.inf, F32), jnp.full((FOLD_ROWS, tq), jnp.inf, F32),
            zeros, zeros, zeros)


def _stats_update(stats, s):
    mx, mn, n_above, n_ge0, n_gt0 = stats
    ones_where = lambda m: _fold(jnp.where(m, 1.0, 0.0), "sum")
    above = s > -jnp.inf
    return (jnp.maximum(mx, _fold(s, "max")),
            jnp.minimum(mn, _fold(jnp.where(above, s, jnp.inf), "min")),
            n_above + ones_where(above), n_ge0 + ones_where(s >= 0.0), n_gt0 + ones_where(s > 0.0))


def _select_to_bias(sc_ref, tie_ref, qpos, nk, kc, stats):
    tq = sc_ref.shape[1]
    topk = float(TOPK)

    def chunk(c):
        return sc_ref[pl.ds(pl.multiple_of(c * kc, kc), kc), :]

    def key_index(c):
        return c * kc + lax.broadcasted_iota(jnp.int32, (kc, tq), 0)

    def ones_where(m):
        return _fold(jnp.where(m, 1.0, 0.0), "sum")

    def count(pred):
        def body(c, acc):
            return acc + ones_where(pred(chunk(c), c))
        acc = lax.fori_loop(0, nk, body, jnp.zeros((FOLD_ROWS, tq), F32))
        return acc.sum(axis=0, keepdims=True)

    mx, mn, n_above, n_ge0, n_gt0 = stats
    mx = mx.max(axis=0, keepdims=True)
    mn = mn.min(axis=0, keepdims=True)
    n_above = n_above.sum(axis=0, keepdims=True)
    n_ge0 = n_ge0.sum(axis=0, keepdims=True)
    n_gt0 = n_gt0.sum(axis=0, keepdims=True)
    n_max = count(lambda s, c: s >= mx)

    few = n_above < topk
    flat = n_max >= topk
    zero = (n_gt0 < topk) & (n_ge0 >= topk)
    pos = n_gt0 >= topk
    lo = jnp.where(few, -jnp.inf, jnp.where(flat, mx, jnp.where(zero | pos, 0.0, mn)))
    hi = jnp.where(few, mn, jnp.where(flat, jnp.inf, jnp.where(pos, mx, 0.0)))
    n_lo = jnp.where(few, 2.0 * topk, jnp.where(flat, n_max, jnp.where(zero | pos, n_ge0, n_above)))
    n_hi = jnp.where(few, n_above, jnp.where(flat, 0.0, jnp.where(zero, n_gt0, jnp.where(pos, n_max, n_ge0))))
    done = jnp.where(few | flat | zero | (n_lo == topk), 1.0, 0.0)

    def not_finished(d):
        return (jnp.min(d) < 0.5).astype(jnp.int32)

    def bisect(carry):
        lo, hi, n_lo, n_hi, done, _ = carry
        again = not_finished(done)
        mid = jnp.clip(0.5 * lo + 0.5 * hi, -F32_MAX, F32_MAX)
        stuck = (mid <= lo) | (mid >= hi)
        n_mid = count(lambda s, c: s >= mid)
        move = (done < 0.5) & jnp.logical_not(stuck)
        up = move & (n_mid >= topk)
        down = move & (n_mid < topk)
        lo = jnp.where(up, mid, lo)
        n_lo = jnp.where(up, n_mid, n_lo)
        hi = jnp.where(down, mid, hi)
        n_hi = jnp.where(down, n_mid, n_hi)
        done = jnp.where(stuck | (n_lo == topk), 1.0, done)
        return lo, hi, n_lo, n_hi, done, again

    lo, hi, n_lo, n_hi, done, _ = lax.while_loop(
        lambda carry: carry[5] > 0, bisect, (lo, hi, n_lo, n_hi, done, not_finished(done)))

    thr = lo
    exact = n_lo == topk
    need = topk - n_hi
    all_keys = jnp.full((1, tq), 2 ** 30, jnp.int32)

    def tie_break():
        def mark(c, _):
            tie_ref[pl.ds(pl.multiple_of(c * kc, kc), kc), :] = jnp.where(
                chunk(c) == thr, key_index(c), 2 ** 30)
            return 0
        lax.fori_loop(0, nk, mark, 0)

        n_bits = (sc_ref.shape[0] - 1).bit_length()

        def step(i, j):
            cand = j + lax.shift_right_logical(jnp.int32(1 << (n_bits - 1)), i)

            def body(c, acc):
                tied = tie_ref[pl.ds(pl.multiple_of(c * kc, kc), kc), :]
                return acc + ones_where(tied < cand)
            n_before = lax.fori_loop(0, nk, body, jnp.zeros((FOLD_ROWS, tq), F32)).sum(axis=0, keepdims=True)
            return jnp.where(n_before < need, cand, j)
        return lax.fori_loop(0, n_bits, step, jnp.zeros((1, tq), jnp.int32))

    any_tie = jnp.min(jnp.where(exact, 1.0, 0.0)) < 0.5

    def write(keep_fn):
        def body(c, _):
            sc_ref[pl.ds(pl.multiple_of(c * kc, kc), kc), :] = jnp.where(keep_fn(chunk(c), c), 0.0, MASK_BIAS)
            return 0
        return lax.fori_loop(0, nk, body, 0)

    def write_with_ties():
        last_tied = jnp.minimum(jnp.where(exact, all_keys, tie_break()), qpos)
        return write(lambda s, c: (s > thr) | ((s == thr) & (key_index(c) <= last_tied)))

    lax.cond(any_tie, write_with_ties, lambda: write(lambda s, c: s >= thr))


def _prompt_attn_kernel(qt_ref, qit_ref, wit_ref, kg_ref, vt_ref, kib_ref, o_ref,
                        sc_ref, tie_ref, acc_ref, m_ref, mx_ref, s_ref):
    i = pl.program_id(1)
    tq = qt_ref.shape[1]
    qpos = i * tq + lax.broadcasted_iota(jnp.int32, (1, tq), 1)
    n_keys = (i + 1) * tq

    n_score = (n_keys + SCORE_CHUNK - 1) // SCORE_CHUNK

    def score_chunk(c, stats):
        start = pl.multiple_of(c * SCORE_CHUNK, SCORE_CHUNK)
        kc = kib_ref[pl.ds(start, SCORE_CHUNK), :]
        acc = jnp.zeros((SCORE_CHUNK, tq), F32)
        for h in range(IDX_HEADS):
            d = _dot(kc, qit_ref[h * IDX_DIM:(h + 1) * IDX_DIM, :])
            acc = acc + jnp.maximum(d, 0.0) * wit_ref[h:h + 1, :]
        idx = start + lax.broadcasted_iota(jnp.int32, (SCORE_CHUNK, tq), 0)
        s = jnp.where(idx <= qpos, acc, -jnp.inf)
        sc_ref[pl.ds(start, SCORE_CHUNK), :] = s
        return _stats_update(stats, s)

    stats = lax.fori_loop(0, n_score, score_chunk, _stats_init(tq))

    _select_to_bias(sc_ref, tie_ref, qpos, n_score, SCORE_CHUNK, stats)

    m_ref[...] = jnp.full(m_ref.shape, MASK_BIAS, F32)
    acc_ref[...] = jnp.zeros(acc_ref.shape, F32)

    def attn_chunk(c, _):
        start = pl.multiple_of(c * ATTN_CHUNK, ATTN_CHUNK)
        for h in range(N_HEADS):
            s = _dot(kg_ref[h // HEADS_PER_KV, pl.ds(start, ATTN_CHUNK), :],
                     qt_ref[h * HEAD_DIM:(h + 1) * HEAD_DIM, :])
            s = s + sc_ref[pl.ds(start, ATTN_CHUNK), :]
            s_ref[h] = s
            mx_ref[h] = s.max(axis=0, keepdims=True)
        for h in range(N_HEADS):
            g = h // HEADS_PER_KV
            m_old = m_ref[h]
            m_new = jnp.maximum(m_old, mx_ref[h])
            alpha = jnp.exp2(m_old - m_new)
            p = jnp.exp2(s_ref[h] - m_new).astype(BF16)
            vt = vt_ref[c, g * V_ROWS:(g + 1) * V_ROWS, :]
            acc_ref[h] = alpha * acc_ref[h] + _dot(vt, p)
            m_ref[h] = m_new
        return 0

    lax.fori_loop(0, (i + 1) * (tq // ATTN_CHUNK), attn_chunk, 0)

    for pair in range(N_HEADS // 2):
        two = []
        for h in (2 * pair, 2 * pair + 1):
            a = acc_ref[h]
            two.append(a[0:HEAD_DIM, :] / a[HEAD_DIM:HEAD_DIM + 1, :])
        col = 2 * pair * HEAD_DIM
        o_ref[:, col:col + 2 * HEAD_DIM] = jnp.concatenate(two, axis=0).T.astype(BF16)


def _prompt_attention(qt, qit, wit, kg, vt, kib):
    nb, _, t = qt.shape
    tq = Q_TILE
    col = lambda height: pl.BlockSpec((None, height, tq), lambda b, i: (b, 0, i))
    return pl.pallas_call(
        _prompt_attn_kernel,
        grid=(nb, t // tq),
        in_specs=[
            col(Q_COLS), col(IQ_COLS), col(IDX_HEADS),
            pl.BlockSpec((None, N_KV_HEADS, t, HEAD_DIM), lambda b, i: (b, 0, 0, 0)),
            pl.BlockSpec((None, t // ATTN_CHUNK, N_KV_HEADS * V_ROWS, ATTN_CHUNK), lambda b, i: (b, 0, 0, 0)),
            pl.BlockSpec((None, t, IDX_DIM), lambda b, i: (b, 0, 0)),
        ],
        out_specs=pl.BlockSpec((None, tq, Q_COLS), lambda b, i: (b, i, 0)),
        out_shape=jax.ShapeDtypeStruct((nb, t, Q_COLS), BF16),
        scratch_shapes=[
            pltpu.VMEM((t, tq), F32),
            pltpu.VMEM((t, tq), jnp.int32),
            pltpu.VMEM((N_HEADS, V_ROWS, tq), F32),
            pltpu.VMEM((N_HEADS, 1, tq), F32),
            pltpu.VMEM((N_HEADS, 1, tq), F32),
            pltpu.VMEM((N_HEADS, ATTN_CHUNK, tq), F32),
        ],
        compiler_params=_params("arbitrary", "arbitrary"),
        name="prompt_attention",
    )(qt, qit, wit, kg, vt, kib)


def _sample_score_kernel(pt_ref, qi_ref, w_ref, *rest):
    del pt_ref
    n_pages = len(rest) - 3
    pages, new_ref, o_ref, kall_ref = rest[:n_pages], rest[n_pages], rest[n_pages + 1], rest[n_pages + 2]
    for j in range(n_pages):
        kall_ref[:, j * PAGE_SIZE:(j + 1) * PAGE_SIZE] = pages[j][...].astype(BF16)
    kall_ref[:, n_pages * PAGE_SIZE:(n_pages + 1) * PAGE_SIZE] = new_ref[...].astype(BF16)
    d = _dot(qi_ref[...], kall_ref[...])
    r = jnp.maximum(d, 0.0) * w_ref[...]
    o_ref[...] = r.reshape(IDX_HEADS, SUBLANES, r.shape[1]).sum(axis=0)


def _sample_scores(page_table, qi8, w8, cache_ki_t, ki_new_t):
    nb, n_pages = page_table.shape
    n_keys = (n_pages + 1) * PAGE_SIZE
    rows = IDX_HEADS * SUBLANES
    grid_spec = pltpu.PrefetchScalarGridSpec(
        num_scalar_prefetch=1,
        grid=(nb,),
        in_specs=[pl.BlockSpec((None, rows, IDX_DIM), lambda b, pt: (b, 0, 0)),
                  pl.BlockSpec((None, rows, 1), lambda b, pt: (b, 0, 0))]
        + [pl.BlockSpec((None, IDX_DIM, PAGE_SIZE), lambda b, pt, j=j: (pt[b, j], 0, 0))
           for j in range(n_pages)]
        + [pl.BlockSpec((None, IDX_DIM, PAGE_SIZE), lambda b, pt: (b, 0, 0))],
        out_specs=pl.BlockSpec((None, SUBLANES, n_keys), lambda b, pt: (b, 0, 0)),
        scratch_shapes=[pltpu.VMEM((IDX_DIM, n_keys), BF16)],
    )
    return pl.pallas_call(
        _sample_score_kernel,
        grid_spec=grid_spec,
        out_shape=jax.ShapeDtypeStruct((nb, SUBLANES, n_keys), F32),
        compiler_params=_params("arbitrary"),
        name="sample_indexer_scores",
    )(page_table, qi8, w8, *([cache_ki_t] * n_pages), ki_new_t)


def _sample_select_kernel(s_ref, o_ref, sc_ref, tie_ref, *, past, period):
    tq = s_ref.shape[1]
    n_keys = s_ref.shape[0]
    lane = lax.broadcasted_iota(jnp.int32, (1, tq), 1)
    qpos = past + (lane & (period - 1))
    nk = n_keys // PAGE_SIZE
    stats = _stats_init(tq)
    for c in range(nk):
        idx = c * PAGE_SIZE + lax.broadcasted_iota(jnp.int32, (PAGE_SIZE, tq), 0)
        s = jnp.where(idx <= qpos, s_ref[c * PAGE_SIZE:(c + 1) * PAGE_SIZE, :], -jnp.inf)
        sc_ref[c * PAGE_SIZE:(c + 1) * PAGE_SIZE, :] = s
        stats = _stats_update(stats, s)
    _select_to_bias(sc_ref, tie_ref, qpos, nk, PAGE_SIZE, stats)
    o_ref[...] = sc_ref[...]


def _sample_select(scores_t, past, period):
    n_keys, nq = scores_t.shape
    tq = Q_TILE
    blk = pl.BlockSpec((n_keys, tq), lambda i: (0, i))
    return pl.pallas_call(
        functools.partial(_sample_select_kernel, past=past, period=period),
        grid=(nq // tq,),
        in_specs=[blk],
        out_specs=blk,
        out_shape=jax.ShapeDtypeStruct((n_keys, nq), F32),
        scratch_shapes=[pltpu.VMEM((n_keys, tq), F32), pltpu.VMEM((n_keys, tq), jnp.int32)],
        compiler_params=_params("arbitrary"),
        name="sample_select",
    )(scores_t)


def _sample_attn_kernel(pt_ref, q_ref, bias_ref, *rest):
    del pt_ref
    n_pages = (len(rest) - 8) // 2
    kpages, vpages = rest[:n_pages], rest[n_pages:2 * n_pages]
    knew_ref, vnew_ref, o_ref, kall_ref, vall_ref, s_ref, p_ref, l_ref = rest[2 * n_pages:]
    for j in range(n_pages):
        kall_ref[:, :, j * PAGE_SIZE:(j + 1) * PAGE_SIZE] = kpages[j][...].astype(BF16)
        vall_ref[:, :, j * PAGE_SIZE:(j + 1) * PAGE_SIZE] = vpages[j][...].astype(BF16)
    fill = jnp.zeros((PAGE_SIZE - SUBLANES, LANES), F32)
    for g in range(N_KV_HEADS):
        for new_ref, all_ref in ((knew_ref, kall_ref), (vnew_ref, vall_ref)):
            page = jnp.concatenate([new_ref[g], fill], axis=0).T
            all_ref[g, :, n_pages * PAGE_SIZE:(n_pages + 1) * PAGE_SIZE] = page[0:HEAD_DIM, :].astype(BF16)
    bias = bias_ref[...]
    n_keys = bias.shape[1]
    bias = jnp.broadcast_to(bias[None], (HEADS_PER_KV, SUBLANES, n_keys)).reshape(
        HEADS_PER_KV * SUBLANES, n_keys)
    for g in range(N_KV_HEADS):
        s_ref[g] = _dot(q_ref[g], kall_ref[g]) + bias
    for g in range(N_KV_HEADS):
        s = s_ref[g]
        m = s.max(axis=1, keepdims=True)
        p = jnp.exp(s - m)
        l_ref[g] = p.sum(axis=1, keepdims=True)
        p_ref[g] = p.astype(BF16)
    for g in range(N_KV_HEADS):
        o_ref[g] = _dot_nt(p_ref[g], vall_ref[g]) / l_ref[g]


def _sample_attention(page_table, q8, bias8, cache_k_t, cache_v_t, k_new_t, v_new_t):
    nb, n_pages = page_table.shape
    n_keys = (n_pages + 1) * PAGE_SIZE
    rows = HEADS_PER_KV * SUBLANES
    page = [pl.BlockSpec((None, N_KV_HEADS, HEAD_DIM, PAGE_SIZE), lambda b, pt, j=j: (pt[b, j], 0, 0, 0))
            for j in range(n_pages)]
    new = pl.BlockSpec((None, N_KV_HEADS, SUBLANES, LANES), lambda b, pt: (b, 0, 0, 0))
    qo = pl.BlockSpec((None, N_KV_HEADS, rows, HEAD_DIM), lambda b, pt: (b, 0, 0, 0))
    grid_spec = pltpu.PrefetchScalarGridSpec(
        num_scalar_prefetch=1,
        grid=(nb,),
        in_specs=[qo, pl.BlockSpec((None, SUBLANES, n_keys), lambda b, pt: (b, 0, 0))]
        + page + page + [new, new],
        out_specs=qo,
        scratch_shapes=[pltpu.VMEM((N_KV_HEADS, HEAD_DIM, n_keys), BF16),
                        pltpu.VMEM((N_KV_HEADS, HEAD_DIM, n_keys), BF16),
                        pltpu.VMEM((N_KV_HEADS, rows, n_keys), F32),
                        pltpu.VMEM((N_KV_HEADS, rows, n_keys), BF16),
                        pltpu.VMEM((N_KV_HEADS, rows, 1), F32)],
    )
    return pl.pallas_call(
        _sample_attn_kernel,
        grid_spec=grid_spec,
        out_shape=jax.ShapeDtypeStruct((nb, N_KV_HEADS, rows, HEAD_DIM), F32),
        compiler_params=_params("arbitrary"),
        name="sample_attention",
    )(page_table, q8, bias8, *([cache_k_t] * n_pages), *([cache_v_t] * n_pages), k_new_t, v_new_t)


def _ffn(h, wgu_ref, wd_ref):
    acc = jnp.zeros((h.shape[0], D_MODEL), F32)
    for c in range(D_FF // FFN_CHUNK):
        gate = _dot(h, wgu_ref[:, c * FFN_CHUNK:(c + 1) * FFN_CHUNK])
        up = _dot(h, wgu_ref[:, D_FF + c * FFN_CHUNK:D_FF + (c + 1) * FFN_CHUNK])
        a = _silu(gate) * up
        acc = acc + _dot(a.astype(BF16), wd_ref[c * FFN_CHUNK:(c + 1) * FFN_CHUNK, :])
    return acc


def _row_specs(nb, rows, per_row_mod):
    tm = min(ROW_TILE, rows)
    row = pl.BlockSpec((None, tm, D_MODEL), lambda b, j: (b, j, 0))
    if per_row_mod:
        mod = row
    else:
        mod = pl.BlockSpec((None, 1, D_MODEL), lambda b, j: (b, 0, 0))
    return tm, row, mod


def _attn_out_ffn_kernel(x_ref, o_ref, g1_ref, sc2_ref, sh2_ref, g2_ref, n2_ref,
                         wo_ref, wgu_ref, wd_ref, y_ref):
    y1 = x_ref[...] + g1_ref[...] * _dot(o_ref[...], wo_ref[...])
    h2 = _norm_mod(y1, n2_ref[...], sc2_ref[...], sh2_ref[...]).astype(BF16)
    y_ref[...] = y1 + g2_ref[...] * _ffn(h2, wgu_ref, wd_ref)


def _layer_spec(stacked, layer):
    _, rows, cols = stacked.shape
    return pl.BlockSpec((None, rows, cols), lambda *_: (layer, 0, 0), pipeline_mode=pl.Buffered(1))


def _attn_out_ffn(x, o, g1, sc2, sh2, g2, n2, wo, wgu, wd, layer):
    nb, rows, _ = x.shape
    tm, row, mod = _row_specs(nb, rows, g1.shape[1] == rows)
    return pl.pallas_call(
        _attn_out_ffn_kernel,
        grid=(nb, rows // tm),
        in_specs=[row, row, mod, mod, mod, mod, _const_spec((1, D_MODEL)),
                  _const_spec(wo.shape), _layer_spec(wgu, layer), _layer_spec(wd, layer)],
        out_specs=row,
        out_shape=jax.ShapeDtypeStruct(x.shape, F32),
        compiler_params=_params("arbitrary", "arbitrary"),
        name="attn_out_ffn",
    )(x, o, g1, sc2, sh2, g2, n2, wo, wgu, wd)


def _conv_layer_kernel(x_ref, sc1_ref, sh1_ref, g1_ref, sc2_ref, sh2_ref, g2_ref, p1_ref, p2_ref,
                       n1_ref, n2_ref, nf_ref, win_ref, ck_ref, wout_ref, wgu_ref, wd_ref,
                       y_ref, tail_ref, ubuf_ref, z_ref, *, seg):
    j = pl.program_id(1)
    tm = x_ref.shape[0]
    x = x_ref[...]
    h = _norm_mod(x, n1_ref[...], sc1_ref[...], sh1_ref[...]).astype(BF16)

    @pl.when(j == 0)
    def _():
        ubuf_ref[0:SUBLANES, :] = jnp.zeros((SUBLANES, D_MODEL), F32)

    @pl.when(j > 0)
    def _():
        ubuf_ref[0:SUBLANES, :] = ubuf_ref[tm:tm + SUBLANES, :]

    t = (j * tm + lax.broadcasted_iota(jnp.int32, (tm, 1), 0)) & (seg - 1)
    ck = ck_ref[...]
    cw = FFN_CHUNK
    for c in range(D_MODEL // cw):
        cols = slice(c * cw, (c + 1) * cw)
        bg = _dot(h, win_ref[:, c * cw:(c + 1) * cw])
        cg = _dot(h, win_ref[:, D_MODEL + c * cw:D_MODEL + (c + 1) * cw])
        hv = _dot(h, win_ref[:, 2 * D_MODEL + c * cw:2 * D_MODEL + (c + 1) * cw])
        u = cg * hv
        ubuf_ref[SUBLANES:SUBLANES + tm, cols] = u
        um1 = jnp.where(t >= 1, ubuf_ref[SUBLANES - 1:SUBLANES - 1 + tm, cols], p1_ref[:, cols])
        um2 = jnp.where(t >= 2, ubuf_ref[SUBLANES - 2:SUBLANES - 2 + tm, cols], p2_ref[:, cols])
        conv = ck[0:1, cols] * um2 + ck[1:2, cols] * um1 + ck[2:3, cols] * u
        z_ref[:, cols] = (bg * conv).astype(BF16)
    r = tail_ref.shape[0]
    tail_ref[...] = ubuf_ref[SUBLANES + tm - r:SUBLANES + tm, :]

    y1 = x + g1_ref[...] * _dot(z_ref[...], wout_ref[...])
    h2 = _norm_mod(y1, n2_ref[...], sc2_ref[...], sh2_ref[...]).astype(BF16)
    y2 = y1 + g2_ref[...] * _ffn(h2, wgu_ref, wd_ref)
    ms = jnp.mean(y2 * y2, axis=-1, keepdims=True)
    y_ref[...] = (y2 * lax.rsqrt(ms + EPS)) * nf_ref[...]


def _conv_layer(x, mods, p1, p2, n1, n2, nf, win, ck, wout, wgu, wd, *, layer, seg, full_tail):
    nb, rows, _ = x.shape
    per_row = mods[0].shape[1] == rows
    tm, row, mod = _row_specs(nb, rows, per_row)
    if per_row:
        prefix = row
    else:
        prefix = pl.BlockSpec((None, 1, D_MODEL), lambda b, j: (0, 0, 0))
    if full_tail:
        tail_spec, tail_rows = row, rows
    else:
        tail_spec, tail_rows = pl.BlockSpec((None, SUBLANES, D_MODEL), lambda b, j: (b, 0, 0)), SUBLANES
    vec = _const_spec((1, D_MODEL))
    return pl.pallas_call(
        functools.partial(_conv_layer_kernel, seg=seg),
        grid=(nb, rows // tm),
        in_specs=[row] + [mod] * 6 + [prefix, prefix, vec, vec, vec,
                                      _const_spec(win.shape), _const_spec(ck.shape), _const_spec(wout.shape),
                                      _layer_spec(wgu, layer), _layer_spec(wd, layer)],
        out_specs=(row, tail_spec),
        out_shape=(jax.ShapeDtypeStruct(x.shape, F32),
                   jax.ShapeDtypeStruct((nb, tail_rows, D_MODEL), F32)),
        scratch_shapes=[pltpu.VMEM((tm + SUBLANES, D_MODEL), F32), pltpu.VMEM((tm, D_MODEL), BF16)],
        compiler_params=_params("arbitrary", "arbitrary"),
        name="conv_layer",
    )(x, *mods, p1, p2, n1, n2, nf, win, ck, wout, wgu, wd)


def _rope_tables(pos):
    half = ROT_DIM // 2
    inv = ROPE_THETA ** (-jnp.arange(0, ROT_DIM, 2, dtype=F32) / ROT_DIM)
    ang = pos[:, None] * inv[None, :]
    cos, sin = jnp.cos(ang), jnp.sin(ang)
    n = pos.shape[0]
    rest = HEAD_DIM - ROT_DIM
    c = jnp.concatenate([cos, cos, jnp.ones((n, rest), F32)], axis=1)
    s1 = jnp.concatenate([-sin, jnp.zeros((n, half + rest), F32)], axis=1)
    s2 = jnp.concatenate([jnp.zeros((n, half), F32), sin, jnp.zeros((n, rest), F32)], axis=1)
    return tuple(jnp.tile(a, (1, LANES // HEAD_DIM)) for a in (c, s1, s2))


def kernel(x_prompt, x_sample, c_prompt, c_sample, cache_k, cache_v, cache_kidx, state_conv, page_table,
           ada_w, ada_b, norm1_g, norm2_g, final_g, attn_w_in, attn_w_out, conv_w_in, conv_k, conv_w_out,
           ffn_w_gu, ffn_w_down):
    nb, t, _ = x_prompt.shape
    ns, tn, _ = x_sample.shape
    n_pages = page_table.shape[1]
    past = n_pages * PAGE_SIZE
    n_phys = cache_k.shape[1]
    rows_s = ns * tn

    pad = (-(nb + ns)) % SUBLANES
    c_all = jnp.concatenate([c_prompt, c_sample, jnp.zeros((pad, D_MODEL), F32)], axis=0)
    mod = _modulation(c_all, ada_w, ada_b)

    def mods(layer):
        m = mod[layer].reshape(-1, 6, D_MODEL)
        prompt = [m[:nb, i][:, None, :] for i in range(6)]
        sample = [jnp.repeat(m[nb:nb + ns, i], tn, axis=0)[None] for i in range(6)]
        return prompt, sample

    vec = lambda a: a.reshape(1, D_MODEL)
    w_in = jnp.pad(attn_w_in[0], ((0, 0), (0, ATTN_IN_PAD - ATTN_IN))).astype(BF16)
    w_o = attn_w_out[0].astype(BF16)
    wgu = ffn_w_gu.astype(BF16)
    wd = ffn_w_down.astype(BF16)

    (sh1p, sc1p, g1p, sh2p, sc2p, g2p), (sh1s, sc1s, g1s, sh2s, sc2s, g2s) = mods(0)
    rope_p = _rope_tables(jnp.arange(t, dtype=F32))
    k_p, v_p, ki_p, qt, qit, kg, vt, kib, wit = _proj_prompt(
        x_prompt, sc1p, sh1p, vec(norm1_g[0]), w_in, rope_p)
    o_p = _prompt_attention(qt, qit, wit, kg, vt, kib)
    y_p = _attn_out_ffn(x_prompt, o_p, g1p, sc2p, sh2p, g2p, vec(norm2_g[0]), w_o, wgu, wd, 0)

    pos_s = jnp.tile(past + jnp.arange(tn, dtype=F32), ns)
    rope_s = _rope_tables(pos_s)
    xs = x_sample.reshape(rows_s, D_MODEL)
    proj = _proj_sample(xs, sc1s[0], sh1s[0], vec(norm1_g[0]), w_in, rope_s)
    o = 0
    q_s = proj[:, o:o + Q_COLS].reshape(ns, tn, N_KV_HEADS, HEADS_PER_KV, HEAD_DIM); o += Q_COLS
    k_s = proj[:, o:o + KV_COLS].reshape(ns, tn, KV_COLS); o += KV_COLS
    v_s = proj[:, o:o + KV_COLS].reshape(ns, tn, KV_COLS); o += KV_COLS
    qi_s = proj[:, o:o + IQ_COLS].reshape(ns, tn, IDX_HEADS, IDX_DIM); o += IQ_COLS
    ki_s = proj[:, o:o + IDX_DIM].reshape(ns, tn, IDX_DIM); o += IDX_DIM
    wi_s = proj[:, o:o + IDX_HEADS].reshape(ns, tn, IDX_HEADS)

    qpad = ((0, 0), (0, SUBLANES - tn))
    qi8 = jnp.pad(qi_s, qpad + ((0, 0), (0, 0))).transpose(0, 2, 1, 3)
    qi8 = qi8.reshape(ns, IDX_HEADS * SUBLANES, IDX_DIM).astype(BF16)
    w8 = jnp.pad(wi_s, qpad + ((0, 0),)).transpose(0, 2, 1).reshape(ns, IDX_HEADS * SUBLANES, 1)
    slot_pad = (0, PAGE_SIZE - tn)
    ki_new_t = jnp.pad(ki_s.transpose(0, 2, 1), ((0, 0), (0, 0), slot_pad))
    scores = _sample_scores(page_table, qi8, w8, cache_kidx[0].transpose(0, 2, 1), ki_new_t)
    n_keys = scores.shape[2]
    scores_t = scores[:, :tn, :].transpose(2, 0, 1).reshape(n_keys, rows_s)
    bias_t = _sample_select(scores_t, past, tn)
    bias8 = jnp.pad(bias_t.reshape(n_keys, ns, tn).transpose(1, 2, 0), qpad + ((0, 0),))

    q8 = jnp.pad(q_s * HEAD_DIM ** -0.5, qpad + ((0, 0), (0, 0), (0, 0))).transpose(0, 2, 3, 1, 4)
    q8 = q8.reshape(ns, N_KV_HEADS, HEADS_PER_KV * SUBLANES, HEAD_DIM).astype(BF16)
    new_t = lambda a: jnp.pad(a.reshape(ns, tn, N_KV_HEADS, HEAD_DIM).transpose(0, 2, 1, 3),
                              ((0, 0), (0, 0), (0, SUBLANES - tn), (0, LANES - HEAD_DIM)))
    o_g = _sample_attention(page_table, q8, bias8,
                            cache_k[0].transpose(0, 2, 3, 1), cache_v[0].transpose(0, 2, 3, 1),
                            new_t(k_s), new_t(v_s))
    o_g = o_g.reshape(ns, N_KV_HEADS, HEADS_PER_KV, SUBLANES, HEAD_DIM)[:, :, :, :tn]
    o_s = o_g.transpose(0, 3, 1, 2, 4).reshape(1, rows_s, Q_COLS).astype(BF16)
    y_s = _attn_out_ffn(xs[None], o_s, g1s, sc2s, sh2s, g2s, vec(norm2_g[0]), w_o, wgu, wd, 0)

    (sh1p, sc1p, g1p, sh2p, sc2p, g2p), (sh1s, sc1s, g1s, sh2s, sc2s, g2s) = mods(1)
    win = conv_w_in[0].astype(BF16)
    wout = conv_w_out[0].astype(BF16)
    zero_prefix = jnp.zeros((1, 1, D_MODEL), F32)
    out_p, tail_p = _conv_layer(
        y_p, (sc1p, sh1p, g1p, sc2p, sh2p, g2p), zero_prefix, zero_prefix,
        vec(norm1_g[1]), vec(norm2_g[1]), vec(final_g), win, conv_k[0], wout, wgu, wd,
        layer=1, seg=t, full_tail=False)
    st = state_conv[0]
    zeros_row = jnp.zeros((ns, 1, D_MODEL), F32)
    p1 = jnp.concatenate([st[:, 1:2], zeros_row, zeros_row, zeros_row], axis=1).reshape(1, rows_s, D_MODEL)
    p2 = jnp.concatenate([st[:, 0:1], st[:, 1:2], zeros_row, zeros_row], axis=1).reshape(1, rows_s, D_MODEL)
    out_s, u_s = _conv_layer(
        y_s, (sc1s, sh1s, g1s, sc2s, sh2s, g2s), p1, p2,
        vec(norm1_g[1]), vec(norm2_g[1]), vec(final_g), win, conv_k[0], wout, wgu, wd,
        layer=1, seg=tn, full_tail=True)

    keep = CONV_WIDTH - 1
    return (
        out_p,
        out_s.reshape(ns, tn, D_MODEL),
        k_p.reshape(1, nb, t, N_KV_HEADS, HEAD_DIM),
        v_p.reshape(1, nb, t, N_KV_HEADS, HEAD_DIM),
        ki_p[None],
        tail_p[None, :, SUBLANES - keep:, :],
        k_s.reshape(1, ns, tn, N_KV_HEADS, HEAD_DIM),
        v_s.reshape(1, ns, tn, N_KV_HEADS, HEAD_DIM),
        ki_s[None],
        u_s.reshape(ns, tn, D_MODEL)[None, :, tn - keep:, :],
    )
```

```python
import functools

import jax
import jax.numpy as jnp
from jax import lax
from jax.experimental import pallas as pl
from jax.experimental.pallas import tpu as pltpu

F32 = jnp.float32
BF16 = jnp.bfloat16

D_MODEL = 1024
N_HEADS = 16
HEAD_DIM = 64
N_KV_HEADS = 4
HEADS_PER_KV = N_HEADS // N_KV_HEADS
ROT_DIM = 16
ROPE_THETA = 500000.0
IDX_HEADS = 8
IDX_DIM = 64
TOPK = 256
PAGE_SIZE = 128
CONV_WIDTH = 3
D_FF = 2816
EPS = 1e-6
Q_COLS = N_HEADS * HEAD_DIM
KV_COLS = N_KV_HEADS * HEAD_DIM
IQ_COLS = IDX_HEADS * IDX_DIM
ATTN_IN = Q_COLS + 2 * KV_COLS + IQ_COLS + IDX_DIM + IDX_HEADS
ATTN_IN_PAD = 2176
LANES = 128
SUBLANES = 8
MASK_BIAS = -1e30
F32_MAX = 3.4028234663852886e38
LOG2_E = 1.4426950408889634
VMEM_LIMIT = 60 * 1024 * 1024

ROW_TILE = 512
Q_TILE = 256
SCORE_CHUNK = 512
ATTN_CHUNK = 256
V_ROWS = HEAD_DIM + 16
FFN_CHUNK = 256


def _dot(a, b):
    return jnp.dot(a, b, preferred_element_type=F32)


def _dot_nt(a, b):
    return lax.dot_general(a, b, (((1,), (1,)), ((), ())), preferred_element_type=F32)


def _silu(x):
    return x / (1.0 + jnp.exp(-x))


def _norm_mod(x, g, sc, sh):
    ms = jnp.mean(x * x, axis=-1, keepdims=True)
    y = x * lax.rsqrt(ms + EPS)
    return (y * g) * (1.0 + sc) + sh


def _params(*sem, flags=None):
    return pltpu.CompilerParams(dimension_semantics=sem, vmem_limit_bytes=VMEM_LIMIT, flags=flags)


def _const_spec(shape):
    nd = len(shape)
    return pl.BlockSpec(shape, lambda *_: (0,) * nd, pipeline_mode=pl.Buffered(1))


def _mod_kernel(c_ref, w_ref, b_ref, o_ref):
    s = _silu(c_ref[...]).astype(BF16)
    o_ref[...] = _dot(s, w_ref[...].astype(BF16)) + b_ref[...]


def _modulation(c_all, ada_w, ada_b):
    depth, _, n = ada_w.shape
    rows = c_all.shape[0]
    tn = 1024
    return pl.pallas_call(
        _mod_kernel,
        grid=(depth, n // tn),
        in_specs=[
            pl.BlockSpec((rows, D_MODEL), lambda l, j: (0, 0)),
            pl.BlockSpec((None, D_MODEL, tn), lambda l, j: (l, 0, j)),
            pl.BlockSpec((None, 1, tn), lambda l, j: (l, 0, j)),
        ],
        out_specs=pl.BlockSpec((None, rows, tn), lambda l, j: (l, 0, j)),
        out_shape=jax.ShapeDtypeStruct((depth, rows, n), F32),
        compiler_params=_params("arbitrary", "arbitrary"),
        name="adaln_modulation",
    )(c_all, ada_w, ada_b.reshape(depth, 1, n))


def _rope_block(x, c, s1, s2):
    return x * c + pltpu.roll(x, LANES - ROT_DIM // 2, 1) * s1 + pltpu.roll(x, ROT_DIM // 2, 1) * s2


def _rope_cols(p, c, s1, s2):
    nblk = p.shape[1] // LANES
    return [_rope_block(p[:, j * LANES:(j + 1) * LANES], c, s1, s2) for j in range(nblk)]


def _last_block_tables(c, s1, s2):
    lane = lax.broadcasted_iota(jnp.int32, c.shape, 1)
    is_key = lane < IDX_DIM
    return jnp.where(is_key, c, 1.0), jnp.where(is_key, s1, 0.0), jnp.where(is_key, s2, 0.0)


def _proj_prompt_kernel(x_ref, sc_ref, sh_ref, g_ref, w_ref, c_ref, s1_ref, s2_ref,
                        k_ref, v_ref, ki_ref, qt_ref, qit_ref, kg_ref, vt_ref, kib_ref, wit_ref):
    h = _norm_mod(x_ref[...], g_ref[...], sc_ref[...], sh_ref[...]).astype(BF16)
    c, s1, s2 = c_ref[...], s1_ref[...], s2_ref[...]

    pq = _dot(h, w_ref[:, 0:Q_COLS])
    for j, blk in enumerate(_rope_cols(pq, c, s1, s2)):
        qt_ref[j * LANES:(j + 1) * LANES, :] = (blk * (HEAD_DIM ** -0.5 * LOG2_E)).astype(BF16).T

    pk = _dot(h, w_ref[:, Q_COLS:Q_COLS + KV_COLS])
    k = jnp.concatenate(_rope_cols(pk, c, s1, s2), axis=1)
    k_ref[...] = k
    for g in range(N_KV_HEADS):
        kg_ref[g] = k[:, g * HEAD_DIM:(g + 1) * HEAD_DIM].astype(BF16)

    pv = _dot(h, w_ref[:, Q_COLS + KV_COLS:Q_COLS + 2 * KV_COLS])
    v_ref[...] = pv
    ones = jnp.ones((V_ROWS - HEAD_DIM, ATTN_CHUNK), BF16)
    for cc in range(pv.shape[0] // ATTN_CHUNK):
        vt = pv[cc * ATTN_CHUNK:(cc + 1) * ATTN_CHUNK, :].astype(BF16).T
        for g in range(N_KV_HEADS):
            vt_ref[cc, g * V_ROWS:g * V_ROWS + HEAD_DIM, :] = vt[g * HEAD_DIM:(g + 1) * HEAD_DIM, :]
            vt_ref[cc, g * V_ROWS + HEAD_DIM:(g + 1) * V_ROWS, :] = ones

    o = Q_COLS + 2 * KV_COLS
    pqi = _dot(h, w_ref[:, o:o + IQ_COLS])
    for j, blk in enumerate(_rope_cols(pqi, c, s1, s2)):
        qit_ref[j * LANES:(j + 1) * LANES, :] = blk.astype(BF16).T

    pl_ = _dot(h, w_ref[:, o + IQ_COLS:ATTN_IN_PAD])
    last = _rope_block(pl_, *_last_block_tables(c, s1, s2))
    ki = last[:, 0:IDX_DIM]
    ki_ref[...] = ki
    kib_ref[...] = ki.astype(BF16)
    wit_ref[...] = last.T[IDX_DIM:IDX_DIM + IDX_HEADS, :]


def _proj_prompt(x, sc, sh, g, w, rope):
    nb, t, _ = x.shape
    tm = ROW_TILE
    row = lambda width: pl.BlockSpec((None, tm, width), lambda b, j: (b, j, 0))
    mod = pl.BlockSpec((None, 1, D_MODEL), lambda b, j: (b, 0, 0))
    tab = pl.BlockSpec((tm, LANES), lambda b, j: (j, 0))
    col = lambda height: pl.BlockSpec((None, height, tm), lambda b, j: (b, 0, j))
    out_shape = (
        jax.ShapeDtypeStruct((nb, t, KV_COLS), F32),
        jax.ShapeDtypeStruct((nb, t, KV_COLS), F32),
        jax.ShapeDtypeStruct((nb, t, IDX_DIM), F32),
        jax.ShapeDtypeStruct((nb, Q_COLS, t), BF16),
        jax.ShapeDtypeStruct((nb, IQ_COLS, t), BF16),
        jax.ShapeDtypeStruct((nb, N_KV_HEADS, t, HEAD_DIM), BF16),
        jax.ShapeDtypeStruct((nb, t // ATTN_CHUNK, N_KV_HEADS * V_ROWS, ATTN_CHUNK), BF16),
        jax.ShapeDtypeStruct((nb, t, IDX_DIM), BF16),
        jax.ShapeDtypeStruct((nb, IDX_HEADS, t), F32),
    )
    out_specs = (
        row(KV_COLS), row(KV_COLS), row(IDX_DIM), col(Q_COLS), col(IQ_COLS),
        pl.BlockSpec((None, N_KV_HEADS, tm, HEAD_DIM), lambda b, j: (b, 0, j, 0)),
        pl.BlockSpec((None, tm // ATTN_CHUNK, N_KV_HEADS * V_ROWS, ATTN_CHUNK), lambda b, j: (b, j, 0, 0)),
        row(IDX_DIM), col(IDX_HEADS),
    )
    return pl.pallas_call(
        _proj_prompt_kernel,
        grid=(nb, t // tm),
        in_specs=[row(D_MODEL), mod, mod, _const_spec((1, D_MODEL)),
                  _const_spec((D_MODEL, ATTN_IN_PAD)), tab, tab, tab],
        out_specs=out_specs,
        out_shape=out_shape,
        compiler_params=_params("arbitrary", "arbitrary"),
        name="attn_proj_prompt",
    )(x, sc, sh, g, w, *rope)


def _proj_sample_kernel(x_ref, sc_ref, sh_ref, g_ref, w_ref, c_ref, s1_ref, s2_ref, p_ref):
    h = _norm_mod(x_ref[...], g_ref[...], sc_ref[...], sh_ref[...]).astype(BF16)
    c, s1, s2 = c_ref[...], s1_ref[...], s2_ref[...]
    p = _dot(h, w_ref[...])
    v_lo = (Q_COLS + KV_COLS) // LANES
    v_hi = (Q_COLS + 2 * KV_COLS) // LANES
    nblk = ATTN_IN_PAD // LANES
    for j in range(nblk):
        blk = p[:, j * LANES:(j + 1) * LANES]
        if j == nblk - 1:
            blk = _rope_block(blk, *_last_block_tables(c, s1, s2))
        elif not (v_lo <= j < v_hi):
            blk = _rope_block(blk, c, s1, s2)
        p_ref[:, j * LANES:(j + 1) * LANES] = blk


def _proj_sample(x, sc, sh, g, w, rope):
    rows = x.shape[0]
    full = lambda width: pl.BlockSpec((rows, width), lambda i: (0, 0))
    return pl.pallas_call(
        _proj_sample_kernel,
        grid=(1,),
        in_specs=[full(D_MODEL), full(D_MODEL), full(D_MODEL),
                  pl.BlockSpec((1, D_MODEL), lambda i: (0, 0)),
                  pl.BlockSpec((D_MODEL, ATTN_IN_PAD), lambda i: (0, 0)),
                  full(LANES), full(LANES), full(LANES)],
        out_specs=full(ATTN_IN_PAD),
        out_shape=jax.ShapeDtypeStruct((rows, ATTN_IN_PAD), F32),
        compiler_params=_params("arbitrary"),
        name="attn_proj_sample",
    )(x, sc, sh, g, w, *rope)


FOLD_ROWS = 32
BISECT_STEPS_PER_TEST = 4


def _fold(x, op):
    rows, tq = x.shape
    x3 = x.reshape(rows // FOLD_ROWS, FOLD_ROWS, tq)
    return {"sum": x3.sum, "max": x3.max, "min": x3.min}[op](axis=0)


def _stats_init(tq):
    zeros = jnp.zeros((FOLD_ROWS, tq), F32)
    return (jnp.full((FOLD_ROWS, tq), -jnp.inf, F32), jnp.full((FOLD_ROWS, tq), jnp.inf, F32),
            zeros, zeros, zeros)


def _stats_update(stats, s):
    mx, mn, n_above, n_ge0, n_gt0 = stats
    ones_where = lambda m: _fold(jnp.where(m, 1.0, 0.0), "sum")
    above = s > -jnp.inf
    return (jnp.maximum(mx, _fold(s, "max")),
            jnp.minimum(mn, _fold(jnp.where(above, s, jnp.inf), "min")),
            n_above + ones_where(above), n_ge0 + ones_where(s >= 0.0), n_gt0 + ones_where(s > 0.0))


def _select_to_bias(sc_ref, tie_ref, qpos, nk, kc, stats):
    tq = sc_ref.shape[1]
    topk = float(TOPK)

    def chunk(c):
        return sc_ref[pl.ds(pl.multiple_of(c * kc, kc), kc), :]

    def key_index(c):
        return c * kc + lax.broadcasted_iota(jnp.int32, (kc, tq), 0)

    def ones_where(m):
        return _fold(jnp.where(m, 1.0, 0.0), "sum")

    def count(pred):
        def body(c, acc):
            return acc + ones_where(pred(chunk(c), c))
        acc = lax.fori_loop(0, nk, body, jnp.zeros((FOLD_ROWS, tq), F32))
        return acc.sum(axis=0, keepdims=True)

    mx, mn, n_above, n_ge0, n_gt0 = stats
    mx = mx.max(axis=0, keepdims=True)
    mn = mn.min(axis=0, keepdims=True)
    n_above = n_above.sum(axis=0, keepdims=True)
    n_ge0 = n_ge0.sum(axis=0, keepdims=True)
    n_gt0 = n_gt0.sum(axis=0, keepdims=True)
    n_max = count(lambda s, c: s >= mx)

    few = n_above < topk
    flat = n_max >= topk
    zero = (n_gt0 < topk) & (n_ge0 >= topk)
    pos = n_gt0 >= topk
    lo = jnp.where(few, -jnp.inf, jnp.where(flat, mx, jnp.where(zero | pos, 0.0, mn)))
    hi = jnp.where(few, mn, jnp.where(flat, jnp.inf, jnp.where(pos, mx, 0.0)))
    n_lo = jnp.where(few, 2.0 * topk, jnp.where(flat, n_max, jnp.where(zero | pos, n_ge0, n_above)))
    n_hi = jnp.where(few, n_above, jnp.where(flat, 0.0, jnp.where(zero, n_gt0, jnp.where(pos, n_max, n_ge0))))
    done = jnp.where(few | flat | zero | (n_lo == topk), 1.0, 0.0)

    def not_finished(d):
        return (jnp.min(d) < 0.5).astype(jnp.int32)

    def bisect_step(state):
        lo, hi, n_lo, n_hi, done = state
        mid = jnp.clip(0.5 * lo + 0.5 * hi, -F32_MAX, F32_MAX)
        stuck = (mid <= lo) | (mid >= hi)
        n_mid = count(lambda s, c: s >= mid)
        move = (done < 0.5) & jnp.logical_not(stuck)
        up = move & (n_mid >= topk)
        down = move & (n_mid < topk)
        lo = jnp.where(up, mid, lo)
        n_lo = jnp.where(up, n_mid, n_lo)
        hi = jnp.where(down, mid, hi)
        n_hi = jnp.where(down, n_mid, n_hi)
        done = jnp.where(stuck | (n_lo == topk), 1.0, done)
        return lo, hi, n_lo, n_hi, done

    def bisect(carry):
        state = carry[:5]
        for _ in range(BISECT_STEPS_PER_TEST):
            state = bisect_step(state)
        return (*state, not_finished(state[4]))

    lo, hi, n_lo, n_hi, done, _ = lax.while_loop(
        lambda carry: carry[5] > 0, bisect, (lo, hi, n_lo, n_hi, done, not_finished(done)))

    thr = lo
    exact = n_lo == topk
    need = topk - n_hi
    all_keys = jnp.full((1, tq), 2 ** 30, jnp.int32)

    def tie_break():
        def mark(c, _):
            tie_ref[pl.ds(pl.multiple_of(c * kc, kc), kc), :] = jnp.where(
                chunk(c) == thr, key_index(c), 2 ** 30)
            return 0
        lax.fori_loop(0, nk, mark, 0)

        n_bits = (sc_ref.shape[0] - 1).bit_length()

        def step(i, j):
            cand = j + lax.shift_right_logical(jnp.int32(1 << (n_bits - 1)), i)

            def body(c, acc):
                tied = tie_ref[pl.ds(pl.multiple_of(c * kc, kc), kc), :]
                return acc + ones_where(tied < cand)
            n_before = lax.fori_loop(0, nk, body, jnp.zeros((FOLD_ROWS, tq), F32)).sum(axis=0, keepdims=True)
            return jnp.where(n_before < need, cand, j)
        return lax.fori_loop(0, n_bits, step, jnp.zeros((1, tq), jnp.int32))

    any_tie = jnp.min(jnp.where(exact, 1.0, 0.0)) < 0.5

    def write(keep_fn):
        def body(c, _):
            sc_ref[pl.ds(pl.multiple_of(c * kc, kc), kc), :] = jnp.where(keep_fn(chunk(c), c), 0.0, MASK_BIAS)
            return 0
        return lax.fori_loop(0, nk, body, 0)

    def write_with_ties():
        last_tied = jnp.minimum(jnp.where(exact, all_keys, tie_break()), qpos)
        return write(lambda s, c: (s > thr) | ((s == thr) & (key_index(c) <= last_tied)))

    lax.cond(any_tie, write_with_ties, lambda: write(lambda s, c: s >= thr))


def _prompt_attn_kernel(qt_ref, qit_ref, wit_ref, kg_ref, vt_ref, kib_ref, o_ref,
                        sc_ref, tie_ref, acc_ref, m_ref, mx_ref, s_ref):
    i = pl.program_id(1)
    tq = qt_ref.shape[1]
    qpos = i * tq + lax.broadcasted_iota(jnp.int32, (1, tq), 1)
    n_keys = (i + 1) * tq

    n_score = (n_keys + SCORE_CHUNK - 1) // SCORE_CHUNK

    def score_chunk(c, stats):
        start = pl.multiple_of(c * SCORE_CHUNK, SCORE_CHUNK)
        kc = kib_ref[pl.ds(start, SCORE_CHUNK), :]
        acc = jnp.zeros((SCORE_CHUNK, tq), F32)
        for h in range(IDX_HEADS):
            d = _dot(kc, qit_ref[h * IDX_DIM:(h + 1) * IDX_DIM, :])
            acc = acc + jnp.maximum(d, 0.0) * wit_ref[h:h + 1, :]
        idx = start + lax.broadcasted_iota(jnp.int32, (SCORE_CHUNK, tq), 0)
        s = jnp.where(idx <= qpos, acc, -jnp.inf)
        sc_ref[pl.ds(start, SCORE_CHUNK), :] = s
        return _stats_update(stats, s)

    stats = lax.fori_loop(0, n_score, score_chunk, _stats_init(tq))

    _select_to_bias(sc_ref, tie_ref, qpos, n_score, SCORE_CHUNK, stats)

    m_ref[...] = jnp.full(m_ref.shape, MASK_BIAS, F32)
    acc_ref[...] = jnp.zeros(acc_ref.shape, F32)

    def attn_chunk(c, _):
        start = pl.multiple_of(c * ATTN_CHUNK, ATTN_CHUNK)
        for h in range(N_HEADS):
            s = _dot(kg_ref[h // HEADS_PER_KV, pl.ds(start, ATTN_CHUNK), :],
                     qt_ref[h * HEAD_DIM:(h + 1) * HEAD_DIM, :])
            s = s + sc_ref[pl.ds(start, ATTN_CHUNK), :]
            s_ref[h] = s
            mx_ref[h] = s.max(axis=0, keepdims=True)
        for h in range(N_HEADS):
            g = h // HEADS_PER_KV
            m_old = m_ref[h]
            m_new = jnp.maximum(m_old, mx_ref[h])
            alpha = jnp.exp2(m_old - m_new)
            p = jnp.exp2(s_ref[h] - m_new).astype(BF16)
            vt = vt_ref[c, g * V_ROWS:(g + 1) * V_ROWS, :]
            acc_ref[h] = alpha * acc_ref[h] + _dot(vt, p)
            m_ref[h] = m_new
        return 0

    lax.fori_loop(0, (i + 1) * (tq // ATTN_CHUNK), attn_chunk, 0)

    for pair in range(N_HEADS // 2):
        two = []
        for h in (2 * pair, 2 * pair + 1):
            a = acc_ref[h]
            two.append(a[0:HEAD_DIM, :] / a[HEAD_DIM:HEAD_DIM + 1, :])
        col = 2 * pair * HEAD_DIM
        o_ref[:, col:col + 2 * HEAD_DIM] = jnp.concatenate(two, axis=0).T.astype(BF16)


def _prompt_attention(qt, qit, wit, kg, vt, kib):
    nb, _, t = qt.shape
    tq = Q_TILE
    col = lambda height: pl.BlockSpec((None, height, tq), lambda b, i: (b, 0, i))
    return pl.pallas_call(
        _prompt_attn_kernel,
        grid=(nb, t // tq),
        in_specs=[
            col(Q_COLS), col(IQ_COLS), col(IDX_HEADS),
            pl.BlockSpec((None, N_KV_HEADS, t, HEAD_DIM), lambda b, i: (b, 0, 0, 0)),
            pl.BlockSpec((None, t // ATTN_CHUNK, N_KV_HEADS * V_ROWS, ATTN_CHUNK), lambda b, i: (b, 0, 0, 0)),
            pl.BlockSpec((None, t, IDX_DIM), lambda b, i: (b, 0, 0)),
        ],
        out_specs=pl.BlockSpec((None, tq, Q_COLS), lambda b, i: (b, i, 0)),
        out_shape=jax.ShapeDtypeStruct((nb, t, Q_COLS), BF16),
        scratch_shapes=[
            pltpu.VMEM((t, tq), F32),
            pltpu.VMEM((t, tq), jnp.int32),
            pltpu.VMEM((N_HEADS, V_ROWS, tq), F32),
            pltpu.VMEM((N_HEADS, 1, tq), F32),
            pltpu.VMEM((N_HEADS, 1, tq), F32),
            pltpu.VMEM((N_HEADS, ATTN_CHUNK, tq), F32),
        ],
        compiler_params=_params("arbitrary", "arbitrary"),
        name="prompt_attention",
    )(qt, qit, wit, kg, vt, kib)


def _sample_score_kernel(pt_ref, qi_ref, w_ref, *rest):
    del pt_ref
    n_pages = len(rest) - 3
    pages, new_ref, o_ref, kall_ref = rest[:n_pages], rest[n_pages], rest[n_pages + 1], rest[n_pages + 2]
    for j in range(n_pages):
        kall_ref[:, j * PAGE_SIZE:(j + 1) * PAGE_SIZE] = pages[j][...].astype(BF16)
    kall_ref[:, n_pages * PAGE_SIZE:(n_pages + 1) * PAGE_SIZE] = new_ref[...].astype(BF16)
    d = _dot(qi_ref[...], kall_ref[...])
    r = jnp.maximum(d, 0.0) * w_ref[...]
    o_ref[...] = r.reshape(IDX_HEADS, SUBLANES, r.shape[1]).sum(axis=0)


def _sample_scores(page_table, qi8, w8, cache_ki_t, ki_new_t):
    nb, n_pages = page_table.shape
    n_keys = (n_pages + 1) * PAGE_SIZE
    rows = IDX_HEADS * SUBLANES
    grid_spec = pltpu.PrefetchScalarGridSpec(
        num_scalar_prefetch=1,
        grid=(nb,),
        in_specs=[pl.BlockSpec((None, rows, IDX_DIM), lambda b, pt: (b, 0, 0)),
                  pl.BlockSpec((None, rows, 1), lambda b, pt: (b, 0, 0))]
        + [pl.BlockSpec((None, IDX_DIM, PAGE_SIZE), lambda b, pt, j=j: (pt[b, j], 0, 0))
           for j in range(n_pages)]
        + [pl.BlockSpec((None, IDX_DIM, PAGE_SIZE), lambda b, pt: (b, 0, 0))],
        out_specs=pl.BlockSpec((None, SUBLANES, n_keys), lambda b, pt: (b, 0, 0)),
        scratch_shapes=[pltpu.VMEM((IDX_DIM, n_keys), BF16)],
    )
    return pl.pallas_call(
        _sample_score_kernel,
        grid_spec=grid_spec,
        out_shape=jax.ShapeDtypeStruct((nb, SUBLANES, n_keys), F32),
        compiler_params=_params("arbitrary"),
        name="sample_indexer_scores",
    )(page_table, qi8, w8, *([cache_ki_t] * n_pages), ki_new_t)


def _sample_select_kernel(s_ref, o_ref, sc_ref, tie_ref, *, past, period):
    tq = s_ref.shape[1]
    n_keys = s_ref.shape[0]
    lane = lax.broadcasted_iota(jnp.int32, (1, tq), 1)
    qpos = past + (lane & (period - 1))
    nk = n_keys // PAGE_SIZE
    stats = _stats_init(tq)
    for c in range(nk):
        idx = c * PAGE_SIZE + lax.broadcasted_iota(jnp.int32, (PAGE_SIZE, tq), 0)
        s = jnp.where(idx <= qpos, s_ref[c * PAGE_SIZE:(c + 1) * PAGE_SIZE, :], -jnp.inf)
        sc_ref[c * PAGE_SIZE:(c + 1) * PAGE_SIZE, :] = s
        stats = _stats_update(stats, s)
    _select_to_bias(sc_ref, tie_ref, qpos, nk, PAGE_SIZE, stats)
    o_ref[...] = sc_ref[...]


def _sample_select(scores_t, past, period):
    n_keys, nq = scores_t.shape
    tq = Q_TILE
    blk = pl.BlockSpec((n_keys, tq), lambda i: (0, i))
    return pl.pallas_call(
        functools.partial(_sample_select_kernel, past=past, period=period),
        grid=(nq // tq,),
        in_specs=[blk],
        out_specs=blk,
        out_shape=jax.ShapeDtypeStruct((n_keys, nq), F32),
        scratch_shapes=[pltpu.VMEM((n_keys, tq), F32), pltpu.VMEM((n_keys, tq), jnp.int32)],
        compiler_params=_params("arbitrary"),
        name="sample_select",
    )(scores_t)


def _sample_attn_kernel(pt_ref, q_ref, bias_ref, *rest):
    del pt_ref
    n_pages = (len(rest) - 8) // 2
    kpages, vpages = rest[:n_pages], rest[n_pages:2 * n_pages]
    knew_ref, vnew_ref, o_ref, kall_ref, vall_ref, s_ref, p_ref, l_ref = rest[2 * n_pages:]
    for j in range(n_pages):
        kall_ref[:, :, j * PAGE_SIZE:(j + 1) * PAGE_SIZE] = kpages[j][...].astype(BF16)
        vall_ref[:, :, j * PAGE_SIZE:(j + 1) * PAGE_SIZE] = vpages[j][...].astype(BF16)
    fill = jnp.zeros((PAGE_SIZE - SUBLANES, LANES), F32)
    for g in range(N_KV_HEADS):
        for new_ref, all_ref in ((knew_ref, kall_ref), (vnew_ref, vall_ref)):
            page = jnp.concatenate([new_ref[g], fill], axis=0).T
            all_ref[g, :, n_pages * PAGE_SIZE:(n_pages + 1) * PAGE_SIZE] = page[0:HEAD_DIM, :].astype(BF16)
    bias = bias_ref[...]
    n_keys = bias.shape[1]
    bias = jnp.broadcast_to(bias[None], (HEADS_PER_KV, SUBLANES, n_keys)).reshape(
        HEADS_PER_KV * SUBLANES, n_keys)
    for g in range(N_KV_HEADS):
        s_ref[g] = _dot(q_ref[g], kall_ref[g]) + bias
    for g in range(N_KV_HEADS):
        s = s_ref[g]
        m = s.max(axis=1, keepdims=True)
        p = jnp.exp(s - m)
        l_ref[g] = p.sum(axis=1, keepdims=True)
        p_ref[g] = p.astype(BF16)
    for g in range(N_KV_HEADS):
        o_ref[g] = _dot_nt(p_ref[g], vall_ref[g]) / l_ref[g]


def _sample_attention(page_table, q8, bias8, cache_k_t, cache_v_t, k_new_t, v_new_t):
    nb, n_pages = page_table.shape
    n_keys = (n_pages + 1) * PAGE_SIZE
    rows = HEADS_PER_KV * SUBLANES
    page = [pl.BlockSpec((None, N_KV_HEADS, HEAD_DIM, PAGE_SIZE), lambda b, pt, j=j: (pt[b, j], 0, 0, 0))
            for j in range(n_pages)]
    new = pl.BlockSpec((None, N_KV_HEADS, SUBLANES, LANES), lambda b, pt: (b, 0, 0, 0))
    qo = pl.BlockSpec((None, N_KV_HEADS, rows, HEAD_DIM), lambda b, pt: (b, 0, 0, 0))
    grid_spec = pltpu.PrefetchScalarGridSpec(
        num_scalar_prefetch=1,
        grid=(nb,),
        in_specs=[qo, pl.BlockSpec((None, SUBLANES, n_keys), lambda b, pt: (b, 0, 0))]
        + page + page + [new, new],
        out_specs=qo,
        scratch_shapes=[pltpu.VMEM((N_KV_HEADS, HEAD_DIM, n_keys), BF16),
                        pltpu.VMEM((N_KV_HEADS, HEAD_DIM, n_keys), BF16),
                        pltpu.VMEM((N_KV_HEADS, rows, n_keys), F32),
                        pltpu.VMEM((N_KV_HEADS, rows, n_keys), BF16),
                        pltpu.VMEM((N_KV_HEADS, rows, 1), F32)],
    )
    return pl.pallas_call(
        _sample_attn_kernel,
        grid_spec=grid_spec,
        out_shape=jax.ShapeDtypeStruct((nb, N_KV_HEADS, rows, HEAD_DIM), F32),
        compiler_params=_params("arbitrary"),
        name="sample_attention",
    )(page_table, q8, bias8, *([cache_k_t] * n_pages), *([cache_v_t] * n_pages), k_new_t, v_new_t)


def _ffn(h, wgu_ref, wd_ref):
    acc = jnp.zeros((h.shape[0], D_MODEL), F32)
    for c in range(D_FF // FFN_CHUNK):
        gate = _dot(h, wgu_ref[:, c * FFN_CHUNK:(c + 1) * FFN_CHUNK])
        up = _dot(h, wgu_ref[:, D_FF + c * FFN_CHUNK:D_FF + (c + 1) * FFN_CHUNK])
        a = _silu(gate) * up
        acc = acc + _dot(a.astype(BF16), wd_ref[c * FFN_CHUNK:(c + 1) * FFN_CHUNK, :])
    return acc


def _row_specs(nb, rows, per_row_mod):
    tm = min(ROW_TILE, rows)
    row = pl.BlockSpec((None, tm, D_MODEL), lambda b, j: (b, j, 0))
    if per_row_mod:
        mod = row
    else:
        mod = pl.BlockSpec((None, 1, D_MODEL), lambda b, j: (b, 0, 0))
    return tm, row, mod


def _attn_out_ffn_kernel(x_ref, o_ref, g1_ref, sc2_ref, sh2_ref, g2_ref, n2_ref,
                         wo_ref, wgu_ref, wd_ref, y_ref):
    y1 = x_ref[...] + g1_ref[...] * _dot(o_ref[...], wo_ref[...])
    h2 = _norm_mod(y1, n2_ref[...], sc2_ref[...], sh2_ref[...]).astype(BF16)
    y_ref[...] = y1 + g2_ref[...] * _ffn(h2, wgu_ref, wd_ref)


def _layer_spec(stacked, layer):
    _, rows, cols = stacked.shape
    return pl.BlockSpec((None, rows, cols), lambda *_: (layer, 0, 0), pipeline_mode=pl.Buffered(1))


def _attn_out_ffn(x, o, g1, sc2, sh2, g2, n2, wo, wgu, wd, layer):
    nb, rows, _ = x.shape
    tm, row, mod = _row_specs(nb, rows, g1.shape[1] == rows)
    return pl.pallas_call(
        _attn_out_ffn_kernel,
        grid=(nb, rows // tm),
        in_specs=[row, row, mod, mod, mod, mod, _const_spec((1, D_MODEL)),
                  _const_spec(wo.shape), _layer_spec(wgu, layer), _layer_spec(wd, layer)],
        out_specs=row,
        out_shape=jax.ShapeDtypeStruct(x.shape, F32),
        compiler_params=_params("arbitrary", "arbitrary"),
        name="attn_out_ffn",
    )(x, o, g1, sc2, sh2, g2, n2, wo, wgu, wd)


def _conv_layer_kernel(x_ref, sc1_ref, sh1_ref, g1_ref, sc2_ref, sh2_ref, g2_ref, p1_ref, p2_ref,
                       n1_ref, n2_ref, nf_ref, win_ref, ck_ref, wout_ref, wgu_ref, wd_ref,
                       y_ref, tail_ref, ubuf_ref, z_ref, *, seg):
    j = pl.program_id(1)
    tm = x_ref.shape[0]
    x = x_ref[...]
    h = _norm_mod(x, n1_ref[...], sc1_ref[...], sh1_ref[...]).astype(BF16)

    @pl.when(j == 0)
    def _():
        ubuf_ref[0:SUBLANES, :] = jnp.zeros((SUBLANES, D_MODEL), F32)

    @pl.when(j > 0)
    def _():
        ubuf_ref[0:SUBLANES, :] = ubuf_ref[tm:tm + SUBLANES, :]

    t = (j * tm + lax.broadcasted_iota(jnp.int32, (tm, 1), 0)) & (seg - 1)
    ck = ck_ref[...]
    cw = FFN_CHUNK
    for c in range(D_MODEL // cw):
        cols = slice(c * cw, (c + 1) * cw)
        bg = _dot(h, win_ref[:, c * cw:(c + 1) * cw])
        cg = _dot(h, win_ref[:, D_MODEL + c * cw:D_MODEL + (c + 1) * cw])
        hv = _dot(h, win_ref[:, 2 * D_MODEL + c * cw:2 * D_MODEL + (c + 1) * cw])
        u = cg * hv
        ubuf_ref[SUBLANES:SUBLANES + tm, cols] = u
        um1 = jnp.where(t >= 1, ubuf_ref[SUBLANES - 1:SUBLANES - 1 + tm, cols], p1_ref[:, cols])
        um2 = jnp.where(t >= 2, ubuf_ref[SUBLANES - 2:SUBLANES - 2 + tm, cols], p2_ref[:, cols])
        conv = ck[0:1, cols] * um2 + ck[1:2, cols] * um1 + ck[2:3, cols] * u
        z_ref[:, cols] = (bg * conv).astype(BF16)
    r = tail_ref.shape[0]
    tail_ref[...] = ubuf_ref[SUBLANES + tm - r:SUBLANES + tm, :]

    y1 = x + g1_ref[...] * _dot(z_ref[...], wout_ref[...])
    h2 = _norm_mod(y1, n2_ref[...], sc2_ref[...], sh2_ref[...]).astype(BF16)
    y2 = y1 + g2_ref[...] * _ffn(h2, wgu_ref, wd_ref)
    ms = jnp.mean(y2 * y2, axis=-1, keepdims=True)
    y_ref[...] = (y2 * lax.rsqrt(ms + EPS)) * nf_ref[...]


def _conv_layer(x, mods, p1, p2, n1, n2, nf, win, ck, wout, wgu, wd, *, layer, seg, full_tail):
    nb, rows, _ = x.shape
    per_row = mods[0].shape[1] == rows
    tm, row, mod = _row_specs(nb, rows, per_row)
    if per_row:
        prefix = row
    else:
        prefix = pl.BlockSpec((None, 1, D_MODEL), lambda b, j: (0, 0, 0))
    if full_tail:
        tail_spec, tail_rows = row, rows
    else:
        tail_spec, tail_rows = pl.BlockSpec((None, SUBLANES, D_MODEL), lambda b, j: (b, 0, 0)), SUBLANES
    vec = _const_spec((1, D_MODEL))
    return pl.pallas_call(
        functools.partial(_conv_layer_kernel, seg=seg),
        grid=(nb, rows // tm),
        in_specs=[row] + [mod] * 6 + [prefix, prefix, vec, vec, vec,
                                      _const_spec(win.shape), _const_spec(ck.shape), _const_spec(wout.shape),
                                      _layer_spec(wgu, layer), _layer_spec(wd, layer)],
        out_specs=(row, tail_spec),
        out_shape=(jax.ShapeDtypeStruct(x.shape, F32),
                   jax.ShapeDtypeStruct((nb, tail_rows, D_MODEL), F32)),
        scratch_shapes=[pltpu.VMEM((tm + SUBLANES, D_MODEL), F32), pltpu.VMEM((tm, D_MODEL), BF16)],
        compiler_params=_params("arbitrary", "arbitrary"),
        name="conv_layer",
    )(x, *mods, p1, p2, n1, n2, nf, win, ck, wout, wgu, wd)


def _rope_tables(pos):
    half = ROT_DIM // 2
    inv = ROPE_THETA ** (-jnp.arange(0, ROT_DIM, 2, dtype=F32) / ROT_DIM)
    ang = pos[:, None] * inv[None, :]
    cos, sin = jnp.cos(ang), jnp.sin(ang)
    n = pos.shape[0]
    rest = HEAD_DIM - ROT_DIM
    c = jnp.concatenate([cos, cos, jnp.ones((n, rest), F32)], axis=1)
    s1 = jnp.concatenate([-sin, jnp.zeros((n, half + rest), F32)], axis=1)
    s2 = jnp.concatenate([jnp.zeros((n, half), F32), sin, jnp.zeros((n, rest), F32)], axis=1)
    return tuple(jnp.tile(a, (1, LANES // HEAD_DIM)) for a in (c, s1, s2))


def kernel(x_prompt, x_sample, c_prompt, c_sample, cache_k, cache_v, cache_kidx, state_conv, page_table,
           ada_w, ada_b, norm1_g, norm2_g, final_g, attn_w_in, attn_w_out, conv_w_in, conv_k, conv_w_out,
           ffn_w_gu, ffn_w_down):
    nb, t, _ = x_prompt.shape
    ns, tn, _ = x_sample.shape
    n_pages = page_table.shape[1]
    past = n_pages * PAGE_SIZE
    n_phys = cache_k.shape[1]
    rows_s = ns * tn

    pad = (-(nb + ns)) % SUBLANES
    c_all = jnp.concatenate([c_prompt, c_sample, jnp.zeros((pad, D_MODEL), F32)], axis=0)
    mod = _modulation(c_all, ada_w, ada_b)

    def mods(layer):
        m = mod[layer].reshape(-1, 6, D_MODEL)
        prompt = [m[:nb, i][:, None, :] for i in range(6)]
        sample = [jnp.repeat(m[nb:nb + ns, i], tn, axis=0)[None] for i in range(6)]
        return prompt, sample

    vec = lambda a: a.reshape(1, D_MODEL)
    w_in = jnp.pad(attn_w_in[0], ((0, 0), (0, ATTN_IN_PAD - ATTN_IN))).astype(BF16)
    w_o = attn_w_out[0].astype(BF16)
    wgu = ffn_w_gu.astype(BF16)
    wd = ffn_w_down.astype(BF16)

    (sh1p, sc1p, g1p, sh2p, sc2p, g2p), (sh1s, sc1s, g1s, sh2s, sc2s, g2s) = mods(0)
    rope_p = _rope_tables(jnp.arange(t, dtype=F32))
    k_p, v_p, ki_p, qt, qit, kg, vt, kib, wit = _proj_prompt(
        x_prompt, sc1p, sh1p, vec(norm1_g[0]), w_in, rope_p)
    o_p = _prompt_attention(qt, qit, wit, kg, vt, kib)
    y_p = _attn_out_ffn(x_prompt, o_p, g1p, sc2p, sh2p, g2p, vec(norm2_g[0]), w_o, wgu, wd, 0)

    pos_s = jnp.tile(past + jnp.arange(tn, dtype=F32), ns)
    rope_s = _rope_tables(pos_s)
    xs = x_sample.reshape(rows_s, D_MODEL)
    proj = _proj_sample(xs, sc1s[0], sh1s[0], vec(norm1_g[0]), w_in, rope_s)
    o = 0
    q_s = proj[:, o:o + Q_COLS].reshape(ns, tn, N_KV_HEADS, HEADS_PER_KV, HEAD_DIM); o += Q_COLS
    k_s = proj[:, o:o + KV_COLS].reshape(ns, tn, KV_COLS); o += KV_COLS
    v_s = proj[:, o:o + KV_COLS].reshape(ns, tn, KV_COLS); o += KV_COLS
    qi_s = proj[:, o:o + IQ_COLS].reshape(ns, tn, IDX_HEADS, IDX_DIM); o += IQ_COLS
    ki_s = proj[:, o:o + IDX_DIM].reshape(ns, tn, IDX_DIM); o += IDX_DIM
    wi_s = proj[:, o:o + IDX_HEADS].reshape(ns, tn, IDX_HEADS)

    qpad = ((0, 0), (0, SUBLANES - tn))
    qi8 = jnp.pad(qi_s, qpad + ((0, 0), (0, 0))).transpose(0, 2, 1, 3)
    qi8 = qi8.reshape(ns, IDX_HEADS * SUBLANES, IDX_DIM).astype(BF16)
    w8 = jnp.pad(wi_s, qpad + ((0, 0),)).transpose(0, 2, 1).reshape(ns, IDX_HEADS * SUBLANES, 1)
    slot_pad = (0, PAGE_SIZE - tn)
    ki_new_t = jnp.pad(ki_s.transpose(0, 2, 1), ((0, 0), (0, 0), slot_pad))
    scores = _sample_scores(page_table, qi8, w8, cache_kidx[0].transpose(0, 2, 1), ki_new_t)
    n_keys = scores.shape[2]
    scores_t = scores[:, :tn, :].transpose(2, 0, 1).reshape(n_keys, rows_s)
    bias_t = _sample_select(scores_t, past, tn)
    bias8 = jnp.pad(bias_t.reshape(n_keys, ns, tn).transpose(1, 2, 0), qpad + ((0, 0),))

    q8 = jnp.pad(q_s * HEAD_DIM ** -0.5, qpad + ((0, 0), (0, 0), (0, 0))).transpose(0, 2, 3, 1, 4)
    q8 = q8.reshape(ns, N_KV_HEADS, HEADS_PER_KV * SUBLANES, HEAD_DIM).astype(BF16)
    new_t = lambda a: jnp.pad(a.reshape(ns, tn, N_KV_HEADS, HEAD_DIM).transpose(0, 2, 1, 3),
                              ((0, 0), (0, 0), (0, SUBLANES - tn), (0, LANES - HEAD_DIM)))
    o_g = _sample_attention(page_table, q8, bias8,
                            cache_k[0].transpose(0, 2, 3, 1), cache_v[0].transpose(0, 2, 3, 1),
                            new_t(k_s), new_t(v_s))
    o_g = o_g.reshape(ns, N_KV_HEADS, HEADS_PER_KV, SUBLANES, HEAD_DIM)[:, :, :, :tn]
    o_s = o_g.transpose(0, 3, 1, 2, 4).reshape(1, rows_s, Q_COLS).astype(BF16)
    y_s = _attn_out_ffn(xs[None], o_s, g1s, sc2s, sh2s, g2s, vec(norm2_g[0]), w_o, wgu, wd, 0)

    (sh1p, sc1p, g1p, sh2p, sc2p, g2p), (sh1s, sc1s, g1s, sh2s, sc2s, g2s) = mods(1)
    win = conv_w_in[0].astype(BF16)
    wout = conv_w_out[0].astype(BF16)
    zero_prefix = jnp.zeros((1, 1, D_MODEL), F32)
    out_p, tail_p = _conv_layer(
        y_p, (sc1p, sh1p, g1p, sc2p, sh2p, g2p), zero_prefix, zero_prefix,
        vec(norm1_g[1]), vec(norm2_g[1]), vec(final_g), win, conv_k[0], wout, wgu, wd,
        layer=1, seg=t, full_tail=False)
    st = state_conv[0]
    zeros_row = jnp.zeros((ns, 1, D_MODEL), F32)
    p1 = jnp.concatenate([st[:, 1:2], zeros_row, zeros_row, zeros_row], axis=1).reshape(1, rows_s, D_MODEL)
    p2 = jnp.concatenate([st[:, 0:1], st[:, 1:2], zeros_row, zeros_row], axis=1).reshape(1, rows_s, D_MODEL)
    out_s, u_s = _conv_layer(
        y_s, (sc1s, sh1s, g1s, sc2s, sh2s, g2s), p1, p2,
        vec(norm1_g[1]), vec(norm2_g[1]), vec(final_g), win, conv_k[0], wout, wgu, wd,
        layer=1, seg=tn, full_tail=True)

    keep = CONV_WIDTH - 1
    return (
        out_p,
        out_s.reshape(ns, tn, D_MODEL),
        k_p.reshape(1, nb, t, N_KV_HEADS, HEAD_DIM),
        v_p.reshape(1, nb, t, N_KV_HEADS, HEAD_DIM),
        ki_p[None],
        tail_p[None, :, SUBLANES - keep:, :],
        k_s.reshape(1, ns, tn, N_KV_HEADS, HEAD_DIM),
        v_s.reshape(1, ns, tn, N_KV_HEADS, HEAD_DIM),
        ki_s[None],
        u_s.reshape(ns, tn, D_MODEL)[None, :, tn - keep:, :],
    )
```

```python
import functools

import jax
import jax.numpy as jnp
from jax import lax
from jax.experimental import pallas as pl
from jax.experimental.pallas import tpu as pltpu

F32 = jnp.float32
BF16 = jnp.bfloat16

D_MODEL = 1024
N_HEADS = 16
HEAD_DIM = 64
N_KV_HEADS = 4
HEADS_PER_KV = N_HEADS // N_KV_HEADS
ROT_DIM = 16
ROPE_THETA = 500000.0
IDX_HEADS = 8
IDX_DIM = 64
TOPK = 256
PAGE_SIZE = 128
CONV_WIDTH = 3
D_FF = 2816
EPS = 1e-6
Q_COLS = N_HEADS * HEAD_DIM
KV_COLS = N_KV_HEADS * HEAD_DIM
IQ_COLS = IDX_HEADS * IDX_DIM
ATTN_IN = Q_COLS + 2 * KV_COLS + IQ_COLS + IDX_DIM + IDX_HEADS
ATTN_IN_PAD = 2176
LANES = 128
SUBLANES = 8
MASK_BIAS = -1e30
F32_MAX = 3.4028234663852886e38
LOG2_E = 1.4426950408889634
VMEM_LIMIT = 60 * 1024 * 1024

ROW_TILE = 512
Q_TILE = 256
SCORE_CHUNK = 512
SCORE_SLAB = 128
ATTN_CHUNK = 256
V_ROWS = HEAD_DIM + 16
FFN_CHUNK = 256


def _dot(a, b):
    return jnp.dot(a, b, preferred_element_type=F32)


def _dot_nt(a, b):
    return lax.dot_general(a, b, (((1,), (1,)), ((), ())), preferred_element_type=F32)


def _silu(x):
    return x / (1.0 + jnp.exp(-x))


def _norm_mod(x, g, sc, sh):
    ms = jnp.mean(x * x, axis=-1, keepdims=True)
    y = x * lax.rsqrt(ms + EPS)
    return (y * g) * (1.0 + sc) + sh


def _params(*sem, flags=None):
    return pltpu.CompilerParams(dimension_semantics=sem, vmem_limit_bytes=VMEM_LIMIT, flags=flags)


def _const_spec(shape):
    nd = len(shape)
    return pl.BlockSpec(shape, lambda *_: (0,) * nd, pipeline_mode=pl.Buffered(1))


def _mod_kernel(c_ref, w_ref, b_ref, o_ref):
    s = _silu(c_ref[...]).astype(BF16)
    o_ref[...] = _dot(s, w_ref[...].astype(BF16)) + b_ref[...]


def _modulation(c_all, ada_w, ada_b):
    depth, _, n = ada_w.shape
    rows = c_all.shape[0]
    tn = 1024
    return pl.pallas_call(
        _mod_kernel,
        grid=(depth, n // tn),
        in_specs=[
            pl.BlockSpec((rows, D_MODEL), lambda l, j: (0, 0)),
            pl.BlockSpec((None, D_MODEL, tn), lambda l, j: (l, 0, j)),
            pl.BlockSpec((None, 1, tn), lambda l, j: (l, 0, j)),
        ],
        out_specs=pl.BlockSpec((None, rows, tn), lambda l, j: (l, 0, j)),
        out_shape=jax.ShapeDtypeStruct((depth, rows, n), F32),
        compiler_params=_params("arbitrary", "arbitrary"),
        name="adaln_modulation",
    )(c_all, ada_w, ada_b.reshape(depth, 1, n))


def _rope_block(x, c, s1, s2):
    return x * c + pltpu.roll(x, LANES - ROT_DIM // 2, 1) * s1 + pltpu.roll(x, ROT_DIM // 2, 1) * s2


def _rope_cols(p, c, s1, s2):
    nblk = p.shape[1] // LANES
    return [_rope_block(p[:, j * LANES:(j + 1) * LANES], c, s1, s2) for j in range(nblk)]


def _last_block_tables(c, s1, s2):
    lane = lax.broadcasted_iota(jnp.int32, c.shape, 1)
    is_key = lane < IDX_DIM
    return jnp.where(is_key, c, 1.0), jnp.where(is_key, s1, 0.0), jnp.where(is_key, s2, 0.0)


def _proj_prompt_kernel(x_ref, sc_ref, sh_ref, g_ref, w_ref, c_ref, s1_ref, s2_ref,
                        k_ref, v_ref, ki_ref, qt_ref, qit_ref, kg_ref, vt_ref, kib_ref, wit_ref):
    h = _norm_mod(x_ref[...], g_ref[...], sc_ref[...], sh_ref[...]).astype(BF16)
    c, s1, s2 = c_ref[...], s1_ref[...], s2_ref[...]

    pq = _dot(h, w_ref[:, 0:Q_COLS])
    for j, blk in enumerate(_rope_cols(pq, c, s1, s2)):
        qt_ref[j * LANES:(j + 1) * LANES, :] = (blk * (HEAD_DIM ** -0.5 * LOG2_E)).astype(BF16).T

    pk = _dot(h, w_ref[:, Q_COLS:Q_COLS + KV_COLS])
    k = jnp.concatenate(_rope_cols(pk, c, s1, s2), axis=1)
    k_ref[...] = k
    for g in range(N_KV_HEADS):
        kg_ref[g] = k[:, g * HEAD_DIM:(g + 1) * HEAD_DIM].astype(BF16)

    pv = _dot(h, w_ref[:, Q_COLS + KV_COLS:Q_COLS + 2 * KV_COLS])
    v_ref[...] = pv
    ones = jnp.ones((V_ROWS - HEAD_DIM, ATTN_CHUNK), BF16)
    for cc in range(pv.shape[0] // ATTN_CHUNK):
        vt = pv[cc * ATTN_CHUNK:(cc + 1) * ATTN_CHUNK, :].astype(BF16).T
        for g in range(N_KV_HEADS):
            vt_ref[cc, g * V_ROWS:g * V_ROWS + HEAD_DIM, :] = vt[g * HEAD_DIM:(g + 1) * HEAD_DIM, :]
            vt_ref[cc, g * V_ROWS + HEAD_DIM:(g + 1) * V_ROWS, :] = ones

    o = Q_COLS + 2 * KV_COLS
    pqi = _dot(h, w_ref[:, o:o + IQ_COLS])
    for j, blk in enumerate(_rope_cols(pqi, c, s1, s2)):
        qit_ref[j * LANES:(j + 1) * LANES, :] = blk.astype(BF16).T

    pl_ = _dot(h, w_ref[:, o + IQ_COLS:ATTN_IN_PAD])
    last = _rope_block(pl_, *_last_block_tables(c, s1, s2))
    ki = last[:, 0:IDX_DIM]
    ki_ref[...] = ki
    kib_ref[...] = ki.astype(BF16)
    wit_ref[...] = last.T[IDX_DIM:IDX_DIM + IDX_HEADS, :]


def _proj_prompt(x, sc, sh, g, w, rope):
    nb, t, _ = x.shape
    tm = ROW_TILE
    row = lambda width: pl.BlockSpec((None, tm, width), lambda b, j: (b, j, 0))
    mod = pl.BlockSpec((None, 1, D_MODEL), lambda b, j: (b, 0, 0))
    tab = pl.BlockSpec((tm, LANES), lambda b, j: (j, 0))
    col = lambda height: pl.BlockSpec((None, height, tm), lambda b, j: (b, 0, j))
    out_shape = (
        jax.ShapeDtypeStruct((nb, t, KV_COLS), F32),
        jax.ShapeDtypeStruct((nb, t, KV_COLS), F32),
        jax.ShapeDtypeStruct((nb, t, IDX_DIM), F32),
        jax.ShapeDtypeStruct((nb, Q_COLS, t), BF16),
        jax.ShapeDtypeStruct((nb, IQ_COLS, t), BF16),
        jax.ShapeDtypeStruct((nb, N_KV_HEADS, t, HEAD_DIM), BF16),
        jax.ShapeDtypeStruct((nb, t // ATTN_CHUNK, N_KV_HEADS * V_ROWS, ATTN_CHUNK), BF16),
        jax.ShapeDtypeStruct((nb, t, IDX_DIM), BF16),
        jax.ShapeDtypeStruct((nb, IDX_HEADS, t), F32),
    )
    out_specs = (
        row(KV_COLS), row(KV_COLS), row(IDX_DIM), col(Q_COLS), col(IQ_COLS),
        pl.BlockSpec((None, N_KV_HEADS, tm, HEAD_DIM), lambda b, j: (b, 0, j, 0)),
        pl.BlockSpec((None, tm // ATTN_CHUNK, N_KV_HEADS * V_ROWS, ATTN_CHUNK), lambda b, j: (b, j, 0, 0)),
        row(IDX_DIM), col(IDX_HEADS),
    )
    return pl.pallas_call(
        _proj_prompt_kernel,
        grid=(nb, t // tm),
        in_specs=[row(D_MODEL), mod, mod, _const_spec((1, D_MODEL)),
                  _const_spec((D_MODEL, ATTN_IN_PAD)), tab, tab, tab],
        out_specs=out_specs,
        out_shape=out_shape,
        compiler_params=_params("arbitrary", "arbitrary"),
        name="attn_proj_prompt",
    )(x, sc, sh, g, w, *rope)


def _proj_sample_kernel(x_ref, sc_ref, sh_ref, g_ref, w_ref, c_ref, s1_ref, s2_ref, p_ref):
    h = _norm_mod(x_ref[...], g_ref[...], sc_ref[...], sh_ref[...]).astype(BF16)
    c, s1, s2 = c_ref[...], s1_ref[...], s2_ref[...]
    p = _dot(h, w_ref[...])
    v_lo = (Q_COLS + KV_COLS) // LANES
    v_hi = (Q_COLS + 2 * KV_COLS) // LANES
    nblk = ATTN_IN_PAD // LANES
    for j in range(nblk):
        blk = p[:, j * LANES:(j + 1) * LANES]
        if j == nblk - 1:
            blk = _rope_block(blk, *_last_block_tables(c, s1, s2))
        elif not (v_lo <= j < v_hi):
            blk = _rope_block(blk, c, s1, s2)
        p_ref[:, j * LANES:(j + 1) * LANES] = blk


def _proj_sample(x, sc, sh, g, w, rope):
    rows = x.shape[0]
    full = lambda width: pl.BlockSpec((rows, width), lambda i: (0, 0))
    return pl.pallas_call(
        _proj_sample_kernel,
        grid=(1,),
        in_specs=[full(D_MODEL), full(D_MODEL), full(D_MODEL),
                  pl.BlockSpec((1, D_MODEL), lambda i: (0, 0)),
                  pl.BlockSpec((D_MODEL, ATTN_IN_PAD), lambda i: (0, 0)),
                  full(LANES), full(LANES), full(LANES)],
        out_specs=full(ATTN_IN_PAD),
        out_shape=jax.ShapeDtypeStruct((rows, ATTN_IN_PAD), F32),
        compiler_params=_params("arbitrary"),
        name="attn_proj_sample",
    )(x, sc, sh, g, w, *rope)


FOLD_ROWS = 32
BISECT_STEPS_PER_TEST = 4


def _fold(x, op):
    rows, tq = x.shape
    x3 = x.reshape(rows // FOLD_ROWS, FOLD_ROWS, tq)
    return {"sum": x3.sum, "max": x3.max, "min": x3.min}[op](axis=0)


def _stats_init(tq):
    zeros = jnp.zeros((FOLD_ROWS, tq), F32)
    return (jnp.full((FOLD_ROWS, tq), -jnp.inf, F32), jnp.full((FOLD_ROWS, tq), jnp.inf, F32),
            zeros, zeros, zeros)


def _stats_update(stats, s):
    mx, mn, n_above, n_ge0, n_gt0 = stats
    ones_where = lambda m: _fold(jnp.where(m, 1.0, 0.0), "sum")
    above = s > -jnp.inf
    return (jnp.maximum(mx, _fold(s, "max")),
            jnp.minimum(mn, _fold(jnp.where(above, s, jnp.inf), "min")),
            n_above + ones_where(above), n_ge0 + ones_where(s >= 0.0), n_gt0 + ones_where(s > 0.0))


def _select_to_bias(sc_ref, tie_ref, qpos, nk, kc, stats):
    tq = sc_ref.shape[1]
    topk = float(TOPK)

    slab = 4 * FOLD_ROWS

    def for_slabs(fn, init, slab=slab):
        def body(c, carry):
            for j in range(kc // slab):
                start = pl.multiple_of(c * kc + j * slab, slab)
                idx = start + lax.broadcasted_iota(jnp.int32, (slab, tq), 0)
                carry = fn(pl.ds(start, slab), idx, carry)
            return carry
        return lax.fori_loop(0, nk, body, init)

    def ones_where(m):
        return _fold(jnp.where(m, 1.0, 0.0), "sum")

    def count(pred):
        acc = for_slabs(lambda rows, idx, acc: acc + ones_where(pred(rows)), jnp.zeros((FOLD_ROWS, tq), F32))
        return acc.sum(axis=0, keepdims=True)

    def count_ge(t):
        return count(lambda rows: sc_ref[rows, :] >= t)

    mx, mn, n_above, n_ge0, n_gt0 = stats
    mx = mx.max(axis=0, keepdims=True)
    mn = mn.min(axis=0, keepdims=True)
    n_above = n_above.sum(axis=0, keepdims=True)
    n_ge0 = n_ge0.sum(axis=0, keepdims=True)
    n_gt0 = n_gt0.sum(axis=0, keepdims=True)
    n_max = count_ge(mx)

    few = n_above < topk
    flat = n_max >= topk
    zero = (n_gt0 < topk) & (n_ge0 >= topk)
    pos = n_gt0 >= topk
    lo = jnp.where(few, -jnp.inf, jnp.where(flat, mx, jnp.where(zero | pos, 0.0, mn)))
    hi = jnp.where(few, mn, jnp.where(flat, jnp.inf, jnp.where(pos, mx, 0.0)))
    n_lo = jnp.where(few, 2.0 * topk, jnp.where(flat, n_max, jnp.where(zero | pos, n_ge0, n_above)))
    n_hi = jnp.where(few, n_above, jnp.where(flat, 0.0, jnp.where(zero, n_gt0, jnp.where(pos, n_max, n_ge0))))
    done = jnp.where(few | flat | zero | (n_lo == topk), 1.0, 0.0)

    def not_finished(d):
        return (jnp.min(d) < 0.5).astype(jnp.int32)

    def bisect_step(state):
        lo, hi, n_lo, n_hi, done = state
        mid = jnp.clip(0.5 * lo + 0.5 * hi, -F32_MAX, F32_MAX)
        stuck = (mid <= lo) | (mid >= hi)
        n_mid = count_ge(mid)
        move = (done < 0.5) & jnp.logical_not(stuck)
        up = move & (n_mid >= topk)
        down = move & (n_mid < topk)
        lo = jnp.where(up, mid, lo)
        n_lo = jnp.where(up, n_mid, n_lo)
        hi = jnp.where(down, mid, hi)
        n_hi = jnp.where(down, n_mid, n_hi)
        done = jnp.where(stuck | (n_lo == topk), 1.0, done)
        return lo, hi, n_lo, n_hi, done

    def bisect(carry):
        state = carry[:5]
        for _ in range(BISECT_STEPS_PER_TEST):
            state = bisect_step(state)
        return (*state, not_finished(state[4]))

    lo, hi, n_lo, n_hi, done, _ = lax.while_loop(
        lambda carry: carry[5] > 0, bisect, (lo, hi, n_lo, n_hi, done, not_finished(done)))

    thr = lo
    exact = n_lo == topk
    need = topk - n_hi
    all_keys = jnp.full((1, tq), 2 ** 30, jnp.int32)

    def tie_break():
        def mark(rows, idx, _):
            tie_ref[rows, :] = jnp.where(sc_ref[rows, :] == thr, idx, 2 ** 30)
            return 0
        for_slabs(mark, 0)

        n_bits = (sc_ref.shape[0] - 1).bit_length()

        def step(i, j):
            cand = j + lax.shift_right_logical(jnp.int32(1 << (n_bits - 1)), i)
            n_before = count(lambda rows: tie_ref[rows, :] < cand)
            return jnp.where(n_before < need, cand, j)
        return lax.fori_loop(0, n_bits, step, jnp.zeros((1, tq), jnp.int32))

    any_tie = jnp.min(jnp.where(exact, 1.0, 0.0)) < 0.5

    def write(keep_fn):
        def body(rows, idx, _):
            sc_ref[rows, :] = jnp.where(keep_fn(sc_ref[rows, :], idx), 0.0, MASK_BIAS)
            return 0
        return for_slabs(body, 0, slab=kc)

    def write_with_ties():
        last_tied = jnp.minimum(jnp.where(exact, all_keys, tie_break()), qpos)
        return write(lambda s, idx: (s > thr) | ((s == thr) & (idx <= last_tied)))

    lax.cond(any_tie, write_with_ties, lambda: write(lambda s, idx: s >= thr))


def _prompt_attn_kernel(qt_ref, qit_ref, wit_ref, kg_ref, vt_ref, kib_ref, o_ref,
                        sc_ref, tie_ref, acc_ref, m_ref, mx_ref, s_ref):
    i = pl.program_id(1)
    tq = qt_ref.shape[1]
    qpos = i * tq + lax.broadcasted_iota(jnp.int32, (1, tq), 1)
    n_keys = (i + 1) * tq

    n_score = (n_keys + SCORE_CHUNK - 1) // SCORE_CHUNK

    def score_chunk(c, stats):
        for j in range(SCORE_CHUNK // SCORE_SLAB):
            start = pl.multiple_of(c * SCORE_CHUNK + j * SCORE_SLAB, SCORE_SLAB)
            kc = kib_ref[pl.ds(start, SCORE_SLAB), :]
            acc = jnp.zeros((SCORE_SLAB, tq), F32)
            for h in range(IDX_HEADS):
                d = _dot(kc, qit_ref[h * IDX_DIM:(h + 1) * IDX_DIM, :])
                acc = acc + jnp.maximum(d, 0.0) * wit_ref[h:h + 1, :]
            idx = start + lax.broadcasted_iota(jnp.int32, (SCORE_SLAB, tq), 0)
            s = jnp.where(idx <= qpos, acc, -jnp.inf)
            sc_ref[pl.ds(start, SCORE_SLAB), :] = s
            stats = _stats_update(stats, s)
        return stats

    stats = lax.fori_loop(0, n_score, score_chunk, _stats_init(tq))

    _select_to_bias(sc_ref, tie_ref, qpos, n_score, SCORE_CHUNK, stats)

    m_ref[...] = jnp.full(m_ref.shape, MASK_BIAS, F32)
    acc_ref[...] = jnp.zeros(acc_ref.shape, F32)

    def attn_chunk(c, _):
        start = pl.multiple_of(c * ATTN_CHUNK, ATTN_CHUNK)
        for h in range(N_HEADS):
            s = _dot(kg_ref[h // HEADS_PER_KV, pl.ds(start, ATTN_CHUNK), :],
                     qt_ref[h * HEAD_DIM:(h + 1) * HEAD_DIM, :])
            s = s + sc_ref[pl.ds(start, ATTN_CHUNK), :]
            s_ref[h] = s
            mx_ref[h] = s.max(axis=0, keepdims=True)
        for h in range(N_HEADS):
            g = h // HEADS_PER_KV
            m_old = m_ref[h]
            m_new = jnp.maximum(m_old, mx_ref[h])
            alpha = jnp.exp2(m_old - m_new)
            p = jnp.exp2(s_ref[h] - m_new).astype(BF16)
            vt = vt_ref[c, g * V_ROWS:(g + 1) * V_ROWS, :]
            acc_ref[h] = alpha * acc_ref[h] + _dot(vt, p)
            m_ref[h] = m_new
        return 0

    lax.fori_loop(0, (i + 1) * (tq // ATTN_CHUNK), attn_chunk, 0)

    for pair in range(N_HEADS // 2):
        two = []
        for h in (2 * pair, 2 * pair + 1):
            a = acc_ref[h]
            two.append(a[0:HEAD_DIM, :] / a[HEAD_DIM:HEAD_DIM + 1, :])
        col = 2 * pair * HEAD_DIM
        o_ref[:, col:col + 2 * HEAD_DIM] = jnp.concatenate(two, axis=0).T.astype(BF16)


def _prompt_attention(qt, qit, wit, kg, vt, kib):
    nb, _, t = qt.shape
    tq = Q_TILE
    col = lambda height: pl.BlockSpec((None, height, tq), lambda b, i: (b, 0, i))
    return pl.pallas_call(
        _prompt_attn_kernel,
        grid=(nb, t // tq),
        in_specs=[
            col(Q_COLS), col(IQ_COLS), col(IDX_HEADS),
            pl.BlockSpec((None, N_KV_HEADS, t, HEAD_DIM), lambda b, i: (b, 0, 0, 0)),
            pl.BlockSpec((None, t // ATTN_CHUNK, N_KV_HEADS * V_ROWS, ATTN_CHUNK), lambda b, i: (b, 0, 0, 0)),
            pl.BlockSpec((None, t, IDX_DIM), lambda b, i: (b, 0, 0)),
        ],
        out_specs=pl.BlockSpec((None, tq, Q_COLS), lambda b, i: (b, i, 0)),
        out_shape=jax.ShapeDtypeStruct((nb, t, Q_COLS), BF16),
        scratch_shapes=[
            pltpu.VMEM((t, tq), F32),
            pltpu.VMEM((t, tq), jnp.int32),
            pltpu.VMEM((N_HEADS, V_ROWS, tq), F32),
            pltpu.VMEM((N_HEADS, 1, tq), F32),
            pltpu.VMEM((N_HEADS, 1, tq), F32),
            pltpu.VMEM((N_HEADS, ATTN_CHUNK, tq), F32),
        ],
        compiler_params=_params("arbitrary", "arbitrary"),
        name="prompt_attention",
    )(qt, qit, wit, kg, vt, kib)


def _sample_score_kernel(pt_ref, qi_ref, w_ref, *rest):
    del pt_ref
    n_pages = len(rest) - 3
    pages, new_ref, o_ref, kall_ref = rest[:n_pages], rest[n_pages], rest[n_pages + 1], rest[n_pages + 2]
    for j in range(n_pages):
        kall_ref[:, j * PAGE_SIZE:(j + 1) * PAGE_SIZE] = pages[j][...].astype(BF16)
    kall_ref[:, n_pages * PAGE_SIZE:(n_pages + 1) * PAGE_SIZE] = new_ref[...].astype(BF16)
    d = _dot(qi_ref[...], kall_ref[...])
    r = jnp.maximum(d, 0.0) * w_ref[...]
    o_ref[...] = r.reshape(IDX_HEADS, SUBLANES, r.shape[1]).sum(axis=0)


def _sample_scores(page_table, qi8, w8, cache_ki_t, ki_new_t):
    nb, n_pages = page_table.shape
    n_keys = (n_pages + 1) * PAGE_SIZE
    rows = IDX_HEADS * SUBLANES
    grid_spec = pltpu.PrefetchScalarGridSpec(
        num_scalar_prefetch=1,
        grid=(nb,),
        in_specs=[pl.BlockSpec((None, rows, IDX_DIM), lambda b, pt: (b, 0, 0)),
                  pl.BlockSpec((None, rows, 1), lambda b, pt: (b, 0, 0))]
        + [pl.BlockSpec((None, IDX_DIM, PAGE_SIZE), lambda b, pt, j=j: (pt[b, j], 0, 0))
           for j in range(n_pages)]
        + [pl.BlockSpec((None, IDX_DIM, PAGE_SIZE), lambda b, pt: (b, 0, 0))],
        out_specs=pl.BlockSpec((None, SUBLANES, n_keys), lambda b, pt: (b, 0, 0)),
        scratch_shapes=[pltpu.VMEM((IDX_DIM, n_keys), BF16)],
    )
    return pl.pallas_call(
        _sample_score_kernel,
        grid_spec=grid_spec,
        out_shape=jax.ShapeDtypeStruct((nb, SUBLANES, n_keys), F32),
        compiler_params=_params("arbitrary"),
        name="sample_indexer_scores",
    )(page_table, qi8, w8, *([cache_ki_t] * n_pages), ki_new_t)


def _sample_select_kernel(s_ref, o_ref, sc_ref, tie_ref, *, past, period):
    tq = s_ref.shape[1]
    n_keys = s_ref.shape[0]
    lane = lax.broadcasted_iota(jnp.int32, (1, tq), 1)
    qpos = past + (lane & (period - 1))
    nk = n_keys // PAGE_SIZE
    stats = _stats_init(tq)
    for c in range(nk):
        idx = c * PAGE_SIZE + lax.broadcasted_iota(jnp.int32, (PAGE_SIZE, tq), 0)
        s = jnp.where(idx <= qpos, s_ref[c * PAGE_SIZE:(c + 1) * PAGE_SIZE, :], -jnp.inf)
        sc_ref[c * PAGE_SIZE:(c + 1) * PAGE_SIZE, :] = s
        stats = _stats_update(stats, s)
    _select_to_bias(sc_ref, tie_ref, qpos, nk, PAGE_SIZE, stats)
    o_ref[...] = sc_ref[...]


def _sample_select(scores_t, past, period):
    n_keys, nq = scores_t.shape
    tq = Q_TILE
    blk = pl.BlockSpec((n_keys, tq), lambda i: (0, i))
    return pl.pallas_call(
        functools.partial(_sample_select_kernel, past=past, period=period),
        grid=(nq // tq,),
        in_specs=[blk],
        out_specs=blk,
        out_shape=jax.ShapeDtypeStruct((n_keys, nq), F32),
        scratch_shapes=[pltpu.VMEM((n_keys, tq), F32), pltpu.VMEM((n_keys, tq), jnp.int32)],
        compiler_params=_params("arbitrary"),
        name="sample_select",
    )(scores_t)


def _sample_attn_kernel(pt_ref, q_ref, bias_ref, *rest):
    del pt_ref
    n_pages = (len(rest) - 8) // 2
    kpages, vpages = rest[:n_pages], rest[n_pages:2 * n_pages]
    knew_ref, vnew_ref, o_ref, kall_ref, vall_ref, s_ref, p_ref, l_ref = rest[2 * n_pages:]
    for j in range(n_pages):
        kall_ref[:, :, j * PAGE_SIZE:(j + 1) * PAGE_SIZE] = kpages[j][...].astype(BF16)
        vall_ref[:, :, j * PAGE_SIZE:(j + 1) * PAGE_SIZE] = vpages[j][...].astype(BF16)
    fill = jnp.zeros((PAGE_SIZE - SUBLANES, LANES), F32)
    for g in range(N_KV_HEADS):
        for new_ref, all_ref in ((knew_ref, kall_ref), (vnew_ref, vall_ref)):
            page = jnp.concatenate([new_ref[g], fill], axis=0).T
            all_ref[g, :, n_pages * PAGE_SIZE:(n_pages + 1) * PAGE_SIZE] = page[0:HEAD_DIM, :].astype(BF16)
    bias = bias_ref[...]
    n_keys = bias.shape[1]
    bias = jnp.broadcast_to(bias[None], (HEADS_PER_KV, SUBLANES, n_keys)).reshape(
        HEADS_PER_KV * SUBLANES, n_keys)
    for g in range(N_KV_HEADS):
        s_ref[g] = _dot(q_ref[g], kall_ref[g]) + bias
    for g in range(N_KV_HEADS):
        s = s_ref[g]
        m = s.max(axis=1, keepdims=True)
        p = jnp.exp(s - m)
        l_ref[g] = p.sum(axis=1, keepdims=True)
        p_ref[g] = p.astype(BF16)
    for g in range(N_KV_HEADS):
        o_ref[g] = _dot_nt(p_ref[g], vall_ref[g]) / l_ref[g]


def _sample_attention(page_table, q8, bias8, cache_k_t, cache_v_t, k_new_t, v_new_t):
    nb, n_pages = page_table.shape
    n_keys = (n_pages + 1) * PAGE_SIZE
    rows = HEADS_PER_KV * SUBLANES
    page = [pl.BlockSpec((None, N_KV_HEADS, HEAD_DIM, PAGE_SIZE), lambda b, pt, j=j: (pt[b, j], 0, 0, 0))
            for j in range(n_pages)]
    new = pl.BlockSpec((None, N_KV_HEADS, SUBLANES, LANES), lambda b, pt: (b, 0, 0, 0))
    qo = pl.BlockSpec((None, N_KV_HEADS, rows, HEAD_DIM), lambda b, pt: (b, 0, 0, 0))
    grid_spec = pltpu.PrefetchScalarGridSpec(
        num_scalar_prefetch=1,
        grid=(nb,),
        in_specs=[qo, pl.BlockSpec((None, SUBLANES, n_keys), lambda b, pt: (b, 0, 0))]
        + page + page + [new, new],
        out_specs=qo,
        scratch_shapes=[pltpu.VMEM((N_KV_HEADS, HEAD_DIM, n_keys), BF16),
                        pltpu.VMEM((N_KV_HEADS, HEAD_DIM, n_keys), BF16),
                        pltpu.VMEM((N_KV_HEADS, rows, n_keys), F32),
                        pltpu.VMEM((N_KV_HEADS, rows, n_keys), BF16),
                        pltpu.VMEM((N_KV_HEADS, rows, 1), F32)],
    )
    return pl.pallas_call(
        _sample_attn_kernel,
        grid_spec=grid_spec,
        out_shape=jax.ShapeDtypeStruct((nb, N_KV_HEADS, rows, HEAD_DIM), F32),
        compiler_params=_params("arbitrary"),
        name="sample_attention",
    )(page_table, q8, bias8, *([cache_k_t] * n_pages), *([cache_v_t] * n_pages), k_new_t, v_new_t)


def _ffn(h, wgu_ref, wd_ref):
    acc = jnp.zeros((h.shape[0], D_MODEL), F32)
    for c in range(D_FF // FFN_CHUNK):
        gate = _dot(h, wgu_ref[:, c * FFN_CHUNK:(c + 1) * FFN_CHUNK])
        up = _dot(h, wgu_ref[:, D_FF + c * FFN_CHUNK:D_FF + (c + 1) * FFN_CHUNK])
        a = _silu(gate) * up
        acc = acc + _dot(a.astype(BF16), wd_ref[c * FFN_CHUNK:(c + 1) * FFN_CHUNK, :])
    return acc


def _row_specs(nb, rows, per_row_mod):
    tm = min(ROW_TILE, rows)
    row = pl.BlockSpec((None, tm, D_MODEL), lambda b, j: (b, j, 0))
    if per_row_mod:
        mod = row
    else:
        mod = pl.BlockSpec((None, 1, D_MODEL), lambda b, j: (b, 0, 0))
    return tm, row, mod


def _attn_out_ffn_kernel(x_ref, o_ref, g1_ref, sc2_ref, sh2_ref, g2_ref, n2_ref,
                         wo_ref, wgu_ref, wd_ref, y_ref):
    y1 = x_ref[...] + g1_ref[...] * _dot(o_ref[...], wo_ref[...])
    h2 = _norm_mod(y1, n2_ref[...], sc2_ref[...], sh2_ref[...]).astype(BF16)
    y_ref[...] = y1 + g2_ref[...] * _ffn(h2, wgu_ref, wd_ref)


def _layer_spec(stacked, layer):
    _, rows, cols = stacked.shape
    return pl.BlockSpec((None, rows, cols), lambda *_: (layer, 0, 0), pipeline_mode=pl.Buffered(1))


def _attn_out_ffn(x, o, g1, sc2, sh2, g2, n2, wo, wgu, wd, layer):
    nb, rows, _ = x.shape
    tm, row, mod = _row_specs(nb, rows, g1.shape[1] == rows)
    return pl.pallas_call(
        _attn_out_ffn_kernel,
        grid=(nb, rows // tm),
        in_specs=[row, row, mod, mod, mod, mod, _const_spec((1, D_MODEL)),
                  _const_spec(wo.shape), _layer_spec(wgu, layer), _layer_spec(wd, layer)],
        out_specs=row,
        out_shape=jax.ShapeDtypeStruct(x.shape, F32),
        compiler_params=_params("arbitrary", "arbitrary"),
        name="attn_out_ffn",
    )(x, o, g1, sc2, sh2, g2, n2, wo, wgu, wd)


def _conv_layer_kernel(x_ref, sc1_ref, sh1_ref, g1_ref, sc2_ref, sh2_ref, g2_ref, p1_ref, p2_ref,
                       n1_ref, n2_ref, nf_ref, win_ref, ck_ref, wout_ref, wgu_ref, wd_ref,
                       y_ref, tail_ref, ubuf_ref, z_ref, *, seg):
    j = pl.program_id(1)
    tm = x_ref.shape[0]
    x = x_ref[...]
    h = _norm_mod(x, n1_ref[...], sc1_ref[...], sh1_ref[...]).astype(BF16)

    @pl.when(j == 0)
    def _():
        ubuf_ref[0:SUBLANES, :] = jnp.zeros((SUBLANES, D_MODEL), F32)

    @pl.when(j > 0)
    def _():
        ubuf_ref[0:SUBLANES, :] = ubuf_ref[tm:tm + SUBLANES, :]

    t = (j * tm + lax.broadcasted_iota(jnp.int32, (tm, 1), 0)) & (seg - 1)
    ck = ck_ref[...]
    cw = FFN_CHUNK
    for c in range(D_MODEL // cw):
        cols = slice(c * cw, (c + 1) * cw)
        bg = _dot(h, win_ref[:, c * cw:(c + 1) * cw])
        cg = _dot(h, win_ref[:, D_MODEL + c * cw:D_MODEL + (c + 1) * cw])
        hv = _dot(h, win_ref[:, 2 * D_MODEL + c * cw:2 * D_MODEL + (c + 1) * cw])
        u = cg * hv
        ubuf_ref[SUBLANES:SUBLANES + tm, cols] = u
        um1 = jnp.where(t >= 1, ubuf_ref[SUBLANES - 1:SUBLANES - 1 + tm, cols], p1_ref[:, cols])
        um2 = jnp.where(t >= 2, ubuf_ref[SUBLANES - 2:SUBLANES - 2 + tm, cols], p2_ref[:, cols])
        conv = ck[0:1, cols] * um2 + ck[1:2, cols] * um1 + ck[2:3, cols] * u
        z_ref[:, cols] = (bg * conv).astype(BF16)
    r = tail_ref.shape[0]
    tail_ref[...] = ubuf_ref[SUBLANES + tm - r:SUBLANES + tm, :]

    y1 = x + g1_ref[...] * _dot(z_ref[...], wout_ref[...])
    h2 = _norm_mod(y1, n2_ref[...], sc2_ref[...], sh2_ref[...]).astype(BF16)
    y2 = y1 + g2_ref[...] * _ffn(h2, wgu_ref, wd_ref)
    ms = jnp.mean(y2 * y2, axis=-1, keepdims=True)
    y_ref[...] = (y2 * lax.rsqrt(ms + EPS)) * nf_ref[...]


def _conv_layer(x, mods, p1, p2, n1, n2, nf, win, ck, wout, wgu, wd, *, layer, seg, full_tail):
    nb, rows, _ = x.shape
    per_row = mods[0].shape[1] == rows
    tm, row, mod = _row_specs(nb, rows, per_row)
    if per_row:
        prefix = row
    else:
        prefix = pl.BlockSpec((None, 1, D_MODEL), lambda b, j: (0, 0, 0))
    if full_tail:
        tail_spec, tail_rows = row, rows
    else:
        tail_spec, tail_rows = pl.BlockSpec((None, SUBLANES, D_MODEL), lambda b, j: (b, 0, 0)), SUBLANES
    vec = _const_spec((1, D_MODEL))
    return pl.pallas_call(
        functools.partial(_conv_layer_kernel, seg=seg),
        grid=(nb, rows // tm),
        in_specs=[row] + [mod] * 6 + [prefix, prefix, vec, vec, vec,
                                      _const_spec(win.shape), _const_spec(ck.shape), _const_spec(wout.shape),
                                      _layer_spec(wgu, layer), _layer_spec(wd, layer)],
        out_specs=(row, tail_spec),
        out_shape=(jax.ShapeDtypeStruct(x.shape, F32),
                   jax.ShapeDtypeStruct((nb, tail_rows, D_MODEL), F32)),
        scratch_shapes=[pltpu.VMEM((tm + SUBLANES, D_MODEL), F32), pltpu.VMEM((tm, D_MODEL), BF16)],
        compiler_params=_params("arbitrary", "arbitrary"),
        name="conv_layer",
    )(x, *mods, p1, p2, n1, n2, nf, win, ck, wout, wgu, wd)


def _rope_tables(pos):
    half = ROT_DIM // 2
    inv = ROPE_THETA ** (-jnp.arange(0, ROT_DIM, 2, dtype=F32) / ROT_DIM)
    ang = pos[:, None] * inv[None, :]
    cos, sin = jnp.cos(ang), jnp.sin(ang)
    n = pos.shape[0]
    rest = HEAD_DIM - ROT_DIM
    c = jnp.concatenate([cos, cos, jnp.ones((n, rest), F32)], axis=1)
    s1 = jnp.concatenate([-sin, jnp.zeros((n, half + rest), F32)], axis=1)
    s2 = jnp.concatenate([jnp.zeros((n, half), F32), sin, jnp.zeros((n, rest), F32)], axis=1)
    return tuple(jnp.tile(a, (1, LANES // HEAD_DIM)) for a in (c, s1, s2))


def kernel(x_prompt, x_sample, c_prompt, c_sample, cache_k, cache_v, cache_kidx, state_conv, page_table,
           ada_w, ada_b, norm1_g, norm2_g, final_g, attn_w_in, attn_w_out, conv_w_in, conv_k, conv_w_out,
           ffn_w_gu, ffn_w_down):
    nb, t, _ = x_prompt.shape
    ns, tn, _ = x_sample.shape
    n_pages = page_table.shape[1]
    past = n_pages * PAGE_SIZE
    n_phys = cache_k.shape[1]
    rows_s = ns * tn

    pad = (-(nb + ns)) % SUBLANES
    c_all = jnp.concatenate([c_prompt, c_sample, jnp.zeros((pad, D_MODEL), F32)], axis=0)
    mod = _modulation(c_all, ada_w, ada_b)

    def mods(layer):
        m = mod[layer].reshape(-1, 6, D_MODEL)
        prompt = [m[:nb, i][:, None, :] for i in range(6)]
        sample = [jnp.repeat(m[nb:nb + ns, i], tn, axis=0)[None] for i in range(6)]
        return prompt, sample

    vec = lambda a: a.reshape(1, D_MODEL)
    w_in = jnp.pad(attn_w_in[0], ((0, 0), (0, ATTN_IN_PAD - ATTN_IN))).astype(BF16)
    w_o = attn_w_out[0].astype(BF16)
    wgu = ffn_w_gu.astype(BF16)
    wd = ffn_w_down.astype(BF16)

    (sh1p, sc1p, g1p, sh2p, sc2p, g2p), (sh1s, sc1s, g1s, sh2s, sc2s, g2s) = mods(0)
    rope_p = _rope_tables(jnp.arange(t, dtype=F32))
    k_p, v_p, ki_p, qt, qit, kg, vt, kib, wit = _proj_prompt(
        x_prompt, sc1p, sh1p, vec(norm1_g[0]), w_in, rope_p)
    o_p = _prompt_attention(qt, qit, wit, kg, vt, kib)
    y_p = _attn_out_ffn(x_prompt, o_p, g1p, sc2p, sh2p, g2p, vec(norm2_g[0]), w_o, wgu, wd, 0)

    pos_s = jnp.tile(past + jnp.arange(tn, dtype=F32), ns)
    rope_s = _rope_tables(pos_s)
    xs = x_sample.reshape(rows_s, D_MODEL)
    proj = _proj_sample(xs, sc1s[0], sh1s[0], vec(norm1_g[0]), w_in, rope_s)
    o = 0
    q_s = proj[:, o:o + Q_COLS].reshape(ns, tn, N_KV_HEADS, HEADS_PER_KV, HEAD_DIM); o += Q_COLS
    k_s = proj[:, o:o + KV_COLS].reshape(ns, tn, KV_COLS); o += KV_COLS
    v_s = proj[:, o:o + KV_COLS].reshape(ns, tn, KV_COLS); o += KV_COLS
    qi_s = proj[:, o:o + IQ_COLS].reshape(ns, tn, IDX_HEADS, IDX_DIM); o += IQ_COLS
    ki_s = proj[:, o:o + IDX_DIM].reshape(ns, tn, IDX_DIM); o += IDX_DIM
    wi_s = proj[:, o:o + IDX_HEADS].reshape(ns, tn, IDX_HEADS)

    qpad = ((0, 0), (0, SUBLANES - tn))
    qi8 = jnp.pad(qi_s, qpad + ((0, 0), (0, 0))).transpose(0, 2, 1, 3)
    qi8 = qi8.reshape(ns, IDX_HEADS * SUBLANES, IDX_DIM).astype(BF16)
    w8 = jnp.pad(wi_s, qpad + ((0, 0),)).transpose(0, 2, 1).reshape(ns, IDX_HEADS * SUBLANES, 1)
    slot_pad = (0, PAGE_SIZE - tn)
    ki_new_t = jnp.pad(ki_s.transpose(0, 2, 1), ((0, 0), (0, 0), slot_pad))
    scores = _sample_scores(page_table, qi8, w8, cache_kidx[0].transpose(0, 2, 1), ki_new_t)
    n_keys = scores.shape[2]
    scores_t = scores[:, :tn, :].transpose(2, 0, 1).reshape(n_keys, rows_s)
    bias_t = _sample_select(scores_t, past, tn)
    bias8 = jnp.pad(bias_t.reshape(n_keys, ns, tn).transpose(1, 2, 0), qpad + ((0, 0),))

    q8 = jnp.pad(q_s * HEAD_DIM ** -0.5, qpad + ((0, 0), (0, 0), (0, 0))).transpose(0, 2, 3, 1, 4)
    q8 = q8.reshape(ns, N_KV_HEADS, HEADS_PER_KV * SUBLANES, HEAD_DIM).astype(BF16)
    new_t = lambda a: jnp.pad(a.reshape(ns, tn, N_KV_HEADS, HEAD_DIM).transpose(0, 2, 1, 3),
                              ((0, 0), (0, 0), (0, SUBLANES - tn), (0, LANES - HEAD_DIM)))
    o_g = _sample_attention(page_table, q8, bias8,
                            cache_k[0].transpose(0, 2, 3, 1), cache_v[0].transpose(0, 2, 3, 1),
                            new_t(k_s), new_t(v_s))
    o_g = o_g.reshape(ns, N_KV_HEADS, HEADS_PER_KV, SUBLANES, HEAD_DIM)[:, :, :, :tn]
    o_s = o_g.transpose(0, 3, 1, 2, 4).reshape(1, rows_s, Q_COLS).astype(BF16)
    y_s = _attn_out_ffn(xs[None], o_s, g1s, sc2s, sh2s, g2s, vec(norm2_g[0]), w_o, wgu, wd, 0)

    (sh1p, sc1p, g1p, sh2p, sc2p, g2p), (sh1s, sc1s, g1s, sh2s, sc2s, g2s) = mods(1)
    win = conv_w_in[0].astype(BF16)
    wout = conv_w_out[0].astype(BF16)
    zero_prefix = jnp.zeros((1, 1, D_MODEL), F32)
    out_p, tail_p = _conv_layer(
        y_p, (sc1p, sh1p, g1p, sc2p, sh2p, g2p), zero_prefix, zero_prefix,
        vec(norm1_g[1]), vec(norm2_g[1]), vec(final_g), win, conv_k[0], wout, wgu, wd,
        layer=1, seg=t, full_tail=False)
    st = state_conv[0]
    zeros_row = jnp.zeros((ns, 1, D_MODEL), F32)
    p1 = jnp.concatenate([st[:, 1:2], zeros_row, zeros_row, zeros_row], axis=1).reshape(1, rows_s, D_MODEL)
    p2 = jnp.concatenate([st[:, 0:1], st[:, 1:2], zeros_row, zeros_row], axis=1).reshape(1, rows_s, D_MODEL)
    out_s, u_s = _conv_layer(
        y_s, (sc1s, sh1s, g1s, sc2s, sh2s, g2s), p1, p2,
        vec(norm1_g[1]), vec(norm2_g[1]), vec(final_g), win, conv_k[0], wout, wgu, wd,
        layer=1, seg=tn, full_tail=True)

    keep = CONV_WIDTH - 1
    return (
        out_p,
        out_s.reshape(ns, tn, D_MODEL),
        k_p.reshape(1, nb, t, N_KV_HEADS, HEAD_DIM),
        v_p.reshape(1, nb, t, N_KV_HEADS, HEAD_DIM),
        ki_p[None],
        tail_p[None, :, SUBLANES - keep:, :],
        k_s.reshape(1, ns, tn, N_KV_HEADS, HEAD_DIM),
        v_s.reshape(1, ns, tn, N_KV_HEADS, HEAD_DIM),
        ki_s[None],
        u_s.reshape(ns, tn, D_MODEL)[None, :, tn - keep:, :],
    )
```

```python
import functools

import jax
import jax.numpy as jnp
from jax import lax
from jax.experimental import pallas as pl
from jax.experimental.pallas import tpu as pltpu

F32 = jnp.float32
BF16 = jnp.bfloat16

D_MODEL = 1024
N_HEADS = 16
HEAD_DIM = 64
N_KV_HEADS = 4
HEADS_PER_KV = N_HEADS // N_KV_HEADS
ROT_DIM = 16
ROPE_THETA = 500000.0
IDX_HEADS = 8
IDX_DIM = 64
TOPK = 256
PAGE_SIZE = 128
CONV_WIDTH = 3
D_FF = 2816
EPS = 1e-6
Q_COLS = N_HEADS * HEAD_DIM
KV_COLS = N_KV_HEADS * HEAD_DIM
IQ_COLS = IDX_HEADS * IDX_DIM
ATTN_IN = Q_COLS + 2 * KV_COLS + IQ_COLS + IDX_DIM + IDX_HEADS
ATTN_IN_PAD = 2176
LANES = 128
SUBLANES = 8
MASK_BIAS = -1e30
F32_MAX = 3.4028234663852886e38
LOG2_E = 1.4426950408889634
VMEM_LIMIT = 60 * 1024 * 1024

ROW_TILE = 512
Q_TILE = 256
SCORE_CHUNK = 512
SCORE_SLAB = 128
ATTN_CHUNK = 256
V_ROWS = HEAD_DIM + 16
FFN_CHUNK = 256


def _dot(a, b):
    return jnp.dot(a, b, preferred_element_type=F32)


def _dot_nt(a, b):
    return lax.dot_general(a, b, (((1,), (1,)), ((), ())), preferred_element_type=F32)


def _silu(x):
    return x / (1.0 + jnp.exp(-x))


def _norm_mod(x, g, sc, sh):
    ms = jnp.mean(x * x, axis=-1, keepdims=True)
    y = x * lax.rsqrt(ms + EPS)
    return (y * g) * (1.0 + sc) + sh


def _params(*sem, flags=None):
    return pltpu.CompilerParams(dimension_semantics=sem, vmem_limit_bytes=VMEM_LIMIT, flags=flags)


def _const_spec(shape):
    nd = len(shape)
    return pl.BlockSpec(shape, lambda *_: (0,) * nd, pipeline_mode=pl.Buffered(1))


def _mod_kernel(c_ref, w_ref, b_ref, o_ref):
    s = _silu(c_ref[...]).astype(BF16)
    o_ref[...] = _dot(s, w_ref[...].astype(BF16)) + b_ref[...]


def _modulation(c_all, ada_w, ada_b):
    depth, _, n = ada_w.shape
    rows = c_all.shape[0]
    tn = 1024
    return pl.pallas_call(
        _mod_kernel,
        grid=(depth, n // tn),
        in_specs=[
            pl.BlockSpec((rows, D_MODEL), lambda l, j: (0, 0)),
            pl.BlockSpec((None, D_MODEL, tn), lambda l, j: (l, 0, j)),
            pl.BlockSpec((None, 1, tn), lambda l, j: (l, 0, j)),
        ],
        out_specs=pl.BlockSpec((None, rows, tn), lambda l, j: (l, 0, j)),
        out_shape=jax.ShapeDtypeStruct((depth, rows, n), F32),
        compiler_params=_params("arbitrary", "arbitrary"),
        name="adaln_modulation",
    )(c_all, ada_w, ada_b.reshape(depth, 1, n))


def _rope_block(x, c, s1, s2):
    return x * c + pltpu.roll(x, LANES - ROT_DIM // 2, 1) * s1 + pltpu.roll(x, ROT_DIM // 2, 1) * s2


def _rope_cols(p, c, s1, s2):
    nblk = p.shape[1] // LANES
    return [_rope_block(p[:, j * LANES:(j + 1) * LANES], c, s1, s2) for j in range(nblk)]


def _last_block_tables(c, s1, s2):
    lane = lax.broadcasted_iota(jnp.int32, c.shape, 1)
    is_key = lane < IDX_DIM
    return jnp.where(is_key, c, 1.0), jnp.where(is_key, s1, 0.0), jnp.where(is_key, s2, 0.0)


def _proj_prompt_kernel(x_ref, sc_ref, sh_ref, g_ref, w_ref, c_ref, s1_ref, s2_ref,
                        k_ref, v_ref, ki_ref, qt_ref, qit_ref, kg_ref, vt_ref, kib_ref, wit_ref):
    h = _norm_mod(x_ref[...], g_ref[...], sc_ref[...], sh_ref[...]).astype(BF16)
    c, s1, s2 = c_ref[...], s1_ref[...], s2_ref[...]

    pq = _dot(h, w_ref[:, 0:Q_COLS])
    for j, blk in enumerate(_rope_cols(pq, c, s1, s2)):
        qt_ref[j * LANES:(j + 1) * LANES, :] = (blk * (HEAD_DIM ** -0.5 * LOG2_E)).astype(BF16).T

    pk = _dot(h, w_ref[:, Q_COLS:Q_COLS + KV_COLS])
    k = jnp.concatenate(_rope_cols(pk, c, s1, s2), axis=1)
    k_ref[...] = k
    for g in range(N_KV_HEADS):
        kg_ref[g] = k[:, g * HEAD_DIM:(g + 1) * HEAD_DIM].astype(BF16)

    pv = _dot(h, w_ref[:, Q_COLS + KV_COLS:Q_COLS + 2 * KV_COLS])
    v_ref[...] = pv
    ones = jnp.ones((V_ROWS - HEAD_DIM, ATTN_CHUNK), BF16)
    for cc in range(pv.shape[0] // ATTN_CHUNK):
        vt = pv[cc * ATTN_CHUNK:(cc + 1) * ATTN_CHUNK, :].astype(BF16).T
        for g in range(N_KV_HEADS):
            vt_ref[cc, g * V_ROWS:g * V_ROWS + HEAD_DIM, :] = vt[g * HEAD_DIM:(g + 1) * HEAD_DIM, :]
            vt_ref[cc, g * V_ROWS + HEAD_DIM:(g + 1) * V_ROWS, :] = ones

    o = Q_COLS + 2 * KV_COLS
    pqi = _dot(h, w_ref[:, o:o + IQ_COLS])
    for j, blk in enumerate(_rope_cols(pqi, c, s1, s2)):
        qit_ref[j * LANES:(j + 1) * LANES, :] = blk.astype(BF16).T

    pl_ = _dot(h, w_ref[:, o + IQ_COLS:ATTN_IN_PAD])
    last = _rope_block(pl_, *_last_block_tables(c, s1, s2))
    ki = last[:, 0:IDX_DIM]
    ki_ref[...] = ki
    kib_ref[...] = ki.astype(BF16)
    wit_ref[...] = last.T[IDX_DIM:IDX_DIM + IDX_HEADS, :]


def _proj_prompt(x, sc, sh, g, w, rope):
    nb, t, _ = x.shape
    tm = ROW_TILE
    row = lambda width: pl.BlockSpec((None, tm, width), lambda b, j: (b, j, 0))
    mod = pl.BlockSpec((None, 1, D_MODEL), lambda b, j: (b, 0, 0))
    tab = pl.BlockSpec((tm, LANES), lambda b, j: (j, 0))
    col = lambda height: pl.BlockSpec((None, height, tm), lambda b, j: (b, 0, j))
    out_shape = (
        jax.ShapeDtypeStruct((nb, t, KV_COLS), F32),
        jax.ShapeDtypeStruct((nb, t, KV_COLS), F32),
        jax.ShapeDtypeStruct((nb, t, IDX_DIM), F32),
        jax.ShapeDtypeStruct((nb, Q_COLS, t), BF16),
        jax.ShapeDtypeStruct((nb, IQ_COLS, t), BF16),
        jax.ShapeDtypeStruct((nb, N_KV_HEADS, t, HEAD_DIM), BF16),
        jax.ShapeDtypeStruct((nb, t // ATTN_CHUNK, N_KV_HEADS * V_ROWS, ATTN_CHUNK), BF16),
        jax.ShapeDtypeStruct((nb, t, IDX_DIM), BF16),
        jax.ShapeDtypeStruct((nb, IDX_HEADS, t), F32),
    )
    out_specs = (
        row(KV_COLS), row(KV_COLS), row(IDX_DIM), col(Q_COLS), col(IQ_COLS),
        pl.BlockSpec((None, N_KV_HEADS, tm, HEAD_DIM), lambda b, j: (b, 0, j, 0)),
        pl.BlockSpec((None, tm // ATTN_CHUNK, N_KV_HEADS * V_ROWS, ATTN_CHUNK), lambda b, j: (b, j, 0, 0)),
        row(IDX_DIM), col(IDX_HEADS),
    )
    return pl.pallas_call(
        _proj_prompt_kernel,
        grid=(nb, t // tm),
        in_specs=[row(D_MODEL), mod, mod, _const_spec((1, D_MODEL)),
                  _const_spec((D_MODEL, ATTN_IN_PAD)), tab, tab, tab],
        out_specs=out_specs,
        out_shape=out_shape,
        compiler_params=_params("arbitrary", "arbitrary"),
        name="attn_proj_prompt",
    )(x, sc, sh, g, w, *rope)


def _proj_sample_kernel(x_ref, sc_ref, sh_ref, g_ref, w_ref, c_ref, s1_ref, s2_ref, p_ref):
    h = _norm_mod(x_ref[...], g_ref[...], sc_ref[...], sh_ref[...]).astype(BF16)
    c, s1, s2 = c_ref[...], s1_ref[...], s2_ref[...]
    p = _dot(h, w_ref[...])
    v_lo = (Q_COLS + KV_COLS) // LANES
    v_hi = (Q_COLS + 2 * KV_COLS) // LANES
    nblk = ATTN_IN_PAD // LANES
    for j in range(nblk):
        blk = p[:, j * LANES:(j + 1) * LANES]
        if j == nblk - 1:
            blk = _rope_block(blk, *_last_block_tables(c, s1, s2))
        elif not (v_lo <= j < v_hi):
            blk = _rope_block(blk, c, s1, s2)
        p_ref[:, j * LANES:(j + 1) * LANES] = blk


def _proj_sample(x, sc, sh, g, w, rope):
    rows = x.shape[0]
    full = lambda width: pl.BlockSpec((rows, width), lambda i: (0, 0))
    return pl.pallas_call(
        _proj_sample_kernel,
        grid=(1,),
        in_specs=[full(D_MODEL), full(D_MODEL), full(D_MODEL),
                  pl.BlockSpec((1, D_MODEL), lambda i: (0, 0)),
                  pl.BlockSpec((D_MODEL, ATTN_IN_PAD), lambda i: (0, 0)),
                  full(LANES), full(LANES), full(LANES)],
        out_specs=full(ATTN_IN_PAD),
        out_shape=jax.ShapeDtypeStruct((rows, ATTN_IN_PAD), F32),
        compiler_params=_params("arbitrary"),
        name="attn_proj_sample",
    )(x, sc, sh, g, w, *rope)


FOLD_ROWS = 32
BISECT_STEPS_FIRST = 14
BISECT_STEPS_PER_TEST = 3


def _fold(x, op):
    rows, tq = x.shape
    x3 = x.reshape(rows // FOLD_ROWS, FOLD_ROWS, tq)
    return {"sum": x3.sum, "max": x3.max, "min": x3.min}[op](axis=0)


def _stats_init(tq):
    zeros = jnp.zeros((FOLD_ROWS, tq), F32)
    return (jnp.full((FOLD_ROWS, tq), -jnp.inf, F32), jnp.full((FOLD_ROWS, tq), jnp.inf, F32),
            zeros, zeros, zeros)


def _stats_update(stats, s):
    mx, mn, n_above, n_ge0, n_gt0 = stats
    ones_where = lambda m: _fold(jnp.where(m, 1.0, 0.0), "sum")
    above = s > -jnp.inf
    return (jnp.maximum(mx, _fold(s, "max")),
            jnp.minimum(mn, _fold(jnp.where(above, s, jnp.inf), "min")),
            n_above + ones_where(above), n_ge0 + ones_where(s >= 0.0), n_gt0 + ones_where(s > 0.0))


def _select_to_bias(sc_ref, tie_ref, qpos, nk, kc, stats):
    tq = sc_ref.shape[1]
    topk = float(TOPK)

    slab = 4 * FOLD_ROWS

    def for_slabs(fn, init, slab=slab):
        def body(c, carry):
            for j in range(kc // slab):
                start = pl.multiple_of(c * kc + j * slab, slab)
                idx = start + lax.broadcasted_iota(jnp.int32, (slab, tq), 0)
                carry = fn(pl.ds(start, slab), idx, carry)
            return carry
        return lax.fori_loop(0, nk, body, init)

    def ones_where(m):
        return _fold(jnp.where(m, 1.0, 0.0), "sum")

    def count(pred):
        acc = for_slabs(lambda rows, idx, acc: acc + ones_where(pred(rows)), jnp.zeros((FOLD_ROWS, tq), F32))
        return acc.sum(axis=0, keepdims=True)

    def count_ge(t):
        return count(lambda rows: sc_ref[rows, :] >= t)

    mx, mn, n_above, n_ge0, n_gt0 = stats
    mx = mx.max(axis=0, keepdims=True)
    mn = mn.min(axis=0, keepdims=True)
    n_above = n_above.sum(axis=0, keepdims=True)
    n_ge0 = n_ge0.sum(axis=0, keepdims=True)
    n_gt0 = n_gt0.sum(axis=0, keepdims=True)
    n_max = count_ge(mx)

    few = n_above < topk
    flat = n_max >= topk
    zero = (n_gt0 < topk) & (n_ge0 >= topk)
    pos = n_gt0 >= topk
    lo = jnp.where(few, -jnp.inf, jnp.where(flat, mx, jnp.where(zero | pos, 0.0, mn)))
    hi = jnp.where(few, mn, jnp.where(flat, jnp.inf, jnp.where(pos, mx, 0.0)))
    n_lo = jnp.where(few, 2.0 * topk, jnp.where(flat, n_max, jnp.where(zero | pos, n_ge0, n_above)))
    n_hi = jnp.where(few, n_above, jnp.where(flat, 0.0, jnp.where(zero, n_gt0, jnp.where(pos, n_max, n_ge0))))
    done = jnp.where(few | flat | zero | (n_lo == topk), 1.0, 0.0)

    def not_finished(d):
        return (jnp.min(d) < 0.5).astype(jnp.int32)

    def bisect_step(state):
        lo, hi, n_lo, n_hi, done = state
        mid = jnp.clip(0.5 * lo + 0.5 * hi, -F32_MAX, F32_MAX)
        stuck = (mid <= lo) | (mid >= hi)
        n_mid = count_ge(mid)
        move = (done < 0.5) & jnp.logical_not(stuck)
        up = move & (n_mid >= topk)
        down = move & (n_mid < topk)
        lo = jnp.where(up, mid, lo)
        n_lo = jnp.where(up, n_mid, n_lo)
        hi = jnp.where(down, mid, hi)
        n_hi = jnp.where(down, n_mid, n_hi)
        done = jnp.where(stuck | (n_lo == topk), 1.0, done)
        return lo, hi, n_lo, n_hi, done

    def bisect(carry):
        state = lax.fori_loop(0, carry[6], lambda _, st: bisect_step(st), carry[:5])
        return (*state, not_finished(state[4]), jnp.int32(BISECT_STEPS_PER_TEST))

    lo, hi, n_lo, n_hi, done, _, _ = lax.while_loop(
        lambda carry: carry[5] > 0, bisect,
        (lo, hi, n_lo, n_hi, done, not_finished(done), jnp.int32(BISECT_STEPS_FIRST)))

    thr = lo
    exact = n_lo == topk
    need = topk - n_hi
    all_keys = jnp.full((1, tq), 2 ** 30, jnp.int32)

    def tie_break():
        def mark(rows, idx, _):
            tie_ref[rows, :] = jnp.where(sc_ref[rows, :] == thr, idx, 2 ** 30)
            return 0
        for_slabs(mark, 0)

        n_bits = (sc_ref.shape[0] - 1).bit_length()

        def step(i, j):
            cand = j + lax.shift_right_logical(jnp.int32(1 << (n_bits - 1)), i)
            n_before = count(lambda rows: tie_ref[rows, :] < cand)
            return jnp.where(n_before < need, cand, j)
        return lax.fori_loop(0, n_bits, step, jnp.zeros((1, tq), jnp.int32))

    any_tie = jnp.min(jnp.where(exact, 1.0, 0.0)) < 0.5

    def write(keep_fn):
        def body(rows, idx, _):
            sc_ref[rows, :] = jnp.where(keep_fn(sc_ref[rows, :], idx), 0.0, MASK_BIAS)
            return 0
        return for_slabs(body, 0, slab=kc)

    def write_with_ties():
        last_tied = jnp.minimum(jnp.where(exact, all_keys, tie_break()), qpos)
        return write(lambda s, idx: (s > thr) | ((s == thr) & (idx <= last_tied)))

    lax.cond(any_tie, write_with_ties, lambda: write(lambda s, idx: s >= thr))


def _prompt_attn_kernel(qt_ref, qit_ref, wit_ref, kg_ref, vt_ref, kib_ref, o_ref,
                        sc_ref, tie_ref, acc_ref, m_ref, mx_ref, s_ref):
    i = pl.program_id(1)
    tq = qt_ref.shape[1]
    qpos = i * tq + lax.broadcasted_iota(jnp.int32, (1, tq), 1)
    n_keys = (i + 1) * tq

    n_score = (n_keys + SCORE_CHUNK - 1) // SCORE_CHUNK

    def score_chunk(c, stats):
        for j in range(SCORE_CHUNK // SCORE_SLAB):
            start = pl.multiple_of(c * SCORE_CHUNK + j * SCORE_SLAB, SCORE_SLAB)
            kc = kib_ref[pl.ds(start, SCORE_SLAB), :]
            acc = jnp.zeros((SCORE_SLAB, tq), F32)
            for h in range(IDX_HEADS):
                d = _dot(kc, qit_ref[h * IDX_DIM:(h + 1) * IDX_DIM, :])
                acc = acc + jnp.maximum(d, 0.0) * wit_ref[h:h + 1, :]
            idx = start + lax.broadcasted_iota(jnp.int32, (SCORE_SLAB, tq), 0)
            s = jnp.where(idx <= qpos, acc, -jnp.inf)
            sc_ref[pl.ds(start, SCORE_SLAB), :] = s
            stats = _stats_update(stats, s)
        return stats

    stats = lax.fori_loop(0, n_score, score_chunk, _stats_init(tq))

    _select_to_bias(sc_ref, tie_ref, qpos, n_score, SCORE_CHUNK, stats)

    m_ref[...] = jnp.full(m_ref.shape, MASK_BIAS, F32)
    acc_ref[...] = jnp.zeros(acc_ref.shape, F32)

    def attn_chunk(c, _):
        start = pl.multiple_of(c * ATTN_CHUNK, ATTN_CHUNK)
        for h in range(N_HEADS):
            s = _dot(kg_ref[h // HEADS_PER_KV, pl.ds(start, ATTN_CHUNK), :],
                     qt_ref[h * HEAD_DIM:(h + 1) * HEAD_DIM, :])
            s = s + sc_ref[pl.ds(start, ATTN_CHUNK), :]
            s_ref[h] = s
            mx_ref[h] = s.max(axis=0, keepdims=True)
        for h in range(N_HEADS):
            g = h // HEADS_PER_KV
            m_old = m_ref[h]
            m_new = jnp.maximum(m_old, mx_ref[h])
            alpha = jnp.exp2(m_old - m_new)
            p = jnp.exp2(s_ref[h] - m_new).astype(BF16)
            vt = vt_ref[c, g * V_ROWS:(g + 1) * V_ROWS, :]
            acc_ref[h] = alpha * acc_ref[h] + _dot(vt, p)
            m_ref[h] = m_new
        return 0

    lax.fori_loop(0, (i + 1) * (tq // ATTN_CHUNK), attn_chunk, 0)

    for pair in range(N_HEADS // 2):
        two = []
        for h in (2 * pair, 2 * pair + 1):
            a = acc_ref[h]
            two.append(a[0:HEAD_DIM, :] / a[HEAD_DIM:HEAD_DIM + 1, :])
        col = 2 * pair * HEAD_DIM
        o_ref[:, col:col + 2 * HEAD_DIM] = jnp.concatenate(two, axis=0).T.astype(BF16)


def _prompt_attention(qt, qit, wit, kg, vt, kib):
    nb, _, t = qt.shape
    tq = Q_TILE
    col = lambda height: pl.BlockSpec((None, height, tq), lambda b, i: (b, 0, i))
    return pl.pallas_call(
        _prompt_attn_kernel,
        grid=(nb, t // tq),
        in_specs=[
            col(Q_COLS), col(IQ_COLS), col(IDX_HEADS),
            pl.BlockSpec((None, N_KV_HEADS, t, HEAD_DIM), lambda b, i: (b, 0, 0, 0)),
            pl.BlockSpec((None, t // ATTN_CHUNK, N_KV_HEADS * V_ROWS, ATTN_CHUNK), lambda b, i: (b, 0, 0, 0)),
            pl.BlockSpec((None, t, IDX_DIM), lambda b, i: (b, 0, 0)),
        ],
        out_specs=pl.BlockSpec((None, tq, Q_COLS), lambda b, i: (b, i, 0)),
        out_shape=jax.ShapeDtypeStruct((nb, t, Q_COLS), BF16),
        scratch_shapes=[
            pltpu.VMEM((t, tq), F32),
            pltpu.VMEM((t, tq), jnp.int32),
            pltpu.VMEM((N_HEADS, V_ROWS, tq), F32),
            pltpu.VMEM((N_HEADS, 1, tq), F32),
            pltpu.VMEM((N_HEADS, 1, tq), F32),
            pltpu.VMEM((N_HEADS, ATTN_CHUNK, tq), F32),
        ],
        compiler_params=_params("arbitrary", "arbitrary"),
        name="prompt_attention",
    )(qt, qit, wit, kg, vt, kib)


def _sample_score_kernel(pt_ref, qi_ref, w_ref, *rest):
    del pt_ref
    n_pages = len(rest) - 3
    pages, new_ref, o_ref, kall_ref = rest[:n_pages], rest[n_pages], rest[n_pages + 1], rest[n_pages + 2]
    for j in range(n_pages):
        kall_ref[:, j * PAGE_SIZE:(j + 1) * PAGE_SIZE] = pages[j][...].astype(BF16)
    kall_ref[:, n_pages * PAGE_SIZE:(n_pages + 1) * PAGE_SIZE] = new_ref[...].astype(BF16)
    d = _dot(qi_ref[...], kall_ref[...])
    r = jnp.maximum(d, 0.0) * w_ref[...]
    o_ref[...] = r.reshape(IDX_HEADS, SUBLANES, r.shape[1]).sum(axis=0)


def _sample_scores(page_table, qi8, w8, cache_ki_t, ki_new_t):
    nb, n_pages = page_table.shape
    n_keys = (n_pages + 1) * PAGE_SIZE
    rows = IDX_HEADS * SUBLANES
    grid_spec = pltpu.PrefetchScalarGridSpec(
        num_scalar_prefetch=1,
        grid=(nb,),
        in_specs=[pl.BlockSpec((None, rows, IDX_DIM), lambda b, pt: (b, 0, 0)),
                  pl.BlockSpec((None, rows, 1), lambda b, pt: (b, 0, 0))]
        + [pl.BlockSpec((None, IDX_DIM, PAGE_SIZE), lambda b, pt, j=j: (pt[b, j], 0, 0))
           for j in range(n_pages)]
        + [pl.BlockSpec((None, IDX_DIM, PAGE_SIZE), lambda b, pt: (b, 0, 0))],
        out_specs=pl.BlockSpec((None, SUBLANES, n_keys), lambda b, pt: (b, 0, 0)),
        scratch_shapes=[pltpu.VMEM((IDX_DIM, n_keys), BF16)],
    )
    return pl.pallas_call(
        _sample_score_kernel,
        grid_spec=grid_spec,
        out_shape=jax.ShapeDtypeStruct((nb, SUBLANES, n_keys), F32),
        compiler_params=_params("arbitrary"),
        name="sample_indexer_scores",
    )(page_table, qi8, w8, *([cache_ki_t] * n_pages), ki_new_t)


def _sample_select_kernel(s_ref, o_ref, sc_ref, tie_ref, *, past, period):
    tq = s_ref.shape[1]
    n_keys = s_ref.shape[0]
    lane = lax.broadcasted_iota(jnp.int32, (1, tq), 1)
    qpos = past + (lane & (period - 1))
    nk = n_keys // PAGE_SIZE
    stats = _stats_init(tq)
    for c in range(nk):
        idx = c * PAGE_SIZE + lax.broadcasted_iota(jnp.int32, (PAGE_SIZE, tq), 0)
        s = jnp.where(idx <= qpos, s_ref[c * PAGE_SIZE:(c + 1) * PAGE_SIZE, :], -jnp.inf)
        sc_ref[c * PAGE_SIZE:(c + 1) * PAGE_SIZE, :] = s
        stats = _stats_update(stats, s)
    _select_to_bias(sc_ref, tie_ref, qpos, nk, PAGE_SIZE, stats)
    o_ref[...] = sc_ref[...]


def _sample_select(scores_t, past, period):
    n_keys, nq = scores_t.shape
    tq = Q_TILE
    blk = pl.BlockSpec((n_keys, tq), lambda i: (0, i))
    return pl.pallas_call(
        functools.partial(_sample_select_kernel, past=past, period=period),
        grid=(nq // tq,),
        in_specs=[blk],
        out_specs=blk,
        out_shape=jax.ShapeDtypeStruct((n_keys, nq), F32),
        scratch_shapes=[pltpu.VMEM((n_keys, tq), F32), pltpu.VMEM((n_keys, tq), jnp.int32)],
        compiler_params=_params("arbitrary"),
        name="sample_select",
    )(scores_t)


def _sample_attn_kernel(pt_ref, q_ref, bias_ref, *rest):
    del pt_ref
    n_pages = (len(rest) - 8) // 2
    kpages, vpages = rest[:n_pages], rest[n_pages:2 * n_pages]
    knew_ref, vnew_ref, o_ref, kall_ref, vall_ref, s_ref, p_ref, l_ref = rest[2 * n_pages:]
    for j in range(n_pages):
        kall_ref[:, :, j * PAGE_SIZE:(j + 1) * PAGE_SIZE] = kpages[j][...].astype(BF16)
        vall_ref[:, :, j * PAGE_SIZE:(j + 1) * PAGE_SIZE] = vpages[j][...].astype(BF16)
    fill = jnp.zeros((PAGE_SIZE - SUBLANES, LANES), F32)
    for g in range(N_KV_HEADS):
        for new_ref, all_ref in ((knew_ref, kall_ref), (vnew_ref, vall_ref)):
            page = jnp.concatenate([new_ref[g], fill], axis=0).T
            all_ref[g, :, n_pages * PAGE_SIZE:(n_pages + 1) * PAGE_SIZE] = page[0:HEAD_DIM, :].astype(BF16)
    bias = bias_ref[...]
    n_keys = bias.shape[1]
    bias = jnp.broadcast_to(bias[None], (HEADS_PER_KV, SUBLANES, n_keys)).reshape(
        HEADS_PER_KV * SUBLANES, n_keys)
    for g in range(N_KV_HEADS):
        s_ref[g] = _dot(q_ref[g], kall_ref[g]) + bias
    for g in range(N_KV_HEADS):
        s = s_ref[g]
        m = s.max(axis=1, keepdims=True)
        p = jnp.exp(s - m)
        l_ref[g] = p.sum(axis=1, keepdims=True)
        p_ref[g] = p.astype(BF16)
    for g in range(N_KV_HEADS):
        o_ref[g] = _dot_nt(p_ref[g], vall_ref[g]) / l_ref[g]


def _sample_attention(page_table, q8, bias8, cache_k_t, cache_v_t, k_new_t, v_new_t):
    nb, n_pages = page_table.shape
    n_keys = (n_pages + 1) * PAGE_SIZE
    rows = HEADS_PER_KV * SUBLANES
    page = [pl.BlockSpec((None, N_KV_HEADS, HEAD_DIM, PAGE_SIZE), lambda b, pt, j=j: (pt[b, j], 0, 0, 0))
            for j in range(n_pages)]
    new = pl.BlockSpec((None, N_KV_HEADS, SUBLANES, LANES), lambda b, pt: (b, 0, 0, 0))
    qo = pl.BlockSpec((None, N_KV_HEADS, rows, HEAD_DIM), lambda b, pt: (b, 0, 0, 0))
    grid_spec = pltpu.PrefetchScalarGridSpec(
        num_scalar_prefetch=1,
        grid=(nb,),
        in_specs=[qo, pl.BlockSpec((None, SUBLANES, n_keys), lambda b, pt: (b, 0, 0))]
        + page + page + [new, new],
        out_specs=qo,
        scratch_shapes=[pltpu.VMEM((N_KV_HEADS, HEAD_DIM, n_keys), BF16),
                        pltpu.VMEM((N_KV_HEADS, HEAD_DIM, n_keys), BF16),
                        pltpu.VMEM((N_KV_HEADS, rows, n_keys), F32),
                        pltpu.VMEM((N_KV_HEADS, rows, n_keys), BF16),
                        pltpu.VMEM((N_KV_HEADS, rows, 1), F32)],
    )
    return pl.pallas_call(
        _sample_attn_kernel,
        grid_spec=grid_spec,
        out_shape=jax.ShapeDtypeStruct((nb, N_KV_HEADS, rows, HEAD_DIM), F32),
        compiler_params=_params("arbitrary"),
        name="sample_attention",
    )(page_table, q8, bias8, *([cache_k_t] * n_pages), *([cache_v_t] * n_pages), k_new_t, v_new_t)


def _ffn(h, wgu_ref, wd_ref):
    acc = jnp.zeros((h.shape[0], D_MODEL), F32)
    for c in range(D_FF // FFN_CHUNK):
        gate = _dot(h, wgu_ref[:, c * FFN_CHUNK:(c + 1) * FFN_CHUNK])
        up = _dot(h, wgu_ref[:, D_FF + c * FFN_CHUNK:D_FF + (c + 1) * FFN_CHUNK])
        a = _silu(gate) * up
        acc = acc + _dot(a.astype(BF16), wd_ref[c * FFN_CHUNK:(c + 1) * FFN_CHUNK, :])
    return acc


def _row_specs(nb, rows, per_row_mod):
    tm = min(ROW_TILE, rows)
    row = pl.BlockSpec((None, tm, D_MODEL), lambda b, j: (b, j, 0))
    if per_row_mod:
        mod = row
    else:
        mod = pl.BlockSpec((None, 1, D_MODEL), lambda b, j: (b, 0, 0))
    return tm, row, mod


def _attn_out_ffn_kernel(x_ref, o_ref, g1_ref, sc2_ref, sh2_ref, g2_ref, n2_ref,
                         wo_ref, wgu_ref, wd_ref, y_ref):
    y1 = x_ref[...] + g1_ref[...] * _dot(o_ref[...], wo_ref[...])
    h2 = _norm_mod(y1, n2_ref[...], sc2_ref[...], sh2_ref[...]).astype(BF16)
    y_ref[...] = y1 + g2_ref[...] * _ffn(h2, wgu_ref, wd_ref)


def _layer_spec(stacked, layer):
    _, rows, cols = stacked.shape
    return pl.BlockSpec((None, rows, cols), lambda *_: (layer, 0, 0), pipeline_mode=pl.Buffered(1))


def _attn_out_ffn(x, o, g1, sc2, sh2, g2, n2, wo, wgu, wd, layer):
    nb, rows, _ = x.shape
    tm, row, mod = _row_specs(nb, rows, g1.shape[1] == rows)
    return pl.pallas_call(
        _attn_out_ffn_kernel,
        grid=(nb, rows // tm),
        in_specs=[row, row, mod, mod, mod, mod, _const_spec((1, D_MODEL)),
                  _const_spec(wo.shape), _layer_spec(wgu, layer), _layer_spec(wd, layer)],
        out_specs=row,
        out_shape=jax.ShapeDtypeStruct(x.shape, F32),
        compiler_params=_params("arbitrary", "arbitrary"),
        name="attn_out_ffn",
    )(x, o, g1, sc2, sh2, g2, n2, wo, wgu, wd)


def _conv_layer_kernel(x_ref, sc1_ref, sh1_ref, g1_ref, sc2_ref, sh2_ref, g2_ref, p1_ref, p2_ref,
                       n1_ref, n2_ref, nf_ref, win_ref, ck_ref, wout_ref, wgu_ref, wd_ref,
                       y_ref, tail_ref, ubuf_ref, z_ref, *, seg):
    j = pl.program_id(1)
    tm = x_ref.shape[0]
    x = x_ref[...]
    h = _norm_mod(x, n1_ref[...], sc1_ref[...], sh1_ref[...]).astype(BF16)

    @pl.when(j == 0)
    def _():
        ubuf_ref[0:SUBLANES, :] = jnp.zeros((SUBLANES, D_MODEL), F32)

    @pl.when(j > 0)
    def _():
        ubuf_ref[0:SUBLANES, :] = ubuf_ref[tm:tm + SUBLANES, :]

    t = (j * tm + lax.broadcasted_iota(jnp.int32, (tm, 1), 0)) & (seg - 1)
    ck = ck_ref[...]
    cw = FFN_CHUNK
    for c in range(D_MODEL // cw):
        cols = slice(c * cw, (c + 1) * cw)
        bg = _dot(h, win_ref[:, c * cw:(c + 1) * cw])
        cg = _dot(h, win_ref[:, D_MODEL + c * cw:D_MODEL + (c + 1) * cw])
        hv = _dot(h, win_ref[:, 2 * D_MODEL + c * cw:2 * D_MODEL + (c + 1) * cw])
        u = cg * hv
        ubuf_ref[SUBLANES:SUBLANES + tm, cols] = u
        um1 = jnp.where(t >= 1, ubuf_ref[SUBLANES - 1:SUBLANES - 1 + tm, cols], p1_ref[:, cols])
        um2 = jnp.where(t >= 2, ubuf_ref[SUBLANES - 2:SUBLANES - 2 + tm, cols], p2_ref[:, cols])
        conv = ck[0:1, cols] * um2 + ck[1:2, cols] * um1 + ck[2:3, cols] * u
        z_ref[:, cols] = (bg * conv).astype(BF16)
    r = tail_ref.shape[0]
    tail_ref[...] = ubuf_ref[SUBLANES + tm - r:SUBLANES + tm, :]

    y1 = x + g1_ref[...] * _dot(z_ref[...], wout_ref[...])
    h2 = _norm_mod(y1, n2_ref[...], sc2_ref[...], sh2_ref[...]).astype(BF16)
    y2 = y1 + g2_ref[...] * _ffn(h2, wgu_ref, wd_ref)
    ms = jnp.mean(y2 * y2, axis=-1, keepdims=True)
    y_ref[...] = (y2 * lax.rsqrt(ms + EPS)) * nf_ref[...]


def _conv_layer(x, mods, p1, p2, n1, n2, nf, win, ck, wout, wgu, wd, *, layer, seg, full_tail):
    nb, rows, _ = x.shape
    per_row = mods[0].shape[1] == rows
    tm, row, mod = _row_specs(nb, rows, per_row)
    if per_row:
        prefix = row
    else:
        prefix = pl.BlockSpec((None, 1, D_MODEL), lambda b, j: (0, 0, 0))
    if full_tail:
        tail_spec, tail_rows = row, rows
    else:
        tail_spec, tail_rows = pl.BlockSpec((None, SUBLANES, D_MODEL), lambda b, j: (b, 0, 0)), SUBLANES
    vec = _const_spec((1, D_MODEL))
    return pl.pallas_call(
        functools.partial(_conv_layer_kernel, seg=seg),
        grid=(nb, rows // tm),
        in_specs=[row] + [mod] * 6 + [prefix, prefix, vec, vec, vec,
                                      _const_spec(win.shape), _const_spec(ck.shape), _const_spec(wout.shape),
                                      _layer_spec(wgu, layer), _layer_spec(wd, layer)],
        out_specs=(row, tail_spec),
        out_shape=(jax.ShapeDtypeStruct(x.shape, F32),
                   jax.ShapeDtypeStruct((nb, tail_rows, D_MODEL), F32)),
        scratch_shapes=[pltpu.VMEM((tm + SUBLANES, D_MODEL), F32), pltpu.VMEM((tm, D_MODEL), BF16)],
        compiler_params=_params("arbitrary", "arbitrary"),
        name="conv_layer",
    )(x, *mods, p1, p2, n1, n2, nf, win, ck, wout, wgu, wd)


def _rope_tables(pos):
    half = ROT_DIM // 2
    inv = ROPE_THETA ** (-jnp.arange(0, ROT_DIM, 2, dtype=F32) / ROT_DIM)
    ang = pos[:, None] * inv[None, :]
    cos, sin = jnp.cos(ang), jnp.sin(ang)
    n = pos.shape[0]
    rest = HEAD_DIM - ROT_DIM
    c = jnp.concatenate([cos, cos, jnp.ones((n, rest), F32)], axis=1)
    s1 = jnp.concatenate([-sin, jnp.zeros((n, half + rest), F32)], axis=1)
    s2 = jnp.concatenate([jnp.zeros((n, half), F32), sin, jnp.zeros((n, rest), F32)], axis=1)
    return tuple(jnp.tile(a, (1, LANES // HEAD_DIM)) for a in (c, s1, s2))


def kernel(x_prompt, x_sample, c_prompt, c_sample, cache_k, cache_v, cache_kidx, state_conv, page_table,
           ada_w, ada_b, norm1_g, norm2_g, final_g, attn_w_in, attn_w_out, conv_w_in, conv_k, conv_w_out,
           ffn_w_gu, ffn_w_down):
    nb, t, _ = x_prompt.shape
    ns, tn, _ = x_sample.shape
    n_pages = page_table.shape[1]
    past = n_pages * PAGE_SIZE
    n_phys = cache_k.shape[1]
    rows_s = ns * tn

    pad = (-(nb + ns)) % SUBLANES
    c_all = jnp.concatenate([c_prompt, c_sample, jnp.zeros((pad, D_MODEL), F32)], axis=0)
    mod = _modulation(c_all, ada_w, ada_b)

    def mods(layer):
        cols = [mod[layer, :, i * D_MODEL:(i + 1) * D_MODEL] for i in range(6)]
        prompt = [m[:nb, None, :] for m in cols]
        sample = [jnp.repeat(m[nb:nb + ns], tn, axis=0)[None] for m in cols]
        return prompt, sample

    vec = lambda a: a.reshape(1, D_MODEL)
    w_in = jnp.pad(attn_w_in[0], ((0, 0), (0, ATTN_IN_PAD - ATTN_IN))).astype(BF16)
    w_o = attn_w_out[0].astype(BF16)
    wgu = ffn_w_gu.astype(BF16)
    wd = ffn_w_down.astype(BF16)

    (sh1p, sc1p, g1p, sh2p, sc2p, g2p), (sh1s, sc1s, g1s, sh2s, sc2s, g2s) = mods(0)
    rope_p = _rope_tables(jnp.arange(t, dtype=F32))
    k_p, v_p, ki_p, qt, qit, kg, vt, kib, wit = _proj_prompt(
        x_prompt, sc1p, sh1p, vec(norm1_g[0]), w_in, rope_p)
    o_p = _prompt_attention(qt, qit, wit, kg, vt, kib)
    y_p = _attn_out_ffn(x_prompt, o_p, g1p, sc2p, sh2p, g2p, vec(norm2_g[0]), w_o, wgu, wd, 0)

    pos_s = jnp.tile(past + jnp.arange(tn, dtype=F32), ns)
    rope_s = _rope_tables(pos_s)
    xs = x_sample.reshape(rows_s, D_MODEL)
    proj = _proj_sample(xs, sc1s[0], sh1s[0], vec(norm1_g[0]), w_in, rope_s)
    o = 0
    q_s = proj[:, o:o + Q_COLS].reshape(ns, tn, N_KV_HEADS, HEADS_PER_KV, HEAD_DIM); o += Q_COLS
    k_s = proj[:, o:o + KV_COLS].reshape(ns, tn, KV_COLS); o += KV_COLS
    v_s = proj[:, o:o + KV_COLS].reshape(ns, tn, KV_COLS); o += KV_COLS
    qi_s = proj[:, o:o + IQ_COLS].reshape(ns, tn, IDX_HEADS, IDX_DIM); o += IQ_COLS
    ki_s = proj[:, o:o + IDX_DIM].reshape(ns, tn, IDX_DIM); o += IDX_DIM
    wi_s = proj[:, o:o + IDX_HEADS].reshape(ns, tn, IDX_HEADS)

    qpad = ((0, 0), (0, SUBLANES - tn))
    qi8 = jnp.pad(qi_s, qpad + ((0, 0), (0, 0))).transpose(0, 2, 1, 3)
    qi8 = qi8.reshape(ns, IDX_HEADS * SUBLANES, IDX_DIM).astype(BF16)
    w8 = jnp.pad(wi_s, qpad + ((0, 0),)).transpose(0, 2, 1).reshape(ns, IDX_HEADS * SUBLANES, 1)
    slot_pad = (0, PAGE_SIZE - tn)
    ki_new_t = jnp.pad(ki_s.transpose(0, 2, 1), ((0, 0), (0, 0), slot_pad))
    scores = _sample_scores(page_table, qi8, w8, cache_kidx[0].transpose(0, 2, 1), ki_new_t)
    n_keys = scores.shape[2]
    scores_t = scores[:, :tn, :].transpose(2, 0, 1).reshape(n_keys, rows_s)
    bias_t = _sample_select(scores_t, past, tn)
    bias8 = jnp.pad(bias_t.reshape(n_keys, ns, tn).transpose(1, 2, 0), qpad + ((0, 0),))

    q8 = jnp.pad(q_s * HEAD_DIM ** -0.5, qpad + ((0, 0), (0, 0), (0, 0))).transpose(0, 2, 3, 1, 4)
    q8 = q8.reshape(ns, N_KV_HEADS, HEADS_PER_KV * SUBLANES, HEAD_DIM).astype(BF16)
    new_t = lambda a: jnp.pad(a.reshape(ns, tn, N_KV_HEADS, HEAD_DIM).transpose(0, 2, 1, 3),
                              ((0, 0), (0, 0), (0, SUBLANES - tn), (0, LANES - HEAD_DIM)))
    o_g = _sample_attention(page_table, q8, bias8,
                            cache_k[0].transpose(0, 2, 3, 1), cache_v[0].transpose(0, 2, 3, 1),
                            new_t(k_s), new_t(v_s))
    o_g = o_g.reshape(ns, N_KV_HEADS, HEADS_PER_KV, SUBLANES, HEAD_DIM)[:, :, :, :tn]
    o_s = o_g.transpose(0, 3, 1, 2, 4).reshape(1, rows_s, Q_COLS).astype(BF16)
    y_s = _attn_out_ffn(xs[None], o_s, g1s, sc2s, sh2s, g2s, vec(norm2_g[0]), w_o, wgu, wd, 0)

    (sh1p, sc1p, g1p, sh2p, sc2p, g2p), (sh1s, sc1s, g1s, sh2s, sc2s, g2s) = mods(1)
    win = conv_w_in[0].astype(BF16)
    wout = conv_w_out[0].astype(BF16)
    zero_prefix = jnp.zeros((1, 1, D_MODEL), F32)
    out_p, tail_p = _conv_layer(
        y_p, (sc1p, sh1p, g1p, sc2p, sh2p, g2p), zero_prefix, zero_prefix,
        vec(norm1_g[1]), vec(norm2_g[1]), vec(final_g), win, conv_k[0], wout, wgu, wd,
        layer=1, seg=t, full_tail=False)
    st = state_conv[0]
    zeros_row = jnp.zeros((ns, 1, D_MODEL), F32)
    p1 = jnp.concatenate([st[:, 1:2], zeros_row, zeros_row, zeros_row], axis=1).reshape(1, rows_s, D_MODEL)
    p2 = jnp.concatenate([st[:, 0:1], st[:, 1:2], zeros_row, zeros_row], axis=1).reshape(1, rows_s, D_MODEL)
    out_s, u_s = _conv_layer(
        y_s, (sc1s, sh1s, g1s, sc2s, sh2s, g2s), p1, p2,
        vec(norm1_g[1]), vec(norm2_g[1]), vec(final_g), win, conv_k[0], wout, wgu, wd,
        layer=1, seg=tn, full_tail=True)

    keep = CONV_WIDTH - 1
    return (
        out_p,
        out_s.reshape(ns, tn, D_MODEL),
        k_p.reshape(1, nb, t, N_KV_HEADS, HEAD_DIM),
        v_p.reshape(1, nb, t, N_KV_HEADS, HEAD_DIM),
        ki_p[None],
        tail_p[None, :, SUBLANES - keep:, :],
        k_s.reshape(1, ns, tn, N_KV_HEADS, HEAD_DIM),
        v_s.reshape(1, ns, tn, N_KV_HEADS, HEAD_DIM),
        ki_s[None],
        u_s.reshape(ns, tn, D_MODEL)[None, :, tn - keep:, :],
    )
```

```python
import functools

import jax
import jax.numpy as jnp
from jax import lax
from jax.experimental import pallas as pl
from jax.experimental.pallas import tpu as pltpu

F32 = jnp.float32
BF16 = jnp.bfloat16

D_MODEL = 1024
N_HEADS = 16
HEAD_DIM = 64
N_KV_HEADS = 4
HEADS_PER_KV = N_HEADS // N_KV_HEADS
ROT_DIM = 16
ROPE_THETA = 500000.0
IDX_HEADS = 8
IDX_DIM = 64
TOPK = 256
PAGE_SIZE = 128
CONV_WIDTH = 3
D_FF = 2816
EPS = 1e-6
Q_COLS = N_HEADS * HEAD_DIM
KV_COLS = N_KV_HEADS * HEAD_DIM
IQ_COLS = IDX_HEADS * IDX_DIM
ATTN_IN = Q_COLS + 2 * KV_COLS + IQ_COLS + IDX_DIM + IDX_HEADS
ATTN_IN_PAD = 2176
LANES = 128
SUBLANES = 8
MASK_BIAS = -1e30
F32_MAX = 3.4028234663852886e38
LOG2_E = 1.4426950408889634
VMEM_LIMIT = 60 * 1024 * 1024

ROW_TILE = 512
Q_TILE = 256
SCORE_CHUNK = 512
SCORE_SLAB = 128
ATTN_CHUNK = 256
V_ROWS = HEAD_DIM + 16
FFN_CHUNK = 256
SCORE_SEQS = 4
ATTN_SEQS = 2


def _dot(a, b):
    return jnp.dot(a, b, preferred_element_type=F32)


def _dot_nt(a, b):
    return lax.dot_general(a, b, (((1,), (1,)), ((), ())), preferred_element_type=F32)


def _silu(x):
    return x / (1.0 + jnp.exp(-x))


def _norm_mod(x, g, sc, sh):
    ms = jnp.mean(x * x, axis=-1, keepdims=True)
    y = x * lax.rsqrt(ms + EPS)
    return (y * g) * (1.0 + sc) + sh


def _params(*sem, flags=None):
    return pltpu.CompilerParams(dimension_semantics=sem, vmem_limit_bytes=VMEM_LIMIT, flags=flags)


def _const_spec(shape):
    nd = len(shape)
    return pl.BlockSpec(shape, lambda *_: (0,) * nd, pipeline_mode=pl.Buffered(1))


def _mod_kernel(c_ref, w_ref, b_ref, o_ref):
    s = _silu(c_ref[...]).astype(BF16)
    o_ref[...] = _dot(s, w_ref[...].astype(BF16)) + b_ref[...]


def _modulation(c_all, ada_w, ada_b):
    depth, _, n = ada_w.shape
    rows = c_all.shape[0]
    tn = 1024
    return pl.pallas_call(
        _mod_kernel,
        grid=(depth, n // tn),
        in_specs=[
            pl.BlockSpec((rows, D_MODEL), lambda l, j: (0, 0)),
            pl.BlockSpec((None, D_MODEL, tn), lambda l, j: (l, 0, j)),
            pl.BlockSpec((None, 1, tn), lambda l, j: (l, 0, j)),
        ],
        out_specs=pl.BlockSpec((None, rows, tn), lambda l, j: (l, 0, j)),
        out_shape=jax.ShapeDtypeStruct((depth, rows, n), F32),
        compiler_params=_params("arbitrary", "arbitrary"),
        name="adaln_modulation",
    )(c_all, ada_w, ada_b.reshape(depth, 1, n))


def _rope_block(x, c, s1, s2):
    return x * c + pltpu.roll(x, LANES - ROT_DIM // 2, 1) * s1 + pltpu.roll(x, ROT_DIM // 2, 1) * s2


def _rope_cols(p, c, s1, s2):
    nblk = p.shape[1] // LANES
    return [_rope_block(p[:, j * LANES:(j + 1) * LANES], c, s1, s2) for j in range(nblk)]


def _last_block_tables(c, s1, s2):
    lane = lax.broadcasted_iota(jnp.int32, c.shape, 1)
    is_key = lane < IDX_DIM
    return jnp.where(is_key, c, 1.0), jnp.where(is_key, s1, 0.0), jnp.where(is_key, s2, 0.0)


def _proj_prompt_kernel(x_ref, sc_ref, sh_ref, g_ref, w_ref, c_ref, s1_ref, s2_ref,
                        k_ref, v_ref, ki_ref, qt_ref, qit_ref, kg_ref, vt_ref, kib_ref, wit_ref):
    h = _norm_mod(x_ref[...], g_ref[...], sc_ref[...], sh_ref[...]).astype(BF16)
    c, s1, s2 = c_ref[...], s1_ref[...], s2_ref[...]

    pq = _dot(h, w_ref[:, 0:Q_COLS])
    for j, blk in enumerate(_rope_cols(pq, c, s1, s2)):
        qt_ref[j * LANES:(j + 1) * LANES, :] = (blk * (HEAD_DIM ** -0.5 * LOG2_E)).astype(BF16).T

    pk = _dot(h, w_ref[:, Q_COLS:Q_COLS + KV_COLS])
    k = jnp.concatenate(_rope_cols(pk, c, s1, s2), axis=1)
    k_ref[...] = k
    for g in range(N_KV_HEADS):
        kg_ref[g] = k[:, g * HEAD_DIM:(g + 1) * HEAD_DIM].astype(BF16)

    pv = _dot(h, w_ref[:, Q_COLS + KV_COLS:Q_COLS + 2 * KV_COLS])
    v_ref[...] = pv
    ones = jnp.ones((V_ROWS - HEAD_DIM, ATTN_CHUNK), BF16)
    for cc in range(pv.shape[0] // ATTN_CHUNK):
        vt = pv[cc * ATTN_CHUNK:(cc + 1) * ATTN_CHUNK, :].astype(BF16).T
        for g in range(N_KV_HEADS):
            vt_ref[cc, g * V_ROWS:g * V_ROWS + HEAD_DIM, :] = vt[g * HEAD_DIM:(g + 1) * HEAD_DIM, :]
            vt_ref[cc, g * V_ROWS + HEAD_DIM:(g + 1) * V_ROWS, :] = ones

    o = Q_COLS + 2 * KV_COLS
    pqi = _dot(h, w_ref[:, o:o + IQ_COLS])
    for j, blk in enumerate(_rope_cols(pqi, c, s1, s2)):
        qit_ref[j * LANES:(j + 1) * LANES, :] = blk.astype(BF16).T

    pl_ = _dot(h, w_ref[:, o + IQ_COLS:ATTN_IN_PAD])
    last = _rope_block(pl_, *_last_block_tables(c, s1, s2))
    ki = last[:, 0:IDX_DIM]
    ki_ref[...] = ki
    kib_ref[...] = ki.astype(BF16)
    wit_ref[...] = last.T[IDX_DIM:IDX_DIM + IDX_HEADS, :]


def _proj_prompt(x, sc, sh, g, w, rope):
    nb, t, _ = x.shape
    tm = ROW_TILE
    row = lambda width: pl.BlockSpec((None, tm, width), lambda b, j: (b, j, 0))
    mod = pl.BlockSpec((None, 1, D_MODEL), lambda b, j: (b, 0, 0))
    tab = pl.BlockSpec((tm, LANES), lambda b, j: (j, 0))
    col = lambda height: pl.BlockSpec((None, height, tm), lambda b, j: (b, 0, j))
    out_shape = (
        jax.ShapeDtypeStruct((nb, t, KV_COLS), F32),
        jax.ShapeDtypeStruct((nb, t, KV_COLS), F32),
        jax.ShapeDtypeStruct((nb, t, IDX_DIM), F32),
        jax.ShapeDtypeStruct((nb, Q_COLS, t), BF16),
        jax.ShapeDtypeStruct((nb, IQ_COLS, t), BF16),
        jax.ShapeDtypeStruct((nb, N_KV_HEADS, t, HEAD_DIM), BF16),
        jax.ShapeDtypeStruct((nb, t // ATTN_CHUNK, N_KV_HEADS * V_ROWS, ATTN_CHUNK), BF16),
        jax.ShapeDtypeStruct((nb, t, IDX_DIM), BF16),
        jax.ShapeDtypeStruct((nb, IDX_HEADS, t), F32),
    )
    out_specs = (
        row(KV_COLS), row(KV_COLS), row(IDX_DIM), col(Q_COLS), col(IQ_COLS),
        pl.BlockSpec((None, N_KV_HEADS, tm, HEAD_DIM), lambda b, j: (b, 0, j, 0)),
        pl.BlockSpec((None, tm // ATTN_CHUNK, N_KV_HEADS * V_ROWS, ATTN_CHUNK), lambda b, j: (b, j, 0, 0)),
        row(IDX_DIM), col(IDX_HEADS),
    )
    return pl.pallas_call(
        _proj_prompt_kernel,
        grid=(nb, t // tm),
        in_specs=[row(D_MODEL), mod, mod, _const_spec((1, D_MODEL)),
                  _const_spec((D_MODEL, ATTN_IN_PAD)), tab, tab, tab],
        out_specs=out_specs,
        out_shape=out_shape,
        compiler_params=_params("arbitrary", "arbitrary"),
        name="attn_proj_prompt",
    )(x, sc, sh, g, w, *rope)


def _proj_sample_kernel(x_ref, sc_ref, sh_ref, g_ref, w_ref, c_ref, s1_ref, s2_ref, p_ref):
    h = _norm_mod(x_ref[...], g_ref[...], sc_ref[...], sh_ref[...]).astype(BF16)
    c, s1, s2 = c_ref[...], s1_ref[...], s2_ref[...]
    p = _dot(h, w_ref[...])
    v_lo = (Q_COLS + KV_COLS) // LANES
    v_hi = (Q_COLS + 2 * KV_COLS) // LANES
    nblk = ATTN_IN_PAD // LANES
    for j in range(nblk):
        blk = p[:, j * LANES:(j + 1) * LANES]
        if j == nblk - 1:
            blk = _rope_block(blk, *_last_block_tables(c, s1, s2))
        elif not (v_lo <= j < v_hi):
            blk = _rope_block(blk, c, s1, s2)
        p_ref[:, j * LANES:(j + 1) * LANES] = blk


def _proj_sample(x, sc, sh, g, w, rope):
    rows = x.shape[0]
    full = lambda width: pl.BlockSpec((rows, width), lambda i: (0, 0))
    return pl.pallas_call(
        _proj_sample_kernel,
        grid=(1,),
        in_specs=[full(D_MODEL), full(D_MODEL), full(D_MODEL),
                  pl.BlockSpec((1, D_MODEL), lambda i: (0, 0)),
                  pl.BlockSpec((D_MODEL, ATTN_IN_PAD), lambda i: (0, 0)),
                  full(LANES), full(LANES), full(LANES)],
        out_specs=full(ATTN_IN_PAD),
        out_shape=jax.ShapeDtypeStruct((rows, ATTN_IN_PAD), F32),
        compiler_params=_params("arbitrary"),
        name="attn_proj_sample",
    )(x, sc, sh, g, w, *rope)


FOLD_ROWS = 32
BISECT_STEPS_FIRST = 14
BISECT_STEPS_PER_TEST = 3


def _fold(x, op):
    rows, tq = x.shape
    x3 = x.reshape(rows // FOLD_ROWS, FOLD_ROWS, tq)
    return {"sum": x3.sum, "max": x3.max, "min": x3.min}[op](axis=0)


def _stats_init(tq):
    zeros = jnp.zeros((FOLD_ROWS, tq), F32)
    return (jnp.full((FOLD_ROWS, tq), -jnp.inf, F32), jnp.full((FOLD_ROWS, tq), jnp.inf, F32),
            zeros, zeros, zeros)


def _stats_update(stats, s):
    mx, mn, n_above, n_ge0, n_gt0 = stats
    ones_where = lambda m: _fold(jnp.where(m, 1.0, 0.0), "sum")
    above = s > -jnp.inf
    return (jnp.maximum(mx, _fold(s, "max")),
            jnp.minimum(mn, _fold(jnp.where(above, s, jnp.inf), "min")),
            n_above + ones_where(above), n_ge0 + ones_where(s >= 0.0), n_gt0 + ones_where(s > 0.0))


def _select_to_bias(sc_ref, tie_ref, qpos, nk, kc, stats):
    tq = sc_ref.shape[1]
    topk = float(TOPK)

    slab = 4 * FOLD_ROWS

    def for_slabs(fn, init, slab=slab):
        def body(c, carry):
            for j in range(kc // slab):
                start = pl.multiple_of(c * kc + j * slab, slab)
                idx = start + lax.broadcasted_iota(jnp.int32, (slab, tq), 0)
                carry = fn(pl.ds(start, slab), idx, carry)
            return carry
        return lax.fori_loop(0, nk, body, init)

    def ones_where(m):
        return _fold(jnp.where(m, 1.0, 0.0), "sum")

    def count(pred):
        acc = for_slabs(lambda rows, idx, acc: acc + ones_where(pred(rows)), jnp.zeros((FOLD_ROWS, tq), F32))
        return acc.sum(axis=0, keepdims=True)

    def count_ge(t):
        return count(lambda rows: sc_ref[rows, :] >= t)

    mx, mn, n_above, n_ge0, n_gt0 = stats
    mx = mx.max(axis=0, keepdims=True)
    mn = mn.min(axis=0, keepdims=True)
    n_above = n_above.sum(axis=0, keepdims=True)
    n_ge0 = n_ge0.sum(axis=0, keepdims=True)
    n_gt0 = n_gt0.sum(axis=0, keepdims=True)
    n_max = count_ge(mx)

    few = n_above < topk
    flat = n_max >= topk
    zero = (n_gt0 < topk) & (n_ge0 >= topk)
    pos = n_gt0 >= topk
    lo = jnp.where(few, -jnp.inf, jnp.where(flat, mx, jnp.where(zero | pos, 0.0, mn)))
    hi = jnp.where(few, mn, jnp.where(flat, jnp.inf, jnp.where(pos, mx, 0.0)))
    n_lo = jnp.where(few, 2.0 * topk, jnp.where(flat, n_max, jnp.where(zero | pos, n_ge0, n_above)))
    n_hi = jnp.where(few, n_above, jnp.where(flat, 0.0, jnp.where(zero, n_gt0, jnp.where(pos, n_max, n_ge0))))
    done = jnp.where(few | flat | zero | (n_lo == topk), 1.0, 0.0)

    def not_finished(d):
        return (jnp.min(d) < 0.5).astype(jnp.int32)

    def bisect_step(state):
        lo, hi, n_lo, n_hi, done = state
        mid = jnp.clip(0.5 * lo + 0.5 * hi, -F32_MAX, F32_MAX)
        stuck = (mid <= lo) | (mid >= hi)
        n_mid = count_ge(mid)
        move = (done < 0.5) & jnp.logical_not(stuck)
        up = move & (n_mid >= topk)
        down = move & (n_mid < topk)
        lo = jnp.where(up, mid, lo)
        n_lo = jnp.where(up, n_mid, n_lo)
        hi = jnp.where(down, mid, hi)
        n_hi = jnp.where(down, n_mid, n_hi)
        done = jnp.where(stuck | (n_lo == topk), 1.0, done)
        return lo, hi, n_lo, n_hi, done

    def bisect(carry):
        state = lax.fori_loop(0, carry[6], lambda _, st: bisect_step(st), carry[:5])
        return (*state, not_finished(state[4]), jnp.int32(BISECT_STEPS_PER_TEST))

    lo, hi, n_lo, n_hi, done, _, _ = lax.while_loop(
        lambda carry: carry[5] > 0, bisect,
        (lo, hi, n_lo, n_hi, done, not_finished(done), jnp.int32(BISECT_STEPS_FIRST)))

    thr = lo
    exact = n_lo == topk
    need = topk - n_hi
    all_keys = jnp.full((1, tq), 2 ** 30, jnp.int32)

    def tie_break():
        def mark(rows, idx, _):
            tie_ref[rows, :] = jnp.where(sc_ref[rows, :] == thr, idx, 2 ** 30)
            return 0
        for_slabs(mark, 0)

        n_bits = (sc_ref.shape[0] - 1).bit_length()

        def step(i, j):
            cand = j + lax.shift_right_logical(jnp.int32(1 << (n_bits - 1)), i)
            n_before = count(lambda rows: tie_ref[rows, :] < cand)
            return jnp.where(n_before < need, cand, j)
        return lax.fori_loop(0, n_bits, step, jnp.zeros((1, tq), jnp.int32))

    any_tie = jnp.min(jnp.where(exact, 1.0, 0.0)) < 0.5

    def write(keep_fn):
        def body(rows, idx, _):
            sc_ref[rows, :] = jnp.where(keep_fn(sc_ref[rows, :], idx), 0.0, MASK_BIAS)
            return 0
        return for_slabs(body, 0, slab=kc)

    def write_with_ties():
        last_tied = jnp.minimum(jnp.where(exact, all_keys, tie_break()), qpos)
        return write(lambda s, idx: (s > thr) | ((s == thr) & (idx <= last_tied)))

    lax.cond(any_tie, write_with_ties, lambda: write(lambda s, idx: s >= thr))


def _prompt_attn_kernel(qt_ref, qit_ref, wit_ref, kg_ref, vt_ref, kib_ref, o_ref,
                        sc_ref, tie_ref, acc_ref, m_ref, mx_ref, s_ref):
    i = pl.program_id(1)
    tq = qt_ref.shape[1]
    qpos = i * tq + lax.broadcasted_iota(jnp.int32, (1, tq), 1)
    n_keys = (i + 1) * tq

    n_score = (n_keys + SCORE_CHUNK - 1) // SCORE_CHUNK

    def score_chunk(c, stats):
        for j in range(SCORE_CHUNK // SCORE_SLAB):
            start = pl.multiple_of(c * SCORE_CHUNK + j * SCORE_SLAB, SCORE_SLAB)
            kc = kib_ref[pl.ds(start, SCORE_SLAB), :]
            acc = jnp.zeros((SCORE_SLAB, tq), F32)
            for h in range(IDX_HEADS):
                d = _dot(kc, qit_ref[h * IDX_DIM:(h + 1) * IDX_DIM, :])
                acc = acc + jnp.maximum(d, 0.0) * wit_ref[h:h + 1, :]
            idx = start + lax.broadcasted_iota(jnp.int32, (SCORE_SLAB, tq), 0)
            s = jnp.where(idx <= qpos, acc, -jnp.inf)
            sc_ref[pl.ds(start, SCORE_SLAB), :] = s
            stats = _stats_update(stats, s)
        return stats

    stats = lax.fori_loop(0, n_score, score_chunk, _stats_init(tq))

    _select_to_bias(sc_ref, tie_ref, qpos, n_score, SCORE_CHUNK, stats)

    m_ref[...] = jnp.full(m_ref.shape, MASK_BIAS, F32)
    acc_ref[...] = jnp.zeros(acc_ref.shape, F32)

    def attn_chunk(c, _):
        start = pl.multiple_of(c * ATTN_CHUNK, ATTN_CHUNK)
        for h in range(N_HEADS):
            s = _dot(kg_ref[h // HEADS_PER_KV, pl.ds(start, ATTN_CHUNK), :],
                     qt_ref[h * HEAD_DIM:(h + 1) * HEAD_DIM, :])
            s = s + sc_ref[pl.ds(start, ATTN_CHUNK), :]
            s_ref[h] = s
            mx_ref[h] = s.max(axis=0, keepdims=True)
        for h in range(N_HEADS):
            g = h // HEADS_PER_KV
            m_old = m_ref[h]
            m_new = jnp.maximum(m_old, mx_ref[h])
            alpha = jnp.exp2(m_old - m_new)
            p = jnp.exp2(s_ref[h] - m_new).astype(BF16)
            vt = vt_ref[c, g * V_ROWS:(g + 1) * V_ROWS, :]
            acc_ref[h] = alpha * acc_ref[h] + _dot(vt, p)
            m_ref[h] = m_new
        return 0

    lax.fori_loop(0, (i + 1) * (tq // ATTN_CHUNK), attn_chunk, 0)

    for pair in range(N_HEADS // 2):
        two = []
        for h in (2 * pair, 2 * pair + 1):
            a = acc_ref[h]
            two.append(a[0:HEAD_DIM, :] / a[HEAD_DIM:HEAD_DIM + 1, :])
        col = 2 * pair * HEAD_DIM
        o_ref[:, col:col + 2 * HEAD_DIM] = jnp.concatenate(two, axis=0).T.astype(BF16)


def _prompt_attention(qt, qit, wit, kg, vt, kib):
    nb, _, t = qt.shape
    tq = Q_TILE
    col = lambda height: pl.BlockSpec((None, height, tq), lambda b, i: (b, 0, i))
    return pl.pallas_call(
        _prompt_attn_kernel,
        grid=(nb, t // tq),
        in_specs=[
            col(Q_COLS), col(IQ_COLS), col(IDX_HEADS),
            pl.BlockSpec((None, N_KV_HEADS, t, HEAD_DIM), lambda b, i: (b, 0, 0, 0)),
            pl.BlockSpec((None, t // ATTN_CHUNK, N_KV_HEADS * V_ROWS, ATTN_CHUNK), lambda b, i: (b, 0, 0, 0)),
            pl.BlockSpec((None, t, IDX_DIM), lambda b, i: (b, 0, 0)),
        ],
        out_specs=pl.BlockSpec((None, tq, Q_COLS), lambda b, i: (b, i, 0)),
        out_shape=jax.ShapeDtypeStruct((nb, t, Q_COLS), BF16),
        scratch_shapes=[
            pltpu.VMEM((t, tq), F32),
            pltpu.VMEM((t, tq), jnp.int32),
            pltpu.VMEM((N_HEADS, V_ROWS, tq), F32),
            pltpu.VMEM((N_HEADS, 1, tq), F32),
            pltpu.VMEM((N_HEADS, 1, tq), F32),
            pltpu.VMEM((N_HEADS, ATTN_CHUNK, tq), F32),
        ],
        compiler_params=_params("arbitrary", "arbitrary"),
        name="prompt_attention",
    )(qt, qit, wit, kg, vt, kib)


def _sample_score_kernel(pt_ref, qi_ref, w_ref, *rest, n_pages):
    del pt_ref
    n_seq = qi_ref.shape[0]
    pages, (new_ref, o_ref, kall_ref) = rest[:n_seq * n_pages], rest[n_seq * n_pages:]
    for q in range(n_seq):
        for j in range(n_pages):
            kall_ref[q, :, j * PAGE_SIZE:(j + 1) * PAGE_SIZE] = pages[q * n_pages + j][...].astype(BF16)
        kall_ref[q, :, n_pages * PAGE_SIZE:(n_pages + 1) * PAGE_SIZE] = new_ref[q].astype(BF16)
    for q in range(n_seq):
        d = _dot(qi_ref[q], kall_ref[q])
        r = jnp.maximum(d, 0.0) * w_ref[q]
        o_ref[q] = r.reshape(IDX_HEADS, SUBLANES, r.shape[1]).sum(axis=0)


def _sample_scores(page_table, qi8, w8, cache_ki_t, ki_new_t):
    nb, n_pages = page_table.shape
    n_keys = (n_pages + 1) * PAGE_SIZE
    rows = IDX_HEADS * SUBLANES
    ns = SCORE_SEQS
    per_step = lambda r, w: pl.BlockSpec((ns, r, w), lambda b, pt: (b, 0, 0))
    grid_spec = pltpu.PrefetchScalarGridSpec(
        num_scalar_prefetch=1,
        grid=(nb // ns,),
        in_specs=[per_step(rows, IDX_DIM), per_step(rows, 1)]
        + [pl.BlockSpec((None, IDX_DIM, PAGE_SIZE), lambda b, pt, q=q, j=j: (pt[b * ns + q, j], 0, 0))
           for q in range(ns) for j in range(n_pages)]
        + [per_step(IDX_DIM, PAGE_SIZE)],
        out_specs=per_step(SUBLANES, n_keys),
        scratch_shapes=[pltpu.VMEM((ns, IDX_DIM, n_keys), BF16)],
    )
    return pl.pallas_call(
        functools.partial(_sample_score_kernel, n_pages=n_pages),
        grid_spec=grid_spec,
        out_shape=jax.ShapeDtypeStruct((nb, SUBLANES, n_keys), F32),
        compiler_params=_params("arbitrary"),
        name="sample_indexer_scores",
    )(page_table, qi8, w8, *([cache_ki_t] * (ns * n_pages)), ki_new_t)


def _sample_select_kernel(s_ref, o_ref, sc_ref, tie_ref, *, past, period):
    tq = s_ref.shape[1]
    n_keys = s_ref.shape[0]
    lane = lax.broadcasted_iota(jnp.int32, (1, tq), 1)
    qpos = past + (lane & (period - 1))
    nk = n_keys // PAGE_SIZE
    stats = _stats_init(tq)
    for c in range(nk):
        idx = c * PAGE_SIZE + lax.broadcasted_iota(jnp.int32, (PAGE_SIZE, tq), 0)
        s = jnp.where(idx <= qpos, s_ref[c * PAGE_SIZE:(c + 1) * PAGE_SIZE, :], -jnp.inf)
        sc_ref[c * PAGE_SIZE:(c + 1) * PAGE_SIZE, :] = s
        stats = _stats_update(stats, s)
    _select_to_bias(sc_ref, tie_ref, qpos, nk, PAGE_SIZE, stats)
    o_ref[...] = sc_ref[...]


def _sample_select(scores_t, past, period):
    n_keys, nq = scores_t.shape
    tq = Q_TILE
    blk = pl.BlockSpec((n_keys, tq), lambda i: (0, i))
    return pl.pallas_call(
        functools.partial(_sample_select_kernel, past=past, period=period),
        grid=(nq // tq,),
        in_specs=[blk],
        out_specs=blk,
        out_shape=jax.ShapeDtypeStruct((n_keys, nq), F32),
        scratch_shapes=[pltpu.VMEM((n_keys, tq), F32), pltpu.VMEM((n_keys, tq), jnp.int32)],
        compiler_params=_params("arbitrary"),
        name="sample_select",
    )(scores_t)


def _sample_attn_kernel(pt_ref, q_ref, bias_ref, *rest, n_pages):
    del pt_ref
    n_seq = q_ref.shape[0]
    kpages, vpages = rest[:n_seq * n_pages], rest[n_seq * n_pages:2 * n_seq * n_pages]
    knew_ref, vnew_ref, o_ref, kall_ref, vall_ref, s_ref, p_ref, l_ref = rest[2 * n_seq * n_pages:]
    n_keys = bias_ref.shape[2]
    fill = jnp.zeros((PAGE_SIZE - SUBLANES, LANES), F32)
    for q in range(n_seq):
        for j in range(n_pages):
            kall_ref[q, :, :, j * PAGE_SIZE:(j + 1) * PAGE_SIZE] = kpages[q * n_pages + j][...].astype(BF16)
            vall_ref[q, :, :, j * PAGE_SIZE:(j + 1) * PAGE_SIZE] = vpages[q * n_pages + j][...].astype(BF16)
        for g in range(N_KV_HEADS):
            for new_ref, all_ref in ((knew_ref, kall_ref), (vnew_ref, vall_ref)):
                page = jnp.concatenate([new_ref[q, g], fill], axis=0).T
                all_ref[q, g, :, n_pages * PAGE_SIZE:(n_pages + 1) * PAGE_SIZE] = (
                    page[0:HEAD_DIM, :].astype(BF16))
    for q in range(n_seq):
        bias = jnp.broadcast_to(bias_ref[q][None], (HEADS_PER_KV, SUBLANES, n_keys)).reshape(
            HEADS_PER_KV * SUBLANES, n_keys)
        for g in range(N_KV_HEADS):
            s_ref[q, g] = _dot(q_ref[q, g], kall_ref[q, g]) + bias
    for q in range(n_seq):
        for g in range(N_KV_HEADS):
            s = s_ref[q, g]
            m = s.max(axis=1, keepdims=True)
            p = jnp.exp(s - m)
            l_ref[q, g] = p.sum(axis=1, keepdims=True)
            p_ref[q, g] = p.astype(BF16)
    for q in range(n_seq):
        for g in range(N_KV_HEADS):
            o_ref[q, g] = _dot_nt(p_ref[q, g], vall_ref[q, g]) / l_ref[q, g]


def _sample_attention(page_table, q8, bias8, cache_k_t, cache_v_t, k_new_t, v_new_t):
    nb, n_pages = page_table.shape
    n_keys = (n_pages + 1) * PAGE_SIZE
    rows = HEADS_PER_KV * SUBLANES
    ns = ATTN_SEQS
    page = [pl.BlockSpec((None, N_KV_HEADS, HEAD_DIM, PAGE_SIZE),
                         lambda b, pt, q=q, j=j: (pt[b * ns + q, j], 0, 0, 0))
            for q in range(ns) for j in range(n_pages)]
    new = pl.BlockSpec((ns, N_KV_HEADS, SUBLANES, LANES), lambda b, pt: (b, 0, 0, 0))
    qo = pl.BlockSpec((ns, N_KV_HEADS, rows, HEAD_DIM), lambda b, pt: (b, 0, 0, 0))
    grid_spec = pltpu.PrefetchScalarGridSpec(
        num_scalar_prefetch=1,
        grid=(nb // ns,),
        in_specs=[qo, pl.BlockSpec((ns, SUBLANES, n_keys), lambda b, pt: (b, 0, 0))]
        + page + page + [new, new],
        out_specs=qo,
        scratch_shapes=[pltpu.VMEM((ns, N_KV_HEADS, HEAD_DIM, n_keys), BF16),
                        pltpu.VMEM((ns, N_KV_HEADS, HEAD_DIM, n_keys), BF16),
                        pltpu.VMEM((ns, N_KV_HEADS, rows, n_keys), F32),
                        pltpu.VMEM((ns, N_KV_HEADS, rows, n_keys), BF16),
                        pltpu.VMEM((ns, N_KV_HEADS, rows, 1), F32)],
    )
    pages = ns * n_pages
    return pl.pallas_call(
        functools.partial(_sample_attn_kernel, n_pages=n_pages),
        grid_spec=grid_spec,
        out_shape=jax.ShapeDtypeStruct((nb, N_KV_HEADS, rows, HEAD_DIM), F32),
        compiler_params=_params("arbitrary"),
        name="sample_attention",
    )(page_table, q8, bias8, *([cache_k_t] * pages), *([cache_v_t] * pages), k_new_t, v_new_t)


def _ffn(h, wgu_ref, wd_ref):
    acc = jnp.zeros((h.shape[0], D_MODEL), F32)
    for c in range(D_FF // FFN_CHUNK):
        gate = _dot(h, wgu_ref[:, c * FFN_CHUNK:(c + 1) * FFN_CHUNK])
        up = _dot(h, wgu_ref[:, D_FF + c * FFN_CHUNK:D_FF + (c + 1) * FFN_CHUNK])
        a = _silu(gate) * up
        acc = acc + _dot(a.astype(BF16), wd_ref[c * FFN_CHUNK:(c + 1) * FFN_CHUNK, :])
    return acc


def _row_specs(nb, rows, per_row_mod):
    tm = min(ROW_TILE, rows)
    row = pl.BlockSpec((None, tm, D_MODEL), lambda b, j: (b, j, 0))
    if per_row_mod:
        mod = row
    else:
        mod = pl.BlockSpec((None, 1, D_MODEL), lambda b, j: (b, 0, 0))
    return tm, row, mod


def _attn_out_ffn_kernel(x_ref, o_ref, g1_ref, sc2_ref, sh2_ref, g2_ref, n2_ref,
                         wo_ref, wgu_ref, wd_ref, y_ref):
    y1 = x_ref[...] + g1_ref[...] * _dot(o_ref[...], wo_ref[...])
    h2 = _norm_mod(y1, n2_ref[...], sc2_ref[...], sh2_ref[...]).astype(BF16)
    y_ref[...] = y1 + g2_ref[...] * _ffn(h2, wgu_ref, wd_ref)


def _layer_spec(stacked, layer):
    _, rows, cols = stacked.shape
    return pl.BlockSpec((None, rows, cols), lambda *_: (layer, 0, 0), pipeline_mode=pl.Buffered(1))


def _attn_out_ffn(x, o, g1, sc2, sh2, g2, n2, wo, wgu, wd, layer):
    nb, rows, _ = x.shape
    tm, row, mod = _row_specs(nb, rows, g1.shape[1] == rows)
    return pl.pallas_call(
        _attn_out_ffn_kernel,
        grid=(nb, rows // tm),
        in_specs=[row, row, mod, mod, mod, mod, _const_spec((1, D_MODEL)),
                  _const_spec(wo.shape), _layer_spec(wgu, layer), _layer_spec(wd, layer)],
        out_specs=row,
        out_shape=jax.ShapeDtypeStruct(x.shape, F32),
        compiler_params=_params("arbitrary", "arbitrary"),
        name="attn_out_ffn",
    )(x, o, g1, sc2, sh2, g2, n2, wo, wgu, wd)


def _conv_layer_kernel(x_ref, sc1_ref, sh1_ref, g1_ref, sc2_ref, sh2_ref, g2_ref, p1_ref, p2_ref,
                       n1_ref, n2_ref, nf_ref, win_ref, ck_ref, wout_ref, wgu_ref, wd_ref,
                       y_ref, tail_ref, ubuf_ref, z_ref, *, seg):
    j = pl.program_id(1)
    tm = x_ref.shape[0]
    x = x_ref[...]
    h = _norm_mod(x, n1_ref[...], sc1_ref[...], sh1_ref[...]).astype(BF16)

    @pl.when(j == 0)
    def _():
        ubuf_ref[0:SUBLANES, :] = jnp.zeros((SUBLANES, D_MODEL), F32)

    @pl.when(j > 0)
    def _():
        ubuf_ref[0:SUBLANES, :] = ubuf_ref[tm:tm + SUBLANES, :]

    t = (j * tm + lax.broadcasted_iota(jnp.int32, (tm, 1), 0)) & (seg - 1)
    ck = ck_ref[...]
    cw = FFN_CHUNK
    for c in range(D_MODEL // cw):
        cols = slice(c * cw, (c + 1) * cw)
        bg = _dot(h, win_ref[:, c * cw:(c + 1) * cw])
        cg = _dot(h, win_ref[:, D_MODEL + c * cw:D_MODEL + (c + 1) * cw])
        hv = _dot(h, win_ref[:, 2 * D_MODEL + c * cw:2 * D_MODEL + (c + 1) * cw])
        u = cg * hv
        ubuf_ref[SUBLANES:SUBLANES + tm, cols] = u
        um1 = jnp.where(t >= 1, ubuf_ref[SUBLANES - 1:SUBLANES - 1 + tm, cols], p1_ref[:, cols])
        um2 = jnp.where(t >= 2, ubuf_ref[SUBLANES - 2:SUBLANES - 2 + tm, cols], p2_ref[:, cols])
        conv = ck[0:1, cols] * um2 + ck[1:2, cols] * um1 + ck[2:3, cols] * u
        z_ref[:, cols] = (bg * conv).astype(BF16)
    r = tail_ref.shape[0]
    tail_ref[...] = ubuf_ref[SUBLANES + tm - r:SUBLANES + tm, :]

    y1 = x + g1_ref[...] * _dot(z_ref[...], wout_ref[...])
    h2 = _norm_mod(y1, n2_ref[...], sc2_ref[...], sh2_ref[...]).astype(BF16)
    y2 = y1 + g2_ref[...] * _ffn(h2, wgu_ref, wd_ref)
    ms = jnp.mean(y2 * y2, axis=-1, keepdims=True)
    y_ref[...] = (y2 * lax.rsqrt(ms + EPS)) * nf_ref[...]


def _conv_layer(x, mods, p1, p2, n1, n2, nf, win, ck, wout, wgu, wd, *, layer, seg, full_tail):
    nb, rows, _ = x.shape
    per_row = mods[0].shape[1] == rows
    tm, row, mod = _row_specs(nb, rows, per_row)
    if per_row:
        prefix = row
    else:
        prefix = pl.BlockSpec((None, 1, D_MODEL), lambda b, j: (0, 0, 0))
    if full_tail:
        tail_spec, tail_rows = row, rows
    else:
        tail_spec, tail_rows = pl.BlockSpec((None, SUBLANES, D_MODEL), lambda b, j: (b, 0, 0)), SUBLANES
    vec = _const_spec((1, D_MODEL))
    return pl.pallas_call(
        functools.partial(_conv_layer_kernel, seg=seg),
        grid=(nb, rows // tm),
        in_specs=[row] + [mod] * 6 + [prefix, prefix, vec, vec, vec,
                                      _const_spec(win.shape), _const_spec(ck.shape), _const_spec(wout.shape),
                                      _layer_spec(wgu, layer), _layer_spec(wd, layer)],
        out_specs=(row, tail_spec),
        out_shape=(jax.ShapeDtypeStruct(x.shape, F32),
                   jax.ShapeDtypeStruct((nb, tail_rows, D_MODEL), F32)),
        scratch_shapes=[pltpu.VMEM((tm + SUBLANES, D_MODEL), F32), pltpu.VMEM((tm, D_MODEL), BF16)],
        compiler_params=_params("arbitrary", "arbitrary"),
        name="conv_layer",
    )(x, *mods, p1, p2, n1, n2, nf, win, ck, wout, wgu, wd)


def _rope_tables(pos):
    half = ROT_DIM // 2
    inv = ROPE_THETA ** (-jnp.arange(0, ROT_DIM, 2, dtype=F32) / ROT_DIM)
    ang = pos[:, None] * inv[None, :]
    cos, sin = jnp.cos(ang), jnp.sin(ang)
    n = pos.shape[0]
    rest = HEAD_DIM - ROT_DIM
    c = jnp.concatenate([cos, cos, jnp.ones((n, rest), F32)], axis=1)
    s1 = jnp.concatenate([-sin, jnp.zeros((n, half + rest), F32)], axis=1)
    s2 = jnp.concatenate([jnp.zeros((n, half), F32), sin, jnp.zeros((n, rest), F32)], axis=1)
    return tuple(jnp.tile(a, (1, LANES // HEAD_DIM)) for a in (c, s1, s2))


def kernel(x_prompt, x_sample, c_prompt, c_sample, cache_k, cache_v, cache_kidx, state_conv, page_table,
           ada_w, ada_b, norm1_g, norm2_g, final_g, attn_w_in, attn_w_out, conv_w_in, conv_k, conv_w_out,
           ffn_w_gu, ffn_w_down):
    nb, t, _ = x_prompt.shape
    ns, tn, _ = x_sample.shape
    n_pages = page_table.shape[1]
    past = n_pages * PAGE_SIZE
    n_phys = cache_k.shape[1]
    rows_s = ns * tn

    pad = (-(nb + ns)) % SUBLANES
    c_all = jnp.concatenate([c_prompt, c_sample, jnp.zeros((pad, D_MODEL), F32)], axis=0)
    mod = _modulation(c_all, ada_w, ada_b)

    def mods(layer):
        cols = [mod[layer, :, i * D_MODEL:(i + 1) * D_MODEL] for i in range(6)]
        prompt = [m[:nb, None, :] for m in cols]
        sample = [jnp.repeat(m[nb:nb + ns], tn, axis=0)[None] for m in cols]
        return prompt, sample

    vec = lambda a: a.reshape(1, D_MODEL)
    w_in = jnp.pad(attn_w_in[0], ((0, 0), (0, ATTN_IN_PAD - ATTN_IN))).astype(BF16)
    w_o = attn_w_out[0].astype(BF16)
    wgu = ffn_w_gu.astype(BF16)
    wd = ffn_w_down.astype(BF16)

    (sh1p, sc1p, g1p, sh2p, sc2p, g2p), (sh1s, sc1s, g1s, sh2s, sc2s, g2s) = mods(0)
    rope_p = _rope_tables(jnp.arange(t, dtype=F32))
    k_p, v_p, ki_p, qt, qit, kg, vt, kib, wit = _proj_prompt(
        x_prompt, sc1p, sh1p, vec(norm1_g[0]), w_in, rope_p)
    o_p = _prompt_attention(qt, qit, wit, kg, vt, kib)
    y_p = _attn_out_ffn(x_prompt, o_p, g1p, sc2p, sh2p, g2p, vec(norm2_g[0]), w_o, wgu, wd, 0)

    pos_s = jnp.tile(past + jnp.arange(tn, dtype=F32), ns)
    rope_s = _rope_tables(pos_s)
    xs = x_sample.reshape(rows_s, D_MODEL)
    proj = _proj_sample(xs, sc1s[0], sh1s[0], vec(norm1_g[0]), w_in, rope_s)
    o = 0
    q_s = proj[:, o:o + Q_COLS].reshape(ns, tn, N_KV_HEADS, HEADS_PER_KV, HEAD_DIM); o += Q_COLS
    k_s = proj[:, o:o + KV_COLS].reshape(ns, tn, KV_COLS); o += KV_COLS
    v_s = proj[:, o:o + KV_COLS].reshape(ns, tn, KV_COLS); o += KV_COLS
    qi_s = proj[:, o:o + IQ_COLS].reshape(ns, tn, IDX_HEADS, IDX_DIM); o += IQ_COLS
    ki_s = proj[:, o:o + IDX_DIM].reshape(ns, tn, IDX_DIM); o += IDX_DIM
    wi_s = proj[:, o:o + IDX_HEADS].reshape(ns, tn, IDX_HEADS)

    qpad = ((0, 0), (0, SUBLANES - tn))
    qi8 = jnp.pad(qi_s, qpad + ((0, 0), (0, 0))).transpose(0, 2, 1, 3)
    qi8 = qi8.reshape(ns, IDX_HEADS * SUBLANES, IDX_DIM).astype(BF16)
    w8 = jnp.pad(wi_s, qpad + ((0, 0),)).transpose(0, 2, 1).reshape(ns, IDX_HEADS * SUBLANES, 1)
    slot_pad = (0, PAGE_SIZE - tn)
    ki_new_t = jnp.pad(ki_s.transpose(0, 2, 1), ((0, 0), (0, 0), slot_pad))
    scores = _sample_scores(page_table, qi8, w8, cache_kidx[0].transpose(0, 2, 1), ki_new_t)
    n_keys = scores.shape[2]
    scores_t = scores[:, :tn, :].transpose(2, 0, 1).reshape(n_keys, rows_s)
    bias_t = _sample_select(scores_t, past, tn)
    bias8 = jnp.pad(bias_t.reshape(n_keys, ns, tn).transpose(1, 2, 0), qpad + ((0, 0),))

    q8 = jnp.pad(q_s * HEAD_DIM ** -0.5, qpad + ((0, 0), (0, 0), (0, 0))).transpose(0, 2, 3, 1, 4)
    q8 = q8.reshape(ns, N_KV_HEADS, HEADS_PER_KV * SUBLANES, HEAD_DIM).astype(BF16)
    new_t = lambda a: jnp.pad(a.reshape(ns, tn, N_KV_HEADS, HEAD_DIM).transpose(0, 2, 1, 3),
                              ((0, 0), (0, 0), (0, SUBLANES - tn), (0, LANES - HEAD_DIM)))
    o_g = _sample_attention(page_table, q8, bias8,
                            cache_k[0].transpose(0, 2, 3, 1), cache_v[0].transpose(0, 2, 3, 1),
                            new_t(k_s), new_t(v_s))
    o_g = o_g.reshape(ns, N_KV_HEADS, HEADS_PER_KV, SUBLANES, HEAD_DIM)[:, :, :, :tn]
    o_s = o_g.transpose(0, 3, 1, 2, 4).reshape(1, rows_s, Q_COLS).astype(BF16)
    y_s = _attn_out_ffn(xs[None], o_s, g1s, sc2s, sh2s, g2s, vec(norm2_g[0]), w_o, wgu, wd, 0)

    (sh1p, sc1p, g1p, sh2p, sc2p, g2p), (sh1s, sc1s, g1s, sh2s, sc2s, g2s) = mods(1)
    win = conv_w_in[0].astype(BF16)
    wout = conv_w_out[0].astype(BF16)
    zero_prefix = jnp.zeros((1, 1, D_MODEL), F32)
    out_p, tail_p = _conv_layer(
        y_p, (sc1p, sh1p, g1p, sc2p, sh2p, g2p), zero_prefix, zero_prefix,
        vec(norm1_g[1]), vec(norm2_g[1]), vec(final_g), win, conv_k[0], wout, wgu, wd,
        layer=1, seg=t, full_tail=False)
    st = state_conv[0]
    zeros_row = jnp.zeros((ns, 1, D_MODEL), F32)
    p1 = jnp.concatenate([st[:, 1:2], zeros_row, zeros_row, zeros_row], axis=1).reshape(1, rows_s, D_MODEL)
    p2 = jnp.concatenate([st[:, 0:1], st[:, 1:2], zeros_row, zeros_row], axis=1).reshape(1, rows_s, D_MODEL)
    out_s, u_s = _conv_layer(
        y_s, (sc1s, sh1s, g1s, sc2s, sh2s, g2s), p1, p2,
        vec(norm1_g[1]), vec(norm2_g[1]), vec(final_g), win, conv_k[0], wout, wgu, wd,
        layer=1, seg=tn, full_tail=True)

    keep = CONV_WIDTH - 1
    return (
        out_p,
        out_s.reshape(ns, tn, D_MODEL),
        k_p.reshape(1, nb, t, N_KV_HEADS, HEAD_DIM),
        v_p.reshape(1, nb, t, N_KV_HEADS, HEAD_DIM),
        ki_p[None],
        tail_p[None, :, SUBLANES - keep:, :],
        k_s.reshape(1, ns, tn, N_KV_HEADS, HEAD_DIM),
        v_s.reshape(1, ns, tn, N_KV_HEADS, HEAD_DIM),
        ki_s[None],
        u_s.reshape(ns, tn, D_MODEL)[None, :, tn - keep:, :],
    )
```

```python
import functools

import jax
import jax.numpy as jnp
from jax import lax
from jax.experimental import pallas as pl
from jax.experimental.pallas import tpu as pltpu

F32 = jnp.float32
BF16 = jnp.bfloat16

D_MODEL = 1024
N_HEADS = 16
HEAD_DIM = 64
N_KV_HEADS = 4
HEADS_PER_KV = N_HEADS // N_KV_HEADS
ROT_DIM = 16
ROPE_THETA = 500000.0
IDX_HEADS = 8
IDX_DIM = 64
TOPK = 256
PAGE_SIZE = 128
CONV_WIDTH = 3
D_FF = 2816
EPS = 1e-6
Q_COLS = N_HEADS * HEAD_DIM
KV_COLS = N_KV_HEADS * HEAD_DIM
IQ_COLS = IDX_HEADS * IDX_DIM
ATTN_IN = Q_COLS + 2 * KV_COLS + IQ_COLS + IDX_DIM + IDX_HEADS
ATTN_IN_PAD = 2176
LANES = 128
SUBLANES = 8
MASK_BIAS = -1e30
F32_MAX = 3.4028234663852886e38
LOG2_E = 1.4426950408889634
VMEM_LIMIT = 60 * 1024 * 1024

ROW_TILE = 512
Q_TILE = 256
SCORE_CHUNK = 512
SCORE_SLAB = 128
ATTN_CHUNK = 256
V_ROWS = HEAD_DIM + 16
FFN_CHUNK = 256
SCORE_SEQS = 8
ATTN_SEQS = 4


def _dot(a, b):
    return jnp.dot(a, b, preferred_element_type=F32)


def _dot_nt(a, b):
    return lax.dot_general(a, b, (((1,), (1,)), ((), ())), preferred_element_type=F32)


def _silu(x):
    return x / (1.0 + jnp.exp(-x))


def _norm_mod(x, g, sc, sh):
    ms = jnp.mean(x * x, axis=-1, keepdims=True)
    y = x * lax.rsqrt(ms + EPS)
    return (y * g) * (1.0 + sc) + sh


def _params(*sem, flags=None):
    return pltpu.CompilerParams(dimension_semantics=sem, vmem_limit_bytes=VMEM_LIMIT, flags=flags)


def _const_spec(shape):
    nd = len(shape)
    return pl.BlockSpec(shape, lambda *_: (0,) * nd, pipeline_mode=pl.Buffered(1))


def _mod_kernel(c_ref, w_ref, b_ref, o_ref):
    s = _silu(c_ref[...]).astype(BF16)
    o_ref[...] = _dot(s, w_ref[...].astype(BF16)) + b_ref[...]


def _modulation(c_all, ada_w, ada_b):
    depth, _, n = ada_w.shape
    rows = c_all.shape[0]
    tn = 1024
    return pl.pallas_call(
        _mod_kernel,
        grid=(depth, n // tn),
        in_specs=[
            pl.BlockSpec((rows, D_MODEL), lambda l, j: (0, 0)),
            pl.BlockSpec((None, D_MODEL, tn), lambda l, j: (l, 0, j)),
            pl.BlockSpec((None, 1, tn), lambda l, j: (l, 0, j)),
        ],
        out_specs=pl.BlockSpec((None, rows, tn), lambda l, j: (l, 0, j)),
        out_shape=jax.ShapeDtypeStruct((depth, rows, n), F32),
        compiler_params=_params("arbitrary", "arbitrary"),
        name="adaln_modulation",
    )(c_all, ada_w, ada_b.reshape(depth, 1, n))


def _rope_block(x, c, s1, s2):
    return x * c + pltpu.roll(x, LANES - ROT_DIM // 2, 1) * s1 + pltpu.roll(x, ROT_DIM // 2, 1) * s2


def _rope_cols(p, c, s1, s2):
    nblk = p.shape[1] // LANES
    return [_rope_block(p[:, j * LANES:(j + 1) * LANES], c, s1, s2) for j in range(nblk)]


def _last_block_tables(c, s1, s2):
    lane = lax.broadcasted_iota(jnp.int32, c.shape, 1)
    is_key = lane < IDX_DIM
    return jnp.where(is_key, c, 1.0), jnp.where(is_key, s1, 0.0), jnp.where(is_key, s2, 0.0)


def _proj_prompt_kernel(x_ref, sc_ref, sh_ref, g_ref, w_ref, c_ref, s1_ref, s2_ref,
                        k_ref, v_ref, ki_ref, qt_ref, qit_ref, kg_ref, vt_ref, kib_ref, wit_ref):
    h = _norm_mod(x_ref[...], g_ref[...], sc_ref[...], sh_ref[...]).astype(BF16)
    c, s1, s2 = c_ref[...], s1_ref[...], s2_ref[...]

    pq = _dot(h, w_ref[:, 0:Q_COLS])
    for j, blk in enumerate(_rope_cols(pq, c, s1, s2)):
        qt_ref[j * LANES:(j + 1) * LANES, :] = (blk * (HEAD_DIM ** -0.5 * LOG2_E)).astype(BF16).T

    pk = _dot(h, w_ref[:, Q_COLS:Q_COLS + KV_COLS])
    k = jnp.concatenate(_rope_cols(pk, c, s1, s2), axis=1)
    k_ref[...] = k
    for g in range(N_KV_HEADS):
        kg_ref[g] = k[:, g * HEAD_DIM:(g + 1) * HEAD_DIM].astype(BF16)

    pv = _dot(h, w_ref[:, Q_COLS + KV_COLS:Q_COLS + 2 * KV_COLS])
    v_ref[...] = pv
    ones = jnp.ones((V_ROWS - HEAD_DIM, ATTN_CHUNK), BF16)
    for cc in range(pv.shape[0] // ATTN_CHUNK):
        vt = pv[cc * ATTN_CHUNK:(cc + 1) * ATTN_CHUNK, :].astype(BF16).T
        for g in range(N_KV_HEADS):
            vt_ref[cc, g * V_ROWS:g * V_ROWS + HEAD_DIM, :] = vt[g * HEAD_DIM:(g + 1) * HEAD_DIM, :]
            vt_ref[cc, g * V_ROWS + HEAD_DIM:(g + 1) * V_ROWS, :] = ones

    o = Q_COLS + 2 * KV_COLS
    pqi = _dot(h, w_ref[:, o:o + IQ_COLS])
    for j, blk in enumerate(_rope_cols(pqi, c, s1, s2)):
        qit_ref[j * LANES:(j + 1) * LANES, :] = blk.astype(BF16).T

    pl_ = _dot(h, w_ref[:, o + IQ_COLS:ATTN_IN_PAD])
    last = _rope_block(pl_, *_last_block_tables(c, s1, s2))
    ki = last[:, 0:IDX_DIM]
    ki_ref[...] = ki
    kib_ref[...] = ki.astype(BF16)
    wit_ref[...] = last.T[IDX_DIM:IDX_DIM + IDX_HEADS, :]


def _proj_prompt(x, sc, sh, g, w, rope):
    nb, t, _ = x.shape
    tm = ROW_TILE
    row = lambda width: pl.BlockSpec((None, tm, width), lambda b, j: (b, j, 0))
    mod = pl.BlockSpec((None, 1, D_MODEL), lambda b, j: (b, 0, 0))
    tab = pl.BlockSpec((tm, LANES), lambda b, j: (j, 0))
    col = lambda height: pl.BlockSpec((None, height, tm), lambda b, j: (b, 0, j))
    out_shape = (
        jax.ShapeDtypeStruct((nb, t, KV_COLS), F32),
        jax.ShapeDtypeStruct((nb, t, KV_COLS), F32),
        jax.ShapeDtypeStruct((nb, t, IDX_DIM), F32),
        jax.ShapeDtypeStruct((nb, Q_COLS, t), BF16),
        jax.ShapeDtypeStruct((nb, IQ_COLS, t), BF16),
        jax.ShapeDtypeStruct((nb, N_KV_HEADS, t, HEAD_DIM), BF16),
        jax.ShapeDtypeStruct((nb, t // ATTN_CHUNK, N_KV_HEADS * V_ROWS, ATTN_CHUNK), BF16),
        jax.ShapeDtypeStruct((nb, t, IDX_DIM), BF16),
        jax.ShapeDtypeStruct((nb, IDX_HEADS, t), F32),
    )
    out_specs = (
        row(KV_COLS), row(KV_COLS), row(IDX_DIM), col(Q_COLS), col(IQ_COLS),
        pl.BlockSpec((None, N_KV_HEADS, tm, HEAD_DIM), lambda b, j: (b, 0, j, 0)),
        pl.BlockSpec((None, tm // ATTN_CHUNK, N_KV_HEADS * V_ROWS, ATTN_CHUNK), lambda b, j: (b, j, 0, 0)),
        row(IDX_DIM), col(IDX_HEADS),
    )
    return pl.pallas_call(
        _proj_prompt_kernel,
        grid=(nb, t // tm),
        in_specs=[row(D_MODEL), mod, mod, _const_spec((1, D_MODEL)),
                  _const_spec((D_MODEL, ATTN_IN_PAD)), tab, tab, tab],
        out_specs=out_specs,
        out_shape=out_shape,
        compiler_params=_params("arbitrary", "arbitrary"),
        name="attn_proj_prompt",
    )(x, sc, sh, g, w, *rope)


def _proj_sample_kernel(x_ref, sc_ref, sh_ref, g_ref, w_ref, c_ref, s1_ref, s2_ref, p_ref):
    h = _norm_mod(x_ref[...], g_ref[...], sc_ref[...], sh_ref[...]).astype(BF16)
    c, s1, s2 = c_ref[...], s1_ref[...], s2_ref[...]
    p = _dot(h, w_ref[...])
    v_lo = (Q_COLS + KV_COLS) // LANES
    v_hi = (Q_COLS + 2 * KV_COLS) // LANES
    nblk = ATTN_IN_PAD // LANES
    for j in range(nblk):
        blk = p[:, j * LANES:(j + 1) * LANES]
        if j == nblk - 1:
            blk = _rope_block(blk, *_last_block_tables(c, s1, s2))
        elif not (v_lo <= j < v_hi):
            blk = _rope_block(blk, c, s1, s2)
        p_ref[:, j * LANES:(j + 1) * LANES] = blk


def _proj_sample(x, sc, sh, g, w, rope):
    rows = x.shape[0]
    full = lambda width: pl.BlockSpec((rows, width), lambda i: (0, 0))
    return pl.pallas_call(
        _proj_sample_kernel,
        grid=(1,),
        in_specs=[full(D_MODEL), full(D_MODEL), full(D_MODEL),
                  pl.BlockSpec((1, D_MODEL), lambda i: (0, 0)),
                  pl.BlockSpec((D_MODEL, ATTN_IN_PAD), lambda i: (0, 0)),
                  full(LANES), full(LANES), full(LANES)],
        out_specs=full(ATTN_IN_PAD),
        out_shape=jax.ShapeDtypeStruct((rows, ATTN_IN_PAD), F32),
        compiler_params=_params("arbitrary"),
        name="attn_proj_sample",
    )(x, sc, sh, g, w, *rope)


FOLD_ROWS = 32
BISECT_STEPS_FIRST = 14
BISECT_STEPS_PER_TEST = 3


def _fold(x, op):
    rows, tq = x.shape
    x3 = x.reshape(rows // FOLD_ROWS, FOLD_ROWS, tq)
    return {"sum": x3.sum, "max": x3.max, "min": x3.min}[op](axis=0)


def _stats_init(tq):
    zeros = jnp.zeros((FOLD_ROWS, tq), F32)
    return (jnp.full((FOLD_ROWS, tq), -jnp.inf, F32), jnp.full((FOLD_ROWS, tq), jnp.inf, F32),
            zeros, zeros, zeros)


def _stats_update(stats, s):
    mx, mn, n_above, n_ge0, n_gt0 = stats
    ones_where = lambda m: _fold(jnp.where(m, 1.0, 0.0), "sum")
    above = s > -jnp.inf
    return (jnp.maximum(mx, _fold(s, "max")),
            jnp.minimum(mn, _fold(jnp.where(above, s, jnp.inf), "min")),
            n_above + ones_where(above), n_ge0 + ones_where(s >= 0.0), n_gt0 + ones_where(s > 0.0))


def _select_to_bias(sc_ref, tie_ref, qpos, nk, kc, stats):
    tq = sc_ref.shape[1]
    topk = float(TOPK)

    slab = 4 * FOLD_ROWS

    def for_slabs(fn, init, slab=slab):
        def body(c, carry):
            for j in range(kc // slab):
                start = pl.multiple_of(c * kc + j * slab, slab)
                idx = start + lax.broadcasted_iota(jnp.int32, (slab, tq), 0)
                carry = fn(pl.ds(start, slab), idx, carry)
            return carry
        return lax.fori_loop(0, nk, body, init)

    def ones_where(m):
        return _fold(jnp.where(m, 1.0, 0.0), "sum")

    def count(pred):
        acc = for_slabs(lambda rows, idx, acc: acc + ones_where(pred(rows)), jnp.zeros((FOLD_ROWS, tq), F32))
        return acc.sum(axis=0, keepdims=True)

    def count_ge(t):
        return count(lambda rows: sc_ref[rows, :] >= t)

    mx, mn, n_above, n_ge0, n_gt0 = stats
    mx = mx.max(axis=0, keepdims=True)
    mn = mn.min(axis=0, keepdims=True)
    n_above = n_above.sum(axis=0, keepdims=True)
    n_ge0 = n_ge0.sum(axis=0, keepdims=True)
    n_gt0 = n_gt0.sum(axis=0, keepdims=True)
    n_max = count_ge(mx)

    few = n_above < topk
    flat = n_max >= topk
    zero = (n_gt0 < topk) & (n_ge0 >= topk)
    pos = n_gt0 >= topk
    lo = jnp.where(few, -jnp.inf, jnp.where(flat, mx, jnp.where(zero | pos, 0.0, mn)))
    hi = jnp.where(few, mn, jnp.where(flat, jnp.inf, jnp.where(pos, mx, 0.0)))
    n_lo = jnp.where(few, 2.0 * topk, jnp.where(flat, n_max, jnp.where(zero | pos, n_ge0, n_above)))
    n_hi = jnp.where(few, n_above, jnp.where(flat, 0.0, jnp.where(zero, n_gt0, jnp.where(pos, n_max, n_ge0))))
    done = jnp.where(few | flat | zero | (n_lo == topk), 1.0, 0.0)

    def not_finished(d):
        return (jnp.min(d) < 0.5).astype(jnp.int32)

    def bisect_step(state):
        lo, hi, n_lo, n_hi, done = state
        mid = jnp.clip(0.5 * lo + 0.5 * hi, -F32_MAX, F32_MAX)
        stuck = (mid <= lo) | (mid >= hi)
        n_mid = count_ge(mid)
        move = (done < 0.5) & jnp.logical_not(stuck)
        up = move & (n_mid >= topk)
        down = move & (n_mid < topk)
        lo = jnp.where(up, mid, lo)
        n_lo = jnp.where(up, n_mid, n_lo)
        hi = jnp.where(down, mid, hi)
        n_hi = jnp.where(down, n_mid, n_hi)
        done = jnp.where(stuck | (n_lo == topk), 1.0, done)
        return lo, hi, n_lo, n_hi, done

    def bisect(carry):
        state = lax.fori_loop(0, carry[6], lambda _, st: bisect_step(st), carry[:5])
        return (*state, not_finished(state[4]), jnp.int32(BISECT_STEPS_PER_TEST))

    lo, hi, n_lo, n_hi, done, _, _ = lax.while_loop(
        lambda carry: carry[5] > 0, bisect,
        (lo, hi, n_lo, n_hi, done, not_finished(done), jnp.int32(BISECT_STEPS_FIRST)))

    thr = lo
    exact = n_lo == topk
    need = topk - n_hi
    all_keys = jnp.full((1, tq), 2 ** 30, jnp.int32)

    def tie_break():
        def mark(rows, idx, _):
            tie_ref[rows, :] = jnp.where(sc_ref[rows, :] == thr, idx, 2 ** 30)
            return 0
        for_slabs(mark, 0)

        n_bits = (sc_ref.shape[0] - 1).bit_length()

        def step(i, j):
            cand = j + lax.shift_right_logical(jnp.int32(1 << (n_bits - 1)), i)
            n_before = count(lambda rows: tie_ref[rows, :] < cand)
            return jnp.where(n_before < need, cand, j)
        return lax.fori_loop(0, n_bits, step, jnp.zeros((1, tq), jnp.int32))

    any_tie = jnp.min(jnp.where(exact, 1.0, 0.0)) < 0.5

    def write(keep_fn):
        def body(rows, idx, _):
            sc_ref[rows, :] = jnp.where(keep_fn(sc_ref[rows, :], idx), 0.0, MASK_BIAS)
            return 0
        return for_slabs(body, 0, slab=kc)

    def write_with_ties():
        last_tied = jnp.minimum(jnp.where(exact, all_keys, tie_break()), qpos)
        return write(lambda s, idx: (s > thr) | ((s == thr) & (idx <= last_tied)))

    lax.cond(any_tie, write_with_ties, lambda: write(lambda s, idx: s >= thr))


def _prompt_attn_kernel(qt_ref, qit_ref, wit_ref, kg_ref, vt_ref, kib_ref, o_ref,
                        sc_ref, tie_ref, acc_ref, m_ref, mx_ref, s_ref):
    i = pl.program_id(1)
    tq = qt_ref.shape[1]
    qpos = i * tq + lax.broadcasted_iota(jnp.int32, (1, tq), 1)
    n_keys = (i + 1) * tq

    n_score = (n_keys + SCORE_CHUNK - 1) // SCORE_CHUNK

    def score_chunk(c, stats):
        for j in range(SCORE_CHUNK // SCORE_SLAB):
            start = pl.multiple_of(c * SCORE_CHUNK + j * SCORE_SLAB, SCORE_SLAB)
            kc = kib_ref[pl.ds(start, SCORE_SLAB), :]
            acc = jnp.zeros((SCORE_SLAB, tq), F32)
            for h in range(IDX_HEADS):
                d = _dot(kc, qit_ref[h * IDX_DIM:(h + 1) * IDX_DIM, :])
                acc = acc + jnp.maximum(d, 0.0) * wit_ref[h:h + 1, :]
            idx = start + lax.broadcasted_iota(jnp.int32, (SCORE_SLAB, tq), 0)
            s = jnp.where(idx <= qpos, acc, -jnp.inf)
            sc_ref[pl.ds(start, SCORE_SLAB), :] = s
            stats = _stats_update(stats, s)
        return stats

    stats = lax.fori_loop(0, n_score, score_chunk, _stats_init(tq))

    _select_to_bias(sc_ref, tie_ref, qpos, n_score, SCORE_CHUNK, stats)

    m_ref[...] = jnp.full(m_ref.shape, MASK_BIAS, F32)
    acc_ref[...] = jnp.zeros(acc_ref.shape, F32)

    def attn_chunk(c, _):
        start = pl.multiple_of(c * ATTN_CHUNK, ATTN_CHUNK)
        for h in range(N_HEADS):
            s = _dot(kg_ref[h // HEADS_PER_KV, pl.ds(start, ATTN_CHUNK), :],
                     qt_ref[h * HEAD_DIM:(h + 1) * HEAD_DIM, :])
            s = s + sc_ref[pl.ds(start, ATTN_CHUNK), :]
            s_ref[h] = s
            mx_ref[h] = s.max(axis=0, keepdims=True)
        for h in range(N_HEADS):
            g = h // HEADS_PER_KV
            m_old = m_ref[h]
            m_new = jnp.maximum(m_old, mx_ref[h])
            alpha = jnp.exp2(m_old - m_new)
            p = jnp.exp2(s_ref[h] - m_new).astype(BF16)
            vt = vt_ref[c, g * V_ROWS:(g + 1) * V_ROWS, :]
            acc_ref[h] = alpha * acc_ref[h] + _dot(vt, p)
            m_ref[h] = m_new
        return 0

    lax.fori_loop(0, (i + 1) * (tq // ATTN_CHUNK), attn_chunk, 0)

    for pair in range(N_HEADS // 2):
        two = []
        for h in (2 * pair, 2 * pair + 1):
            a = acc_ref[h]
            two.append(a[0:HEAD_DIM, :] / a[HEAD_DIM:HEAD_DIM + 1, :])
        col = 2 * pair * HEAD_DIM
        o_ref[:, col:col + 2 * HEAD_DIM] = jnp.concatenate(two, axis=0).T.astype(BF16)


def _prompt_attention(qt, qit, wit, kg, vt, kib):
    nb, _, t = qt.shape
    tq = Q_TILE
    col = lambda height: pl.BlockSpec((None, height, tq), lambda b, i: (b, 0, i))
    return pl.pallas_call(
        _prompt_attn_kernel,
        grid=(nb, t // tq),
        in_specs=[
            col(Q_COLS), col(IQ_COLS), col(IDX_HEADS),
            pl.BlockSpec((None, N_KV_HEADS, t, HEAD_DIM), lambda b, i: (b, 0, 0, 0)),
            pl.BlockSpec((None, t // ATTN_CHUNK, N_KV_HEADS * V_ROWS, ATTN_CHUNK), lambda b, i: (b, 0, 0, 0)),
            pl.BlockSpec((None, t, IDX_DIM), lambda b, i: (b, 0, 0)),
        ],
        out_specs=pl.BlockSpec((None, tq, Q_COLS), lambda b, i: (b, i, 0)),
        out_shape=jax.ShapeDtypeStruct((nb, t, Q_COLS), BF16),
        scratch_shapes=[
            pltpu.VMEM((t, tq), F32),
            pltpu.VMEM((t, tq), jnp.int32),
            pltpu.VMEM((N_HEADS, V_ROWS, tq), F32),
            pltpu.VMEM((N_HEADS, 1, tq), F32),
            pltpu.VMEM((N_HEADS, 1, tq), F32),
            pltpu.VMEM((N_HEADS, ATTN_CHUNK, tq), F32),
        ],
        compiler_params=_params("arbitrary", "arbitrary"),
        name="prompt_attention",
    )(qt, qit, wit, kg, vt, kib)


def _sample_score_kernel(pt_ref, qi_ref, w_ref, *rest, n_pages):
    del pt_ref
    n_seq = qi_ref.shape[0]
    pages, (new_ref, o_ref, kall_ref) = rest[:n_seq * n_pages], rest[n_seq * n_pages:]
    for q in range(n_seq):
        for j in range(n_pages):
            kall_ref[q, :, j * PAGE_SIZE:(j + 1) * PAGE_SIZE] = pages[q * n_pages + j][...].astype(BF16)
        kall_ref[q, :, n_pages * PAGE_SIZE:(n_pages + 1) * PAGE_SIZE] = new_ref[q].astype(BF16)
    for q in range(n_seq):
        d = _dot(qi_ref[q], kall_ref[q])
        r = jnp.maximum(d, 0.0) * w_ref[q]
        o_ref[q] = r.reshape(IDX_HEADS, SUBLANES, r.shape[1]).sum(axis=0)


def _sample_scores(page_table, qi8, w8, cache_ki_t, ki_new_t):
    nb, n_pages = page_table.shape
    n_keys = (n_pages + 1) * PAGE_SIZE
    rows = IDX_HEADS * SUBLANES
    ns = SCORE_SEQS
    per_step = lambda r, w: pl.BlockSpec((ns, r, w), lambda b, pt: (b, 0, 0))
    grid_spec = pltpu.PrefetchScalarGridSpec(
        num_scalar_prefetch=1,
        grid=(nb // ns,),
        in_specs=[per_step(rows, IDX_DIM), per_step(rows, 1)]
        + [pl.BlockSpec((None, IDX_DIM, PAGE_SIZE), lambda b, pt, q=q, j=j: (pt[b * ns + q, j], 0, 0))
           for q in range(ns) for j in range(n_pages)]
        + [per_step(IDX_DIM, PAGE_SIZE)],
        out_specs=per_step(SUBLANES, n_keys),
        scratch_shapes=[pltpu.VMEM((ns, IDX_DIM, n_keys), BF16)],
    )
    return pl.pallas_call(
        functools.partial(_sample_score_kernel, n_pages=n_pages),
        grid_spec=grid_spec,
        out_shape=jax.ShapeDtypeStruct((nb, SUBLANES, n_keys), F32),
        compiler_params=_params("arbitrary"),
        name="sample_indexer_scores",
    )(page_table, qi8, w8, *([cache_ki_t] * (ns * n_pages)), ki_new_t)


def _sample_select_kernel(s_ref, o_ref, sc_ref, tie_ref, *, past, period):
    tq = s_ref.shape[1]
    n_keys = s_ref.shape[0]
    lane = lax.broadcasted_iota(jnp.int32, (1, tq), 1)
    qpos = past + (lane & (period - 1))
    nk = n_keys // PAGE_SIZE
    stats = _stats_init(tq)
    for c in range(nk):
        idx = c * PAGE_SIZE + lax.broadcasted_iota(jnp.int32, (PAGE_SIZE, tq), 0)
        s = jnp.where(idx <= qpos, s_ref[c * PAGE_SIZE:(c + 1) * PAGE_SIZE, :], -jnp.inf)
        sc_ref[c * PAGE_SIZE:(c + 1) * PAGE_SIZE, :] = s
        stats = _stats_update(stats, s)
    _select_to_bias(sc_ref, tie_ref, qpos, nk, PAGE_SIZE, stats)
    o_ref[...] = sc_ref[...]


def _sample_select(scores_t, past, period):
    n_keys, nq = scores_t.shape
    tq = Q_TILE
    blk = pl.BlockSpec((n_keys, tq), lambda i: (0, i))
    return pl.pallas_call(
        functools.partial(_sample_select_kernel, past=past, period=period),
        grid=(nq // tq,),
        in_specs=[blk],
        out_specs=blk,
        out_shape=jax.ShapeDtypeStruct((n_keys, nq), F32),
        scratch_shapes=[pltpu.VMEM((n_keys, tq), F32), pltpu.VMEM((n_keys, tq), jnp.int32)],
        compiler_params=_params("arbitrary"),
        name="sample_select",
    )(scores_t)


def _sample_attn_kernel(pt_ref, q_ref, bias_ref, *rest, n_pages):
    del pt_ref
    n_seq = q_ref.shape[0]
    kpages, vpages = rest[:n_seq * n_pages], rest[n_seq * n_pages:2 * n_seq * n_pages]
    knew_ref, vnew_ref, o_ref, kall_ref, vall_ref, s_ref, p_ref, l_ref = rest[2 * n_seq * n_pages:]
    n_keys = bias_ref.shape[2]
    fill = jnp.zeros((PAGE_SIZE - SUBLANES, LANES), F32)
    for q in range(n_seq):
        for j in range(n_pages):
            kall_ref[q, :, :, j * PAGE_SIZE:(j + 1) * PAGE_SIZE] = kpages[q * n_pages + j][...].astype(BF16)
            vall_ref[q, :, :, j * PAGE_SIZE:(j + 1) * PAGE_SIZE] = vpages[q * n_pages + j][...].astype(BF16)
        for g in range(N_KV_HEADS):
            for new_ref, all_ref in ((knew_ref, kall_ref), (vnew_ref, vall_ref)):
                page = jnp.concatenate([new_ref[q, g], fill], axis=0).T
                all_ref[q, g, :, n_pages * PAGE_SIZE:(n_pages + 1) * PAGE_SIZE] = (
                    page[0:HEAD_DIM, :].astype(BF16))
    for q in range(n_seq):
        bias = jnp.broadcast_to(bias_ref[q][None], (HEADS_PER_KV, SUBLANES, n_keys)).reshape(
            HEADS_PER_KV * SUBLANES, n_keys)
        for g in range(N_KV_HEADS):
            s_ref[q, g] = _dot(q_ref[q, g], kall_ref[q, g]) + bias
    for q in range(n_seq):
        for g in range(N_KV_HEADS):
            s = s_ref[q, g]
            m = s.max(axis=1, keepdims=True)
            p = jnp.exp(s - m)
            l_ref[q, g] = p.sum(axis=1, keepdims=True)
            p_ref[q, g] = p.astype(BF16)
    for q in range(n_seq):
        for g in range(N_KV_HEADS):
            o_ref[q, g] = _dot_nt(p_ref[q, g], vall_ref[q, g]) / l_ref[q, g]


def _sample_attention(page_table, q8, bias8, cache_k_t, cache_v_t, k_new_t, v_new_t):
    nb, n_pages = page_table.shape
    n_keys = (n_pages + 1) * PAGE_SIZE
    rows = HEADS_PER_KV * SUBLANES
    ns = ATTN_SEQS
    page = [pl.BlockSpec((None, N_KV_HEADS, HEAD_DIM, PAGE_SIZE),
                         lambda b, pt, q=q, j=j: (pt[b * ns + q, j], 0, 0, 0))
            for q in range(ns) for j in range(n_pages)]
    new = pl.BlockSpec((ns, N_KV_HEADS, SUBLANES, LANES), lambda b, pt: (b, 0, 0, 0))
    qo = pl.BlockSpec((ns, N_KV_HEADS, rows, HEAD_DIM), lambda b, pt: (b, 0, 0, 0))
    grid_spec = pltpu.PrefetchScalarGridSpec(
        num_scalar_prefetch=1,
        grid=(nb // ns,),
        in_specs=[qo, pl.BlockSpec((ns, SUBLANES, n_keys), lambda b, pt: (b, 0, 0))]
        + page + page + [new, new],
        out_specs=qo,
        scratch_shapes=[pltpu.VMEM((ns, N_KV_HEADS, HEAD_DIM, n_keys), BF16),
                        pltpu.VMEM((ns, N_KV_HEADS, HEAD_DIM, n_keys), BF16),
                        pltpu.VMEM((ns, N_KV_HEADS, rows, n_keys), F32),
                        pltpu.VMEM((ns, N_KV_HEADS, rows, n_keys), BF16),
                        pltpu.VMEM((ns, N_KV_HEADS, rows, 1), F32)],
    )
    pages = ns * n_pages
    return pl.pallas_call(
        functools.partial(_sample_attn_kernel, n_pages=n_pages),
        grid_spec=grid_spec,
        out_shape=jax.ShapeDtypeStruct((nb, N_KV_HEADS, rows, HEAD_DIM), F32),
        compiler_params=_params("arbitrary"),
        name="sample_attention",
    )(page_table, q8, bias8, *([cache_k_t] * pages), *([cache_v_t] * pages), k_new_t, v_new_t)


def _ffn(h, wgu_ref, wd_ref):
    acc = jnp.zeros((h.shape[0], D_MODEL), F32)
    for c in range(D_FF // FFN_CHUNK):
        gate = _dot(h, wgu_ref[:, c * FFN_CHUNK:(c + 1) * FFN_CHUNK])
        up = _dot(h, wgu_ref[:, D_FF + c * FFN_CHUNK:D_FF + (c + 1) * FFN_CHUNK])
        a = _silu(gate) * up
        acc = acc + _dot(a.astype(BF16), wd_ref[c * FFN_CHUNK:(c + 1) * FFN_CHUNK, :])
    return acc


def _row_specs(nb, rows, per_row_mod):
    tm = min(ROW_TILE, rows)
    row = pl.BlockSpec((None, tm, D_MODEL), lambda b, j: (b, j, 0))
    if per_row_mod:
        mod = row
    else:
        mod = pl.BlockSpec((None, 1, D_MODEL), lambda b, j: (b, 0, 0))
    return tm, row, mod


def _attn_out_ffn_kernel(x_ref, o_ref, g1_ref, sc2_ref, sh2_ref, g2_ref, n2_ref,
                         wo_ref, wgu_ref, wd_ref, y_ref):
    y1 = x_ref[...] + g1_ref[...] * _dot(o_ref[...], wo_ref[...])
    h2 = _norm_mod(y1, n2_ref[...], sc2_ref[...], sh2_ref[...]).astype(BF16)
    y_ref[...] = y1 + g2_ref[...] * _ffn(h2, wgu_ref, wd_ref)


def _layer_spec(stacked, layer):
    _, rows, cols = stacked.shape
    return pl.BlockSpec((None, rows, cols), lambda *_: (layer, 0, 0), pipeline_mode=pl.Buffered(1))


def _attn_out_ffn(x, o, g1, sc2, sh2, g2, n2, wo, wgu, wd, layer):
    nb, rows, _ = x.shape
    tm, row, mod = _row_specs(nb, rows, g1.shape[1] == rows)
    return pl.pallas_call(
        _attn_out_ffn_kernel,
        grid=(nb, rows // tm),
        in_specs=[row, row, mod, mod, mod, mod, _const_spec((1, D_MODEL)),
                  _const_spec(wo.shape), _layer_spec(wgu, layer), _layer_spec(wd, layer)],
        out_specs=row,
        out_shape=jax.ShapeDtypeStruct(x.shape, F32),
        compiler_params=_params("arbitrary", "arbitrary"),
        name="attn_out_ffn",
    )(x, o, g1, sc2, sh2, g2, n2, wo, wgu, wd)


def _conv_layer_kernel(x_ref, sc1_ref, sh1_ref, g1_ref, sc2_ref, sh2_ref, g2_ref, p1_ref, p2_ref,
                       n1_ref, n2_ref, nf_ref, win_ref, ck_ref, wout_ref, wgu_ref, wd_ref,
                       y_ref, tail_ref, ubuf_ref, z_ref, *, seg):
    j = pl.program_id(1)
    tm = x_ref.shape[0]
    x = x_ref[...]
    h = _norm_mod(x, n1_ref[...], sc1_ref[...], sh1_ref[...]).astype(BF16)

    @pl.when(j == 0)
    def _():
        ubuf_ref[0:SUBLANES, :] = jnp.zeros((SUBLANES, D_MODEL), F32)

    @pl.when(j > 0)
    def _():
        ubuf_ref[0:SUBLANES, :] = ubuf_ref[tm:tm + SUBLANES, :]

    t = (j * tm + lax.broadcasted_iota(jnp.int32, (tm, 1), 0)) & (seg - 1)
    ck = ck_ref[...]
    cw = FFN_CHUNK
    for c in range(D_MODEL // cw):
        cols = slice(c * cw, (c + 1) * cw)
        bg = _dot(h, win_ref[:, c * cw:(c + 1) * cw])
        cg = _dot(h, win_ref[:, D_MODEL + c * cw:D_MODEL + (c + 1) * cw])
        hv = _dot(h, win_ref[:, 2 * D_MODEL + c * cw:2 * D_MODEL + (c + 1) * cw])
        u = cg * hv
        ubuf_ref[SUBLANES:SUBLANES + tm, cols] = u
        um1 = jnp.where(t >= 1, ubuf_ref[SUBLANES - 1:SUBLANES - 1 + tm, cols], p1_ref[:, cols])
        um2 = jnp.where(t >= 2, ubuf_ref[SUBLANES - 2:SUBLANES - 2 + tm, cols], p2_ref[:, cols])
        conv = ck[0:1, cols] * um2 + ck[1:2, cols] * um1 + ck[2:3, cols] * u
        z_ref[:, cols] = (bg * conv).astype(BF16)
    r = tail_ref.shape[0]
    tail_ref[...] = ubuf_ref[SUBLANES + tm - r:SUBLANES + tm, :]

    y1 = x + g1_ref[...] * _dot(z_ref[...], wout_ref[...])
    h2 = _norm_mod(y1, n2_ref[...], sc2_ref[...], sh2_ref[...]).astype(BF16)
    y2 = y1 + g2_ref[...] * _ffn(h2, wgu_ref, wd_ref)
    ms = jnp.mean(y2 * y2, axis=-1, keepdims=True)
    y_ref[...] = (y2 * lax.rsqrt(ms + EPS)) * nf_ref[...]


def _conv_layer(x, mods, p1, p2, n1, n2, nf, win, ck, wout, wgu, wd, *, layer, seg, full_tail):
    nb, rows, _ = x.shape
    per_row = mods[0].shape[1] == rows
    tm, row, mod = _row_specs(nb, rows, per_row)
    if per_row:
        prefix = row
    else:
        prefix = pl.BlockSpec((None, 1, D_MODEL), lambda b, j: (0, 0, 0))
    if full_tail:
        tail_spec, tail_rows = row, rows
    else:
        tail_spec, tail_rows = pl.BlockSpec((None, SUBLANES, D_MODEL), lambda b, j: (b, 0, 0)), SUBLANES
    vec = _const_spec((1, D_MODEL))
    return pl.pallas_call(
        functools.partial(_conv_layer_kernel, seg=seg),
        grid=(nb, rows // tm),
        in_specs=[row] + [mod] * 6 + [prefix, prefix, vec, vec, vec,
                                      _const_spec(win.shape), _const_spec(ck.shape), _const_spec(wout.shape),
                                      _layer_spec(wgu, layer), _layer_spec(wd, layer)],
        out_specs=(row, tail_spec),
        out_shape=(jax.ShapeDtypeStruct(x.shape, F32),
                   jax.ShapeDtypeStruct((nb, tail_rows, D_MODEL), F32)),
        scratch_shapes=[pltpu.VMEM((tm + SUBLANES, D_MODEL), F32), pltpu.VMEM((tm, D_MODEL), BF16)],
        compiler_params=_params("arbitrary", "arbitrary"),
        name="conv_layer",
    )(x, *mods, p1, p2, n1, n2, nf, win, ck, wout, wgu, wd)


def _rope_tables(pos):
    half = ROT_DIM // 2
    inv = ROPE_THETA ** (-jnp.arange(0, ROT_DIM, 2, dtype=F32) / ROT_DIM)
    ang = pos[:, None] * inv[None, :]
    cos, sin = jnp.cos(ang), jnp.sin(ang)
    n = pos.shape[0]
    rest = HEAD_DIM - ROT_DIM
    c = jnp.concatenate([cos, cos, jnp.ones((n, rest), F32)], axis=1)
    s1 = jnp.concatenate([-sin, jnp.zeros((n, half + rest), F32)], axis=1)
    s2 = jnp.concatenate([jnp.zeros((n, half), F32), sin, jnp.zeros((n, rest), F32)], axis=1)
    return tuple(jnp.tile(a, (1, LANES // HEAD_DIM)) for a in (c, s1, s2))


def kernel(x_prompt, x_sample, c_prompt, c_sample, cache_k, cache_v, cache_kidx, state_conv, page_table,
           ada_w, ada_b, norm1_g, norm2_g, final_g, attn_w_in, attn_w_out, conv_w_in, conv_k, conv_w_out,
           ffn_w_gu, ffn_w_down):
    nb, t, _ = x_prompt.shape
    ns, tn, _ = x_sample.shape
    n_pages = page_table.shape[1]
    past = n_pages * PAGE_SIZE
    n_phys = cache_k.shape[1]
    rows_s = ns * tn

    pad = (-(nb + ns)) % SUBLANES
    c_all = jnp.concatenate([c_prompt, c_sample, jnp.zeros((pad, D_MODEL), F32)], axis=0)
    mod = _modulation(c_all, ada_w, ada_b)

    def mods(layer):
        cols = [mod[layer, :, i * D_MODEL:(i + 1) * D_MODEL] for i in range(6)]
        prompt = [m[:nb, None, :] for m in cols]
        sample = [jnp.repeat(m[nb:nb + ns], tn, axis=0)[None] for m in cols]
        return prompt, sample

    vec = lambda a: a.reshape(1, D_MODEL)
    w_in = jnp.pad(attn_w_in[0], ((0, 0), (0, ATTN_IN_PAD - ATTN_IN))).astype(BF16)
    w_o = attn_w_out[0].astype(BF16)
    wgu = ffn_w_gu.astype(BF16)
    wd = ffn_w_down.astype(BF16)

    (sh1p, sc1p, g1p, sh2p, sc2p, g2p), (sh1s, sc1s, g1s, sh2s, sc2s, g2s) = mods(0)
    rope_p = _rope_tables(jnp.arange(t, dtype=F32))
    k_p, v_p, ki_p, qt, qit, kg, vt, kib, wit = _proj_prompt(
        x_prompt, sc1p, sh1p, vec(norm1_g[0]), w_in, rope_p)
    o_p = _prompt_attention(qt, qit, wit, kg, vt, kib)
    y_p = _attn_out_ffn(x_prompt, o_p, g1p, sc2p, sh2p, g2p, vec(norm2_g[0]), w_o, wgu, wd, 0)

    pos_s = jnp.tile(past + jnp.arange(tn, dtype=F32), ns)
    rope_s = _rope_tables(pos_s)
    xs = x_sample.reshape(rows_s, D_MODEL)
    proj = _proj_sample(xs, sc1s[0], sh1s[0], vec(norm1_g[0]), w_in, rope_s)
    o = 0
    q_s = proj[:, o:o + Q_COLS].reshape(ns, tn, N_KV_HEADS, HEADS_PER_KV, HEAD_DIM); o += Q_COLS
    k_s = proj[:, o:o + KV_COLS].reshape(ns, tn, KV_COLS); o += KV_COLS
    v_s = proj[:, o:o + KV_COLS].reshape(ns, tn, KV_COLS); o += KV_COLS
    qi_s = proj[:, o:o + IQ_COLS].reshape(ns, tn, IDX_HEADS, IDX_DIM); o += IQ_COLS
    ki_s = proj[:, o:o + IDX_DIM].reshape(ns, tn, IDX_DIM); o += IDX_DIM
    wi_s = proj[:, o:o + IDX_HEADS].reshape(ns, tn, IDX_HEADS)

    qpad = ((0, 0), (0, SUBLANES - tn))
    qi8 = jnp.pad(qi_s, qpad + ((0, 0), (0, 0))).transpose(0, 2, 1, 3)
    qi8 = qi8.reshape(ns, IDX_HEADS * SUBLANES, IDX_DIM).astype(BF16)
    w8 = jnp.pad(wi_s, qpad + ((0, 0),)).transpose(0, 2, 1).reshape(ns, IDX_HEADS * SUBLANES, 1)
    slot_pad = (0, PAGE_SIZE - tn)
    ki_new_t = jnp.pad(ki_s.transpose(0, 2, 1), ((0, 0), (0, 0), slot_pad))
    scores = _sample_scores(page_table, qi8, w8, cache_kidx[0].transpose(0, 2, 1), ki_new_t)
    n_keys = scores.shape[2]
    scores_t = scores[:, :tn, :].transpose(2, 0, 1).reshape(n_keys, rows_s)
    bias_t = _sample_select(scores_t, past, tn)
    bias8 = jnp.pad(bias_t.reshape(n_keys, ns, tn).transpose(1, 2, 0), qpad + ((0, 0),))

    q8 = jnp.pad(q_s * HEAD_DIM ** -0.5, qpad + ((0, 0), (0, 0), (0, 0))).transpose(0, 2, 3, 1, 4)
    q8 = q8.reshape(ns, N_KV_HEADS, HEADS_PER_KV * SUBLANES, HEAD_DIM).astype(BF16)
    new_t = lambda a: jnp.pad(a.reshape(ns, tn, N_KV_HEADS, HEAD_DIM).transpose(0, 2, 1, 3),
                              ((0, 0), (0, 0), (0, SUBLANES - tn), (0, LANES - HEAD_DIM)))
    o_g = _sample_attention(page_table, q8, bias8,
                            cache_k[0].transpose(0, 2, 3, 1), cache_v[0].transpose(0, 2, 3, 1),
                            new_t(k_s), new_t(v_s))
    o_g = o_g.reshape(ns, N_KV_HEADS, HEADS_PER_KV, SUBLANES, HEAD_DIM)[:, :, :, :tn]
    o_s = o_g.transpose(0, 3, 1, 2, 4).reshape(1, rows_s, Q_COLS).astype(BF16)
    y_s = _attn_out_ffn(xs[None], o_s, g1s, sc2s, sh2s, g2s, vec(norm2_g[0]), w_o, wgu, wd, 0)

    (sh1p, sc1p, g1p, sh2p, sc2p, g2p), (sh1s, sc1s, g1s, sh2s, sc2s, g2s) = mods(1)
    win = conv_w_in[0].astype(BF16)
    wout = conv_w_out[0].astype(BF16)
    zero_prefix = jnp.zeros((1, 1, D_MODEL), F32)
    out_p, tail_p = _conv_layer(
        y_p, (sc1p, sh1p, g1p, sc2p, sh2p, g2p), zero_prefix, zero_prefix,
        vec(norm1_g[1]), vec(norm2_g[1]), vec(final_g), win, conv_k[0], wout, wgu, wd,
        layer=1, seg=t, full_tail=False)
    st = state_conv[0]
    zeros_row = jnp.zeros((ns, 1, D_MODEL), F32)
    p1 = jnp.concatenate([st[:, 1:2], zeros_row, zeros_row, zeros_row], axis=1).reshape(1, rows_s, D_MODEL)
    p2 = jnp.concatenate([st[:, 0:1], st[:, 1:2], zeros_row, zeros_row], axis=1).reshape(1, rows_s, D_MODEL)
    out_s, u_s = _conv_layer(
        y_s, (sc1s, sh1s, g1s, sc2s, sh2s, g2s), p1, p2,
        vec(norm1_g[1]), vec(norm2_g[1]), vec(final_g), win, conv_k[0], wout, wgu, wd,
        layer=1, seg=tn, full_tail=True)

    keep = CONV_WIDTH - 1
    return (
        out_p,
        out_s.reshape(ns, tn, D_MODEL),
        k_p.reshape(1, nb, t, N_KV_HEADS, HEAD_DIM),
        v_p.reshape(1, nb, t, N_KV_HEADS, HEAD_DIM),
        ki_p[None],
        tail_p[None, :, SUBLANES - keep:, :],
        k_s.reshape(1, ns, tn, N_KV_HEADS, HEAD_DIM),
        v_s.reshape(1, ns, tn, N_KV_HEADS, HEAD_DIM),
        ki_s[None],
        u_s.reshape(ns, tn, D_MODEL)[None, :, tn - keep:, :],
    )
```

```python
import functools

import jax
import jax.numpy as jnp
from jax import lax
from jax.experimental import pallas as pl
from jax.experimental.pallas import tpu as pltpu

F32 = jnp.float32
BF16 = jnp.bfloat16

D_MODEL = 1024
N_HEADS = 16
HEAD_DIM = 64
N_KV_HEADS = 4
HEADS_PER_KV = N_HEADS // N_KV_HEADS
ROT_DIM = 16
ROPE_THETA = 500000.0
IDX_HEADS = 8
IDX_DIM = 64
TOPK = 256
PAGE_SIZE = 128
CONV_WIDTH = 3
D_FF = 2816
EPS = 1e-6
Q_COLS = N_HEADS * HEAD_DIM
KV_COLS = N_KV_HEADS * HEAD_DIM
IQ_COLS = IDX_HEADS * IDX_DIM
ATTN_IN = Q_COLS + 2 * KV_COLS + IQ_COLS + IDX_DIM + IDX_HEADS
ATTN_IN_PAD = 2176
LANES = 128
SUBLANES = 8
MASK_BIAS = -1e30
F32_MAX = 3.4028234663852886e38
LOG2_E = 1.4426950408889634
VMEM_LIMIT = 60 * 1024 * 1024

ROW_TILE = 512
Q_TILE = 256
SCORE_CHUNK = 512
SCORE_SLAB = 128
ATTN_CHUNK = 256
V_ROWS = HEAD_DIM + 16
FFN_CHUNK = 256
SCORE_SEQS = 4
ATTN_SEQS = 2


def _dot(a, b):
    return jnp.dot(a, b, preferred_element_type=F32)


def _dot_nt(a, b):
    return lax.dot_general(a, b, (((1,), (1,)), ((), ())), preferred_element_type=F32)


def _silu(x):
    return x / (1.0 + jnp.exp(-x))


def _norm_mod(x, g, sc, sh):
    ms = jnp.mean(x * x, axis=-1, keepdims=True)
    y = x * lax.rsqrt(ms + EPS)
    return (y * g) * (1.0 + sc) + sh


def _params(*sem, flags=None):
    return pltpu.CompilerParams(dimension_semantics=sem, vmem_limit_bytes=VMEM_LIMIT, flags=flags)


def _const_spec(shape):
    nd = len(shape)
    return pl.BlockSpec(shape, lambda *_: (0,) * nd, pipeline_mode=pl.Buffered(1))


def _mod_kernel(c_ref, w_ref, b_ref, o_ref):
    s = _silu(c_ref[...]).astype(BF16)
    o_ref[...] = _dot(s, w_ref[...].astype(BF16)) + b_ref[...]


def _modulation(c_all, ada_w, ada_b):
    depth, _, n = ada_w.shape
    rows = c_all.shape[0]
    tn = 1024
    return pl.pallas_call(
        _mod_kernel,
        grid=(depth, n // tn),
        in_specs=[
            pl.BlockSpec((rows, D_MODEL), lambda l, j: (0, 0)),
            pl.BlockSpec((None, D_MODEL, tn), lambda l, j: (l, 0, j)),
            pl.BlockSpec((None, 1, tn), lambda l, j: (l, 0, j)),
        ],
        out_specs=pl.BlockSpec((None, rows, tn), lambda l, j: (l, 0, j)),
        out_shape=jax.ShapeDtypeStruct((depth, rows, n), F32),
        compiler_params=_params("arbitrary", "arbitrary"),
        name="adaln_modulation",
    )(c_all, ada_w, ada_b.reshape(depth, 1, n))


def _rope_block(x, c, s1, s2):
    return x * c + pltpu.roll(x, LANES - ROT_DIM // 2, 1) * s1 + pltpu.roll(x, ROT_DIM // 2, 1) * s2


def _rope_cols(p, c, s1, s2):
    nblk = p.shape[1] // LANES
    return [_rope_block(p[:, j * LANES:(j + 1) * LANES], c, s1, s2) for j in range(nblk)]


def _last_block_tables(c, s1, s2):
    lane = lax.broadcasted_iota(jnp.int32, c.shape, 1)
    is_key = lane < IDX_DIM
    return jnp.where(is_key, c, 1.0), jnp.where(is_key, s1, 0.0), jnp.where(is_key, s2, 0.0)


def _rope_rows(xt, cos_t, sin_t):
    half = ROT_DIM // 2
    out = []
    for base in range(0, LANES, HEAD_DIM):
        x1, x2 = xt[base:base + half], xt[base + half:base + ROT_DIM]
        out += [x1 * cos_t - x2 * sin_t, x2 * cos_t + x1 * sin_t, xt[base + ROT_DIM:base + HEAD_DIM]]
    return jnp.concatenate(out, axis=0)


def _proj_prompt_kernel(x_ref, sc_ref, sh_ref, g_ref, w_ref, c_ref, s1_ref, s2_ref, ct_ref, st_ref,
                        k_ref, v_ref, ki_ref, qt_ref, qit_ref, kg_ref, vt_ref, kib_ref, wit_ref):
    h = _norm_mod(x_ref[...], g_ref[...], sc_ref[...], sh_ref[...]).astype(BF16)
    c, s1, s2 = c_ref[...], s1_ref[...], s2_ref[...]
    cos_t, sin_t = ct_ref[...], st_ref[...]

    pq = _dot(h, w_ref[:, 0:Q_COLS])
    for j in range(Q_COLS // LANES):
        qt = _rope_rows(pq[:, j * LANES:(j + 1) * LANES].T, cos_t, sin_t)
        qt_ref[j * LANES:(j + 1) * LANES, :] = (qt * (HEAD_DIM ** -0.5 * LOG2_E)).astype(BF16)

    pk = _dot(h, w_ref[:, Q_COLS:Q_COLS + KV_COLS])
    k = jnp.concatenate(_rope_cols(pk, c, s1, s2), axis=1)
    k_ref[...] = k
    for g in range(N_KV_HEADS):
        kg_ref[g] = k[:, g * HEAD_DIM:(g + 1) * HEAD_DIM].astype(BF16)

    pv = _dot(h, w_ref[:, Q_COLS + KV_COLS:Q_COLS + 2 * KV_COLS])
    v_ref[...] = pv
    ones = jnp.ones((V_ROWS - HEAD_DIM, ATTN_CHUNK), BF16)
    for cc in range(pv.shape[0] // ATTN_CHUNK):
        vt = pv[cc * ATTN_CHUNK:(cc + 1) * ATTN_CHUNK, :].astype(BF16).T
        for g in range(N_KV_HEADS):
            vt_ref[cc, g * V_ROWS:g * V_ROWS + HEAD_DIM, :] = vt[g * HEAD_DIM:(g + 1) * HEAD_DIM, :]
            vt_ref[cc, g * V_ROWS + HEAD_DIM:(g + 1) * V_ROWS, :] = ones

    o = Q_COLS + 2 * KV_COLS
    pqi = _dot(h, w_ref[:, o:o + IQ_COLS])
    for j in range(IQ_COLS // LANES):
        qit = _rope_rows(pqi[:, j * LANES:(j + 1) * LANES].T, cos_t, sin_t)
        qit_ref[j * LANES:(j + 1) * LANES, :] = qit.astype(BF16)

    pl_ = _dot(h, w_ref[:, o + IQ_COLS:ATTN_IN_PAD])
    last = _rope_block(pl_, *_last_block_tables(c, s1, s2))
    ki = last[:, 0:IDX_DIM]
    ki_ref[...] = ki
    kib_ref[...] = ki.astype(BF16)
    wit_ref[...] = last.T[IDX_DIM:IDX_DIM + IDX_HEADS, :]


def _proj_prompt(x, sc, sh, g, w, rope, rope_t):
    nb, t, _ = x.shape
    tm = ROW_TILE
    row = lambda width: pl.BlockSpec((None, tm, width), lambda b, j: (b, j, 0))
    mod = pl.BlockSpec((None, 1, D_MODEL), lambda b, j: (b, 0, 0))
    tab = pl.BlockSpec((tm, LANES), lambda b, j: (j, 0))
    tab_t = pl.BlockSpec((ROT_DIM // 2, tm), lambda b, j: (0, j))
    col = lambda height: pl.BlockSpec((None, height, tm), lambda b, j: (b, 0, j))
    out_shape = (
        jax.ShapeDtypeStruct((nb, t, KV_COLS), F32),
        jax.ShapeDtypeStruct((nb, t, KV_COLS), F32),
        jax.ShapeDtypeStruct((nb, t, IDX_DIM), F32),
        jax.ShapeDtypeStruct((nb, Q_COLS, t), BF16),
        jax.ShapeDtypeStruct((nb, IQ_COLS, t), BF16),
        jax.ShapeDtypeStruct((nb, N_KV_HEADS, t, HEAD_DIM), BF16),
        jax.ShapeDtypeStruct((nb, t // ATTN_CHUNK, N_KV_HEADS * V_ROWS, ATTN_CHUNK), BF16),
        jax.ShapeDtypeStruct((nb, t, IDX_DIM), BF16),
        jax.ShapeDtypeStruct((nb, IDX_HEADS, t), F32),
    )
    out_specs = (
        row(KV_COLS), row(KV_COLS), row(IDX_DIM), col(Q_COLS), col(IQ_COLS),
        pl.BlockSpec((None, N_KV_HEADS, tm, HEAD_DIM), lambda b, j: (b, 0, j, 0)),
        pl.BlockSpec((None, tm // ATTN_CHUNK, N_KV_HEADS * V_ROWS, ATTN_CHUNK), lambda b, j: (b, j, 0, 0)),
        row(IDX_DIM), col(IDX_HEADS),
    )
    return pl.pallas_call(
        _proj_prompt_kernel,
        grid=(nb, t // tm),
        in_specs=[row(D_MODEL), mod, mod, _const_spec((1, D_MODEL)),
                  _const_spec((D_MODEL, ATTN_IN_PAD)), tab, tab, tab, tab_t, tab_t],
        out_specs=out_specs,
        out_shape=out_shape,
        compiler_params=_params("arbitrary", "arbitrary"),
        name="attn_proj_prompt",
    )(x, sc, sh, g, w, *rope, *rope_t)


def _proj_sample_kernel(x_ref, sc_ref, sh_ref, g_ref, w_ref, c_ref, s1_ref, s2_ref, p_ref):
    h = _norm_mod(x_ref[...], g_ref[...], sc_ref[...], sh_ref[...]).astype(BF16)
    c, s1, s2 = c_ref[...], s1_ref[...], s2_ref[...]
    p = _dot(h, w_ref[...])
    v_lo = (Q_COLS + KV_COLS) // LANES
    v_hi = (Q_COLS + 2 * KV_COLS) // LANES
    nblk = ATTN_IN_PAD // LANES
    for j in range(nblk):
        blk = p[:, j * LANES:(j + 1) * LANES]
        if j == nblk - 1:
            blk = _rope_block(blk, *_last_block_tables(c, s1, s2))
        elif not (v_lo <= j < v_hi):
            blk = _rope_block(blk, c, s1, s2)
        p_ref[:, j * LANES:(j + 1) * LANES] = blk


def _proj_sample(x, sc, sh, g, w, rope):
    rows = x.shape[0]
    full = lambda width: pl.BlockSpec((rows, width), lambda i: (0, 0))
    return pl.pallas_call(
        _proj_sample_kernel,
        grid=(1,),
        in_specs=[full(D_MODEL), full(D_MODEL), full(D_MODEL),
                  pl.BlockSpec((1, D_MODEL), lambda i: (0, 0)),
                  pl.BlockSpec((D_MODEL, ATTN_IN_PAD), lambda i: (0, 0)),
                  full(LANES), full(LANES), full(LANES)],
        out_specs=full(ATTN_IN_PAD),
        out_shape=jax.ShapeDtypeStruct((rows, ATTN_IN_PAD), F32),
        compiler_params=_params("arbitrary"),
        name="attn_proj_sample",
    )(x, sc, sh, g, w, *rope)


FOLD_ROWS = 32
BISECT_STEPS_FIRST = 14
BISECT_STEPS_PER_TEST = 3


def _fold(x, op):
    rows, tq = x.shape
    x3 = x.reshape(rows // FOLD_ROWS, FOLD_ROWS, tq)
    return {"sum": x3.sum, "max": x3.max, "min": x3.min}[op](axis=0)


def _stats_init(tq):
    zeros = jnp.zeros((FOLD_ROWS, tq), F32)
    return (jnp.full((FOLD_ROWS, tq), -jnp.inf, F32), jnp.full((FOLD_ROWS, tq), jnp.inf, F32),
            zeros, zeros, zeros)


def _stats_update(stats, s):
    mx, mn, n_above, n_ge0, n_gt0 = stats
    ones_where = lambda m: _fold(jnp.where(m, 1.0, 0.0), "sum")
    above = s > -jnp.inf
    return (jnp.maximum(mx, _fold(s, "max")),
            jnp.minimum(mn, _fold(jnp.where(above, s, jnp.inf), "min")),
            n_above + ones_where(above), n_ge0 + ones_where(s >= 0.0), n_gt0 + ones_where(s > 0.0))


def _select_to_bias(sc_ref, tie_ref, qpos, nk, kc, stats):
    tq = sc_ref.shape[1]
    topk = float(TOPK)

    slab = 4 * FOLD_ROWS

    def for_slabs(fn, init, slab=slab):
        def body(c, carry):
            for j in range(kc // slab):
                start = pl.multiple_of(c * kc + j * slab, slab)
                idx = start + lax.broadcasted_iota(jnp.int32, (slab, tq), 0)
                carry = fn(pl.ds(start, slab), idx, carry)
            return carry
        return lax.fori_loop(0, nk, body, init)

    def ones_where(m):
        return _fold(jnp.where(m, 1.0, 0.0), "sum")

    def count(pred):
        acc = for_slabs(lambda rows, idx, acc: acc + ones_where(pred(rows)), jnp.zeros((FOLD_ROWS, tq), F32))
        return acc.sum(axis=0, keepdims=True)

    def count_ge(t):
        return count(lambda rows: sc_ref[rows, :] >= t)

    mx, mn, n_above, n_ge0, n_gt0 = stats
    mx = mx.max(axis=0, keepdims=True)
    mn = mn.min(axis=0, keepdims=True)
    n_above = n_above.sum(axis=0, keepdims=True)
    n_ge0 = n_ge0.sum(axis=0, keepdims=True)
    n_gt0 = n_gt0.sum(axis=0, keepdims=True)
    n_max = count_ge(mx)

    few = n_above < topk
    flat = n_max >= topk
    zero = (n_gt0 < topk) & (n_ge0 >= topk)
    pos = n_gt0 >= topk
    lo = jnp.where(few, -jnp.inf, jnp.where(flat, mx, jnp.where(zero | pos, 0.0, mn)))
    hi = jnp.where(few, mn, jnp.where(flat, jnp.inf, jnp.where(pos, mx, 0.0)))
    n_lo = jnp.where(few, 2.0 * topk, jnp.where(flat, n_max, jnp.where(zero | pos, n_ge0, n_above)))
    n_hi = jnp.where(few, n_above, jnp.where(flat, 0.0, jnp.where(zero, n_gt0, jnp.where(pos, n_max, n_ge0))))
    done = jnp.where(few | flat | zero | (n_lo == topk), 1.0, 0.0)

    def not_finished(d):
        return (jnp.min(d) < 0.5).astype(jnp.int32)

    def bisect_step(state):
        lo, hi, n_lo, n_hi, done = state
        mid = jnp.clip(0.5 * lo + 0.5 * hi, -F32_MAX, F32_MAX)
        stuck = (mid <= lo) | (mid >= hi)
        n_mid = count_ge(mid)
        move = (done < 0.5) & jnp.logical_not(stuck)
        up = move & (n_mid >= topk)
        down = move & (n_mid < topk)
        lo = jnp.where(up, mid, lo)
        n_lo = jnp.where(up, n_mid, n_lo)
        hi = jnp.where(down, mid, hi)
        n_hi = jnp.where(down, n_mid, n_hi)
        done = jnp.where(stuck | (n_lo == topk), 1.0, done)
        return lo, hi, n_lo, n_hi, done

    def bisect(carry):
        state = lax.fori_loop(0, carry[6], lambda _, st: bisect_step(st), carry[:5])
        return (*state, not_finished(state[4]), jnp.int32(BISECT_STEPS_PER_TEST))

    lo, hi, n_lo, n_hi, done, _, _ = lax.while_loop(
        lambda carry: carry[5] > 0, bisect,
        (lo, hi, n_lo, n_hi, done, not_finished(done), jnp.int32(BISECT_STEPS_FIRST)))

    thr = lo
    exact = n_lo == topk
    need = topk - n_hi
    all_keys = jnp.full((1, tq), 2 ** 30, jnp.int32)

    def tie_break():
        def mark(rows, idx, _):
            tie_ref[rows, :] = jnp.where(sc_ref[rows, :] == thr, idx, 2 ** 30)
            return 0
        for_slabs(mark, 0)

        n_bits = (sc_ref.shape[0] - 1).bit_length()

        def step(i, j):
            cand = j + lax.shift_right_logical(jnp.int32(1 << (n_bits - 1)), i)
            n_before = count(lambda rows: tie_ref[rows, :] < cand)
            return jnp.where(n_before < need, cand, j)
        return lax.fori_loop(0, n_bits, step, jnp.zeros((1, tq), jnp.int32))

    any_tie = jnp.min(jnp.where(exact, 1.0, 0.0)) < 0.5

    def write(keep_fn):
        def body(rows, idx, _):
            sc_ref[rows, :] = jnp.where(keep_fn(sc_ref[rows, :], idx), 0.0, MASK_BIAS)
            return 0
        return for_slabs(body, 0, slab=kc)

    def write_with_ties():
        last_tied = jnp.minimum(jnp.where(exact, all_keys, tie_break()), qpos)
        return write(lambda s, idx: (s > thr) | ((s == thr) & (idx <= last_tied)))

    lax.cond(any_tie, write_with_ties, lambda: write(lambda s, idx: s >= thr))


def _prompt_attn_kernel(qt_ref, qit_ref, wit_ref, kg_ref, vt_ref, kib_ref, o_ref,
                        sc_ref, tie_ref, acc_ref, m_ref, mx_ref, s_ref):
    i = pl.program_id(1)
    tq = qt_ref.shape[1]
    qpos = i * tq + lax.broadcasted_iota(jnp.int32, (1, tq), 1)
    n_keys = (i + 1) * tq

    n_score = (n_keys + SCORE_CHUNK - 1) // SCORE_CHUNK

    def score_chunk(c, stats):
        for j in range(SCORE_CHUNK // SCORE_SLAB):
            start = pl.multiple_of(c * SCORE_CHUNK + j * SCORE_SLAB, SCORE_SLAB)
            kc = kib_ref[pl.ds(start, SCORE_SLAB), :]
            acc = jnp.zeros((SCORE_SLAB, tq), F32)
            for h in range(IDX_HEADS):
                d = _dot(kc, qit_ref[h * IDX_DIM:(h + 1) * IDX_DIM, :])
                acc = acc + jnp.maximum(d, 0.0) * wit_ref[h:h + 1, :]
            idx = start + lax.broadcasted_iota(jnp.int32, (SCORE_SLAB, tq), 0)
            s = jnp.where(idx <= qpos, acc, -jnp.inf)
            sc_ref[pl.ds(start, SCORE_SLAB), :] = s
            stats = _stats_update(stats, s)
        return stats

    stats = lax.fori_loop(0, n_score, score_chunk, _stats_init(tq))

    _select_to_bias(sc_ref, tie_ref, qpos, n_score, SCORE_CHUNK, stats)

    m_ref[...] = jnp.full(m_ref.shape, MASK_BIAS, F32)
    acc_ref[...] = jnp.zeros(acc_ref.shape, F32)

    def attn_chunk(c, _):
        start = pl.multiple_of(c * ATTN_CHUNK, ATTN_CHUNK)
        for h in range(N_HEADS):
            s = _dot(kg_ref[h // HEADS_PER_KV, pl.ds(start, ATTN_CHUNK), :],
                     qt_ref[h * HEAD_DIM:(h + 1) * HEAD_DIM, :])
            s = s + sc_ref[pl.ds(start, ATTN_CHUNK), :]
            s_ref[h] = s
            mx_ref[h] = s.max(axis=0, keepdims=True)
        for h in range(N_HEADS):
            g = h // HEADS_PER_KV
            m_old = m_ref[h]
            m_new = jnp.maximum(m_old, mx_ref[h])
            alpha = jnp.exp2(m_old - m_new)
            p = jnp.exp2(s_ref[h] - m_new).astype(BF16)
            vt = vt_ref[c, g * V_ROWS:(g + 1) * V_ROWS, :]
            acc_ref[h] = alpha * acc_ref[h] + _dot(vt, p)
            m_ref[h] = m_new
        return 0

    lax.fori_loop(0, (i + 1) * (tq // ATTN_CHUNK), attn_chunk, 0)

    for pair in range(N_HEADS // 2):
        two = []
        for h in (2 * pair, 2 * pair + 1):
            a = acc_ref[h]
            two.append(a[0:HEAD_DIM, :] / a[HEAD_DIM:HEAD_DIM + 1, :])
        col = 2 * pair * HEAD_DIM
        o_ref[:, col:col + 2 * HEAD_DIM] = jnp.concatenate(two, axis=0).T.astype(BF16)


def _prompt_attention(qt, qit, wit, kg, vt, kib):
    nb, _, t = qt.shape
    tq = Q_TILE
    col = lambda height: pl.BlockSpec((None, height, tq), lambda b, i: (b, 0, i))
    return pl.pallas_call(
        _prompt_attn_kernel,
        grid=(nb, t // tq),
        in_specs=[
            col(Q_COLS), col(IQ_COLS), col(IDX_HEADS),
            pl.BlockSpec((None, N_KV_HEADS, t, HEAD_DIM), lambda b, i: (b, 0, 0, 0)),
            pl.BlockSpec((None, t // ATTN_CHUNK, N_KV_HEADS * V_ROWS, ATTN_CHUNK), lambda b, i: (b, 0, 0, 0)),
            pl.BlockSpec((None, t, IDX_DIM), lambda b, i: (b, 0, 0)),
        ],
        out_specs=pl.BlockSpec((None, tq, Q_COLS), lambda b, i: (b, i, 0)),
        out_shape=jax.ShapeDtypeStruct((nb, t, Q_COLS), BF16),
        scratch_shapes=[
            pltpu.VMEM((t, tq), F32),
            pltpu.VMEM((t, tq), jnp.int32),
            pltpu.VMEM((N_HEADS, V_ROWS, tq), F32),
            pltpu.VMEM((N_HEADS, 1, tq), F32),
            pltpu.VMEM((N_HEADS, 1, tq), F32),
            pltpu.VMEM((N_HEADS, ATTN_CHUNK, tq), F32),
        ],
        compiler_params=_params("arbitrary", "arbitrary"),
        name="prompt_attention",
    )(qt, qit, wit, kg, vt, kib)


def _sample_score_kernel(pt_ref, qi_ref, w_ref, *rest, n_pages):
    del pt_ref
    n_seq = qi_ref.shape[0]
    pages, (new_ref, o_ref, kall_ref) = rest[:n_seq * n_pages], rest[n_seq * n_pages:]
    for q in range(n_seq):
        for j in range(n_pages):
            kall_ref[q, :, j * PAGE_SIZE:(j + 1) * PAGE_SIZE] = pages[q * n_pages + j][...].astype(BF16)
        kall_ref[q, :, n_pages * PAGE_SIZE:(n_pages + 1) * PAGE_SIZE] = new_ref[q].astype(BF16)
    for q in range(n_seq):
        d = _dot(qi_ref[q], kall_ref[q])
        r = jnp.maximum(d, 0.0) * w_ref[q]
        o_ref[q] = r.reshape(IDX_HEADS, SUBLANES, r.shape[1]).sum(axis=0)


def _sample_scores(page_table, qi8, w8, cache_ki_t, ki_new_t):
    nb, n_pages = page_table.shape
    n_keys = (n_pages + 1) * PAGE_SIZE
    rows = IDX_HEADS * SUBLANES
    ns = SCORE_SEQS
    per_step = lambda r, w: pl.BlockSpec((ns, r, w), lambda b, pt: (b, 0, 0))
    grid_spec = pltpu.PrefetchScalarGridSpec(
        num_scalar_prefetch=1,
        grid=(nb // ns,),
        in_specs=[per_step(rows, IDX_DIM), per_step(rows, 1)]
        + [pl.BlockSpec((None, IDX_DIM, PAGE_SIZE), lambda b, pt, q=q, j=j: (pt[b * ns + q, j], 0, 0))
           for q in range(ns) for j in range(n_pages)]
        + [per_step(IDX_DIM, PAGE_SIZE)],
        out_specs=per_step(SUBLANES, n_keys),
        scratch_shapes=[pltpu.VMEM((ns, IDX_DIM, n_keys), BF16)],
    )
    return pl.pallas_call(
        functools.partial(_sample_score_kernel, n_pages=n_pages),
        grid_spec=grid_spec,
        out_shape=jax.ShapeDtypeStruct((nb, SUBLANES, n_keys), F32),
        compiler_params=_params("arbitrary"),
        name="sample_indexer_scores",
    )(page_table, qi8, w8, *([cache_ki_t] * (ns * n_pages)), ki_new_t)


def _sample_select_kernel(s_ref, o_ref, sc_ref, tie_ref, *, past, period):
    tq = s_ref.shape[1]
    n_keys = s_ref.shape[0]
    lane = lax.broadcasted_iota(jnp.int32, (1, tq), 1)
    qpos = past + (lane & (period - 1))
    nk = n_keys // PAGE_SIZE
    stats = _stats_init(tq)
    for c in range(nk):
        idx = c * PAGE_SIZE + lax.broadcasted_iota(jnp.int32, (PAGE_SIZE, tq), 0)
        s = jnp.where(idx <= qpos, s_ref[c * PAGE_SIZE:(c + 1) * PAGE_SIZE, :], -jnp.inf)
        sc_ref[c * PAGE_SIZE:(c + 1) * PAGE_SIZE, :] = s
        stats = _stats_update(stats, s)
    _select_to_bias(sc_ref, tie_ref, qpos, nk, PAGE_SIZE, stats)
    o_ref[...] = sc_ref[...]


def _sample_select(scores_t, past, period):
    n_keys, nq = scores_t.shape
    tq = Q_TILE
    blk = pl.BlockSpec((n_keys, tq), lambda i: (0, i))
    return pl.pallas_call(
        functools.partial(_sample_select_kernel, past=past, period=period),
        grid=(nq // tq,),
        in_specs=[blk],
        out_specs=blk,
        out_shape=jax.ShapeDtypeStruct((n_keys, nq), F32),
        scratch_shapes=[pltpu.VMEM((n_keys, tq), F32), pltpu.VMEM((n_keys, tq), jnp.int32)],
        compiler_params=_params("arbitrary"),
        name="sample_select",
    )(scores_t)


def _sample_attn_kernel(pt_ref, q_ref, bias_ref, *rest, n_pages):
    del pt_ref
    n_seq = q_ref.shape[0]
    kpages, vpages = rest[:n_seq * n_pages], rest[n_seq * n_pages:2 * n_seq * n_pages]
    knew_ref, vnew_ref, o_ref, kall_ref, vall_ref, s_ref, p_ref, l_ref = rest[2 * n_seq * n_pages:]
    n_keys = bias_ref.shape[2]
    fill = jnp.zeros((PAGE_SIZE - SUBLANES, LANES), F32)
    for q in range(n_seq):
        for j in range(n_pages):
            kall_ref[q, :, :, j * PAGE_SIZE:(j + 1) * PAGE_SIZE] = kpages[q * n_pages + j][...].astype(BF16)
            vall_ref[q, :, :, j * PAGE_SIZE:(j + 1) * PAGE_SIZE] = vpages[q * n_pages + j][...].astype(BF16)
        for g in range(N_KV_HEADS):
            for new_ref, all_ref in ((knew_ref, kall_ref), (vnew_ref, vall_ref)):
                page = jnp.concatenate([new_ref[q, g], fill], axis=0).T
                all_ref[q, g, :, n_pages * PAGE_SIZE:(n_pages + 1) * PAGE_SIZE] = (
                    page[0:HEAD_DIM, :].astype(BF16))
    for q in range(n_seq):
        bias = jnp.broadcast_to(bias_ref[q][None], (HEADS_PER_KV, SUBLANES, n_keys)).reshape(
            HEADS_PER_KV * SUBLANES, n_keys)
        for g in range(N_KV_HEADS):
            s_ref[q, g] = _dot(q_ref[q, g], kall_ref[q, g]) + bias
    for q in range(n_seq):
        for g in range(N_KV_HEADS):
            s = s_ref[q, g]
            m = s.max(axis=1, keepdims=True)
            p = jnp.exp(s - m)
            l_ref[q, g] = p.sum(axis=1, keepdims=True)
            p_ref[q, g] = p.astype(BF16)
    for q in range(n_seq):
        for g in range(N_KV_HEADS):
            o_ref[q, g] = _dot_nt(p_ref[q, g], vall_ref[q, g]) / l_ref[q, g]


def _sample_attention(page_table, q8, bias8, cache_k_t, cache_v_t, k_new_t, v_new_t):
    nb, n_pages = page_table.shape
    n_keys = (n_pages + 1) * PAGE_SIZE
    rows = HEADS_PER_KV * SUBLANES
    ns = ATTN_SEQS
    page = [pl.BlockSpec((None, N_KV_HEADS, HEAD_DIM, PAGE_SIZE),
                         lambda b, pt, q=q, j=j: (pt[b * ns + q, j], 0, 0, 0))
            for q in range(ns) for j in range(n_pages)]
    new = pl.BlockSpec((ns, N_KV_HEADS, SUBLANES, LANES), lambda b, pt: (b, 0, 0, 0))
    qo = pl.BlockSpec((ns, N_KV_HEADS, rows, HEAD_DIM), lambda b, pt: (b, 0, 0, 0))
    grid_spec = pltpu.PrefetchScalarGridSpec(
        num_scalar_prefetch=1,
        grid=(nb // ns,),
        in_specs=[qo, pl.BlockSpec((ns, SUBLANES, n_keys), lambda b, pt: (b, 0, 0))]
        + page + page + [new, new],
        out_specs=qo,
        scratch_shapes=[pltpu.VMEM((ns, N_KV_HEADS, HEAD_DIM, n_keys), BF16),
                        pltpu.VMEM((ns, N_KV_HEADS, HEAD_DIM, n_keys), BF16),
                        pltpu.VMEM((ns, N_KV_HEADS, rows, n_keys), F32),
                        pltpu.VMEM((ns, N_KV_HEADS, rows, n_keys), BF16),
                        pltpu.VMEM((ns, N_KV_HEADS, rows, 1), F32)],
    )
    pages = ns * n_pages
    return pl.pallas_call(
        functools.partial(_sample_attn_kernel, n_pages=n_pages),
        grid_spec=grid_spec,
        out_shape=jax.ShapeDtypeStruct((nb, N_KV_HEADS, rows, HEAD_DIM), F32),
        compiler_params=_params("arbitrary"),
        name="sample_attention",
    )(page_table, q8, bias8, *([cache_k_t] * pages), *([cache_v_t] * pages), k_new_t, v_new_t)


def _ffn(h, wgu_ref, wd_ref):
    acc = jnp.zeros((h.shape[0], D_MODEL), F32)
    for c in range(D_FF // FFN_CHUNK):
        gate = _dot(h, wgu_ref[:, c * FFN_CHUNK:(c + 1) * FFN_CHUNK])
        up = _dot(h, wgu_ref[:, D_FF + c * FFN_CHUNK:D_FF + (c + 1) * FFN_CHUNK])
        a = _silu(gate) * up
        acc = acc + _dot(a.astype(BF16), wd_ref[c * FFN_CHUNK:(c + 1) * FFN_CHUNK, :])
    return acc


def _row_specs(nb, rows, per_row_mod):
    tm = min(ROW_TILE, rows)
    row = pl.BlockSpec((None, tm, D_MODEL), lambda b, j: (b, j, 0))
    if per_row_mod:
        mod = row
    else:
        mod = pl.BlockSpec((None, 1, D_MODEL), lambda b, j: (b, 0, 0))
    return tm, row, mod


def _attn_out_ffn_kernel(x_ref, o_ref, g1_ref, sc2_ref, sh2_ref, g2_ref, n2_ref,
                         wo_ref, wgu_ref, wd_ref, y_ref):
    y1 = x_ref[...] + g1_ref[...] * _dot(o_ref[...], wo_ref[...])
    h2 = _norm_mod(y1, n2_ref[...], sc2_ref[...], sh2_ref[...]).astype(BF16)
    y_ref[...] = y1 + g2_ref[...] * _ffn(h2, wgu_ref, wd_ref)


def _layer_spec(stacked, layer):
    _, rows, cols = stacked.shape
    return pl.BlockSpec((None, rows, cols), lambda *_: (layer, 0, 0), pipeline_mode=pl.Buffered(1))


def _attn_out_ffn(x, o, g1, sc2, sh2, g2, n2, wo, wgu, wd, layer):
    nb, rows, _ = x.shape
    tm, row, mod = _row_specs(nb, rows, g1.shape[1] == rows)
    return pl.pallas_call(
        _attn_out_ffn_kernel,
        grid=(nb, rows // tm),
        in_specs=[row, row, mod, mod, mod, mod, _const_spec((1, D_MODEL)),
                  _const_spec(wo.shape), _layer_spec(wgu, layer), _layer_spec(wd, layer)],
        out_specs=row,
        out_shape=jax.ShapeDtypeStruct(x.shape, F32),
        compiler_params=_params("arbitrary", "arbitrary"),
        name="attn_out_ffn",
    )(x, o, g1, sc2, sh2, g2, n2, wo, wgu, wd)


def _conv_layer_kernel(x_ref, sc1_ref, sh1_ref, g1_ref, sc2_ref, sh2_ref, g2_ref, p1_ref, p2_ref,
                       n1_ref, n2_ref, nf_ref, win_ref, ck_ref, wout_ref, wgu_ref, wd_ref,
                       y_ref, tail_ref, ubuf_ref, z_ref, *, seg):
    j = pl.program_id(1)
    tm = x_ref.shape[0]
    x = x_ref[...]
    h = _norm_mod(x, n1_ref[...], sc1_ref[...], sh1_ref[...]).astype(BF16)

    @pl.when(j == 0)
    def _():
        ubuf_ref[0:SUBLANES, :] = jnp.zeros((SUBLANES, D_MODEL), F32)

    @pl.when(j > 0)
    def _():
        ubuf_ref[0:SUBLANES, :] = ubuf_ref[tm:tm + SUBLANES, :]

    t = (j * tm + lax.broadcasted_iota(jnp.int32, (tm, 1), 0)) & (seg - 1)
    ck = ck_ref[...]
    cw = FFN_CHUNK
    for c in range(D_MODEL // cw):
        cols = slice(c * cw, (c + 1) * cw)
        bg = _dot(h, win_ref[:, c * cw:(c + 1) * cw])
        cg = _dot(h, win_ref[:, D_MODEL + c * cw:D_MODEL + (c + 1) * cw])
        hv = _dot(h, win_ref[:, 2 * D_MODEL + c * cw:2 * D_MODEL + (c + 1) * cw])
        u = cg * hv
        ubuf_ref[SUBLANES:SUBLANES + tm, cols] = u
        um1 = jnp.where(t >= 1, ubuf_ref[SUBLANES - 1:SUBLANES - 1 + tm, cols], p1_ref[:, cols])
        um2 = jnp.where(t >= 2, ubuf_ref[SUBLANES - 2:SUBLANES - 2 + tm, cols], p2_ref[:, cols])
        conv = ck[0:1, cols] * um2 + ck[1:2, cols] * um1 + ck[2:3, cols] * u
        z_ref[:, cols] = (bg * conv).astype(BF16)
    r = tail_ref.shape[0]
    tail_ref[...] = ubuf_ref[SUBLANES + tm - r:SUBLANES + tm, :]

    y1 = x + g1_ref[...] * _dot(z_ref[...], wout_ref[...])
    h2 = _norm_mod(y1, n2_ref[...], sc2_ref[...], sh2_ref[...]).astype(BF16)
    y2 = y1 + g2_ref[...] * _ffn(h2, wgu_ref, wd_ref)
    ms = jnp.mean(y2 * y2, axis=-1, keepdims=True)
    y_ref[...] = (y2 * lax.rsqrt(ms + EPS)) * nf_ref[...]


def _conv_layer(x, mods, p1, p2, n1, n2, nf, win, ck, wout, wgu, wd, *, layer, seg, full_tail):
    nb, rows, _ = x.shape
    per_row = mods[0].shape[1] == rows
    tm, row, mod = _row_specs(nb, rows, per_row)
    if per_row:
        prefix = row
    else:
        prefix = pl.BlockSpec((None, 1, D_MODEL), lambda b, j: (0, 0, 0))
    if full_tail:
        tail_spec, tail_rows = row, rows
    else:
        tail_spec, tail_rows = pl.BlockSpec((None, SUBLANES, D_MODEL), lambda b, j: (b, 0, 0)), SUBLANES
    vec = _const_spec((1, D_MODEL))
    return pl.pallas_call(
        functools.partial(_conv_layer_kernel, seg=seg),
        grid=(nb, rows // tm),
        in_specs=[row] + [mod] * 6 + [prefix, prefix, vec, vec, vec,
                                      _const_spec(win.shape), _const_spec(ck.shape), _const_spec(wout.shape),
                                      _layer_spec(wgu, layer), _layer_spec(wd, layer)],
        out_specs=(row, tail_spec),
        out_shape=(jax.ShapeDtypeStruct(x.shape, F32),
                   jax.ShapeDtypeStruct((nb, tail_rows, D_MODEL), F32)),
        scratch_shapes=[pltpu.VMEM((tm + SUBLANES, D_MODEL), F32), pltpu.VMEM((tm, D_MODEL), BF16)],
        compiler_params=_params("arbitrary", "arbitrary"),
        name="conv_layer",
    )(x, *mods, p1, p2, n1, n2, nf, win, ck, wout, wgu, wd)


def _rope_angles(pos):
    inv = ROPE_THETA ** (-jnp.arange(0, ROT_DIM, 2, dtype=F32) / ROT_DIM)
    ang = pos[:, None] * inv[None, :]
    return jnp.cos(ang), jnp.sin(ang)


def _rope_tables(pos):
    half = ROT_DIM // 2
    cos, sin = _rope_angles(pos)
    n = pos.shape[0]
    rest = HEAD_DIM - ROT_DIM
    c = jnp.concatenate([cos, cos, jnp.ones((n, rest), F32)], axis=1)
    s1 = jnp.concatenate([-sin, jnp.zeros((n, half + rest), F32)], axis=1)
    s2 = jnp.concatenate([jnp.zeros((n, half), F32), sin, jnp.zeros((n, rest), F32)], axis=1)
    return tuple(jnp.tile(a, (1, LANES // HEAD_DIM)) for a in (c, s1, s2))


def kernel(x_prompt, x_sample, c_prompt, c_sample, cache_k, cache_v, cache_kidx, state_conv, page_table,
           ada_w, ada_b, norm1_g, norm2_g, final_g, attn_w_in, attn_w_out, conv_w_in, conv_k, conv_w_out,
           ffn_w_gu, ffn_w_down):
    nb, t, _ = x_prompt.shape
    ns, tn, _ = x_sample.shape
    n_pages = page_table.shape[1]
    past = n_pages * PAGE_SIZE
    n_phys = cache_k.shape[1]
    rows_s = ns * tn

    pad = (-(nb + ns)) % SUBLANES
    c_all = jnp.concatenate([c_prompt, c_sample, jnp.zeros((pad, D_MODEL), F32)], axis=0)
    mod = _modulation(c_all, ada_w, ada_b)

    def mods(layer):
        cols = [mod[layer, :, i * D_MODEL:(i + 1) * D_MODEL] for i in range(6)]
        prompt = [m[:nb, None, :] for m in cols]
        sample = [jnp.repeat(m[nb:nb + ns], tn, axis=0)[None] for m in cols]
        return prompt, sample

    vec = lambda a: a.reshape(1, D_MODEL)
    w_in = jnp.pad(attn_w_in[0], ((0, 0), (0, ATTN_IN_PAD - ATTN_IN))).astype(BF16)
    w_o = attn_w_out[0].astype(BF16)
    wgu = ffn_w_gu.astype(BF16)
    wd = ffn_w_down.astype(BF16)

    (sh1p, sc1p, g1p, sh2p, sc2p, g2p), (sh1s, sc1s, g1s, sh2s, sc2s, g2s) = mods(0)
    pos_p = jnp.arange(t, dtype=F32)
    rope_p = _rope_tables(pos_p)
    rope_pt = tuple(a.T for a in _rope_angles(pos_p))
    k_p, v_p, ki_p, qt, qit, kg, vt, kib, wit = _proj_prompt(
        x_prompt, sc1p, sh1p, vec(norm1_g[0]), w_in, rope_p, rope_pt)
    o_p = _prompt_attention(qt, qit, wit, kg, vt, kib)
    y_p = _attn_out_ffn(x_prompt, o_p, g1p, sc2p, sh2p, g2p, vec(norm2_g[0]), w_o, wgu, wd, 0)

    pos_s = jnp.tile(past + jnp.arange(tn, dtype=F32), ns)
    rope_s = _rope_tables(pos_s)
    xs = x_sample.reshape(rows_s, D_MODEL)
    proj = _proj_sample(xs, sc1s[0], sh1s[0], vec(norm1_g[0]), w_in, rope_s)
    o = 0
    q_s = proj[:, o:o + Q_COLS].reshape(ns, tn, N_KV_HEADS, HEADS_PER_KV, HEAD_DIM); o += Q_COLS
    k_s = proj[:, o:o + KV_COLS].reshape(ns, tn, KV_COLS); o += KV_COLS
    v_s = proj[:, o:o + KV_COLS].reshape(ns, tn, KV_COLS); o += KV_COLS
    qi_s = proj[:, o:o + IQ_COLS].reshape(ns, tn, IDX_HEADS, IDX_DIM); o += IQ_COLS
    ki_s = proj[:, o:o + IDX_DIM].reshape(ns, tn, IDX_DIM); o += IDX_DIM
    wi_s = proj[:, o:o + IDX_HEADS].reshape(ns, tn, IDX_HEADS)

    qpad = ((0, 0), (0, SUBLANES - tn))
    qi8 = jnp.pad(qi_s, qpad + ((0, 0), (0, 0))).transpose(0, 2, 1, 3)
    qi8 = qi8.reshape(ns, IDX_HEADS * SUBLANES, IDX_DIM).astype(BF16)
    w8 = jnp.pad(wi_s, qpad + ((0, 0),)).transpose(0, 2, 1).reshape(ns, IDX_HEADS * SUBLANES, 1)
    slot_pad = (0, PAGE_SIZE - tn)
    ki_new_t = jnp.pad(ki_s.transpose(0, 2, 1), ((0, 0), (0, 0), slot_pad))
    scores = _sample_scores(page_table, qi8, w8, cache_kidx[0].transpose(0, 2, 1), ki_new_t)
    n_keys = scores.shape[2]
    scores_t = scores[:, :tn, :].transpose(2, 0, 1).reshape(n_keys, rows_s)
    bias_t = _sample_select(scores_t, past, tn)
    bias8 = jnp.pad(bias_t.reshape(n_keys, ns, tn).transpose(1, 2, 0), qpad + ((0, 0),))

    q8 = jnp.pad(q_s * HEAD_DIM ** -0.5, qpad + ((0, 0), (0, 0), (0, 0))).transpose(0, 2, 3, 1, 4)
    q8 = q8.reshape(ns, N_KV_HEADS, HEADS_PER_KV * SUBLANES, HEAD_DIM).astype(BF16)
    new_t = lambda a: jnp.pad(a.reshape(ns, tn, N_KV_HEADS, HEAD_DIM).transpose(0, 2, 1, 3),
                              ((0, 0), (0, 0), (0, SUBLANES - tn), (0, LANES - HEAD_DIM)))
    o_g = _sample_attention(page_table, q8, bias8,
                            cache_k[0].transpose(0, 2, 3, 1), cache_v[0].transpose(0, 2, 3, 1),
                            new_t(k_s), new_t(v_s))
    o_g = o_g.reshape(ns, N_KV_HEADS, HEADS_PER_KV, SUBLANES, HEAD_DIM)[:, :, :, :tn]
    o_s = o_g.transpose(0, 3, 1, 2, 4).reshape(1, rows_s, Q_COLS).astype(BF16)
    y_s = _attn_out_ffn(xs[None], o_s, g1s, sc2s, sh2s, g2s, vec(norm2_g[0]), w_o, wgu, wd, 0)

    (sh1p, sc1p, g1p, sh2p, sc2p, g2p), (sh1s, sc1s, g1s, sh2s, sc2s, g2s) = mods(1)
    win = conv_w_in[0].astype(BF16)
    wout = conv_w_out[0].astype(BF16)
    zero_prefix = jnp.zeros((1, 1, D_MODEL), F32)
    out_p, tail_p = _conv_layer(
        y_p, (sc1p, sh1p, g1p, sc2p, sh2p, g2p), zero_prefix, zero_prefix,
        vec(norm1_g[1]), vec(norm2_g[1]), vec(final_g), win, conv_k[0], wout, wgu, wd,
        layer=1, seg=t, full_tail=False)
    st = state_conv[0]
    zeros_row = jnp.zeros((ns, 1, D_MODEL), F32)
    p1 = jnp.concatenate([st[:, 1:2], zeros_row, zeros_row, zeros_row], axis=1).reshape(1, rows_s, D_MODEL)
    p2 = jnp.concatenate([st[:, 0:1], st[:, 1:2], zeros_row, zeros_row], axis=1).reshape(1, rows_s, D_MODEL)
    out_s, u_s = _conv_layer(
        y_s, (sc1s, sh1s, g1s, sc2s, sh2s, g2s), p1, p2,
        vec(norm1_g[1]), vec(norm2_g[1]), vec(final_g), win, conv_k[0], wout, wgu, wd,
        layer=1, seg=tn, full_tail=True)

    keep = CONV_WIDTH - 1
    return (
        out_p,
        out_s.reshape(ns, tn, D_MODEL),
        k_p.reshape(1, nb, t, N_KV_HEADS, HEAD_DIM),
        v_p.reshape(1, nb, t, N_KV_HEADS, HEAD_DIM),
        ki_p[None],
        tail_p[None, :, SUBLANES - keep:, :],
        k_s.reshape(1, ns, tn, N_KV_HEADS, HEAD_DIM),
        v_s.reshape(1, ns, tn, N_KV_HEADS, HEAD_DIM),
        ki_s[None],
        u_s.reshape(ns, tn, D_MODEL)[None, :, tn - keep:, :],
    )
```

```python
import functools

import jax
import jax.numpy as jnp
from jax import lax
from jax.experimental import pallas as pl
from jax.experimental.pallas import tpu as pltpu

F32 = jnp.float32
BF16 = jnp.bfloat16

D_MODEL = 1024
N_HEADS = 16
HEAD_DIM = 64
N_KV_HEADS = 4
HEADS_PER_KV = N_HEADS // N_KV_HEADS
ROT_DIM = 16
ROPE_THETA = 500000.0
IDX_HEADS = 8
IDX_DIM = 64
TOPK = 256
PAGE_SIZE = 128
CONV_WIDTH = 3
D_FF = 2816
EPS = 1e-6
Q_COLS = N_HEADS * HEAD_DIM
KV_COLS = N_KV_HEADS * HEAD_DIM
IQ_COLS = IDX_HEADS * IDX_DIM
ATTN_IN = Q_COLS + 2 * KV_COLS + IQ_COLS + IDX_DIM + IDX_HEADS
ATTN_IN_PAD = 2176
LANES = 128
SUBLANES = 8
MASK_BIAS = -1e30
F32_MAX = 3.4028234663852886e38
LOG2_E = 1.4426950408889634
VMEM_LIMIT = 60 * 1024 * 1024

ROW_TILE = 512
Q_TILE = 256
SCORE_CHUNK = 512
SCORE_SLAB = 128
ATTN_CHUNK = 256
V_ROWS = HEAD_DIM + 16
FFN_CHUNK = 256
SCORE_SEQS = 4
ATTN_SEQS = 2


def _dot(a, b):
    return jnp.dot(a, b, preferred_element_type=F32)


def _dot_nt(a, b):
    return lax.dot_general(a, b, (((1,), (1,)), ((), ())), preferred_element_type=F32)


def _silu(x):
    return x / (1.0 + jnp.exp(-x))


def _norm_mod(x, g, sc, sh):
    ms = jnp.mean(x * x, axis=-1, keepdims=True)
    y = x * lax.rsqrt(ms + EPS)
    return (y * g) * (1.0 + sc) + sh


def _params(*sem, flags=None):
    return pltpu.CompilerParams(dimension_semantics=sem, vmem_limit_bytes=VMEM_LIMIT, flags=flags)


def _const_spec(shape):
    nd = len(shape)
    return pl.BlockSpec(shape, lambda *_: (0,) * nd, pipeline_mode=pl.Buffered(1))


def _mod_kernel(c_ref, w_ref, b_ref, o_ref):
    s = _silu(c_ref[...]).astype(BF16)
    o_ref[...] = _dot(s, w_ref[...].astype(BF16)) + b_ref[...]


def _modulation(c_all, ada_w, ada_b):
    depth, _, n = ada_w.shape
    rows = c_all.shape[0]
    tn = 1024
    return pl.pallas_call(
        _mod_kernel,
        grid=(depth, n // tn),
        in_specs=[
            pl.BlockSpec((rows, D_MODEL), lambda l, j: (0, 0)),
            pl.BlockSpec((None, D_MODEL, tn), lambda l, j: (l, 0, j)),
            pl.BlockSpec((None, 1, tn), lambda l, j: (l, 0, j)),
        ],
        out_specs=pl.BlockSpec((None, rows, tn), lambda l, j: (l, 0, j)),
        out_shape=jax.ShapeDtypeStruct((depth, rows, n), F32),
        compiler_params=_params("arbitrary", "arbitrary"),
        name="adaln_modulation",
    )(c_all, ada_w, ada_b.reshape(depth, 1, n))


def _rope_block(x, c, s1, s2):
    return x * c + pltpu.roll(x, LANES - ROT_DIM // 2, 1) * s1 + pltpu.roll(x, ROT_DIM // 2, 1) * s2


def _rope_cols(p, c, s1, s2):
    nblk = p.shape[1] // LANES
    return [_rope_block(p[:, j * LANES:(j + 1) * LANES], c, s1, s2) for j in range(nblk)]


def _last_block_tables(c, s1, s2):
    lane = lax.broadcasted_iota(jnp.int32, c.shape, 1)
    is_key = lane < IDX_DIM
    return jnp.where(is_key, c, 1.0), jnp.where(is_key, s1, 0.0), jnp.where(is_key, s2, 0.0)


def _rope_rows(xt, cos_t, sin_t):
    half = ROT_DIM // 2
    out = []
    for base in range(0, LANES, HEAD_DIM):
        x1, x2 = xt[base:base + half], xt[base + half:base + ROT_DIM]
        out += [x1 * cos_t - x2 * sin_t, x2 * cos_t + x1 * sin_t, xt[base + ROT_DIM:base + HEAD_DIM]]
    return jnp.concatenate(out, axis=0)


def _proj_prompt_kernel(x_ref, sc_ref, sh_ref, g_ref, w_ref, c_ref, s1_ref, s2_ref, ct_ref, st_ref,
                        k_ref, v_ref, kit_ref, qt_ref, qit_ref, kg_ref, vt_ref, kib_ref, wit_ref):
    h = _norm_mod(x_ref[...], g_ref[...], sc_ref[...], sh_ref[...]).astype(BF16)
    c, s1, s2 = c_ref[...], s1_ref[...], s2_ref[...]
    cos_t, sin_t = ct_ref[...], st_ref[...]

    pq = _dot(h, w_ref[:, 0:Q_COLS])
    for j in range(Q_COLS // LANES):
        qt = _rope_rows(pq[:, j * LANES:(j + 1) * LANES].T, cos_t, sin_t)
        qt_ref[j * LANES:(j + 1) * LANES, :] = (qt * (HEAD_DIM ** -0.5 * LOG2_E)).astype(BF16)

    pk = _dot(h, w_ref[:, Q_COLS:Q_COLS + KV_COLS])
    k = jnp.concatenate(_rope_cols(pk, c, s1, s2), axis=1)
    k_ref[...] = k
    for g in range(N_KV_HEADS):
        kg_ref[g] = k[:, g * HEAD_DIM:(g + 1) * HEAD_DIM].astype(BF16)

    pv = _dot(h, w_ref[:, Q_COLS + KV_COLS:Q_COLS + 2 * KV_COLS])
    v_ref[...] = pv
    ones = jnp.ones((V_ROWS - HEAD_DIM, ATTN_CHUNK), BF16)
    for cc in range(pv.shape[0] // ATTN_CHUNK):
        vt = pv[cc * ATTN_CHUNK:(cc + 1) * ATTN_CHUNK, :].astype(BF16).T
        for g in range(N_KV_HEADS):
            vt_ref[cc, g * V_ROWS:g * V_ROWS + HEAD_DIM, :] = vt[g * HEAD_DIM:(g + 1) * HEAD_DIM, :]
            vt_ref[cc, g * V_ROWS + HEAD_DIM:(g + 1) * V_ROWS, :] = ones

    o = Q_COLS + 2 * KV_COLS
    pqi = _dot(h, w_ref[:, o:o + IQ_COLS])
    for j in range(IQ_COLS // LANES):
        qit = _rope_rows(pqi[:, j * LANES:(j + 1) * LANES].T, cos_t, sin_t)
        qit_ref[j * LANES:(j + 1) * LANES, :] = qit.astype(BF16)

    pl_ = _dot(h, w_ref[:, o + IQ_COLS:ATTN_IN_PAD])
    last = _rope_block(pl_, *_last_block_tables(c, s1, s2))
    last_t = last.T
    kit_ref[...] = last_t[0:IDX_DIM, :]
    kib_ref[...] = last[:, 0:IDX_DIM].astype(BF16)
    wit_ref[...] = last_t[IDX_DIM:IDX_DIM + IDX_HEADS, :]


def _proj_prompt(x, sc, sh, g, w, rope, rope_t):
    nb, t, _ = x.shape
    tm = ROW_TILE
    row = lambda width: pl.BlockSpec((None, tm, width), lambda b, j: (b, j, 0))
    mod = pl.BlockSpec((None, 1, D_MODEL), lambda b, j: (b, 0, 0))
    tab = pl.BlockSpec((tm, LANES), lambda b, j: (j, 0))
    tab_t = pl.BlockSpec((ROT_DIM // 2, tm), lambda b, j: (0, j))
    col = lambda height: pl.BlockSpec((None, height, tm), lambda b, j: (b, 0, j))
    out_shape = (
        jax.ShapeDtypeStruct((nb, t, KV_COLS), F32),
        jax.ShapeDtypeStruct((nb, t, KV_COLS), F32),
        jax.ShapeDtypeStruct((nb, IDX_DIM, t), F32),
        jax.ShapeDtypeStruct((nb, Q_COLS, t), BF16),
        jax.ShapeDtypeStruct((nb, IQ_COLS, t), BF16),
        jax.ShapeDtypeStruct((nb, N_KV_HEADS, t, HEAD_DIM), BF16),
        jax.ShapeDtypeStruct((nb, t // ATTN_CHUNK, N_KV_HEADS * V_ROWS, ATTN_CHUNK), BF16),
        jax.ShapeDtypeStruct((nb, t, IDX_DIM), BF16),
        jax.ShapeDtypeStruct((nb, IDX_HEADS, t), F32),
    )
    out_specs = (
        row(KV_COLS), row(KV_COLS), col(IDX_DIM), col(Q_COLS), col(IQ_COLS),
        pl.BlockSpec((None, N_KV_HEADS, tm, HEAD_DIM), lambda b, j: (b, 0, j, 0)),
        pl.BlockSpec((None, tm // ATTN_CHUNK, N_KV_HEADS * V_ROWS, ATTN_CHUNK), lambda b, j: (b, j, 0, 0)),
        row(IDX_DIM), col(IDX_HEADS),
    )
    return pl.pallas_call(
        _proj_prompt_kernel,
        grid=(nb, t // tm),
        in_specs=[row(D_MODEL), mod, mod, _const_spec((1, D_MODEL)),
                  _const_spec((D_MODEL, ATTN_IN_PAD)), tab, tab, tab, tab_t, tab_t],
        out_specs=out_specs,
        out_shape=out_shape,
        compiler_params=_params("arbitrary", "arbitrary"),
        name="attn_proj_prompt",
    )(x, sc, sh, g, w, *rope, *rope_t)


def _proj_sample_kernel(x_ref, sc_ref, sh_ref, g_ref, w_ref, c_ref, s1_ref, s2_ref, p_ref):
    h = _norm_mod(x_ref[...], g_ref[...], sc_ref[...], sh_ref[...]).astype(BF16)
    c, s1, s2 = c_ref[...], s1_ref[...], s2_ref[...]
    p = _dot(h, w_ref[...])
    v_lo = (Q_COLS + KV_COLS) // LANES
    v_hi = (Q_COLS + 2 * KV_COLS) // LANES
    nblk = ATTN_IN_PAD // LANES
    for j in range(nblk):
        blk = p[:, j * LANES:(j + 1) * LANES]
        if j == nblk - 1:
            blk = _rope_block(blk, *_last_block_tables(c, s1, s2))
        elif not (v_lo <= j < v_hi):
            blk = _rope_block(blk, c, s1, s2)
        p_ref[:, j * LANES:(j + 1) * LANES] = blk


def _proj_sample(x, sc, sh, g, w, rope):
    rows = x.shape[0]
    full = lambda width: pl.BlockSpec((rows, width), lambda i: (0, 0))
    return pl.pallas_call(
        _proj_sample_kernel,
        grid=(1,),
        in_specs=[full(D_MODEL), full(D_MODEL), full(D_MODEL),
                  pl.BlockSpec((1, D_MODEL), lambda i: (0, 0)),
                  pl.BlockSpec((D_MODEL, ATTN_IN_PAD), lambda i: (0, 0)),
                  full(LANES), full(LANES), full(LANES)],
        out_specs=full(ATTN_IN_PAD),
        out_shape=jax.ShapeDtypeStruct((rows, ATTN_IN_PAD), F32),
        compiler_params=_params("arbitrary"),
        name="attn_proj_sample",
    )(x, sc, sh, g, w, *rope)


FOLD_ROWS = 32
BISECT_STEPS_FIRST = 14
BISECT_STEPS_PER_TEST = 3


def _fold(x, op):
    rows, tq = x.shape
    x3 = x.reshape(rows // FOLD_ROWS, FOLD_ROWS, tq)
    return {"sum": x3.sum, "max": x3.max, "min": x3.min}[op](axis=0)


def _stats_init(tq):
    zeros = jnp.zeros((FOLD_ROWS, tq), F32)
    return (jnp.full((FOLD_ROWS, tq), -jnp.inf, F32), jnp.full((FOLD_ROWS, tq), jnp.inf, F32),
            zeros, zeros, zeros)


def _stats_update(stats, s):
    mx, mn, n_above, n_ge0, n_gt0 = stats
    ones_where = lambda m: _fold(jnp.where(m, 1.0, 0.0), "sum")
    above = s > -jnp.inf
    return (jnp.maximum(mx, _fold(s, "max")),
            jnp.minimum(mn, _fold(jnp.where(above, s, jnp.inf), "min")),
            n_above + ones_where(above), n_ge0 + ones_where(s >= 0.0), n_gt0 + ones_where(s > 0.0))


def _select_to_bias(sc_ref, tie_ref, qpos, nk, kc, stats):
    tq = sc_ref.shape[1]
    topk = float(TOPK)

    slab = 4 * FOLD_ROWS

    def for_slabs(fn, init, slab=slab):
        def body(c, carry):
            for j in range(kc // slab):
                start = pl.multiple_of(c * kc + j * slab, slab)
                idx = start + lax.broadcasted_iota(jnp.int32, (slab, tq), 0)
                carry = fn(pl.ds(start, slab), idx, carry)
            return carry
        return lax.fori_loop(0, nk, body, init)

    def ones_where(m):
        return _fold(jnp.where(m, 1.0, 0.0), "sum")

    def count(pred):
        acc = for_slabs(lambda rows, idx, acc: acc + ones_where(pred(rows)), jnp.zeros((FOLD_ROWS, tq), F32))
        return acc.sum(axis=0, keepdims=True)

    def count_ge(t):
        return count(lambda rows: sc_ref[rows, :] >= t)

    mx, mn, n_above, n_ge0, n_gt0 = stats
    mx = mx.max(axis=0, keepdims=True)
    mn = mn.min(axis=0, keepdims=True)
    n_above = n_above.sum(axis=0, keepdims=True)
    n_ge0 = n_ge0.sum(axis=0, keepdims=True)
    n_gt0 = n_gt0.sum(axis=0, keepdims=True)
    n_max = count_ge(mx)

    few = n_above < topk
    flat = n_max >= topk
    zero = (n_gt0 < topk) & (n_ge0 >= topk)
    pos = n_gt0 >= topk
    lo = jnp.where(few, -jnp.inf, jnp.where(flat, mx, jnp.where(zero | pos, 0.0, mn)))
    hi = jnp.where(few, mn, jnp.where(flat, jnp.inf, jnp.where(pos, mx, 0.0)))
    n_lo = jnp.where(few, 2.0 * topk, jnp.where(flat, n_max, jnp.where(zero | pos, n_ge0, n_above)))
    n_hi = jnp.where(few, n_above, jnp.where(flat, 0.0, jnp.where(zero, n_gt0, jnp.where(pos, n_max, n_ge0))))
    done = jnp.where(few | flat | zero | (n_lo == topk), 1.0, 0.0)

    def not_finished(d):
        return (jnp.min(d) < 0.5).astype(jnp.int32)

    def bisect_step(state):
        lo, hi, n_lo, n_hi, done = state
        mid = jnp.clip(0.5 * lo + 0.5 * hi, -F32_MAX, F32_MAX)
        stuck = (mid <= lo) | (mid >= hi)
        n_mid = count_ge(mid)
        move = (done < 0.5) & jnp.logical_not(stuck)
        up = move & (n_mid >= topk)
        down = move & (n_mid < topk)
        lo = jnp.where(up, mid, lo)
        n_lo = jnp.where(up, n_mid, n_lo)
        hi = jnp.where(down, mid, hi)
        n_hi = jnp.where(down, n_mid, n_hi)
        done = jnp.where(stuck | (n_lo == topk), 1.0, done)
        return lo, hi, n_lo, n_hi, done

    def bisect(carry):
        state = lax.fori_loop(0, carry[6], lambda _, st: bisect_step(st), carry[:5])
        return (*state, not_finished(state[4]), jnp.int32(BISECT_STEPS_PER_TEST))

    lo, hi, n_lo, n_hi, done, _, _ = lax.while_loop(
        lambda carry: carry[5] > 0, bisect,
        (lo, hi, n_lo, n_hi, done, not_finished(done), jnp.int32(BISECT_STEPS_FIRST)))

    thr = lo
    exact = n_lo == topk
    need = topk - n_hi
    all_keys = jnp.full((1, tq), 2 ** 30, jnp.int32)

    def tie_break():
        def mark(rows, idx, _):
            tie_ref[rows, :] = jnp.where(sc_ref[rows, :] == thr, idx, 2 ** 30)
            return 0
        for_slabs(mark, 0)

        n_bits = (sc_ref.shape[0] - 1).bit_length()

        def step(i, j):
            cand = j + lax.shift_right_logical(jnp.int32(1 << (n_bits - 1)), i)
            n_before = count(lambda rows: tie_ref[rows, :] < cand)
            return jnp.where(n_before < need, cand, j)
        return lax.fori_loop(0, n_bits, step, jnp.zeros((1, tq), jnp.int32))

    any_tie = jnp.min(jnp.where(exact, 1.0, 0.0)) < 0.5

    def write(keep_fn):
        def body(rows, idx, _):
            sc_ref[rows, :] = jnp.where(keep_fn(sc_ref[rows, :], idx), 0.0, MASK_BIAS)
            return 0
        return for_slabs(body, 0, slab=kc)

    def write_with_ties():
        last_tied = jnp.minimum(jnp.where(exact, all_keys, tie_break()), qpos)
        return write(lambda s, idx: (s > thr) | ((s == thr) & (idx <= last_tied)))

    lax.cond(any_tie, write_with_ties, lambda: write(lambda s, idx: s >= thr))


def _prompt_attn_kernel(qt_ref, qit_ref, wit_ref, kg_ref, vt_ref, kib_ref, o_ref,
                        sc_ref, tie_ref, acc_ref, m_ref, mx_ref, s_ref):
    i = pl.program_id(1)
    tq = qt_ref.shape[1]
    qpos = i * tq + lax.broadcasted_iota(jnp.int32, (1, tq), 1)
    n_keys = (i + 1) * tq

    n_score = (n_keys + SCORE_CHUNK - 1) // SCORE_CHUNK

    def score_chunk(c, stats):
        for j in range(SCORE_CHUNK // SCORE_SLAB):
            start = pl.multiple_of(c * SCORE_CHUNK + j * SCORE_SLAB, SCORE_SLAB)
            kc = kib_ref[pl.ds(start, SCORE_SLAB), :]
            acc = jnp.zeros((SCORE_SLAB, tq), F32)
            for h in range(IDX_HEADS):
                d = _dot(kc, qit_ref[h * IDX_DIM:(h + 1) * IDX_DIM, :])
                acc = acc + jnp.maximum(d, 0.0) * wit_ref[h:h + 1, :]
            idx = start + lax.broadcasted_iota(jnp.int32, (SCORE_SLAB, tq), 0)
            s = jnp.where(idx <= qpos, acc, -jnp.inf)
            sc_ref[pl.ds(start, SCORE_SLAB), :] = s
            stats = _stats_update(stats, s)
        return stats

    stats = lax.fori_loop(0, n_score, score_chunk, _stats_init(tq))

    _select_to_bias(sc_ref, tie_ref, qpos, n_score, SCORE_CHUNK, stats)

    m_ref[...] = jnp.full(m_ref.shape, MASK_BIAS, F32)
    acc_ref[...] = jnp.zeros(acc_ref.shape, F32)

    def attn_chunk(c, _):
        start = pl.multiple_of(c * ATTN_CHUNK, ATTN_CHUNK)
        for h in range(N_HEADS):
            s = _dot(kg_ref[h // HEADS_PER_KV, pl.ds(start, ATTN_CHUNK), :],
                     qt_ref[h * HEAD_DIM:(h + 1) * HEAD_DIM, :])
            s = s + sc_ref[pl.ds(start, ATTN_CHUNK), :]
            s_ref[h] = s
            mx_ref[h] = s.max(axis=0, keepdims=True)
        for h in range(N_HEADS):
            g = h // HEADS_PER_KV
            m_old = m_ref[h]
            m_new = jnp.maximum(m_old, mx_ref[h])
            alpha = jnp.exp2(m_old - m_new)
            p = jnp.exp2(s_ref[h] - m_new).astype(BF16)
            vt = vt_ref[c, g * V_ROWS:(g + 1) * V_ROWS, :]
            acc_ref[h] = alpha * acc_ref[h] + _dot(vt, p)
            m_ref[h] = m_new
        return 0

    lax.fori_loop(0, (i + 1) * (tq // ATTN_CHUNK), attn_chunk, 0)

    for pair in range(N_HEADS // 2):
        two = []
        for h in (2 * pair, 2 * pair + 1):
            a = acc_ref[h]
            two.append(a[0:HEAD_DIM, :] / a[HEAD_DIM:HEAD_DIM + 1, :])
        col = 2 * pair * HEAD_DIM
        o_ref[:, col:col + 2 * HEAD_DIM] = jnp.concatenate(two, axis=0).T.astype(BF16)


def _prompt_attention(qt, qit, wit, kg, vt, kib):
    nb, _, t = qt.shape
    tq = Q_TILE
    col = lambda height: pl.BlockSpec((None, height, tq), lambda b, i: (b, 0, i))
    return pl.pallas_call(
        _prompt_attn_kernel,
        grid=(nb, t // tq),
        in_specs=[
            col(Q_COLS), col(IQ_COLS), col(IDX_HEADS),
            pl.BlockSpec((None, N_KV_HEADS, t, HEAD_DIM), lambda b, i: (b, 0, 0, 0)),
            pl.BlockSpec((None, t // ATTN_CHUNK, N_KV_HEADS * V_ROWS, ATTN_CHUNK), lambda b, i: (b, 0, 0, 0)),
            pl.BlockSpec((None, t, IDX_DIM), lambda b, i: (b, 0, 0)),
        ],
        out_specs=pl.BlockSpec((None, tq, Q_COLS), lambda b, i: (b, i, 0)),
        out_shape=jax.ShapeDtypeStruct((nb, t, Q_COLS), BF16),
        scratch_shapes=[
            pltpu.VMEM((t, tq), F32),
            pltpu.VMEM((t, tq), jnp.int32),
            pltpu.VMEM((N_HEADS, V_ROWS, tq), F32),
            pltpu.VMEM((N_HEADS, 1, tq), F32),
            pltpu.VMEM((N_HEADS, 1, tq), F32),
            pltpu.VMEM((N_HEADS, ATTN_CHUNK, tq), F32),
        ],
        compiler_params=_params("arbitrary", "arbitrary"),
        name="prompt_attention",
    )(qt, qit, wit, kg, vt, kib)


def _sample_score_kernel(pt_ref, qi_ref, w_ref, *rest, n_pages):
    del pt_ref
    n_seq = qi_ref.shape[0]
    pages, (new_ref, o_ref, kall_ref) = rest[:n_seq * n_pages], rest[n_seq * n_pages:]
    for q in range(n_seq):
        for j in range(n_pages):
            kall_ref[q, :, j * PAGE_SIZE:(j + 1) * PAGE_SIZE] = pages[q * n_pages + j][...].astype(BF16)
        kall_ref[q, :, n_pages * PAGE_SIZE:(n_pages + 1) * PAGE_SIZE] = new_ref[q].astype(BF16)
    for q in range(n_seq):
        d = _dot(qi_ref[q], kall_ref[q])
        r = jnp.maximum(d, 0.0) * w_ref[q]
        o_ref[q] = r.reshape(IDX_HEADS, SUBLANES, r.shape[1]).sum(axis=0)


def _sample_scores(page_table, qi8, w8, cache_ki_t, ki_new_t):
    nb, n_pages = page_table.shape
    n_keys = (n_pages + 1) * PAGE_SIZE
    rows = IDX_HEADS * SUBLANES
    ns = SCORE_SEQS
    per_step = lambda r, w: pl.BlockSpec((ns, r, w), lambda b, pt: (b, 0, 0))
    grid_spec = pltpu.PrefetchScalarGridSpec(
        num_scalar_prefetch=1,
        grid=(nb // ns,),
        in_specs=[per_step(rows, IDX_DIM), per_step(rows, 1)]
        + [pl.BlockSpec((None, IDX_DIM, PAGE_SIZE), lambda b, pt, q=q, j=j: (pt[b * ns + q, j], 0, 0))
           for q in range(ns) for j in range(n_pages)]
        + [per_step(IDX_DIM, PAGE_SIZE)],
        out_specs=per_step(SUBLANES, n_keys),
        scratch_shapes=[pltpu.VMEM((ns, IDX_DIM, n_keys), BF16)],
    )
    return pl.pallas_call(
        functools.partial(_sample_score_kernel, n_pages=n_pages),
        grid_spec=grid_spec,
        out_shape=jax.ShapeDtypeStruct((nb, SUBLANES, n_keys), F32),
        compiler_params=_params("arbitrary"),
        name="sample_indexer_scores",
    )(page_table, qi8, w8, *([cache_ki_t] * (ns * n_pages)), ki_new_t)


def _sample_select_kernel(s_ref, o_ref, sc_ref, tie_ref, *, past, period):
    tq = s_ref.shape[1]
    n_keys = s_ref.shape[0]
    lane = lax.broadcasted_iota(jnp.int32, (1, tq), 1)
    qpos = past + (lane & (period - 1))
    nk = n_keys // PAGE_SIZE
    stats = _stats_init(tq)
    for c in range(nk):
        idx = c * PAGE_SIZE + lax.broadcasted_iota(jnp.int32, (PAGE_SIZE, tq), 0)
        s = jnp.where(idx <= qpos, s_ref[c * PAGE_SIZE:(c + 1) * PAGE_SIZE, :], -jnp.inf)
        sc_ref[c * PAGE_SIZE:(c + 1) * PAGE_SIZE, :] = s
        stats = _stats_update(stats, s)
    _select_to_bias(sc_ref, tie_ref, qpos, nk, PAGE_SIZE, stats)
    o_ref[...] = sc_ref[...]


def _sample_select(scores_t, past, period):
    n_keys, nq = scores_t.shape
    tq = Q_TILE
    blk = pl.BlockSpec((n_keys, tq), lambda i: (0, i))
    return pl.pallas_call(
        functools.partial(_sample_select_kernel, past=past, period=period),
        grid=(nq // tq,),
        in_specs=[blk],
        out_specs=blk,
        out_shape=jax.ShapeDtypeStruct((n_keys, nq), F32),
        scratch_shapes=[pltpu.VMEM((n_keys, tq), F32), pltpu.VMEM((n_keys, tq), jnp.int32)],
        compiler_params=_params("arbitrary"),
        name="sample_select",
    )(scores_t)


def _sample_attn_kernel(pt_ref, q_ref, bias_ref, *rest, n_pages):
    del pt_ref
    n_seq = q_ref.shape[0]
    kpages, vpages = rest[:n_seq * n_pages], rest[n_seq * n_pages:2 * n_seq * n_pages]
    knew_ref, vnew_ref, o_ref, kall_ref, vall_ref, s_ref, p_ref, l_ref = rest[2 * n_seq * n_pages:]
    n_keys = bias_ref.shape[2]
    fill = jnp.zeros((PAGE_SIZE - SUBLANES, LANES), F32)
    for q in range(n_seq):
        for j in range(n_pages):
            kall_ref[q, :, :, j * PAGE_SIZE:(j + 1) * PAGE_SIZE] = kpages[q * n_pages + j][...].astype(BF16)
            vall_ref[q, :, :, j * PAGE_SIZE:(j + 1) * PAGE_SIZE] = vpages[q * n_pages + j][...].astype(BF16)
        for g in range(N_KV_HEADS):
            for new_ref, all_ref in ((knew_ref, kall_ref), (vnew_ref, vall_ref)):
                page = jnp.concatenate([new_ref[q, g], fill], axis=0).T
                all_ref[q, g, :, n_pages * PAGE_SIZE:(n_pages + 1) * PAGE_SIZE] = (
                    page[0:HEAD_DIM, :].astype(BF16))
    for q in range(n_seq):
        bias = jnp.broadcast_to(bias_ref[q][None], (HEADS_PER_KV, SUBLANES, n_keys)).reshape(
            HEADS_PER_KV * SUBLANES, n_keys)
        for g in range(N_KV_HEADS):
            s_ref[q, g] = _dot(q_ref[q, g], kall_ref[q, g]) + bias
    for q in range(n_seq):
        for g in range(N_KV_HEADS):
            s = s_ref[q, g]
            m = s.max(axis=1, keepdims=True)
            p = jnp.exp(s - m)
            l_ref[q, g] = p.sum(axis=1, keepdims=True)
            p_ref[q, g] = p.astype(BF16)
    for q in range(n_seq):
        for g in range(N_KV_HEADS):
            o_ref[q, g] = _dot_nt(p_ref[q, g], vall_ref[q, g]) / l_ref[q, g]


def _sample_attention(page_table, q8, bias8, cache_k_t, cache_v_t, k_new_t, v_new_t):
    nb, n_pages = page_table.shape
    n_keys = (n_pages + 1) * PAGE_SIZE
    rows = HEADS_PER_KV * SUBLANES
    ns = ATTN_SEQS
    page = [pl.BlockSpec((None, N_KV_HEADS, HEAD_DIM, PAGE_SIZE),
                         lambda b, pt, q=q, j=j: (pt[b * ns + q, j], 0, 0, 0))
            for q in range(ns) for j in range(n_pages)]
    new = pl.BlockSpec((ns, N_KV_HEADS, SUBLANES, LANES), lambda b, pt: (b, 0, 0, 0))
    qo = pl.BlockSpec((ns, N_KV_HEADS, rows, HEAD_DIM), lambda b, pt: (b, 0, 0, 0))
    grid_spec = pltpu.PrefetchScalarGridSpec(
        num_scalar_prefetch=1,
        grid=(nb // ns,),
        in_specs=[qo, pl.BlockSpec((ns, SUBLANES, n_keys), lambda b, pt: (b, 0, 0))]
        + page + page + [new, new],
        out_specs=qo,
        scratch_shapes=[pltpu.VMEM((ns, N_KV_HEADS, HEAD_DIM, n_keys), BF16),
                        pltpu.VMEM((ns, N_KV_HEADS, HEAD_DIM, n_keys), BF16),
                        pltpu.VMEM((ns, N_KV_HEADS, rows, n_keys), F32),
                        pltpu.VMEM((ns, N_KV_HEADS, rows, n_keys), BF16),
                        pltpu.VMEM((ns, N_KV_HEADS, rows, 1), F32)],
    )
    pages = ns * n_pages
    return pl.pallas_call(
        functools.partial(_sample_attn_kernel, n_pages=n_pages),
        grid_spec=grid_spec,
        out_shape=jax.ShapeDtypeStruct((nb, N_KV_HEADS, rows, HEAD_DIM), F32),
        compiler_params=_params("arbitrary"),
        name="sample_attention",
    )(page_table, q8, bias8, *([cache_k_t] * pages), *([cache_v_t] * pages), k_new_t, v_new_t)


def _ffn(h, wgu_ref, wd_ref):
    acc = jnp.zeros((h.shape[0], D_MODEL), F32)
    for c in range(D_FF // FFN_CHUNK):
        gate = _dot(h, wgu_ref[:, c * FFN_CHUNK:(c + 1) * FFN_CHUNK])
        up = _dot(h, wgu_ref[:, D_FF + c * FFN_CHUNK:D_FF + (c + 1) * FFN_CHUNK])
        a = _silu(gate) * up
        acc = acc + _dot(a.astype(BF16), wd_ref[c * FFN_CHUNK:(c + 1) * FFN_CHUNK, :])
    return acc


def _row_specs(nb, rows, per_row_mod):
    tm = min(ROW_TILE, rows)
    row = pl.BlockSpec((None, tm, D_MODEL), lambda b, j: (b, j, 0))
    if per_row_mod:
        mod = row
    else:
        mod = pl.BlockSpec((None, 1, D_MODEL), lambda b, j: (b, 0, 0))
    return tm, row, mod


def _attn_out_ffn_kernel(x_ref, o_ref, g1_ref, sc2_ref, sh2_ref, g2_ref, n2_ref,
                         wo_ref, wgu_ref, wd_ref, y_ref):
    y1 = x_ref[...] + g1_ref[...] * _dot(o_ref[...], wo_ref[...])
    h2 = _norm_mod(y1, n2_ref[...], sc2_ref[...], sh2_ref[...]).astype(BF16)
    y_ref[...] = y1 + g2_ref[...] * _ffn(h2, wgu_ref, wd_ref)


def _layer_spec(stacked, layer):
    _, rows, cols = stacked.shape
    return pl.BlockSpec((None, rows, cols), lambda *_: (layer, 0, 0), pipeline_mode=pl.Buffered(1))


def _attn_out_ffn(x, o, g1, sc2, sh2, g2, n2, wo, wgu, wd, layer):
    nb, rows, _ = x.shape
    tm, row, mod = _row_specs(nb, rows, g1.shape[1] == rows)
    return pl.pallas_call(
        _attn_out_ffn_kernel,
        grid=(nb, rows // tm),
        in_specs=[row, row, mod, mod, mod, mod, _const_spec((1, D_MODEL)),
                  _const_spec(wo.shape), _layer_spec(wgu, layer), _layer_spec(wd, layer)],
        out_specs=row,
        out_shape=jax.ShapeDtypeStruct(x.shape, F32),
        compiler_params=_params("arbitrary", "arbitrary"),
        name="attn_out_ffn",
    )(x, o, g1, sc2, sh2, g2, n2, wo, wgu, wd)


def _conv_layer_kernel(x_ref, sc1_ref, sh1_ref, g1_ref, sc2_ref, sh2_ref, g2_ref, p1_ref, p2_ref,
                       n1_ref, n2_ref, nf_ref, win_ref, ck_ref, wout_ref, wgu_ref, wd_ref,
                       y_ref, tail_ref, ubuf_ref, z_ref, *, seg):
    j = pl.program_id(1)
    tm = x_ref.shape[0]
    x = x_ref[...]
    h = _norm_mod(x, n1_ref[...], sc1_ref[...], sh1_ref[...]).astype(BF16)

    @pl.when(j == 0)
    def _():
        ubuf_ref[0:SUBLANES, :] = jnp.zeros((SUBLANES, D_MODEL), F32)

    @pl.when(j > 0)
    def _():
        ubuf_ref[0:SUBLANES, :] = ubuf_ref[tm:tm + SUBLANES, :]

    t = (j * tm + lax.broadcasted_iota(jnp.int32, (tm, 1), 0)) & (seg - 1)
    ck = ck_ref[...]
    cw = FFN_CHUNK
    for c in range(D_MODEL // cw):
        cols = slice(c * cw, (c + 1) * cw)
        bg = _dot(h, win_ref[:, c * cw:(c + 1) * cw])
        cg = _dot(h, win_ref[:, D_MODEL + c * cw:D_MODEL + (c + 1) * cw])
        hv = _dot(h, win_ref[:, 2 * D_MODEL + c * cw:2 * D_MODEL + (c + 1) * cw])
        u = cg * hv
        ubuf_ref[SUBLANES:SUBLANES + tm, cols] = u
        um1 = jnp.where(t >= 1, ubuf_ref[SUBLANES - 1:SUBLANES - 1 + tm, cols], p1_ref[:, cols])
        um2 = jnp.where(t >= 2, ubuf_ref[SUBLANES - 2:SUBLANES - 2 + tm, cols], p2_ref[:, cols])
        conv = ck[0:1, cols] * um2 + ck[1:2, cols] * um1 + ck[2:3, cols] * u
        z_ref[:, cols] = (bg * conv).astype(BF16)
    r = tail_ref.shape[0]
    tail_ref[...] = ubuf_ref[SUBLANES + tm - r:SUBLANES + tm, :]

    y1 = x + g1_ref[...] * _dot(z_ref[...], wout_ref[...])
    h2 = _norm_mod(y1, n2_ref[...], sc2_ref[...], sh2_ref[...]).astype(BF16)
    y2 = y1 + g2_ref[...] * _ffn(h2, wgu_ref, wd_ref)
    ms = jnp.mean(y2 * y2, axis=-1, keepdims=True)
    y_ref[...] = (y2 * lax.rsqrt(ms + EPS)) * nf_ref[...]


def _conv_layer(x, mods, p1, p2, n1, n2, nf, win, ck, wout, wgu, wd, *, layer, seg, full_tail):
    nb, rows, _ = x.shape
    per_row = mods[0].shape[1] == rows
    tm, row, mod = _row_specs(nb, rows, per_row)
    if per_row:
        prefix = row
    else:
        prefix = pl.BlockSpec((None, 1, D_MODEL), lambda b, j: (0, 0, 0))
    if full_tail:
        tail_spec, tail_rows = row, rows
    else:
        tail_spec, tail_rows = pl.BlockSpec((None, SUBLANES, D_MODEL), lambda b, j: (b, 0, 0)), SUBLANES
    vec = _const_spec((1, D_MODEL))
    return pl.pallas_call(
        functools.partial(_conv_layer_kernel, seg=seg),
        grid=(nb, rows // tm),
        in_specs=[row] + [mod] * 6 + [prefix, prefix, vec, vec, vec,
                                      _const_spec(win.shape), _const_spec(ck.shape), _const_spec(wout.shape),
                                      _layer_spec(wgu, layer), _layer_spec(wd, layer)],
        out_specs=(row, tail_spec),
        out_shape=(jax.ShapeDtypeStruct(x.shape, F32),
                   jax.ShapeDtypeStruct((nb, tail_rows, D_MODEL), F32)),
        scratch_shapes=[pltpu.VMEM((tm + SUBLANES, D_MODEL), F32), pltpu.VMEM((tm, D_MODEL), BF16)],
        compiler_params=_params("arbitrary", "arbitrary"),
        name="conv_layer",
    )(x, *mods, p1, p2, n1, n2, nf, win, ck, wout, wgu, wd)


def _rope_angles(pos):
    inv = ROPE_THETA ** (-jnp.arange(0, ROT_DIM, 2, dtype=F32) / ROT_DIM)
    ang = pos[:, None] * inv[None, :]
    return jnp.cos(ang), jnp.sin(ang)


def _rope_tables(pos):
    half = ROT_DIM // 2
    cos, sin = _rope_angles(pos)
    n = pos.shape[0]
    rest = HEAD_DIM - ROT_DIM
    c = jnp.concatenate([cos, cos, jnp.ones((n, rest), F32)], axis=1)
    s1 = jnp.concatenate([-sin, jnp.zeros((n, half + rest), F32)], axis=1)
    s2 = jnp.concatenate([jnp.zeros((n, half), F32), sin, jnp.zeros((n, rest), F32)], axis=1)
    return tuple(jnp.tile(a, (1, LANES // HEAD_DIM)) for a in (c, s1, s2))


def kernel(x_prompt, x_sample, c_prompt, c_sample, cache_k, cache_v, cache_kidx, state_conv, page_table,
           ada_w, ada_b, norm1_g, norm2_g, final_g, attn_w_in, attn_w_out, conv_w_in, conv_k, conv_w_out,
           ffn_w_gu, ffn_w_down):
    nb, t, _ = x_prompt.shape
    ns, tn, _ = x_sample.shape
    n_pages = page_table.shape[1]
    past = n_pages * PAGE_SIZE
    rows_s = ns * tn

    pad = (-(nb + ns)) % SUBLANES
    c_all = jnp.concatenate([c_prompt, c_sample, jnp.zeros((pad, D_MODEL), F32)], axis=0)
    mod = _modulation(c_all, ada_w, ada_b)

    def mods(layer):
        cols = [mod[layer, :, i * D_MODEL:(i + 1) * D_MODEL] for i in range(6)]
        prompt = [m[:nb, None, :] for m in cols]
        sample = [jnp.repeat(m[nb:nb + ns], tn, axis=0)[None] for m in cols]
        return prompt, sample

    vec = lambda a: a.reshape(1, D_MODEL)
    w_in = jnp.pad(attn_w_in[0], ((0, 0), (0, ATTN_IN_PAD - ATTN_IN))).astype(BF16)
    w_o = attn_w_out[0].astype(BF16)
    wgu = ffn_w_gu.astype(BF16)
    wd = ffn_w_down.astype(BF16)

    (sh1p, sc1p, g1p, sh2p, sc2p, g2p), (sh1s, sc1s, g1s, sh2s, sc2s, g2s) = mods(0)
    pos_p = jnp.arange(t, dtype=F32)
    rope_p = _rope_tables(pos_p)
    rope_pt = tuple(a.T for a in _rope_angles(pos_p))
    k_p, v_p, ki_p, qt, qit, kg, vt, kib, wit = _proj_prompt(
        x_prompt, sc1p, sh1p, vec(norm1_g[0]), w_in, rope_p, rope_pt)
    o_p = _prompt_attention(qt, qit, wit, kg, vt, kib)
    y_p = _attn_out_ffn(x_prompt, o_p, g1p, sc2p, sh2p, g2p, vec(norm2_g[0]), w_o, wgu, wd, 0)

    pos_s = jnp.tile(past + jnp.arange(tn, dtype=F32), ns)
    rope_s = _rope_tables(pos_s)
    xs = x_sample.reshape(rows_s, D_MODEL)
    proj = _proj_sample(xs, sc1s[0], sh1s[0], vec(norm1_g[0]), w_in, rope_s)
    o = 0
    q_s = proj[:, o:o + Q_COLS].reshape(ns, tn, N_KV_HEADS, HEADS_PER_KV, HEAD_DIM); o += Q_COLS
    k_s = proj[:, o:o + KV_COLS].reshape(ns, tn, KV_COLS); o += KV_COLS
    v_s = proj[:, o:o + KV_COLS].reshape(ns, tn, KV_COLS); o += KV_COLS
    qi_s = proj[:, o:o + IQ_COLS].reshape(ns, tn, IDX_HEADS, IDX_DIM); o += IQ_COLS
    ki_s = proj[:, o:o + IDX_DIM].reshape(ns, tn, IDX_DIM); o += IDX_DIM
    wi_s = proj[:, o:o + IDX_HEADS].reshape(ns, tn, IDX_HEADS)

    qpad = ((0, 0), (0, SUBLANES - tn))
    qi8 = jnp.pad(qi_s, qpad + ((0, 0), (0, 0))).transpose(0, 2, 1, 3)
    qi8 = qi8.reshape(ns, IDX_HEADS * SUBLANES, IDX_DIM).astype(BF16)
    w8 = jnp.pad(wi_s, qpad + ((0, 0),)).transpose(0, 2, 1).reshape(ns, IDX_HEADS * SUBLANES, 1)
    slot_pad = (0, PAGE_SIZE - tn)
    ki_new_t = jnp.pad(ki_s.transpose(0, 2, 1), ((0, 0), (0, 0), slot_pad))
    scores = _sample_scores(page_table, qi8, w8, cache_kidx[0].transpose(0, 2, 1), ki_new_t)
    n_keys = scores.shape[2]
    scores_t = scores[:, :tn, :].transpose(2, 0, 1).reshape(n_keys, rows_s)
    bias_t = _sample_select(scores_t, past, tn)
    bias8 = jnp.pad(bias_t.reshape(n_keys, ns, tn).transpose(1, 2, 0), qpad + ((0, 0),))

    q8 = jnp.pad(q_s * HEAD_DIM ** -0.5, qpad + ((0, 0), (0, 0), (0, 0))).transpose(0, 2, 3, 1, 4)
    q8 = q8.reshape(ns, N_KV_HEADS, HEADS_PER_KV * SUBLANES, HEAD_DIM).astype(BF16)
    new_t = lambda a: jnp.pad(a.reshape(ns, tn, N_KV_HEADS, HEAD_DIM).transpose(0, 2, 1, 3),
                              ((0, 0), (0, 0), (0, SUBLANES - tn), (0, LANES - HEAD_DIM)))
    o_g = _sample_attention(page_table, q8, bias8,
                            cache_k[0].transpose(0, 2, 3, 1), cache_v[0].transpose(0, 2, 3, 1),
                            new_t(k_s), new_t(v_s))
    o_g = o_g.reshape(ns, N_KV_HEADS, HEADS_PER_KV, SUBLANES, HEAD_DIM)[:, :, :, :tn]
    o_s = o_g.transpose(0, 3, 1, 2, 4).reshape(1, rows_s, Q_COLS).astype(BF16)
    y_s = _attn_out_ffn(xs[None], o_s, g1s, sc2s, sh2s, g2s, vec(norm2_g[0]), w_o, wgu, wd, 0)

    (sh1p, sc1p, g1p, sh2p, sc2p, g2p), (sh1s, sc1s, g1s, sh2s, sc2s, g2s) = mods(1)
    win = conv_w_in[0].astype(BF16)
    wout = conv_w_out[0].astype(BF16)
    zero_prefix = jnp.zeros((1, 1, D_MODEL), F32)
    out_p, tail_p = _conv_layer(
        y_p, (sc1p, sh1p, g1p, sc2p, sh2p, g2p), zero_prefix, zero_prefix,
        vec(norm1_g[1]), vec(norm2_g[1]), vec(final_g), win, conv_k[0], wout, wgu, wd,
        layer=1, seg=t, full_tail=False)
    st = state_conv[0]
    zeros_row = jnp.zeros((ns, 1, D_MODEL), F32)
    p1 = jnp.concatenate([st[:, 1:2], zeros_row, zeros_row, zeros_row], axis=1).reshape(1, rows_s, D_MODEL)
    p2 = jnp.concatenate([st[:, 0:1], st[:, 1:2], zeros_row, zeros_row], axis=1).reshape(1, rows_s, D_MODEL)
    out_s, u_s = _conv_layer(
        y_s, (sc1s, sh1s, g1s, sc2s, sh2s, g2s), p1, p2,
        vec(norm1_g[1]), vec(norm2_g[1]), vec(final_g), win, conv_k[0], wout, wgu, wd,
        layer=1, seg=tn, full_tail=True)

    keep = CONV_WIDTH - 1
    return (
        out_p,
        out_s.reshape(ns, tn, D_MODEL),
        k_p.reshape(1, nb, t, N_KV_HEADS, HEAD_DIM),
        v_p.reshape(1, nb, t, N_KV_HEADS, HEAD_DIM),
        ki_p.transpose(0, 2, 1)[None],
        tail_p[None, :, SUBLANES - keep:, :],
        k_s.reshape(1, ns, tn, N_KV_HEADS, HEAD_DIM),
        v_s.reshape(1, ns, tn, N_KV_HEADS, HEAD_DIM),
        ki_s[None],
        u_s.reshape(ns, tn, D_MODEL)[None, :, tn - keep:, :],
    )
```

```python
import functools

import jax
import jax.numpy as jnp
from jax import lax
from jax.experimental import pallas as pl
from jax.experimental.pallas import tpu as pltpu

F32 = jnp.float32
BF16 = jnp.bfloat16

D_MODEL = 1024
N_HEADS = 16
HEAD_DIM = 64
N_KV_HEADS = 4
HEADS_PER_KV = N_HEADS // N_KV_HEADS
ROT_DIM = 16
ROPE_THETA = 500000.0
IDX_HEADS = 8
IDX_DIM = 64
TOPK = 256
PAGE_SIZE = 128
CONV_WIDTH = 3
D_FF = 2816
EPS = 1e-6
Q_COLS = N_HEADS * HEAD_DIM
KV_COLS = N_KV_HEADS * HEAD_DIM
IQ_COLS = IDX_HEADS * IDX_DIM
ATTN_IN = Q_COLS + 2 * KV_COLS + IQ_COLS + IDX_DIM + IDX_HEADS
LANES = 128
SUBLANES = 8
BF16_SUBLANES = 16
ATTN_IN_PAD = -(-ATTN_IN // LANES) * LANES
MASK_BIAS = -1e30
F32_MAX = 3.4028234663852886e38
LOG2_E = 1.4426950408889634
VMEM_LIMIT = 60 * 1024 * 1024

ROW_TILE = 512
Q_TILE = 256
SCORE_CHUNK = 512
SCORE_SLAB = 128
ATTN_CHUNK = 256
V_ROWS = HEAD_DIM + BF16_SUBLANES
FFN_CHUNK = 256
MOD_TILE = 1024
SCORE_SEQS = 4
ATTN_SEQS = 2


def _dot(a, b):
    return jnp.dot(a, b, preferred_element_type=F32)


def _dot_nt(a, b):
    return lax.dot_general(a, b, (((1,), (1,)), ((), ())), preferred_element_type=F32)


def _silu(x):
    return x / (1.0 + jnp.exp(-x))


def _norm_mod(x, g, sc, sh):
    ms = jnp.mean(x * x, axis=-1, keepdims=True)
    y = x * lax.rsqrt(ms + EPS)
    return (y * g) * (1.0 + sc) + sh


def _params(*sem):
    return pltpu.CompilerParams(dimension_semantics=sem, vmem_limit_bytes=VMEM_LIMIT)


def _const_spec(shape):
    nd = len(shape)
    return pl.BlockSpec(shape, lambda *_: (0,) * nd, pipeline_mode=pl.Buffered(1))


def _mod_kernel(c_ref, w_ref, b_ref, o_ref):
    s = _silu(c_ref[...]).astype(BF16)
    o_ref[...] = _dot(s, w_ref[...].astype(BF16)) + b_ref[...]


def _modulation(c_all, ada_w, ada_b):
    depth, _, n = ada_w.shape
    rows = c_all.shape[0]
    tn = MOD_TILE
    return pl.pallas_call(
        _mod_kernel,
        grid=(depth, n // tn),
        in_specs=[
            pl.BlockSpec((rows, D_MODEL), lambda l, j: (0, 0)),
            pl.BlockSpec((None, D_MODEL, tn), lambda l, j: (l, 0, j)),
            pl.BlockSpec((None, 1, tn), lambda l, j: (l, 0, j)),
        ],
        out_specs=pl.BlockSpec((None, rows, tn), lambda l, j: (l, 0, j)),
        out_shape=jax.ShapeDtypeStruct((depth, rows, n), F32),
        compiler_params=_params("arbitrary", "arbitrary"),
        name="adaln_modulation",
    )(c_all, ada_w, ada_b.reshape(depth, 1, n))


def _rope_block(x, c, s1, s2):
    return x * c + pltpu.roll(x, LANES - ROT_DIM // 2, 1) * s1 + pltpu.roll(x, ROT_DIM // 2, 1) * s2


def _rope_cols(p, c, s1, s2):
    nblk = p.shape[1] // LANES
    return [_rope_block(p[:, j * LANES:(j + 1) * LANES], c, s1, s2) for j in range(nblk)]


def _last_block_tables(c, s1, s2):
    lane = lax.broadcasted_iota(jnp.int32, c.shape, 1)
    is_key = lane < IDX_DIM
    return jnp.where(is_key, c, 1.0), jnp.where(is_key, s1, 0.0), jnp.where(is_key, s2, 0.0)


def _rope_rows(xt, cos_t, sin_t):
    half = ROT_DIM // 2
    out = []
    for base in range(0, LANES, HEAD_DIM):
        x1, x2 = xt[base:base + half], xt[base + half:base + ROT_DIM]
        out += [x1 * cos_t - x2 * sin_t, x2 * cos_t + x1 * sin_t, xt[base + ROT_DIM:base + HEAD_DIM]]
    return jnp.concatenate(out, axis=0)


def _proj_prompt_kernel(x_ref, sc_ref, sh_ref, g_ref, w_ref, c_ref, s1_ref, s2_ref, ct_ref, st_ref,
                        k_ref, v_ref, kit_ref, qt_ref, qit_ref, kg_ref, vt_ref, kib_ref, wit_ref):
    h = _norm_mod(x_ref[...], g_ref[...], sc_ref[...], sh_ref[...]).astype(BF16)
    c, s1, s2 = c_ref[...], s1_ref[...], s2_ref[...]
    cos_t, sin_t = ct_ref[...], st_ref[...]

    pq = _dot(h, w_ref[:, 0:Q_COLS])
    for j in range(Q_COLS // LANES):
        qt = _rope_rows(pq[:, j * LANES:(j + 1) * LANES].T, cos_t, sin_t)
        qt_ref[j * LANES:(j + 1) * LANES, :] = (qt * (HEAD_DIM ** -0.5 * LOG2_E)).astype(BF16)

    pk = _dot(h, w_ref[:, Q_COLS:Q_COLS + KV_COLS])
    k = jnp.concatenate(_rope_cols(pk, c, s1, s2), axis=1)
    k_ref[...] = k
    for g in range(N_KV_HEADS):
        kg_ref[g] = k[:, g * HEAD_DIM:(g + 1) * HEAD_DIM].astype(BF16)

    pv = _dot(h, w_ref[:, Q_COLS + KV_COLS:Q_COLS + 2 * KV_COLS])
    v_ref[...] = pv
    ones = jnp.ones((V_ROWS - HEAD_DIM, ATTN_CHUNK), BF16)
    for cc in range(pv.shape[0] // ATTN_CHUNK):
        vt = pv[cc * ATTN_CHUNK:(cc + 1) * ATTN_CHUNK, :].astype(BF16).T
        for g in range(N_KV_HEADS):
            vt_ref[cc, g * V_ROWS:g * V_ROWS + HEAD_DIM, :] = vt[g * HEAD_DIM:(g + 1) * HEAD_DIM, :]
            vt_ref[cc, g * V_ROWS + HEAD_DIM:(g + 1) * V_ROWS, :] = ones

    o = Q_COLS + 2 * KV_COLS
    pqi = _dot(h, w_ref[:, o:o + IQ_COLS])
    for j in range(IQ_COLS // LANES):
        qit = _rope_rows(pqi[:, j * LANES:(j + 1) * LANES].T, cos_t, sin_t)
        qit_ref[j * LANES:(j + 1) * LANES, :] = qit.astype(BF16)

    pl_ = _dot(h, w_ref[:, o + IQ_COLS:ATTN_IN_PAD])
    last = _rope_block(pl_, *_last_block_tables(c, s1, s2))
    last_t = last.T
    kit_ref[...] = last_t[0:IDX_DIM, :]
    kib_ref[...] = last[:, 0:IDX_DIM].astype(BF16)
    wit_ref[...] = last_t[IDX_DIM:IDX_DIM + IDX_HEADS, :]


def _proj_prompt(x, sc, sh, g, w, rope, rope_t):
    nb, t, _ = x.shape
    tm = ROW_TILE
    row = lambda width: pl.BlockSpec((None, tm, width), lambda b, j: (b, j, 0))
    mod = pl.BlockSpec((None, 1, D_MODEL), lambda b, j: (b, 0, 0))
    tab = pl.BlockSpec((tm, LANES), lambda b, j: (j, 0))
    tab_t = pl.BlockSpec((ROT_DIM // 2, tm), lambda b, j: (0, j))
    col = lambda height: pl.BlockSpec((None, height, tm), lambda b, j: (b, 0, j))
    out_shape = (
        jax.ShapeDtypeStruct((nb, t, KV_COLS), F32),
        jax.ShapeDtypeStruct((nb, t, KV_COLS), F32),
        jax.ShapeDtypeStruct((nb, IDX_DIM, t), F32),
        jax.ShapeDtypeStruct((nb, Q_COLS, t), BF16),
        jax.ShapeDtypeStruct((nb, IQ_COLS, t), BF16),
        jax.ShapeDtypeStruct((nb, N_KV_HEADS, t, HEAD_DIM), BF16),
        jax.ShapeDtypeStruct((nb, t // ATTN_CHUNK, N_KV_HEADS * V_ROWS, ATTN_CHUNK), BF16),
        jax.ShapeDtypeStruct((nb, t, IDX_DIM), BF16),
        jax.ShapeDtypeStruct((nb, IDX_HEADS, t), F32),
    )
    out_specs = (
        row(KV_COLS), row(KV_COLS), col(IDX_DIM), col(Q_COLS), col(IQ_COLS),
        pl.BlockSpec((None, N_KV_HEADS, tm, HEAD_DIM), lambda b, j: (b, 0, j, 0)),
        pl.BlockSpec((None, tm // ATTN_CHUNK, N_KV_HEADS * V_ROWS, ATTN_CHUNK), lambda b, j: (b, j, 0, 0)),
        row(IDX_DIM), col(IDX_HEADS),
    )
    return pl.pallas_call(
        _proj_prompt_kernel,
        grid=(nb, t // tm),
        in_specs=[row(D_MODEL), mod, mod, _const_spec((1, D_MODEL)),
                  _const_spec((D_MODEL, ATTN_IN_PAD)), tab, tab, tab, tab_t, tab_t],
        out_specs=out_specs,
        out_shape=out_shape,
        compiler_params=_params("arbitrary", "arbitrary"),
        name="attn_proj_prompt",
    )(x, sc, sh, g, w, *rope, *rope_t)


def _proj_sample_kernel(x_ref, sc_ref, sh_ref, g_ref, w_ref, c_ref, s1_ref, s2_ref, p_ref):
    h = _norm_mod(x_ref[...], g_ref[...], sc_ref[...], sh_ref[...]).astype(BF16)
    c, s1, s2 = c_ref[...], s1_ref[...], s2_ref[...]
    p = _dot(h, w_ref[...])
    v_lo = (Q_COLS + KV_COLS) // LANES
    v_hi = (Q_COLS + 2 * KV_COLS) // LANES
    nblk = ATTN_IN_PAD // LANES
    for j in range(nblk):
        blk = p[:, j * LANES:(j + 1) * LANES]
        if j == nblk - 1:
            blk = _rope_block(blk, *_last_block_tables(c, s1, s2))
        elif not (v_lo <= j < v_hi):
            blk = _rope_block(blk, c, s1, s2)
        p_ref[:, j * LANES:(j + 1) * LANES] = blk


def _proj_sample(x, sc, sh, g, w, rope):
    rows = x.shape[0]
    full = lambda width: pl.BlockSpec((rows, width), lambda i: (0, 0))
    return pl.pallas_call(
        _proj_sample_kernel,
        grid=(1,),
        in_specs=[full(D_MODEL), full(D_MODEL), full(D_MODEL),
                  pl.BlockSpec((1, D_MODEL), lambda i: (0, 0)),
                  pl.BlockSpec((D_MODEL, ATTN_IN_PAD), lambda i: (0, 0)),
                  full(LANES), full(LANES), full(LANES)],
        out_specs=full(ATTN_IN_PAD),
        out_shape=jax.ShapeDtypeStruct((rows, ATTN_IN_PAD), F32),
        compiler_params=_params("arbitrary"),
        name="attn_proj_sample",
    )(x, sc, sh, g, w, *rope)


FOLD_ROWS = 32
BISECT_STEPS_FIRST = 14
BISECT_STEPS_PER_TEST = 3


def _fold(x, op):
    rows, tq = x.shape
    x3 = x.reshape(rows // FOLD_ROWS, FOLD_ROWS, tq)
    return {"sum": x3.sum, "max": x3.max, "min": x3.min}[op](axis=0)


def _stats_init(tq):
    zeros = jnp.zeros((FOLD_ROWS, tq), F32)
    return (jnp.full((FOLD_ROWS, tq), -jnp.inf, F32), jnp.full((FOLD_ROWS, tq), jnp.inf, F32),
            zeros, zeros, zeros)


def _stats_update(stats, s):
    mx, mn, n_above, n_ge0, n_gt0 = stats
    ones_where = lambda m: _fold(jnp.where(m, 1.0, 0.0), "sum")
    above = s > -jnp.inf
    return (jnp.maximum(mx, _fold(s, "max")),
            jnp.minimum(mn, _fold(jnp.where(above, s, jnp.inf), "min")),
            n_above + ones_where(above), n_ge0 + ones_where(s >= 0.0), n_gt0 + ones_where(s > 0.0))


def _select_to_bias(sc_ref, tie_ref, qpos, nk, kc, stats):
    tq = sc_ref.shape[1]
    topk = float(TOPK)

    slab = 4 * FOLD_ROWS

    def for_slabs(fn, init, slab=slab):
        def body(c, carry):
            for j in range(kc // slab):
                start = pl.multiple_of(c * kc + j * slab, slab)
                idx = start + lax.broadcasted_iota(jnp.int32, (slab, tq), 0)
                carry = fn(pl.ds(start, slab), idx, carry)
            return carry
        return lax.fori_loop(0, nk, body, init)

    def ones_where(m):
        return _fold(jnp.where(m, 1.0, 0.0), "sum")

    def count(pred):
        acc = for_slabs(lambda rows, idx, acc: acc + ones_where(pred(rows)), jnp.zeros((FOLD_ROWS, tq), F32))
        return acc.sum(axis=0, keepdims=True)

    def count_ge(t):
        return count(lambda rows: sc_ref[rows, :] >= t)

    mx, mn, n_above, n_ge0, n_gt0 = stats
    mx = mx.max(axis=0, keepdims=True)
    mn = mn.min(axis=0, keepdims=True)
    n_above = n_above.sum(axis=0, keepdims=True)
    n_ge0 = n_ge0.sum(axis=0, keepdims=True)
    n_gt0 = n_gt0.sum(axis=0, keepdims=True)
    n_max = count_ge(mx)

    few = n_above < topk
    flat = n_max >= topk
    zero = (n_gt0 < topk) & (n_ge0 >= topk)
    pos = n_gt0 >= topk
    lo = jnp.where(few, -jnp.inf, jnp.where(flat, mx, jnp.where(zero | pos, 0.0, mn)))
    hi = jnp.where(few, mn, jnp.where(flat, jnp.inf, jnp.where(pos, mx, 0.0)))
    n_lo = jnp.where(few, 2.0 * topk, jnp.where(flat, n_max, jnp.where(zero | pos, n_ge0, n_above)))
    n_hi = jnp.where(few, n_above, jnp.where(flat, 0.0, jnp.where(zero, n_gt0, jnp.where(pos, n_max, n_ge0))))
    done = jnp.where(few | flat | zero | (n_lo == topk), 1.0, 0.0)

    def not_finished(d):
        return (jnp.min(d) < 0.5).astype(jnp.int32)

    def bisect_step(state):
        lo, hi, n_lo, n_hi, done = state
        mid = jnp.clip(0.5 * lo + 0.5 * hi, -F32_MAX, F32_MAX)
        stuck = (mid <= lo) | (mid >= hi)
        n_mid = count_ge(mid)
        move = (done < 0.5) & jnp.logical_not(stuck)
        up = move & (n_mid >= topk)
        down = move & (n_mid < topk)
        lo = jnp.where(up, mid, lo)
        n_lo = jnp.where(up, n_mid, n_lo)
        hi = jnp.where(down, mid, hi)
        n_hi = jnp.where(down, n_mid, n_hi)
        done = jnp.where(stuck | (n_lo == topk), 1.0, done)
        return lo, hi, n_lo, n_hi, done

    def bisect(carry):
        state = lax.fori_loop(0, carry[6], lambda _, st: bisect_step(st), carry[:5])
        return (*state, not_finished(state[4]), jnp.int32(BISECT_STEPS_PER_TEST))

    lo, hi, n_lo, n_hi, done, _, _ = lax.while_loop(
        lambda carry: carry[5] > 0, bisect,
        (lo, hi, n_lo, n_hi, done, not_finished(done), jnp.int32(BISECT_STEPS_FIRST)))

    thr = lo
    exact = n_lo == topk
    need = topk - n_hi
    all_keys = jnp.full((1, tq), 2 ** 30, jnp.int32)

    def tie_break():
        def mark(rows, idx, _):
            tie_ref[rows, :] = jnp.where(sc_ref[rows, :] == thr, idx, 2 ** 30)
            return 0
        for_slabs(mark, 0)

        n_bits = (sc_ref.shape[0] - 1).bit_length()

        def step(i, j):
            cand = j + lax.shift_right_logical(jnp.int32(1 << (n_bits - 1)), i)
            n_before = count(lambda rows: tie_ref[rows, :] < cand)
            return jnp.where(n_before < need, cand, j)
        return lax.fori_loop(0, n_bits, step, jnp.zeros((1, tq), jnp.int32))

    any_tie = jnp.min(jnp.where(exact, 1.0, 0.0)) < 0.5

    def write(keep_fn):
        def body(rows, idx, _):
            sc_ref[rows, :] = jnp.where(keep_fn(sc_ref[rows, :], idx), 0.0, MASK_BIAS)
            return 0
        return for_slabs(body, 0, slab=kc)

    def write_with_ties():
        last_tied = jnp.minimum(jnp.where(exact, all_keys, tie_break()), qpos)
        return write(lambda s, idx: (s > thr) | ((s == thr) & (idx <= last_tied)))

    lax.cond(any_tie, write_with_ties, lambda: write(lambda s, idx: s >= thr))


def _prompt_attn_kernel(qt_ref, qit_ref, wit_ref, kg_ref, vt_ref, kib_ref, o_ref,
                        sc_ref, tie_ref, acc_ref, m_ref, mx_ref, s_ref):
    i = pl.program_id(1)
    tq = qt_ref.shape[1]
    qpos = i * tq + lax.broadcasted_iota(jnp.int32, (1, tq), 1)
    n_keys = (i + 1) * tq

    n_score = (n_keys + SCORE_CHUNK - 1) // SCORE_CHUNK

    def score_chunk(c, stats):
        for j in range(SCORE_CHUNK // SCORE_SLAB):
            start = pl.multiple_of(c * SCORE_CHUNK + j * SCORE_SLAB, SCORE_SLAB)
            kc = kib_ref[pl.ds(start, SCORE_SLAB), :]
            acc = jnp.zeros((SCORE_SLAB, tq), F32)
            for h in range(IDX_HEADS):
                d = _dot(kc, qit_ref[h * IDX_DIM:(h + 1) * IDX_DIM, :])
                acc = acc + jnp.maximum(d, 0.0) * wit_ref[h:h + 1, :]
            idx = start + lax.broadcasted_iota(jnp.int32, (SCORE_SLAB, tq), 0)
            s = jnp.where(idx <= qpos, acc, -jnp.inf)
            sc_ref[pl.ds(start, SCORE_SLAB), :] = s
            stats = _stats_update(stats, s)
        return stats

    stats = lax.fori_loop(0, n_score, score_chunk, _stats_init(tq))

    _select_to_bias(sc_ref, tie_ref, qpos, n_score, SCORE_CHUNK, stats)

    m_ref[...] = jnp.full(m_ref.shape, MASK_BIAS, F32)
    acc_ref[...] = jnp.zeros(acc_ref.shape, F32)

    def attn_chunk(c, _):
        start = pl.multiple_of(c * ATTN_CHUNK, ATTN_CHUNK)
        for h in range(N_HEADS):
            s = _dot(kg_ref[h // HEADS_PER_KV, pl.ds(start, ATTN_CHUNK), :],
                     qt_ref[h * HEAD_DIM:(h + 1) * HEAD_DIM, :])
            s = s + sc_ref[pl.ds(start, ATTN_CHUNK), :]
            s_ref[h] = s
            mx_ref[h] = s.max(axis=0, keepdims=True)
        for h in range(N_HEADS):
            g = h // HEADS_PER_KV
            m_old = m_ref[h]
            m_new = jnp.maximum(m_old, mx_ref[h])
            alpha = jnp.exp2(m_old - m_new)
            p = jnp.exp2(s_ref[h] - m_new).astype(BF16)
            vt = vt_ref[c, g * V_ROWS:(g + 1) * V_ROWS, :]
            acc_ref[h] = alpha * acc_ref[h] + _dot(vt, p)
            m_ref[h] = m_new
        return 0

    lax.fori_loop(0, (i + 1) * (tq // ATTN_CHUNK), attn_chunk, 0)

    for pair in range(N_HEADS // 2):
        two = []
        for h in (2 * pair, 2 * pair + 1):
            a = acc_ref[h]
            two.append(a[0:HEAD_DIM, :] / a[HEAD_DIM:HEAD_DIM + 1, :])
        col = 2 * pair * HEAD_DIM
        o_ref[:, col:col + 2 * HEAD_DIM] = jnp.concatenate(two, axis=0).T.astype(BF16)


def _prompt_attention(qt, qit, wit, kg, vt, kib):
    nb, _, t = qt.shape
    tq = Q_TILE
    col = lambda height: pl.BlockSpec((None, height, tq), lambda b, i: (b, 0, i))
    return pl.pallas_call(
        _prompt_attn_kernel,
        grid=(nb, t // tq),
        in_specs=[
            col(Q_COLS), col(IQ_COLS), col(IDX_HEADS),
            pl.BlockSpec((None, N_KV_HEADS, t, HEAD_DIM), lambda b, i: (b, 0, 0, 0)),
            pl.BlockSpec((None, t // ATTN_CHUNK, N_KV_HEADS * V_ROWS, ATTN_CHUNK), lambda b, i: (b, 0, 0, 0)),
            pl.BlockSpec((None, t, IDX_DIM), lambda b, i: (b, 0, 0)),
        ],
        out_specs=pl.BlockSpec((None, tq, Q_COLS), lambda b, i: (b, i, 0)),
        out_shape=jax.ShapeDtypeStruct((nb, t, Q_COLS), BF16),
        scratch_shapes=[
            pltpu.VMEM((t, tq), F32),
            pltpu.VMEM((t, tq), jnp.int32),
            pltpu.VMEM((N_HEADS, V_ROWS, tq), F32),
            pltpu.VMEM((N_HEADS, 1, tq), F32),
            pltpu.VMEM((N_HEADS, 1, tq), F32),
            pltpu.VMEM((N_HEADS, ATTN_CHUNK, tq), F32),
        ],
        compiler_params=_params("arbitrary", "arbitrary"),
        name="prompt_attention",
    )(qt, qit, wit, kg, vt, kib)


def _sample_score_kernel(pt_ref, qi_ref, w_ref, *rest, n_pages):
    del pt_ref
    n_seq = qi_ref.shape[0]
    pages, (new_ref, o_ref, kall_ref) = rest[:n_seq * n_pages], rest[n_seq * n_pages:]
    for q in range(n_seq):
        for j in range(n_pages):
            kall_ref[q, :, j * PAGE_SIZE:(j + 1) * PAGE_SIZE] = pages[q * n_pages + j][...].astype(BF16)
        kall_ref[q, :, n_pages * PAGE_SIZE:(n_pages + 1) * PAGE_SIZE] = new_ref[q].astype(BF16)
    for q in range(n_seq):
        d = _dot(qi_ref[q], kall_ref[q])
        r = jnp.maximum(d, 0.0) * w_ref[q]
        o_ref[q] = r.reshape(IDX_HEADS, SUBLANES, r.shape[1]).sum(axis=0)


def _sample_scores(page_table, qi8, w8, cache_ki_t, ki_new_t):
    nb, n_pages = page_table.shape
    n_keys = (n_pages + 1) * PAGE_SIZE
    rows = IDX_HEADS * SUBLANES
    ns = SCORE_SEQS
    per_step = lambda r, w: pl.BlockSpec((ns, r, w), lambda b, pt: (b, 0, 0))
    grid_spec = pltpu.PrefetchScalarGridSpec(
        num_scalar_prefetch=1,
        grid=(nb // ns,),
        in_specs=[per_step(rows, IDX_DIM), per_step(rows, 1)]
        + [pl.BlockSpec((None, IDX_DIM, PAGE_SIZE), lambda b, pt, q=q, j=j: (pt[b * ns + q, j], 0, 0))
           for q in range(ns) for j in range(n_pages)]
        + [per_step(IDX_DIM, PAGE_SIZE)],
        out_specs=per_step(SUBLANES, n_keys),
        scratch_shapes=[pltpu.VMEM((ns, IDX_DIM, n_keys), BF16)],
    )
    return pl.pallas_call(
        functools.partial(_sample_score_kernel, n_pages=n_pages),
        grid_spec=grid_spec,
        out_shape=jax.ShapeDtypeStruct((nb, SUBLANES, n_keys), F32),
        compiler_params=_params("arbitrary"),
        name="sample_indexer_scores",
    )(page_table, qi8, w8, *([cache_ki_t] * (ns * n_pages)), ki_new_t)


def _sample_select_kernel(s_ref, o_ref, sc_ref, tie_ref, *, past, period):
    tq = s_ref.shape[1]
    n_keys = s_ref.shape[0]
    lane = lax.broadcasted_iota(jnp.int32, (1, tq), 1)
    qpos = past + (lane & (period - 1))
    nk = n_keys // PAGE_SIZE
    stats = _stats_init(tq)
    for c in range(nk):
        idx = c * PAGE_SIZE + lax.broadcasted_iota(jnp.int32, (PAGE_SIZE, tq), 0)
        s = jnp.where(idx <= qpos, s_ref[c * PAGE_SIZE:(c + 1) * PAGE_SIZE, :], -jnp.inf)
        sc_ref[c * PAGE_SIZE:(c + 1) * PAGE_SIZE, :] = s
        stats = _stats_update(stats, s)
    _select_to_bias(sc_ref, tie_ref, qpos, nk, PAGE_SIZE, stats)
    o_ref[...] = sc_ref[...]


def _sample_select(scores_t, past, period):
    n_keys, nq = scores_t.shape
    tq = Q_TILE
    blk = pl.BlockSpec((n_keys, tq), lambda i: (0, i))
    return pl.pallas_call(
        functools.partial(_sample_select_kernel, past=past, period=period),
        grid=(nq // tq,),
        in_specs=[blk],
        out_specs=blk,
        out_shape=jax.ShapeDtypeStruct((n_keys, nq), F32),
        scratch_shapes=[pltpu.VMEM((n_keys, tq), F32), pltpu.VMEM((n_keys, tq), jnp.int32)],
        compiler_params=_params("arbitrary"),
        name="sample_select",
    )(scores_t)


def _sample_attn_kernel(pt_ref, q_ref, bias_ref, *rest, n_pages):
    del pt_ref
    n_seq = q_ref.shape[0]
    kpages, vpages = rest[:n_seq * n_pages], rest[n_seq * n_pages:2 * n_seq * n_pages]
    knew_ref, vnew_ref, o_ref, kall_ref, vall_ref, s_ref, p_ref, l_ref = rest[2 * n_seq * n_pages:]
    n_keys = bias_ref.shape[2]
    fill = jnp.zeros((PAGE_SIZE - SUBLANES, LANES), F32)
    for q in range(n_seq):
        for j in range(n_pages):
            kall_ref[q, :, :, j * PAGE_SIZE:(j + 1) * PAGE_SIZE] = kpages[q * n_pages + j][...].astype(BF16)
            vall_ref[q, :, :, j * PAGE_SIZE:(j + 1) * PAGE_SIZE] = vpages[q * n_pages + j][...].astype(BF16)
        for g in range(N_KV_HEADS):
            for new_ref, all_ref in ((knew_ref, kall_ref), (vnew_ref, vall_ref)):
                page = jnp.concatenate([new_ref[q, g], fill], axis=0).T
                all_ref[q, g, :, n_pages * PAGE_SIZE:(n_pages + 1) * PAGE_SIZE] = (
                    page[0:HEAD_DIM, :].astype(BF16))
    for q in range(n_seq):
        bias = jnp.broadcast_to(bias_ref[q][None], (HEADS_PER_KV, SUBLANES, n_keys)).reshape(
            HEADS_PER_KV * SUBLANES, n_keys)
        for g in range(N_KV_HEADS):
            s_ref[q, g] = _dot(q_ref[q, g], kall_ref[q, g]) + bias
    for q in range(n_seq):
        for g in range(N_KV_HEADS):
            s = s_ref[q, g]
            m = s.max(axis=1, keepdims=True)
            p = jnp.exp(s - m)
            l_ref[q, g] = p.sum(axis=1, keepdims=True)
            p_ref[q, g] = p.astype(BF16)
    for q in range(n_seq):
        for g in range(N_KV_HEADS):
            o_ref[q, g] = _dot_nt(p_ref[q, g], vall_ref[q, g]) / l_ref[q, g]


def _sample_attention(page_table, q8, bias8, cache_k_t, cache_v_t, k_new_t, v_new_t):
    nb, n_pages = page_table.shape
    n_keys = (n_pages + 1) * PAGE_SIZE
    rows = HEADS_PER_KV * SUBLANES
    ns = ATTN_SEQS
    page = [pl.BlockSpec((None, N_KV_HEADS, HEAD_DIM, PAGE_SIZE),
                         lambda b, pt, q=q, j=j: (pt[b * ns + q, j], 0, 0, 0))
            for q in range(ns) for j in range(n_pages)]
    new = pl.BlockSpec((ns, N_KV_HEADS, SUBLANES, LANES), lambda b, pt: (b, 0, 0, 0))
    qo = pl.BlockSpec((ns, N_KV_HEADS, rows, HEAD_DIM), lambda b, pt: (b, 0, 0, 0))
    grid_spec = pltpu.PrefetchScalarGridSpec(
        num_scalar_prefetch=1,
        grid=(nb // ns,),
        in_specs=[qo, pl.BlockSpec((ns, SUBLANES, n_keys), lambda b, pt: (b, 0, 0))]
        + page + page + [new, new],
        out_specs=qo,
        scratch_shapes=[pltpu.VMEM((ns, N_KV_HEADS, HEAD_DIM, n_keys), BF16),
                        pltpu.VMEM((ns, N_KV_HEADS, HEAD_DIM, n_keys), BF16),
                        pltpu.VMEM((ns, N_KV_HEADS, rows, n_keys), F32),
                        pltpu.VMEM((ns, N_KV_HEADS, rows, n_keys), BF16),
                        pltpu.VMEM((ns, N_KV_HEADS, rows, 1), F32)],
    )
    pages = ns * n_pages
    return pl.pallas_call(
        functools.partial(_sample_attn_kernel, n_pages=n_pages),
        grid_spec=grid_spec,
        out_shape=jax.ShapeDtypeStruct((nb, N_KV_HEADS, rows, HEAD_DIM), F32),
        compiler_params=_params("arbitrary"),
        name="sample_attention",
    )(page_table, q8, bias8, *([cache_k_t] * pages), *([cache_v_t] * pages), k_new_t, v_new_t)


def _ffn(h, wgu_ref, wd_ref):
    acc = jnp.zeros((h.shape[0], D_MODEL), F32)
    for c in range(D_FF // FFN_CHUNK):
        gate = _dot(h, wgu_ref[:, c * FFN_CHUNK:(c + 1) * FFN_CHUNK])
        up = _dot(h, wgu_ref[:, D_FF + c * FFN_CHUNK:D_FF + (c + 1) * FFN_CHUNK])
        a = _silu(gate) * up
        acc = acc + _dot(a.astype(BF16), wd_ref[c * FFN_CHUNK:(c + 1) * FFN_CHUNK, :])
    return acc


def _row_specs(nb, rows, per_row_mod):
    tm = min(ROW_TILE, rows)
    row = pl.BlockSpec((None, tm, D_MODEL), lambda b, j: (b, j, 0))
    if per_row_mod:
        mod = row
    else:
        mod = pl.BlockSpec((None, 1, D_MODEL), lambda b, j: (b, 0, 0))
    return tm, row, mod


def _attn_out_ffn_kernel(x_ref, o_ref, g1_ref, sc2_ref, sh2_ref, g2_ref, n2_ref,
                         wo_ref, wgu_ref, wd_ref, y_ref):
    y1 = x_ref[...] + g1_ref[...] * _dot(o_ref[...], wo_ref[...])
    h2 = _norm_mod(y1, n2_ref[...], sc2_ref[...], sh2_ref[...]).astype(BF16)
    y_ref[...] = y1 + g2_ref[...] * _ffn(h2, wgu_ref, wd_ref)


def _layer_spec(stacked, layer):
    _, rows, cols = stacked.shape
    return pl.BlockSpec((None, rows, cols), lambda *_: (layer, 0, 0), pipeline_mode=pl.Buffered(1))


def _attn_out_ffn(x, o, g1, sc2, sh2, g2, n2, wo, wgu, wd, layer):
    nb, rows, _ = x.shape
    tm, row, mod = _row_specs(nb, rows, g1.shape[1] == rows)
    return pl.pallas_call(
        _attn_out_ffn_kernel,
        grid=(nb, rows // tm),
        in_specs=[row, row, mod, mod, mod, mod, _const_spec((1, D_MODEL)),
                  _const_spec(wo.shape), _layer_spec(wgu, layer), _layer_spec(wd, layer)],
        out_specs=row,
        out_shape=jax.ShapeDtypeStruct(x.shape, F32),
        compiler_params=_params("arbitrary", "arbitrary"),
        name="attn_out_ffn",
    )(x, o, g1, sc2, sh2, g2, n2, wo, wgu, wd)


def _conv_layer_kernel(x_ref, sc1_ref, sh1_ref, g1_ref, sc2_ref, sh2_ref, g2_ref, p1_ref, p2_ref,
                       n1_ref, n2_ref, nf_ref, win_ref, ck_ref, wout_ref, wgu_ref, wd_ref,
                       y_ref, tail_ref, ubuf_ref, z_ref, *, seg):
    j = pl.program_id(1)
    tm = x_ref.shape[0]
    x = x_ref[...]
    h = _norm_mod(x, n1_ref[...], sc1_ref[...], sh1_ref[...]).astype(BF16)

    @pl.when(j == 0)
    def _():
        ubuf_ref[0:SUBLANES, :] = jnp.zeros((SUBLANES, D_MODEL), F32)

    @pl.when(j > 0)
    def _():
        ubuf_ref[0:SUBLANES, :] = ubuf_ref[tm:tm + SUBLANES, :]

    t = (j * tm + lax.broadcasted_iota(jnp.int32, (tm, 1), 0)) & (seg - 1)
    ck = ck_ref[...]
    cw = FFN_CHUNK
    for c in range(D_MODEL // cw):
        cols = slice(c * cw, (c + 1) * cw)
        bg = _dot(h, win_ref[:, c * cw:(c + 1) * cw])
        cg = _dot(h, win_ref[:, D_MODEL + c * cw:D_MODEL + (c + 1) * cw])
        hv = _dot(h, win_ref[:, 2 * D_MODEL + c * cw:2 * D_MODEL + (c + 1) * cw])
        u = cg * hv
        ubuf_ref[SUBLANES:SUBLANES + tm, cols] = u
        um1 = jnp.where(t >= 1, ubuf_ref[SUBLANES - 1:SUBLANES - 1 + tm, cols], p1_ref[:, cols])
        um2 = jnp.where(t >= 2, ubuf_ref[SUBLANES - 2:SUBLANES - 2 + tm, cols], p2_ref[:, cols])
        conv = ck[0:1, cols] * um2 + ck[1:2, cols] * um1 + ck[2:3, cols] * u
        z_ref[:, cols] = (bg * conv).astype(BF16)
    r = tail_ref.shape[0]
    tail_ref[...] = ubuf_ref[SUBLANES + tm - r:SUBLANES + tm, :]

    y1 = x + g1_ref[...] * _dot(z_ref[...], wout_ref[...])
    h2 = _norm_mod(y1, n2_ref[...], sc2_ref[...], sh2_ref[...]).astype(BF16)
    y2 = y1 + g2_ref[...] * _ffn(h2, wgu_ref, wd_ref)
    ms = jnp.mean(y2 * y2, axis=-1, keepdims=True)
    y_ref[...] = (y2 * lax.rsqrt(ms + EPS)) * nf_ref[...]


def _conv_layer(x, mods, p1, p2, n1, n2, nf, win, ck, wout, wgu, wd, *, layer, seg, full_tail):
    nb, rows, _ = x.shape
    per_row = mods[0].shape[1] == rows
    tm, row, mod = _row_specs(nb, rows, per_row)
    if per_row:
        prefix = row
    else:
        prefix = pl.BlockSpec((None, 1, D_MODEL), lambda b, j: (0, 0, 0))
    if full_tail:
        tail_spec, tail_rows = row, rows
    else:
        tail_spec, tail_rows = pl.BlockSpec((None, SUBLANES, D_MODEL), lambda b, j: (b, 0, 0)), SUBLANES
    vec = _const_spec((1, D_MODEL))
    return pl.pallas_call(
        functools.partial(_conv_layer_kernel, seg=seg),
        grid=(nb, rows // tm),
        in_specs=[row] + [mod] * 6 + [prefix, prefix, vec, vec, vec,
                                      _const_spec(win.shape), _const_spec(ck.shape), _const_spec(wout.shape),
                                      _layer_spec(wgu, layer), _layer_spec(wd, layer)],
        out_specs=(row, tail_spec),
        out_shape=(jax.ShapeDtypeStruct(x.shape, F32),
                   jax.ShapeDtypeStruct((nb, tail_rows, D_MODEL), F32)),
        scratch_shapes=[pltpu.VMEM((tm + SUBLANES, D_MODEL), F32), pltpu.VMEM((tm, D_MODEL), BF16)],
        compiler_params=_params("arbitrary", "arbitrary"),
        name="conv_layer",
    )(x, *mods, p1, p2, n1, n2, nf, win, ck, wout, wgu, wd)


def _rope_angles(pos):
    inv = ROPE_THETA ** (-jnp.arange(0, ROT_DIM, 2, dtype=F32) / ROT_DIM)
    ang = pos[:, None] * inv[None, :]
    return jnp.cos(ang), jnp.sin(ang)


def _rope_tables(pos):
    half = ROT_DIM // 2
    cos, sin = _rope_angles(pos)
    n = pos.shape[0]
    rest = HEAD_DIM - ROT_DIM
    c = jnp.concatenate([cos, cos, jnp.ones((n, rest), F32)], axis=1)
    s1 = jnp.concatenate([-sin, jnp.zeros((n, half + rest), F32)], axis=1)
    s2 = jnp.concatenate([jnp.zeros((n, half), F32), sin, jnp.zeros((n, rest), F32)], axis=1)
    return tuple(jnp.tile(a, (1, LANES // HEAD_DIM)) for a in (c, s1, s2))


def kernel(x_prompt, x_sample, c_prompt, c_sample, cache_k, cache_v, cache_kidx, state_conv, page_table,
           ada_w, ada_b, norm1_g, norm2_g, final_g, attn_w_in, attn_w_out, conv_w_in, conv_k, conv_w_out,
           ffn_w_gu, ffn_w_down):
    nb, t, _ = x_prompt.shape
    ns, tn, _ = x_sample.shape
    n_pages = page_table.shape[1]
    past = n_pages * PAGE_SIZE
    rows_s = ns * tn

    pad = (-(nb + ns)) % SUBLANES
    c_all = jnp.concatenate([c_prompt, c_sample, jnp.zeros((pad, D_MODEL), F32)], axis=0)
    mod = _modulation(c_all, ada_w, ada_b)

    def mods(layer):
        cols = [mod[layer, :, i * D_MODEL:(i + 1) * D_MODEL] for i in range(6)]
        prompt = [m[:nb, None, :] for m in cols]
        sample = [jnp.repeat(m[nb:nb + ns], tn, axis=0)[None] for m in cols]
        return prompt, sample

    vec = lambda a: a.reshape(1, D_MODEL)
    w_in = jnp.pad(attn_w_in[0], ((0, 0), (0, ATTN_IN_PAD - ATTN_IN))).astype(BF16)
    w_o = attn_w_out[0].astype(BF16)
    wgu = ffn_w_gu.astype(BF16)
    wd = ffn_w_down.astype(BF16)

    (sh1p, sc1p, g1p, sh2p, sc2p, g2p), (sh1s, sc1s, g1s, sh2s, sc2s, g2s) = mods(0)
    pos_p = jnp.arange(t, dtype=F32)
    rope_p = _rope_tables(pos_p)
    rope_pt = tuple(a.T for a in _rope_angles(pos_p))
    k_p, v_p, ki_p, qt, qit, kg, vt, kib, wit = _proj_prompt(
        x_prompt, sc1p, sh1p, vec(norm1_g[0]), w_in, rope_p, rope_pt)
    o_p = _prompt_attention(qt, qit, wit, kg, vt, kib)
    y_p = _attn_out_ffn(x_prompt, o_p, g1p, sc2p, sh2p, g2p, vec(norm2_g[0]), w_o, wgu, wd, 0)

    pos_s = jnp.tile(past + jnp.arange(tn, dtype=F32), ns)
    rope_s = _rope_tables(pos_s)
    xs = x_sample.reshape(rows_s, D_MODEL)
    proj = _proj_sample(xs, sc1s[0], sh1s[0], vec(norm1_g[0]), w_in, rope_s)
    o = 0
    q_s = proj[:, o:o + Q_COLS].reshape(ns, tn, N_KV_HEADS, HEADS_PER_KV, HEAD_DIM); o += Q_COLS
    k_s = proj[:, o:o + KV_COLS].reshape(ns, tn, KV_COLS); o += KV_COLS
    v_s = proj[:, o:o + KV_COLS].reshape(ns, tn, KV_COLS); o += KV_COLS
    qi_s = proj[:, o:o + IQ_COLS].reshape(ns, tn, IDX_HEADS, IDX_DIM); o += IQ_COLS
    ki_s = proj[:, o:o + IDX_DIM].reshape(ns, tn, IDX_DIM); o += IDX_DIM
    wi_s = proj[:, o:o + IDX_HEADS].reshape(ns, tn, IDX_HEADS)

    qpad = ((0, 0), (0, SUBLANES - tn))
    qi8 = jnp.pad(qi_s, qpad + ((0, 0), (0, 0))).transpose(0, 2, 1, 3)
    qi8 = qi8.reshape(ns, IDX_HEADS * SUBLANES, IDX_DIM).astype(BF16)
    w8 = jnp.pad(wi_s, qpad + ((0, 0),)).transpose(0, 2, 1).reshape(ns, IDX_HEADS * SUBLANES, 1)
    slot_pad = (0, PAGE_SIZE - tn)
    ki_new_t = jnp.pad(ki_s.transpose(0, 2, 1), ((0, 0), (0, 0), slot_pad))
    scores = _sample_scores(page_table, qi8, w8, cache_kidx[0].transpose(0, 2, 1), ki_new_t)
    n_keys = scores.shape[2]
    scores_t = scores[:, :tn, :].transpose(2, 0, 1).reshape(n_keys, rows_s)
    bias_t = _sample_select(scores_t, past, tn)
    bias8 = jnp.pad(bias_t.reshape(n_keys, ns, tn).transpose(1, 2, 0), qpad + ((0, 0),))

    q8 = jnp.pad(q_s * HEAD_DIM ** -0.5, qpad + ((0, 0), (0, 0), (0, 0))).transpose(0, 2, 3, 1, 4)
    q8 = q8.reshape(ns, N_KV_HEADS, HEADS_PER_KV * SUBLANES, HEAD_DIM).astype(BF16)
    new_t = lambda a: jnp.pad(a.reshape(ns, tn, N_KV_HEADS, HEAD_DIM).transpose(0, 2, 1, 3),
                              ((0, 0), (0, 0), (0, SUBLANES - tn), (0, LANES - HEAD_DIM)))
    o_g = _sample_attention(page_table, q8, bias8,
                            cache_k[0].transpose(0, 2, 3, 1), cache_v[0].transpose(0, 2, 3, 1),
                            new_t(k_s), new_t(v_s))
    o_g = o_g.reshape(ns, N_KV_HEADS, HEADS_PER_KV, SUBLANES, HEAD_DIM)[:, :, :, :tn]
    o_s = o_g.transpose(0, 3, 1, 2, 4).reshape(1, rows_s, Q_COLS).astype(BF16)
    y_s = _attn_out_ffn(xs[None], o_s, g1s, sc2s, sh2s, g2s, vec(norm2_g[0]), w_o, wgu, wd, 0)

    (sh1p, sc1p, g1p, sh2p, sc2p, g2p), (sh1s, sc1s, g1s, sh2s, sc2s, g2s) = mods(1)
    win = conv_w_in[0].astype(BF16)
    wout = conv_w_out[0].astype(BF16)
    zero_prefix = jnp.zeros((1, 1, D_MODEL), F32)
    out_p, tail_p = _conv_layer(
        y_p, (sc1p, sh1p, g1p, sc2p, sh2p, g2p), zero_prefix, zero_prefix,
        vec(norm1_g[1]), vec(norm2_g[1]), vec(final_g), win, conv_k[0], wout, wgu, wd,
        layer=1, seg=t, full_tail=False)
    st = state_conv[0]
    zeros_row = jnp.zeros((ns, 1, D_MODEL), F32)
    p1 = jnp.concatenate([st[:, 1:2], zeros_row, zeros_row, zeros_row], axis=1).reshape(1, rows_s, D_MODEL)
    p2 = jnp.concatenate([st[:, 0:1], st[:, 1:2], zeros_row, zeros_row], axis=1).reshape(1, rows_s, D_MODEL)
    out_s, u_s = _conv_layer(
        y_s, (sc1s, sh1s, g1s, sc2s, sh2s, g2s), p1, p2,
        vec(norm1_g[1]), vec(norm2_g[1]), vec(final_g), win, conv_k[0], wout, wgu, wd,
        layer=1, seg=tn, full_tail=True)

    keep = CONV_WIDTH - 1
    return (
        out_p,
        out_s.reshape(ns, tn, D_MODEL),
        k_p.reshape(1, nb, t, N_KV_HEADS, HEAD_DIM),
        v_p.reshape(1, nb, t, N_KV_HEADS, HEAD_DIM),
        ki_p.transpose(0, 2, 1)[None],
        tail_p[None, :, SUBLANES - keep:, :],
        k_s.reshape(1, ns, tn, N_KV_HEADS, HEAD_DIM),
        v_s.reshape(1, ns, tn, N_KV_HEADS, HEAD_DIM),
        ki_s[None],
        u_s.reshape(ns, tn, D_MODEL)[None, :, tn - keep:, :],
    )
```

```python
import functools

import jax
import jax.numpy as jnp
from jax import lax
from jax.experimental import pallas as pl
from jax.experimental.pallas import tpu as pltpu

F32 = jnp.float32
BF16 = jnp.bfloat16

D_MODEL = 1024
N_HEADS = 16
HEAD_DIM = 64
N_KV_HEADS = 4
HEADS_PER_KV = N_HEADS // N_KV_HEADS
ROT_DIM = 16
ROPE_THETA = 500000.0
IDX_HEADS = 8
IDX_DIM = 64
TOPK = 256
PAGE_SIZE = 128
CONV_WIDTH = 3
D_FF = 2816
EPS = 1e-6
Q_COLS = N_HEADS * HEAD_DIM
KV_COLS = N_KV_HEADS * HEAD_DIM
IQ_COLS = IDX_HEADS * IDX_DIM
ATTN_IN = Q_COLS + 2 * KV_COLS + IQ_COLS + IDX_DIM + IDX_HEADS
LANES = 128
SUBLANES = 8
BF16_SUBLANES = 16
ATTN_IN_PAD = -(-ATTN_IN // LANES) * LANES
MASK_BIAS = -1e30
F32_MAX = 3.4028234663852886e38
LOG2_E = 1.4426950408889634
VMEM_LIMIT = 60 * 1024 * 1024

ROW_TILE = 512
Q_TILE = 256
SCORE_CHUNK = 512
SCORE_SLAB = 128
ATTN_CHUNK = 256
V_ROWS = HEAD_DIM + BF16_SUBLANES
FFN_CHUNK = 256
MOD_TILE = 1024
P_SLOTS = 4
SCORE_SEQS = 4
ATTN_SEQS = 2


def _dot(a, b):
    return jnp.dot(a, b, preferred_element_type=F32)


def _dot_nt(a, b):
    return lax.dot_general(a, b, (((1,), (1,)), ((), ())), preferred_element_type=F32)


def _silu(x):
    return x / (1.0 + jnp.exp(-x))


def _norm_mod(x, g, sc, sh):
    ms = jnp.mean(x * x, axis=-1, keepdims=True)
    y = x * lax.rsqrt(ms + EPS)
    return (y * g) * (1.0 + sc) + sh


def _params(*sem):
    return pltpu.CompilerParams(dimension_semantics=sem, vmem_limit_bytes=VMEM_LIMIT)


def _const_spec(shape):
    nd = len(shape)
    return pl.BlockSpec(shape, lambda *_: (0,) * nd, pipeline_mode=pl.Buffered(1))


def _mod_kernel(c_ref, w_ref, b_ref, o_ref):
    s = _silu(c_ref[...]).astype(BF16)
    o_ref[...] = _dot(s, w_ref[...].astype(BF16)) + b_ref[...]


def _modulation(c_all, ada_w, ada_b):
    depth, _, n = ada_w.shape
    rows = c_all.shape[0]
    tn = MOD_TILE
    return pl.pallas_call(
        _mod_kernel,
        grid=(depth, n // tn),
        in_specs=[
            pl.BlockSpec((rows, D_MODEL), lambda l, j: (0, 0)),
            pl.BlockSpec((None, D_MODEL, tn), lambda l, j: (l, 0, j)),
            pl.BlockSpec((None, 1, tn), lambda l, j: (l, 0, j)),
        ],
        out_specs=pl.BlockSpec((None, rows, tn), lambda l, j: (l, 0, j)),
        out_shape=jax.ShapeDtypeStruct((depth, rows, n), F32),
        compiler_params=_params("arbitrary", "arbitrary"),
        name="adaln_modulation",
    )(c_all, ada_w, ada_b.reshape(depth, 1, n))


def _rope_block(x, c, s1, s2):
    return x * c + pltpu.roll(x, LANES - ROT_DIM // 2, 1) * s1 + pltpu.roll(x, ROT_DIM // 2, 1) * s2


def _rope_cols(p, c, s1, s2):
    nblk = p.shape[1] // LANES
    return [_rope_block(p[:, j * LANES:(j + 1) * LANES], c, s1, s2) for j in range(nblk)]


def _last_block_tables(c, s1, s2):
    lane = lax.broadcasted_iota(jnp.int32, c.shape, 1)
    is_key = lane < IDX_DIM
    return jnp.where(is_key, c, 1.0), jnp.where(is_key, s1, 0.0), jnp.where(is_key, s2, 0.0)


def _rope_rows(xt, cos_t, sin_t):
    half = ROT_DIM // 2
    out = []
    for base in range(0, LANES, HEAD_DIM):
        x1, x2 = xt[base:base + half], xt[base + half:base + ROT_DIM]
        out += [x1 * cos_t - x2 * sin_t, x2 * cos_t + x1 * sin_t, xt[base + ROT_DIM:base + HEAD_DIM]]
    return jnp.concatenate(out, axis=0)


def _proj_prompt_kernel(x_ref, sc_ref, sh_ref, g_ref, w_ref, c_ref, s1_ref, s2_ref, ct_ref, st_ref,
                        k_ref, v_ref, kit_ref, qt_ref, qit_ref, kg_ref, vt_ref, kib_ref, wit_ref):
    h = _norm_mod(x_ref[...], g_ref[...], sc_ref[...], sh_ref[...]).astype(BF16)
    c, s1, s2 = c_ref[...], s1_ref[...], s2_ref[...]
    cos_t, sin_t = ct_ref[...], st_ref[...]

    pq = _dot(h, w_ref[:, 0:Q_COLS])
    for j in range(Q_COLS // LANES):
        qt = _rope_rows(pq[:, j * LANES:(j + 1) * LANES].T, cos_t, sin_t)
        qt_ref[j * LANES:(j + 1) * LANES, :] = (qt * (HEAD_DIM ** -0.5 * LOG2_E)).astype(BF16)

    pk = _dot(h, w_ref[:, Q_COLS:Q_COLS + KV_COLS])
    k = jnp.concatenate(_rope_cols(pk, c, s1, s2), axis=1)
    k_ref[...] = k
    for g in range(N_KV_HEADS):
        kg_ref[g] = k[:, g * HEAD_DIM:(g + 1) * HEAD_DIM].astype(BF16)

    pv = _dot(h, w_ref[:, Q_COLS + KV_COLS:Q_COLS + 2 * KV_COLS])
    v_ref[...] = pv
    ones = jnp.ones((V_ROWS - HEAD_DIM, ATTN_CHUNK), BF16)
    for cc in range(pv.shape[0] // ATTN_CHUNK):
        vt = pv[cc * ATTN_CHUNK:(cc + 1) * ATTN_CHUNK, :].astype(BF16).T
        for g in range(N_KV_HEADS):
            vt_ref[cc, g * V_ROWS:g * V_ROWS + HEAD_DIM, :] = vt[g * HEAD_DIM:(g + 1) * HEAD_DIM, :]
            vt_ref[cc, g * V_ROWS + HEAD_DIM:(g + 1) * V_ROWS, :] = ones

    o = Q_COLS + 2 * KV_COLS
    pqi = _dot(h, w_ref[:, o:o + IQ_COLS])
    for j in range(IQ_COLS // LANES):
        qit = _rope_rows(pqi[:, j * LANES:(j + 1) * LANES].T, cos_t, sin_t)
        qit_ref[j * LANES:(j + 1) * LANES, :] = qit.astype(BF16)

    pl_ = _dot(h, w_ref[:, o + IQ_COLS:ATTN_IN_PAD])
    last = _rope_block(pl_, *_last_block_tables(c, s1, s2))
    last_t = last.T
    kit_ref[...] = last_t[0:IDX_DIM, :]
    kib_ref[...] = last[:, 0:IDX_DIM].astype(BF16)
    wit_ref[...] = last_t[IDX_DIM:IDX_DIM + IDX_HEADS, :]


def _proj_prompt(x, sc, sh, g, w, rope, rope_t):
    nb, t, _ = x.shape
    tm = ROW_TILE
    row = lambda width: pl.BlockSpec((None, tm, width), lambda b, j: (b, j, 0))
    mod = pl.BlockSpec((None, 1, D_MODEL), lambda b, j: (b, 0, 0))
    tab = pl.BlockSpec((tm, LANES), lambda b, j: (j, 0))
    tab_t = pl.BlockSpec((ROT_DIM // 2, tm), lambda b, j: (0, j))
    col = lambda height: pl.BlockSpec((None, height, tm), lambda b, j: (b, 0, j))
    out_shape = (
        jax.ShapeDtypeStruct((nb, t, KV_COLS), F32),
        jax.ShapeDtypeStruct((nb, t, KV_COLS), F32),
        jax.ShapeDtypeStruct((nb, IDX_DIM, t), F32),
        jax.ShapeDtypeStruct((nb, Q_COLS, t), BF16),
        jax.ShapeDtypeStruct((nb, IQ_COLS, t), BF16),
        jax.ShapeDtypeStruct((nb, N_KV_HEADS, t, HEAD_DIM), BF16),
        jax.ShapeDtypeStruct((nb, t // ATTN_CHUNK, N_KV_HEADS * V_ROWS, ATTN_CHUNK), BF16),
        jax.ShapeDtypeStruct((nb, t, IDX_DIM), BF16),
        jax.ShapeDtypeStruct((nb, IDX_HEADS, t), F32),
    )
    out_specs = (
        row(KV_COLS), row(KV_COLS), col(IDX_DIM), col(Q_COLS), col(IQ_COLS),
        pl.BlockSpec((None, N_KV_HEADS, tm, HEAD_DIM), lambda b, j: (b, 0, j, 0)),
        pl.BlockSpec((None, tm // ATTN_CHUNK, N_KV_HEADS * V_ROWS, ATTN_CHUNK), lambda b, j: (b, j, 0, 0)),
        row(IDX_DIM), col(IDX_HEADS),
    )
    return pl.pallas_call(
        _proj_prompt_kernel,
        grid=(nb, t // tm),
        in_specs=[row(D_MODEL), mod, mod, _const_spec((1, D_MODEL)),
                  _const_spec((D_MODEL, ATTN_IN_PAD)), tab, tab, tab, tab_t, tab_t],
        out_specs=out_specs,
        out_shape=out_shape,
        compiler_params=_params("arbitrary", "arbitrary"),
        name="attn_proj_prompt",
    )(x, sc, sh, g, w, *rope, *rope_t)


def _proj_sample_kernel(x_ref, sc_ref, sh_ref, g_ref, w_ref, c_ref, s1_ref, s2_ref, p_ref):
    h = _norm_mod(x_ref[...], g_ref[...], sc_ref[...], sh_ref[...]).astype(BF16)
    c, s1, s2 = c_ref[...], s1_ref[...], s2_ref[...]
    p = _dot(h, w_ref[...])
    v_lo = (Q_COLS + KV_COLS) // LANES
    v_hi = (Q_COLS + 2 * KV_COLS) // LANES
    nblk = ATTN_IN_PAD // LANES
    for j in range(nblk):
        blk = p[:, j * LANES:(j + 1) * LANES]
        if j == nblk - 1:
            blk = _rope_block(blk, *_last_block_tables(c, s1, s2))
        elif not (v_lo <= j < v_hi):
            blk = _rope_block(blk, c, s1, s2)
        p_ref[:, j * LANES:(j + 1) * LANES] = blk


def _proj_sample(x, sc, sh, g, w, rope):
    rows = x.shape[0]
    full = lambda width: pl.BlockSpec((rows, width), lambda i: (0, 0))
    return pl.pallas_call(
        _proj_sample_kernel,
        grid=(1,),
        in_specs=[full(D_MODEL), full(D_MODEL), full(D_MODEL),
                  pl.BlockSpec((1, D_MODEL), lambda i: (0, 0)),
                  pl.BlockSpec((D_MODEL, ATTN_IN_PAD), lambda i: (0, 0)),
                  full(LANES), full(LANES), full(LANES)],
        out_specs=full(ATTN_IN_PAD),
        out_shape=jax.ShapeDtypeStruct((rows, ATTN_IN_PAD), F32),
        compiler_params=_params("arbitrary"),
        name="attn_proj_sample",
    )(x, sc, sh, g, w, *rope)


FOLD_ROWS = 32
BISECT_STEPS_FIRST = 14
BISECT_STEPS_PER_TEST = 3


def _fold(x, op):
    rows, tq = x.shape
    x3 = x.reshape(rows // FOLD_ROWS, FOLD_ROWS, tq)
    return {"sum": x3.sum, "max": x3.max, "min": x3.min}[op](axis=0)


def _stats_init(tq):
    zeros = jnp.zeros((FOLD_ROWS, tq), F32)
    return (jnp.full((FOLD_ROWS, tq), -jnp.inf, F32), jnp.full((FOLD_ROWS, tq), jnp.inf, F32),
            zeros, zeros, zeros)


def _stats_update(stats, s):
    mx, mn, n_above, n_ge0, n_gt0 = stats
    ones_where = lambda m: _fold(jnp.where(m, 1.0, 0.0), "sum")
    above = s > -jnp.inf
    return (jnp.maximum(mx, _fold(s, "max")),
            jnp.minimum(mn, _fold(jnp.where(above, s, jnp.inf), "min")),
            n_above + ones_where(above), n_ge0 + ones_where(s >= 0.0), n_gt0 + ones_where(s > 0.0))


def _select_to_bias(sc_ref, tie_ref, qpos, nk, kc, stats):
    tq = sc_ref.shape[1]
    topk = float(TOPK)

    slab = 4 * FOLD_ROWS

    def for_slabs(fn, init, slab=slab):
        def body(c, carry):
            for j in range(kc // slab):
                start = pl.multiple_of(c * kc + j * slab, slab)
                idx = start + lax.broadcasted_iota(jnp.int32, (slab, tq), 0)
                carry = fn(pl.ds(start, slab), idx, carry)
            return carry
        return lax.fori_loop(0, nk, body, init)

    def ones_where(m):
        return _fold(jnp.where(m, 1.0, 0.0), "sum")

    def count(pred):
        acc = for_slabs(lambda rows, idx, acc: acc + ones_where(pred(rows)), jnp.zeros((FOLD_ROWS, tq), F32))
        return acc.sum(axis=0, keepdims=True)

    def count_ge(t):
        return count(lambda rows: sc_ref[rows, :] >= t)

    mx, mn, n_above, n_ge0, n_gt0 = stats
    mx = mx.max(axis=0, keepdims=True)
    mn = mn.min(axis=0, keepdims=True)
    n_above = n_above.sum(axis=0, keepdims=True)
    n_ge0 = n_ge0.sum(axis=0, keepdims=True)
    n_gt0 = n_gt0.sum(axis=0, keepdims=True)
    n_max = count_ge(mx)

    few = n_above < topk
    flat = n_max >= topk
    zero = (n_gt0 < topk) & (n_ge0 >= topk)
    pos = n_gt0 >= topk
    lo = jnp.where(few, -jnp.inf, jnp.where(flat, mx, jnp.where(zero | pos, 0.0, mn)))
    hi = jnp.where(few, mn, jnp.where(flat, jnp.inf, jnp.where(pos, mx, 0.0)))
    n_lo = jnp.where(few, 2.0 * topk, jnp.where(flat, n_max, jnp.where(zero | pos, n_ge0, n_above)))
    n_hi = jnp.where(few, n_above, jnp.where(flat, 0.0, jnp.where(zero, n_gt0, jnp.where(pos, n_max, n_ge0))))
    done = jnp.where(few | flat | zero | (n_lo == topk), 1.0, 0.0)

    def not_finished(d):
        return (jnp.min(d) < 0.5).astype(jnp.int32)

    def bisect_step(state):
        lo, hi, n_lo, n_hi, done = state
        mid = jnp.clip(0.5 * lo + 0.5 * hi, -F32_MAX, F32_MAX)
        stuck = (mid <= lo) | (mid >= hi)
        n_mid = count_ge(mid)
        move = (done < 0.5) & jnp.logical_not(stuck)
        up = move & (n_mid >= topk)
        down = move & (n_mid < topk)
        lo = jnp.where(up, mid, lo)
        n_lo = jnp.where(up, n_mid, n_lo)
        hi = jnp.where(down, mid, hi)
        n_hi = jnp.where(down, n_mid, n_hi)
        done = jnp.where(stuck | (n_lo == topk), 1.0, done)
        return lo, hi, n_lo, n_hi, done

    def bisect(carry):
        state = lax.fori_loop(0, carry[6], lambda _, st: bisect_step(st), carry[:5])
        return (*state, not_finished(state[4]), jnp.int32(BISECT_STEPS_PER_TEST))

    lo, hi, n_lo, n_hi, done, _, _ = lax.while_loop(
        lambda carry: carry[5] > 0, bisect,
        (lo, hi, n_lo, n_hi, done, not_finished(done), jnp.int32(BISECT_STEPS_FIRST)))

    thr = lo
    exact = n_lo == topk
    need = topk - n_hi
    all_keys = jnp.full((1, tq), 2 ** 30, jnp.int32)

    def tie_break():
        def mark(rows, idx, _):
            tie_ref[rows, :] = jnp.where(sc_ref[rows, :] == thr, idx, 2 ** 30)
            return 0
        for_slabs(mark, 0)

        n_bits = (sc_ref.shape[0] - 1).bit_length()

        def step(i, j):
            cand = j + lax.shift_right_logical(jnp.int32(1 << (n_bits - 1)), i)
            n_before = count(lambda rows: tie_ref[rows, :] < cand)
            return jnp.where(n_before < need, cand, j)
        return lax.fori_loop(0, n_bits, step, jnp.zeros((1, tq), jnp.int32))

    any_tie = jnp.min(jnp.where(exact, 1.0, 0.0)) < 0.5

    def write(keep_fn):
        def body(rows, idx, _):
            sc_ref[rows, :] = jnp.where(keep_fn(sc_ref[rows, :], idx), 0.0, MASK_BIAS)
            return 0
        return for_slabs(body, 0, slab=kc)

    def write_with_ties():
        last_tied = jnp.minimum(jnp.where(exact, all_keys, tie_break()), qpos)
        return write(lambda s, idx: (s > thr) | ((s == thr) & (idx <= last_tied)))

    lax.cond(any_tie, write_with_ties, lambda: write(lambda s, idx: s >= thr))


def _prompt_attn_kernel(qt_ref, qit_ref, wit_ref, kg_ref, vt_ref, kib_ref, o_ref,
                        sc_ref, tie_ref, acc_ref, m_ref, mx_ref, s_ref, p_ref):
    i = pl.program_id(1)
    tq = qt_ref.shape[1]
    qpos = i * tq + lax.broadcasted_iota(jnp.int32, (1, tq), 1)
    n_keys = (i + 1) * tq

    n_score = (n_keys + SCORE_CHUNK - 1) // SCORE_CHUNK

    def score_chunk(c, stats):
        for j in range(SCORE_CHUNK // SCORE_SLAB):
            start = pl.multiple_of(c * SCORE_CHUNK + j * SCORE_SLAB, SCORE_SLAB)
            kc = kib_ref[pl.ds(start, SCORE_SLAB), :]
            acc = jnp.zeros((SCORE_SLAB, tq), F32)
            for h in range(IDX_HEADS):
                d = _dot(kc, qit_ref[h * IDX_DIM:(h + 1) * IDX_DIM, :])
                acc = acc + jnp.maximum(d, 0.0) * wit_ref[h:h + 1, :]
            idx = start + lax.broadcasted_iota(jnp.int32, (SCORE_SLAB, tq), 0)
            s = jnp.where(idx <= qpos, acc, -jnp.inf)
            sc_ref[pl.ds(start, SCORE_SLAB), :] = s
            stats = _stats_update(stats, s)
        return stats

    stats = lax.fori_loop(0, n_score, score_chunk, _stats_init(tq))

    _select_to_bias(sc_ref, tie_ref, qpos, n_score, SCORE_CHUNK, stats)

    m_ref[...] = jnp.full(m_ref.shape, MASK_BIAS, F32)
    acc_ref[...] = jnp.zeros(acc_ref.shape, F32)

    dyn0 = jnp.minimum(i, 0)

    def attn_chunk(c, _):
        start = pl.multiple_of(c * ATTN_CHUNK, ATTN_CHUNK)
        for h in range(N_HEADS):
            s = _dot(kg_ref[h // HEADS_PER_KV, pl.ds(start, ATTN_CHUNK), :],
                     qt_ref[h * HEAD_DIM:(h + 1) * HEAD_DIM, :])
            s = s + sc_ref[pl.ds(start, ATTN_CHUNK), :]
            s_ref[h + dyn0] = s
            mx_ref[h] = s.max(axis=0, keepdims=True)
        for h in range(N_HEADS):
            g = h // HEADS_PER_KV
            m_old = m_ref[h]
            m_new = jnp.maximum(m_old, mx_ref[h])
            alpha = jnp.exp2(m_old - m_new)
            p_ref[h % P_SLOTS + dyn0] = jnp.exp2(s_ref[h + dyn0] - m_new).astype(BF16)
            vt = vt_ref[c, g * V_ROWS:(g + 1) * V_ROWS, :]
            acc_ref[h] = alpha * acc_ref[h] + _dot(vt, p_ref[h % P_SLOTS + dyn0])
            m_ref[h] = m_new
        return 0

    lax.fori_loop(0, (i + 1) * (tq // ATTN_CHUNK), attn_chunk, 0)

    for pair in range(N_HEADS // 2):
        two = []
        for h in (2 * pair, 2 * pair + 1):
            a = acc_ref[h]
            two.append(a[0:HEAD_DIM, :] / a[HEAD_DIM:HEAD_DIM + 1, :])
        col = 2 * pair * HEAD_DIM
        o_ref[:, col:col + 2 * HEAD_DIM] = jnp.concatenate(two, axis=0).T.astype(BF16)


def _prompt_attention(qt, qit, wit, kg, vt, kib):
    nb, _, t = qt.shape
    tq = Q_TILE
    col = lambda height: pl.BlockSpec((None, height, tq), lambda b, i: (b, 0, i))
    return pl.pallas_call(
        _prompt_attn_kernel,
        grid=(nb, t // tq),
        in_specs=[
            col(Q_COLS), col(IQ_COLS), col(IDX_HEADS),
            pl.BlockSpec((None, N_KV_HEADS, t, HEAD_DIM), lambda b, i: (b, 0, 0, 0)),
            pl.BlockSpec((None, t // ATTN_CHUNK, N_KV_HEADS * V_ROWS, ATTN_CHUNK), lambda b, i: (b, 0, 0, 0)),
            pl.BlockSpec((None, t, IDX_DIM), lambda b, i: (b, 0, 0)),
        ],
        out_specs=pl.BlockSpec((None, tq, Q_COLS), lambda b, i: (b, i, 0)),
        out_shape=jax.ShapeDtypeStruct((nb, t, Q_COLS), BF16),
        scratch_shapes=[
            pltpu.VMEM((t, tq), F32),
            pltpu.VMEM((t, tq), jnp.int32),
            pltpu.VMEM((N_HEADS, V_ROWS, tq), F32),
            pltpu.VMEM((N_HEADS, 1, tq), F32),
            pltpu.VMEM((N_HEADS, 1, tq), F32),
            pltpu.VMEM((N_HEADS, ATTN_CHUNK, tq), F32),
            pltpu.VMEM((P_SLOTS, ATTN_CHUNK, tq), BF16),
        ],
        compiler_params=_params("arbitrary", "arbitrary"),
        name="prompt_attention",
    )(qt, qit, wit, kg, vt, kib)


def _sample_score_kernel(pt_ref, qi_ref, w_ref, *rest, n_pages):
    del pt_ref
    n_seq = qi_ref.shape[0]
    pages, (new_ref, o_ref, kall_ref) = rest[:n_seq * n_pages], rest[n_seq * n_pages:]
    for q in range(n_seq):
        for j in range(n_pages):
            kall_ref[q, :, j * PAGE_SIZE:(j + 1) * PAGE_SIZE] = pages[q * n_pages + j][...].astype(BF16)
        kall_ref[q, :, n_pages * PAGE_SIZE:(n_pages + 1) * PAGE_SIZE] = new_ref[q].astype(BF16)
    for q in range(n_seq):
        d = _dot(qi_ref[q], kall_ref[q])
        r = jnp.maximum(d, 0.0) * w_ref[q]
        o_ref[q] = r.reshape(IDX_HEADS, SUBLANES, r.shape[1]).sum(axis=0)


def _sample_scores(page_table, qi8, w8, cache_ki_t, ki_new_t):
    nb, n_pages = page_table.shape
    n_keys = (n_pages + 1) * PAGE_SIZE
    rows = IDX_HEADS * SUBLANES
    ns = SCORE_SEQS
    per_step = lambda r, w: pl.BlockSpec((ns, r, w), lambda b, pt: (b, 0, 0))
    grid_spec = pltpu.PrefetchScalarGridSpec(
        num_scalar_prefetch=1,
        grid=(nb // ns,),
        in_specs=[per_step(rows, IDX_DIM), per_step(rows, 1)]
        + [pl.BlockSpec((None, IDX_DIM, PAGE_SIZE), lambda b, pt, q=q, j=j: (pt[b * ns + q, j], 0, 0))
           for q in range(ns) for j in range(n_pages)]
        + [per_step(IDX_DIM, PAGE_SIZE)],
        out_specs=per_step(SUBLANES, n_keys),
        scratch_shapes=[pltpu.VMEM((ns, IDX_DIM, n_keys), BF16)],
    )
    return pl.pallas_call(
        functools.partial(_sample_score_kernel, n_pages=n_pages),
        grid_spec=grid_spec,
        out_shape=jax.ShapeDtypeStruct((nb, SUBLANES, n_keys), F32),
        compiler_params=_params("arbitrary"),
        name="sample_indexer_scores",
    )(page_table, qi8, w8, *([cache_ki_t] * (ns * n_pages)), ki_new_t)


def _sample_select_kernel(s_ref, o_ref, sc_ref, tie_ref, *, past, period):
    tq = s_ref.shape[1]
    n_keys = s_ref.shape[0]
    lane = lax.broadcasted_iota(jnp.int32, (1, tq), 1)
    qpos = past + (lane & (period - 1))
    nk = n_keys // PAGE_SIZE
    stats = _stats_init(tq)
    for c in range(nk):
        idx = c * PAGE_SIZE + lax.broadcasted_iota(jnp.int32, (PAGE_SIZE, tq), 0)
        s = jnp.where(idx <= qpos, s_ref[c * PAGE_SIZE:(c + 1) * PAGE_SIZE, :], -jnp.inf)
        sc_ref[c * PAGE_SIZE:(c + 1) * PAGE_SIZE, :] = s
        stats = _stats_update(stats, s)
    _select_to_bias(sc_ref, tie_ref, qpos, nk, PAGE_SIZE, stats)
    o_ref[...] = sc_ref[...]


def _sample_select(scores_t, past, period):
    n_keys, nq = scores_t.shape
    tq = Q_TILE
    blk = pl.BlockSpec((n_keys, tq), lambda i: (0, i))
    return pl.pallas_call(
        functools.partial(_sample_select_kernel, past=past, period=period),
        grid=(nq // tq,),
        in_specs=[blk],
        out_specs=blk,
        out_shape=jax.ShapeDtypeStruct((n_keys, nq), F32),
        scratch_shapes=[pltpu.VMEM((n_keys, tq), F32), pltpu.VMEM((n_keys, tq), jnp.int32)],
        compiler_params=_params("arbitrary"),
        name="sample_select",
    )(scores_t)


def _sample_attn_kernel(pt_ref, q_ref, bias_ref, *rest, n_pages):
    del pt_ref
    n_seq = q_ref.shape[0]
    kpages, vpages = rest[:n_seq * n_pages], rest[n_seq * n_pages:2 * n_seq * n_pages]
    knew_ref, vnew_ref, o_ref, kall_ref, vall_ref, s_ref, p_ref, l_ref = rest[2 * n_seq * n_pages:]
    n_keys = bias_ref.shape[2]
    fill = jnp.zeros((PAGE_SIZE - SUBLANES, LANES), F32)
    for q in range(n_seq):
        for j in range(n_pages):
            kall_ref[q, :, :, j * PAGE_SIZE:(j + 1) * PAGE_SIZE] = kpages[q * n_pages + j][...].astype(BF16)
            vall_ref[q, :, :, j * PAGE_SIZE:(j + 1) * PAGE_SIZE] = vpages[q * n_pages + j][...].astype(BF16)
        for g in range(N_KV_HEADS):
            for new_ref, all_ref in ((knew_ref, kall_ref), (vnew_ref, vall_ref)):
                page = jnp.concatenate([new_ref[q, g], fill], axis=0).T
                all_ref[q, g, :, n_pages * PAGE_SIZE:(n_pages + 1) * PAGE_SIZE] = (
                    page[0:HEAD_DIM, :].astype(BF16))
    for q in range(n_seq):
        bias = jnp.broadcast_to(bias_ref[q][None], (HEADS_PER_KV, SUBLANES, n_keys)).reshape(
            HEADS_PER_KV * SUBLANES, n_keys)
        for g in range(N_KV_HEADS):
            s_ref[q, g] = _dot(q_ref[q, g], kall_ref[q, g]) + bias
    for q in range(n_seq):
        for g in range(N_KV_HEADS):
            s = s_ref[q, g]
            m = s.max(axis=1, keepdims=True)
            p = jnp.exp(s - m)
            l_ref[q, g] = p.sum(axis=1, keepdims=True)
            p_ref[q, g] = p.astype(BF16)
    for q in range(n_seq):
        for g in range(N_KV_HEADS):
            o_ref[q, g] = _dot_nt(p_ref[q, g], vall_ref[q, g]) / l_ref[q, g]


def _sample_attention(page_table, q8, bias8, cache_k_t, cache_v_t, k_new_t, v_new_t):
    nb, n_pages = page_table.shape
    n_keys = (n_pages + 1) * PAGE_SIZE
    rows = HEADS_PER_KV * SUBLANES
    ns = ATTN_SEQS
    page = [pl.BlockSpec((None, N_KV_HEADS, HEAD_DIM, PAGE_SIZE),
                         lambda b, pt, q=q, j=j: (pt[b * ns + q, j], 0, 0, 0))
            for q in range(ns) for j in range(n_pages)]
    new = pl.BlockSpec((ns, N_KV_HEADS, SUBLANES, LANES), lambda b, pt: (b, 0, 0, 0))
    qo = pl.BlockSpec((ns, N_KV_HEADS, rows, HEAD_DIM), lambda b, pt: (b, 0, 0, 0))
    grid_spec = pltpu.PrefetchScalarGridSpec(
        num_scalar_prefetch=1,
        grid=(nb // ns,),
        in_specs=[qo, pl.BlockSpec((ns, SUBLANES, n_keys), lambda b, pt: (b, 0, 0))]
        + page + page + [new, new],
        out_specs=qo,
        scratch_shapes=[pltpu.VMEM((ns, N_KV_HEADS, HEAD_DIM, n_keys), BF16),
                        pltpu.VMEM((ns, N_KV_HEADS, HEAD_DIM, n_keys), BF16),
                        pltpu.VMEM((ns, N_KV_HEADS, rows, n_keys), F32),
                        pltpu.VMEM((ns, N_KV_HEADS, rows, n_keys), BF16),
                        pltpu.VMEM((ns, N_KV_HEADS, rows, 1), F32)],
    )
    pages = ns * n_pages
    return pl.pallas_call(
        functools.partial(_sample_attn_kernel, n_pages=n_pages),
        grid_spec=grid_spec,
        out_shape=jax.ShapeDtypeStruct((nb, N_KV_HEADS, rows, HEAD_DIM), F32),
        compiler_params=_params("arbitrary"),
        name="sample_attention",
    )(page_table, q8, bias8, *([cache_k_t] * pages), *([cache_v_t] * pages), k_new_t, v_new_t)


def _ffn(h, wgu_ref, wd_ref):
    acc = jnp.zeros((h.shape[0], D_MODEL), F32)
    for c in range(D_FF // FFN_CHUNK):
        gate = _dot(h, wgu_ref[:, c * FFN_CHUNK:(c + 1) * FFN_CHUNK])
        up = _dot(h, wgu_ref[:, D_FF + c * FFN_CHUNK:D_FF + (c + 1) * FFN_CHUNK])
        a = _silu(gate) * up
        acc = acc + _dot(a.astype(BF16), wd_ref[c * FFN_CHUNK:(c + 1) * FFN_CHUNK, :])
    return acc


def _row_specs(nb, rows, per_row_mod):
    tm = min(ROW_TILE, rows)
    row = pl.BlockSpec((None, tm, D_MODEL), lambda b, j: (b, j, 0))
    if per_row_mod:
        mod = row
    else:
        mod = pl.BlockSpec((None, 1, D_MODEL), lambda b, j: (b, 0, 0))
    return tm, row, mod


def _attn_out_ffn_kernel(x_ref, o_ref, g1_ref, sc2_ref, sh2_ref, g2_ref, n2_ref,
                         wo_ref, wgu_ref, wd_ref, y_ref):
    y1 = x_ref[...] + g1_ref[...] * _dot(o_ref[...], wo_ref[...])
    h2 = _norm_mod(y1, n2_ref[...], sc2_ref[...], sh2_ref[...]).astype(BF16)
    y_ref[...] = y1 + g2_ref[...] * _ffn(h2, wgu_ref, wd_ref)


def _layer_spec(stacked, layer):
    _, rows, cols = stacked.shape
    return pl.BlockSpec((None, rows, cols), lambda *_: (layer, 0, 0), pipeline_mode=pl.Buffered(1))


def _attn_out_ffn(x, o, g1, sc2, sh2, g2, n2, wo, wgu, wd, layer):
    nb, rows, _ = x.shape
    tm, row, mod = _row_specs(nb, rows, g1.shape[1] == rows)
    return pl.pallas_call(
        _attn_out_ffn_kernel,
        grid=(nb, rows // tm),
        in_specs=[row, row, mod, mod, mod, mod, _const_spec((1, D_MODEL)),
                  _const_spec(wo.shape), _layer_spec(wgu, layer), _layer_spec(wd, layer)],
        out_specs=row,
        out_shape=jax.ShapeDtypeStruct(x.shape, F32),
        compiler_params=_params("arbitrary", "arbitrary"),
        name="attn_out_ffn",
    )(x, o, g1, sc2, sh2, g2, n2, wo, wgu, wd)


def _conv_layer_kernel(x_ref, sc1_ref, sh1_ref, g1_ref, sc2_ref, sh2_ref, g2_ref, p1_ref, p2_ref,
                       n1_ref, n2_ref, nf_ref, win_ref, ck_ref, wout_ref, wgu_ref, wd_ref,
                       y_ref, tail_ref, ubuf_ref, z_ref, *, seg):
    j = pl.program_id(1)
    tm = x_ref.shape[0]
    x = x_ref[...]
    h = _norm_mod(x, n1_ref[...], sc1_ref[...], sh1_ref[...]).astype(BF16)

    @pl.when(j == 0)
    def _():
        ubuf_ref[0:SUBLANES, :] = jnp.zeros((SUBLANES, D_MODEL), F32)

    @pl.when(j > 0)
    def _():
        ubuf_ref[0:SUBLANES, :] = ubuf_ref[tm:tm + SUBLANES, :]

    t = (j * tm + lax.broadcasted_iota(jnp.int32, (tm, 1), 0)) & (seg - 1)
    ck = ck_ref[...]
    cw = FFN_CHUNK
    for c in range(D_MODEL // cw):
        cols = slice(c * cw, (c + 1) * cw)
        bg = _dot(h, win_ref[:, c * cw:(c + 1) * cw])
        cg = _dot(h, win_ref[:, D_MODEL + c * cw:D_MODEL + (c + 1) * cw])
        hv = _dot(h, win_ref[:, 2 * D_MODEL + c * cw:2 * D_MODEL + (c + 1) * cw])
        u = cg * hv
        ubuf_ref[SUBLANES:SUBLANES + tm, cols] = u
        um1 = jnp.where(t >= 1, ubuf_ref[SUBLANES - 1:SUBLANES - 1 + tm, cols], p1_ref[:, cols])
        um2 = jnp.where(t >= 2, ubuf_ref[SUBLANES - 2:SUBLANES - 2 + tm, cols], p2_ref[:, cols])
        conv = ck[0:1, cols] * um2 + ck[1:2, cols] * um1 + ck[2:3, cols] * u
        z_ref[:, cols] = (bg * conv).astype(BF16)
    r = tail_ref.shape[0]
    tail_ref[...] = ubuf_ref[SUBLANES + tm - r:SUBLANES + tm, :]

    y1 = x + g1_ref[...] * _dot(z_ref[...], wout_ref[...])
    h2 = _norm_mod(y1, n2_ref[...], sc2_ref[...], sh2_ref[...]).astype(BF16)
    y2 = y1 + g2_ref[...] * _ffn(h2, wgu_ref, wd_ref)
    ms = jnp.mean(y2 * y2, axis=-1, keepdims=True)
    y_ref[...] = (y2 * lax.rsqrt(ms + EPS)) * nf_ref[...]


def _conv_layer(x, mods, p1, p2, n1, n2, nf, win, ck, wout, wgu, wd, *, layer, seg, full_tail):
    nb, rows, _ = x.shape
    per_row = mods[0].shape[1] == rows
    tm, row, mod = _row_specs(nb, rows, per_row)
    if per_row:
        prefix = row
    else:
        prefix = pl.BlockSpec((None, 1, D_MODEL), lambda b, j: (0, 0, 0))
    if full_tail:
        tail_spec, tail_rows = row, rows
    else:
        tail_spec, tail_rows = pl.BlockSpec((None, SUBLANES, D_MODEL), lambda b, j: (b, 0, 0)), SUBLANES
    vec = _const_spec((1, D_MODEL))
    return pl.pallas_call(
        functools.partial(_conv_layer_kernel, seg=seg),
        grid=(nb, rows // tm),
        in_specs=[row] + [mod] * 6 + [prefix, prefix, vec, vec, vec,
                                      _const_spec(win.shape), _const_spec(ck.shape), _const_spec(wout.shape),
                                      _layer_spec(wgu, layer), _layer_spec(wd, layer)],
        out_specs=(row, tail_spec),
        out_shape=(jax.ShapeDtypeStruct(x.shape, F32),
                   jax.ShapeDtypeStruct((nb, tail_rows, D_MODEL), F32)),
        scratch_shapes=[pltpu.VMEM((tm + SUBLANES, D_MODEL), F32), pltpu.VMEM((tm, D_MODEL), BF16)],
        compiler_params=_params("arbitrary", "arbitrary"),
        name="conv_layer",
    )(x, *mods, p1, p2, n1, n2, nf, win, ck, wout, wgu, wd)


def _rope_angles(pos):
    inv = ROPE_THETA ** (-jnp.arange(0, ROT_DIM, 2, dtype=F32) / ROT_DIM)
    ang = pos[:, None] * inv[None, :]
    return jnp.cos(ang), jnp.sin(ang)


def _rope_tables(pos):
    half = ROT_DIM // 2
    cos, sin = _rope_angles(pos)
    n = pos.shape[0]
    rest = HEAD_DIM - ROT_DIM
    c = jnp.concatenate([cos, cos, jnp.ones((n, rest), F32)], axis=1)
    s1 = jnp.concatenate([-sin, jnp.zeros((n, half + rest), F32)], axis=1)
    s2 = jnp.concatenate([jnp.zeros((n, half), F32), sin, jnp.zeros((n, rest), F32)], axis=1)
    return tuple(jnp.tile(a, (1, LANES // HEAD_DIM)) for a in (c, s1, s2))


def kernel(x_prompt, x_sample, c_prompt, c_sample, cache_k, cache_v, cache_kidx, state_conv, page_table,
           ada_w, ada_b, norm1_g, norm2_g, final_g, attn_w_in, attn_w_out, conv_w_in, conv_k, conv_w_out,
           ffn_w_gu, ffn_w_down):
    nb, t, _ = x_prompt.shape
    ns, tn, _ = x_sample.shape
    n_pages = page_table.shape[1]
    past = n_pages * PAGE_SIZE
    rows_s = ns * tn

    pad = (-(nb + ns)) % SUBLANES
    c_all = jnp.concatenate([c_prompt, c_sample, jnp.zeros((pad, D_MODEL), F32)], axis=0)
    mod = _modulation(c_all, ada_w, ada_b)

    def mods(layer):
        cols = [mod[layer, :, i * D_MODEL:(i + 1) * D_MODEL] for i in range(6)]
        prompt = [m[:nb, None, :] for m in cols]
        sample = [jnp.repeat(m[nb:nb + ns], tn, axis=0)[None] for m in cols]
        return prompt, sample

    vec = lambda a: a.reshape(1, D_MODEL)
    w_in = jnp.pad(attn_w_in[0], ((0, 0), (0, ATTN_IN_PAD - ATTN_IN))).astype(BF16)
    w_o = attn_w_out[0].astype(BF16)
    wgu = ffn_w_gu.astype(BF16)
    wd = ffn_w_down.astype(BF16)

    (sh1p, sc1p, g1p, sh2p, sc2p, g2p), (sh1s, sc1s, g1s, sh2s, sc2s, g2s) = mods(0)
    pos_p = jnp.arange(t, dtype=F32)
    rope_p = _rope_tables(pos_p)
    rope_pt = tuple(a.T for a in _rope_angles(pos_p))
    k_p, v_p, ki_p, qt, qit, kg, vt, kib, wit = _proj_prompt(
        x_prompt, sc1p, sh1p, vec(norm1_g[0]), w_in, rope_p, rope_pt)
    o_p = _prompt_attention(qt, qit, wit, kg, vt, kib)
    y_p = _attn_out_ffn(x_prompt, o_p, g1p, sc2p, sh2p, g2p, vec(norm2_g[0]), w_o, wgu, wd, 0)

    pos_s = jnp.tile(past + jnp.arange(tn, dtype=F32), ns)
    rope_s = _rope_tables(pos_s)
    xs = x_sample.reshape(rows_s, D_MODEL)
    proj = _proj_sample(xs, sc1s[0], sh1s[0], vec(norm1_g[0]), w_in, rope_s)
    o = 0
    q_s = proj[:, o:o + Q_COLS].reshape(ns, tn, N_KV_HEADS, HEADS_PER_KV, HEAD_DIM); o += Q_COLS
    k_s = proj[:, o:o + KV_COLS].reshape(ns, tn, KV_COLS); o += KV_COLS
    v_s = proj[:, o:o + KV_COLS].reshape(ns, tn, KV_COLS); o += KV_COLS
    qi_s = proj[:, o:o + IQ_COLS].reshape(ns, tn, IDX_HEADS, IDX_DIM); o += IQ_COLS
    ki_s = proj[:, o:o + IDX_DIM].reshape(ns, tn, IDX_DIM); o += IDX_DIM
    wi_s = proj[:, o:o + IDX_HEADS].reshape(ns, tn, IDX_HEADS)

    qpad = ((0, 0), (0, SUBLANES - tn))
    qi8 = jnp.pad(qi_s, qpad + ((0, 0), (0, 0))).transpose(0, 2, 1, 3)
    qi8 = qi8.reshape(ns, IDX_HEADS * SUBLANES, IDX_DIM).astype(BF16)
    w8 = jnp.pad(wi_s, qpad + ((0, 0),)).transpose(0, 2, 1).reshape(ns, IDX_HEADS * SUBLANES, 1)
    slot_pad = (0, PAGE_SIZE - tn)
    ki_new_t = jnp.pad(ki_s.transpose(0, 2, 1), ((0, 0), (0, 0), slot_pad))
    scores = _sample_scores(page_table, qi8, w8, cache_kidx[0].transpose(0, 2, 1), ki_new_t)
    n_keys = scores.shape[2]
    scores_t = scores[:, :tn, :].transpose(2, 0, 1).reshape(n_keys, rows_s)
    bias_t = _sample_select(scores_t, past, tn)
    bias8 = jnp.pad(bias_t.reshape(n_keys, ns, tn).transpose(1, 2, 0), qpad + ((0, 0),))

    q8 = jnp.pad(q_s * HEAD_DIM ** -0.5, qpad + ((0, 0), (0, 0), (0, 0))).transpose(0, 2, 3, 1, 4)
    q8 = q8.reshape(ns, N_KV_HEADS, HEADS_PER_KV * SUBLANES, HEAD_DIM).astype(BF16)
    new_t = lambda a: jnp.pad(a.reshape(ns, tn, N_KV_HEADS, HEAD_DIM).transpose(0, 2, 1, 3),
                              ((0, 0), (0, 0), (0, SUBLANES - tn), (0, LANES - HEAD_DIM)))
    o_g = _sample_attention(page_table, q8, bias8,
                            cache_k[0].transpose(0, 2, 3, 1), cache_v[0].transpose(0, 2, 3, 1),
                            new_t(k_s), new_t(v_s))
    o_g = o_g.reshape(ns, N_KV_HEADS, HEADS_PER_KV, SUBLANES, HEAD_DIM)[:, :, :, :tn]
    o_s = o_g.transpose(0, 3, 1, 2, 4).reshape(1, rows_s, Q_COLS).astype(BF16)
    y_s = _attn_out_ffn(xs[None], o_s, g1s, sc2s, sh2s, g2s, vec(norm2_g[0]), w_o, wgu, wd, 0)

    (sh1p, sc1p, g1p, sh2p, sc2p, g2p), (sh1s, sc1s, g1s, sh2s, sc2s, g2s) = mods(1)
    win = conv_w_in[0].astype(BF16)
    wout = conv_w_out[0].astype(BF16)
    zero_prefix = jnp.zeros((1, 1, D_MODEL), F32)
    out_p, tail_p = _conv_layer(
        y_p, (sc1p, sh1p, g1p, sc2p, sh2p, g2p), zero_prefix, zero_prefix,
        vec(norm1_g[1]), vec(norm2_g[1]), vec(final_g), win, conv_k[0], wout, wgu, wd,
        layer=1, seg=t, full_tail=False)
    st = state_conv[0]
    zeros_row = jnp.zeros((ns, 1, D_MODEL), F32)
    p1 = jnp.concatenate([st[:, 1:2], zeros_row, zeros_row, zeros_row], axis=1).reshape(1, rows_s, D_MODEL)
    p2 = jnp.concatenate([st[:, 0:1], st[:, 1:2], zeros_row, zeros_row], axis=1).reshape(1, rows_s, D_MODEL)
    out_s, u_s = _conv_layer(
        y_s, (sc1s, sh1s, g1s, sc2s, sh2s, g2s), p1, p2,
        vec(norm1_g[1]), vec(norm2_g[1]), vec(final_g), win, conv_k[0], wout, wgu, wd,
        layer=1, seg=tn, full_tail=True)

    keep = CONV_WIDTH - 1
    return (
        out_p,
        out_s.reshape(ns, tn, D_MODEL),
        k_p.reshape(1, nb, t, N_KV_HEADS, HEAD_DIM),
        v_p.reshape(1, nb, t, N_KV_HEADS, HEAD_DIM),
        ki_p.transpose(0, 2, 1)[None],
        tail_p[None, :, SUBLANES - keep:, :],
        k_s.reshape(1, ns, tn, N_KV_HEADS, HEAD_DIM),
        v_s.reshape(1, ns, tn, N_KV_HEADS, HEAD_DIM),
        ki_s[None],
        u_s.reshape(ns, tn, D_MODEL)[None, :, tn - keep:, :],
    )
```

```python
import functools

import jax
import jax.numpy as jnp
from jax import lax
from jax.experimental import pallas as pl
from jax.experimental.pallas import tpu as pltpu

F32 = jnp.float32
BF16 = jnp.bfloat16

D_MODEL = 1024
N_HEADS = 16
HEAD_DIM = 64
N_KV_HEADS = 4
HEADS_PER_KV = N_HEADS // N_KV_HEADS
ROT_DIM = 16
ROPE_THETA = 500000.0
IDX_HEADS = 8
IDX_DIM = 64
TOPK = 256
PAGE_SIZE = 128
CONV_WIDTH = 3
D_FF = 2816
EPS = 1e-6
Q_COLS = N_HEADS * HEAD_DIM
KV_COLS = N_KV_HEADS * HEAD_DIM
IQ_COLS = IDX_HEADS * IDX_DIM
ATTN_IN = Q_COLS + 2 * KV_COLS + IQ_COLS + IDX_DIM + IDX_HEADS
LANES = 128
SUBLANES = 8
BF16_SUBLANES = 16
ATTN_IN_PAD = -(-ATTN_IN // LANES) * LANES
MASK_BIAS = -1e30
F32_MAX = 3.4028234663852886e38
LOG2_E = 1.4426950408889634
VMEM_LIMIT = 60 * 1024 * 1024

ROW_TILE = 512
Q_TILE = 256
SCORE_CHUNK = 512
SCORE_SLAB = 128
ATTN_CHUNK = 256
V_ROWS = HEAD_DIM + BF16_SUBLANES
FFN_CHUNK = 256
MOD_TILE = 1024
P_SLOTS = 4
SCORE_SEQS = 4
ATTN_SEQS = 2


def _dot(a, b):
    return jnp.dot(a, b, preferred_element_type=F32)


def _dot_nt(a, b):
    return lax.dot_general(a, b, (((1,), (1,)), ((), ())), preferred_element_type=F32)


def _silu(x):
    return x / (1.0 + jnp.exp(-x))


def _norm_mod(x, g, sc, sh):
    ms = jnp.mean(x * x, axis=-1, keepdims=True)
    y = x * lax.rsqrt(ms + EPS)
    return (y * g) * (1.0 + sc) + sh


def _params(*sem):
    return pltpu.CompilerParams(dimension_semantics=sem, vmem_limit_bytes=VMEM_LIMIT)


def _const_spec(shape):
    nd = len(shape)
    return pl.BlockSpec(shape, lambda *_: (0,) * nd, pipeline_mode=pl.Buffered(1))


def _mod_kernel(c_ref, w_ref, b_ref, o_ref):
    s = _silu(c_ref[...]).astype(BF16)
    o_ref[...] = _dot(s, w_ref[...].astype(BF16)) + b_ref[...]


def _modulation(c_all, ada_w, ada_b):
    depth, _, n = ada_w.shape
    rows = c_all.shape[0]
    tn = MOD_TILE
    return pl.pallas_call(
        _mod_kernel,
        grid=(depth, n // tn),
        in_specs=[
            pl.BlockSpec((rows, D_MODEL), lambda l, j: (0, 0)),
            pl.BlockSpec((None, D_MODEL, tn), lambda l, j: (l, 0, j)),
            pl.BlockSpec((None, 1, tn), lambda l, j: (l, 0, j)),
        ],
        out_specs=pl.BlockSpec((None, rows, tn), lambda l, j: (l, 0, j)),
        out_shape=jax.ShapeDtypeStruct((depth, rows, n), F32),
        compiler_params=_params("arbitrary", "arbitrary"),
        name="adaln_modulation",
    )(c_all, ada_w, ada_b.reshape(depth, 1, n))


def _rope_block(x, c, s1, s2):
    return x * c + pltpu.roll(x, LANES - ROT_DIM // 2, 1) * s1 + pltpu.roll(x, ROT_DIM // 2, 1) * s2


def _rope_cols(p, c, s1, s2):
    nblk = p.shape[1] // LANES
    return [_rope_block(p[:, j * LANES:(j + 1) * LANES], c, s1, s2) for j in range(nblk)]


def _last_block_tables(c, s1, s2):
    lane = lax.broadcasted_iota(jnp.int32, c.shape, 1)
    is_key = lane < IDX_DIM
    return jnp.where(is_key, c, 1.0), jnp.where(is_key, s1, 0.0), jnp.where(is_key, s2, 0.0)


def _rope_rows(xt, cos_t, sin_t):
    half = ROT_DIM // 2
    out = []
    for base in range(0, LANES, HEAD_DIM):
        x1, x2 = xt[base:base + half], xt[base + half:base + ROT_DIM]
        out += [x1 * cos_t - x2 * sin_t, x2 * cos_t + x1 * sin_t, xt[base + ROT_DIM:base + HEAD_DIM]]
    return jnp.concatenate(out, axis=0)


def _proj_prompt_kernel(x_ref, sc_ref, sh_ref, g_ref, w_ref, c_ref, s1_ref, s2_ref, ct_ref, st_ref,
                        k_ref, v_ref, kit_ref, qt_ref, qit_ref, kg_ref, vt_ref, kib_ref, wit_ref):
    h = _norm_mod(x_ref[...], g_ref[...], sc_ref[...], sh_ref[...]).astype(BF16)
    c, s1, s2 = c_ref[...], s1_ref[...], s2_ref[...]
    cos_t, sin_t = ct_ref[...], st_ref[...]

    pq = _dot(h, w_ref[:, 0:Q_COLS])
    for j in range(Q_COLS // LANES):
        qt = _rope_rows(pq[:, j * LANES:(j + 1) * LANES].T, cos_t, sin_t)
        qt_ref[j * LANES:(j + 1) * LANES, :] = (qt * (HEAD_DIM ** -0.5 * LOG2_E)).astype(BF16)

    pk = _dot(h, w_ref[:, Q_COLS:Q_COLS + KV_COLS])
    k = jnp.concatenate(_rope_cols(pk, c, s1, s2), axis=1)
    k_ref[...] = k
    for g in range(N_KV_HEADS):
        kg_ref[g] = k[:, g * HEAD_DIM:(g + 1) * HEAD_DIM].astype(BF16)

    pv = _dot(h, w_ref[:, Q_COLS + KV_COLS:Q_COLS + 2 * KV_COLS])
    v_ref[...] = pv
    ones = jnp.ones((V_ROWS - HEAD_DIM, ATTN_CHUNK), BF16)
    for cc in range(pv.shape[0] // ATTN_CHUNK):
        vt = pv[cc * ATTN_CHUNK:(cc + 1) * ATTN_CHUNK, :].astype(BF16).T
        for g in range(N_KV_HEADS):
            vt_ref[cc, g * V_ROWS:g * V_ROWS + HEAD_DIM, :] = vt[g * HEAD_DIM:(g + 1) * HEAD_DIM, :]
            vt_ref[cc, g * V_ROWS + HEAD_DIM:(g + 1) * V_ROWS, :] = ones

    o = Q_COLS + 2 * KV_COLS
    pqi = _dot(h, w_ref[:, o:o + IQ_COLS])
    for j in range(IQ_COLS // LANES):
        qit = _rope_rows(pqi[:, j * LANES:(j + 1) * LANES].T, cos_t, sin_t)
        qit_ref[j * LANES:(j + 1) * LANES, :] = qit.astype(BF16)

    pl_ = _dot(h, w_ref[:, o + IQ_COLS:ATTN_IN_PAD])
    last = _rope_block(pl_, *_last_block_tables(c, s1, s2))
    last_t = last.T
    kit_ref[...] = last_t[0:IDX_DIM, :]
    kib_ref[...] = last[:, 0:IDX_DIM].astype(BF16)
    wit_ref[...] = last_t[IDX_DIM:IDX_DIM + IDX_HEADS, :]


def _proj_prompt(x, sc, sh, g, w, rope, rope_t):
    nb, t, _ = x.shape
    tm = ROW_TILE
    row = lambda width: pl.BlockSpec((None, tm, width), lambda b, j: (b, j, 0))
    mod = pl.BlockSpec((None, 1, D_MODEL), lambda b, j: (b, 0, 0))
    tab = pl.BlockSpec((tm, LANES), lambda b, j: (j, 0))
    tab_t = pl.BlockSpec((ROT_DIM // 2, tm), lambda b, j: (0, j))
    col = lambda height: pl.BlockSpec((None, height, tm), lambda b, j: (b, 0, j))
    out_shape = (
        jax.ShapeDtypeStruct((nb, t, KV_COLS), F32),
        jax.ShapeDtypeStruct((nb, t, KV_COLS), F32),
        jax.ShapeDtypeStruct((nb, IDX_DIM, t), F32),
        jax.ShapeDtypeStruct((nb, Q_COLS, t), BF16),
        jax.ShapeDtypeStruct((nb, IQ_COLS, t), BF16),
        jax.ShapeDtypeStruct((nb, N_KV_HEADS, t, HEAD_DIM), BF16),
        jax.ShapeDtypeStruct((nb, t // ATTN_CHUNK, N_KV_HEADS * V_ROWS, ATTN_CHUNK), BF16),
        jax.ShapeDtypeStruct((nb, t, IDX_DIM), BF16),
        jax.ShapeDtypeStruct((nb, IDX_HEADS, t), F32),
    )
    out_specs = (
        row(KV_COLS), row(KV_COLS), col(IDX_DIM), col(Q_COLS), col(IQ_COLS),
        pl.BlockSpec((None, N_KV_HEADS, tm, HEAD_DIM), lambda b, j: (b, 0, j, 0)),
        pl.BlockSpec((None, tm // ATTN_CHUNK, N_KV_HEADS * V_ROWS, ATTN_CHUNK), lambda b, j: (b, j, 0, 0)),
        row(IDX_DIM), col(IDX_HEADS),
    )
    return pl.pallas_call(
        _proj_prompt_kernel,
        grid=(nb, t // tm),
        in_specs=[row(D_MODEL), mod, mod, _const_spec((1, D_MODEL)),
                  _const_spec((D_MODEL, ATTN_IN_PAD)), tab, tab, tab, tab_t, tab_t],
        out_specs=out_specs,
        out_shape=out_shape,
        compiler_params=_params("arbitrary", "arbitrary"),
        name="attn_proj_prompt",
    )(x, sc, sh, g, w, *rope, *rope_t)


def _proj_sample_kernel(x_ref, sc_ref, sh_ref, g_ref, w_ref, c_ref, s1_ref, s2_ref, p_ref):
    h = _norm_mod(x_ref[...], g_ref[...], sc_ref[...], sh_ref[...]).astype(BF16)
    c, s1, s2 = c_ref[...], s1_ref[...], s2_ref[...]
    p = _dot(h, w_ref[...])
    v_lo = (Q_COLS + KV_COLS) // LANES
    v_hi = (Q_COLS + 2 * KV_COLS) // LANES
    nblk = ATTN_IN_PAD // LANES
    for j in range(nblk):
        blk = p[:, j * LANES:(j + 1) * LANES]
        if j == nblk - 1:
            blk = _rope_block(blk, *_last_block_tables(c, s1, s2))
        elif not (v_lo <= j < v_hi):
            blk = _rope_block(blk, c, s1, s2)
        p_ref[:, j * LANES:(j + 1) * LANES] = blk


def _proj_sample(x, sc, sh, g, w, rope):
    rows = x.shape[0]
    full = lambda width: pl.BlockSpec((rows, width), lambda i: (0, 0))
    return pl.pallas_call(
        _proj_sample_kernel,
        grid=(1,),
        in_specs=[full(D_MODEL), full(D_MODEL), full(D_MODEL),
                  pl.BlockSpec((1, D_MODEL), lambda i: (0, 0)),
                  pl.BlockSpec((D_MODEL, ATTN_IN_PAD), lambda i: (0, 0)),
                  full(LANES), full(LANES), full(LANES)],
        out_specs=full(ATTN_IN_PAD),
        out_shape=jax.ShapeDtypeStruct((rows, ATTN_IN_PAD), F32),
        compiler_params=_params("arbitrary"),
        name="attn_proj_sample",
    )(x, sc, sh, g, w, *rope)


FOLD_ROWS = 32
BISECT_STEPS_FIRST = 14
BISECT_STEPS_PER_TEST = 3


def _fold(x, op):
    rows, tq = x.shape
    x3 = x.reshape(rows // FOLD_ROWS, FOLD_ROWS, tq)
    return {"sum": x3.sum, "max": x3.max, "min": x3.min}[op](axis=0)


def _stats_init(tq):
    zeros = jnp.zeros((FOLD_ROWS, tq), F32)
    return (jnp.full((FOLD_ROWS, tq), -jnp.inf, F32), jnp.full((FOLD_ROWS, tq), jnp.inf, F32),
            zeros, zeros, zeros)


def _stats_update(stats, s):
    mx, mn, n_above, n_ge0, n_gt0 = stats
    ones_where = lambda m: _fold(jnp.where(m, 1.0, 0.0), "sum")
    above = s > -jnp.inf
    return (jnp.maximum(mx, _fold(s, "max")),
            jnp.minimum(mn, _fold(jnp.where(above, s, jnp.inf), "min")),
            n_above + ones_where(above), n_ge0 + ones_where(s >= 0.0), n_gt0 + ones_where(s > 0.0))


def _select_to_bias(sc_ref, tie_ref, qpos, nk, kc, stats):
    tq = sc_ref.shape[1]
    topk = float(TOPK)

    slab = 4 * FOLD_ROWS

    def for_slabs(fn, init, slab=slab):
        def body(c, carry):
            for j in range(kc // slab):
                start = pl.multiple_of(c * kc + j * slab, slab)
                idx = start + lax.broadcasted_iota(jnp.int32, (slab, tq), 0)
                carry = fn(pl.ds(start, slab), idx, carry)
            return carry
        return lax.fori_loop(0, nk, body, init)

    def ones_where(m):
        return _fold(jnp.where(m, 1.0, 0.0), "sum")

    def count(pred):
        acc = for_slabs(lambda rows, idx, acc: acc + ones_where(pred(rows)), jnp.zeros((FOLD_ROWS, tq), F32))
        return acc.sum(axis=0, keepdims=True)

    def count_ge(t):
        return count(lambda rows: sc_ref[rows, :] >= t)

    mx, mn, n_above, n_ge0, n_gt0 = stats
    mx = mx.max(axis=0, keepdims=True)
    mn = mn.min(axis=0, keepdims=True)
    n_above = n_above.sum(axis=0, keepdims=True)
    n_ge0 = n_ge0.sum(axis=0, keepdims=True)
    n_gt0 = n_gt0.sum(axis=0, keepdims=True)
    n_max = count_ge(mx)

    few = n_above < topk
    flat = n_max >= topk
    zero = (n_gt0 < topk) & (n_ge0 >= topk)
    pos = n_gt0 >= topk
    lo = jnp.where(few, -jnp.inf, jnp.where(flat, mx, jnp.where(zero | pos, 0.0, mn)))
    hi = jnp.where(few, mn, jnp.where(flat, jnp.inf, jnp.where(pos, mx, 0.0)))
    n_lo = jnp.where(few, 2.0 * topk, jnp.where(flat, n_max, jnp.where(zero | pos, n_ge0, n_above)))
    n_hi = jnp.where(few, n_above, jnp.where(flat, 0.0, jnp.where(zero, n_gt0, jnp.where(pos, n_max, n_ge0))))
    done = jnp.where(few | flat | zero | (n_lo == topk), 1.0, 0.0)

    def not_finished(d):
        return (jnp.min(d) < 0.5).astype(jnp.int32)

    def bisect_step(state):
        lo, hi, n_lo, n_hi, done = state
        mid = jnp.clip(0.5 * lo + 0.5 * hi, -F32_MAX, F32_MAX)
        stuck = (mid <= lo) | (mid >= hi)
        n_mid = count_ge(mid)
        move = (done < 0.5) & jnp.logical_not(stuck)
        up = move & (n_mid >= topk)
        down = move & (n_mid < topk)
        lo = jnp.where(up, mid, lo)
        n_lo = jnp.where(up, n_mid, n_lo)
        hi = jnp.where(down, mid, hi)
        n_hi = jnp.where(down, n_mid, n_hi)
        done = jnp.where(stuck | (n_lo == topk), 1.0, done)
        return lo, hi, n_lo, n_hi, done

    def bisect(carry):
        state = lax.fori_loop(0, carry[6], lambda _, st: bisect_step(st), carry[:5])
        return (*state, not_finished(state[4]), jnp.int32(BISECT_STEPS_PER_TEST))

    lo, hi, n_lo, n_hi, done, _, _ = lax.while_loop(
        lambda carry: carry[5] > 0, bisect,
        (lo, hi, n_lo, n_hi, done, not_finished(done), jnp.int32(BISECT_STEPS_FIRST)))

    thr = lo
    exact = n_lo == topk
    need = topk - n_hi
    all_keys = jnp.full((1, tq), 2 ** 30, jnp.int32)

    def tie_break():
        def mark(rows, idx, _):
            tie_ref[rows, :] = jnp.where(sc_ref[rows, :] == thr, idx, 2 ** 30)
            return 0
        for_slabs(mark, 0)

        n_bits = (sc_ref.shape[0] - 1).bit_length()

        def step(i, j):
            cand = j + lax.shift_right_logical(jnp.int32(1 << (n_bits - 1)), i)
            n_before = count(lambda rows: tie_ref[rows, :] < cand)
            return jnp.where(n_before < need, cand, j)
        return lax.fori_loop(0, n_bits, step, jnp.zeros((1, tq), jnp.int32))

    any_tie = jnp.min(jnp.where(exact, 1.0, 0.0)) < 0.5

    def write(keep_fn):
        def body(rows, idx, _):
            sc_ref[rows, :] = jnp.where(keep_fn(sc_ref[rows, :], idx), 0.0, MASK_BIAS)
            return 0
        return for_slabs(body, 0, slab=kc)

    def write_with_ties():
        last_tied = jnp.minimum(jnp.where(exact, all_keys, tie_break()), qpos)
        return write(lambda s, idx: (s > thr) | ((s == thr) & (idx <= last_tied)))

    lax.cond(any_tie, write_with_ties, lambda: write(lambda s, idx: s >= thr))


def _prompt_attn_kernel(qt_ref, qit_ref, wit_ref, kg_ref, vt_ref, kib_ref, o_ref,
                        sc_ref, tie_ref, acc_ref, m_ref, mx_ref, s_ref, p_ref):
    i = pl.program_id(1)
    tq = qt_ref.shape[1]
    qpos = i * tq + lax.broadcasted_iota(jnp.int32, (1, tq), 1)
    n_keys = (i + 1) * tq

    n_score = (n_keys + SCORE_CHUNK - 1) // SCORE_CHUNK

    def score_chunk(c, stats):
        for j in range(SCORE_CHUNK // SCORE_SLAB):
            start = pl.multiple_of(c * SCORE_CHUNK + j * SCORE_SLAB, SCORE_SLAB)
            kc = kib_ref[pl.ds(start, SCORE_SLAB), :]
            acc = jnp.zeros((SCORE_SLAB, tq), F32)
            for h in range(IDX_HEADS):
                d = _dot(kc, qit_ref[h * IDX_DIM:(h + 1) * IDX_DIM, :])
                acc = acc + jnp.maximum(d, 0.0) * wit_ref[h:h + 1, :]
            idx = start + lax.broadcasted_iota(jnp.int32, (SCORE_SLAB, tq), 0)
            s = jnp.where(idx <= qpos, acc, -jnp.inf)
            sc_ref[pl.ds(start, SCORE_SLAB), :] = s
            stats = _stats_update(stats, s)
        return stats

    stats = lax.fori_loop(0, n_score, score_chunk, _stats_init(tq))

    _select_to_bias(sc_ref, tie_ref, qpos, n_score, SCORE_CHUNK, stats)

    m_ref[...] = jnp.full(m_ref.shape, MASK_BIAS, F32)
    acc_ref[...] = jnp.zeros(acc_ref.shape, F32)

    dyn0 = jnp.minimum(i, 0)

    def attn_chunk(c, _):
        start = pl.multiple_of(c * ATTN_CHUNK, ATTN_CHUNK)
        for h in range(N_HEADS):
            s = _dot(kg_ref[h // HEADS_PER_KV, pl.ds(start, ATTN_CHUNK), :],
                     qt_ref[h * HEAD_DIM:(h + 1) * HEAD_DIM, :])
            s = s + sc_ref[pl.ds(start, ATTN_CHUNK), :]
            s_ref[h + dyn0] = s
            mx_ref[h] = s.max(axis=0, keepdims=True)
        for h in range(N_HEADS):
            g = h // HEADS_PER_KV
            m_old = m_ref[h]
            m_new = jnp.maximum(m_old, mx_ref[h])
            alpha = jnp.exp2(m_old - m_new)
            p_ref[h % P_SLOTS + dyn0] = jnp.exp2(s_ref[h + dyn0] - m_new).astype(BF16)
            vt = vt_ref[c, g * V_ROWS:(g + 1) * V_ROWS, :]
            acc_ref[h] = alpha * acc_ref[h] + _dot(vt, p_ref[h % P_SLOTS + dyn0])
            m_ref[h] = m_new
        return 0

    n_chunks = (i + 1) * (tq // ATTN_CHUNK)

    def attn_pair(k, _):
        attn_chunk(2 * k, 0)
        attn_chunk(2 * k + 1, 0)
        return 0

    lax.fori_loop(0, n_chunks // 2, attn_pair, 0)

    @pl.when(n_chunks % 2 == 1)
    def _():
        attn_chunk(n_chunks - 1, 0)

    for pair in range(N_HEADS // 2):
        two = []
        for h in (2 * pair, 2 * pair + 1):
            a = acc_ref[h]
            two.append(a[0:HEAD_DIM, :] / a[HEAD_DIM:HEAD_DIM + 1, :])
        col = 2 * pair * HEAD_DIM
        o_ref[:, col:col + 2 * HEAD_DIM] = jnp.concatenate(two, axis=0).T.astype(BF16)


def _prompt_attention(qt, qit, wit, kg, vt, kib):
    nb, _, t = qt.shape
    tq = Q_TILE
    col = lambda height: pl.BlockSpec((None, height, tq), lambda b, i: (b, 0, i))
    return pl.pallas_call(
        _prompt_attn_kernel,
        grid=(nb, t // tq),
        in_specs=[
            col(Q_COLS), col(IQ_COLS), col(IDX_HEADS),
            pl.BlockSpec((None, N_KV_HEADS, t, HEAD_DIM), lambda b, i: (b, 0, 0, 0)),
            pl.BlockSpec((None, t // ATTN_CHUNK, N_KV_HEADS * V_ROWS, ATTN_CHUNK), lambda b, i: (b, 0, 0, 0)),
            pl.BlockSpec((None, t, IDX_DIM), lambda b, i: (b, 0, 0)),
        ],
        out_specs=pl.BlockSpec((None, tq, Q_COLS), lambda b, i: (b, i, 0)),
        out_shape=jax.ShapeDtypeStruct((nb, t, Q_COLS), BF16),
        scratch_shapes=[
            pltpu.VMEM((t, tq), F32),
            pltpu.VMEM((t, tq), jnp.int32),
            pltpu.VMEM((N_HEADS, V_ROWS, tq), F32),
            pltpu.VMEM((N_HEADS, 1, tq), F32),
            pltpu.VMEM((N_HEADS, 1, tq), F32),
            pltpu.VMEM((N_HEADS, ATTN_CHUNK, tq), F32),
            pltpu.VMEM((P_SLOTS, ATTN_CHUNK, tq), BF16),
        ],
        compiler_params=_params("arbitrary", "arbitrary"),
        name="prompt_attention",
    )(qt, qit, wit, kg, vt, kib)


def _sample_score_kernel(pt_ref, qi_ref, w_ref, *rest, n_pages):
    del pt_ref
    n_seq = qi_ref.shape[0]
    pages, (new_ref, o_ref, kall_ref) = rest[:n_seq * n_pages], rest[n_seq * n_pages:]
    for q in range(n_seq):
        for j in range(n_pages):
            kall_ref[q, :, j * PAGE_SIZE:(j + 1) * PAGE_SIZE] = pages[q * n_pages + j][...].astype(BF16)
        kall_ref[q, :, n_pages * PAGE_SIZE:(n_pages + 1) * PAGE_SIZE] = new_ref[q].astype(BF16)
    for q in range(n_seq):
        d = _dot(qi_ref[q], kall_ref[q])
        r = jnp.maximum(d, 0.0) * w_ref[q]
        o_ref[q] = r.reshape(IDX_HEADS, SUBLANES, r.shape[1]).sum(axis=0)


def _sample_scores(page_table, qi8, w8, cache_ki_t, ki_new_t):
    nb, n_pages = page_table.shape
    n_keys = (n_pages + 1) * PAGE_SIZE
    rows = IDX_HEADS * SUBLANES
    ns = SCORE_SEQS
    per_step = lambda r, w: pl.BlockSpec((ns, r, w), lambda b, pt: (b, 0, 0))
    grid_spec = pltpu.PrefetchScalarGridSpec(
        num_scalar_prefetch=1,
        grid=(nb // ns,),
        in_specs=[per_step(rows, IDX_DIM), per_step(rows, 1)]
        + [pl.BlockSpec((None, IDX_DIM, PAGE_SIZE), lambda b, pt, q=q, j=j: (pt[b * ns + q, j], 0, 0))
           for q in range(ns) for j in range(n_pages)]
        + [per_step(IDX_DIM, PAGE_SIZE)],
        out_specs=per_step(SUBLANES, n_keys),
        scratch_shapes=[pltpu.VMEM((ns, IDX_DIM, n_keys), BF16)],
    )
    return pl.pallas_call(
        functools.partial(_sample_score_kernel, n_pages=n_pages),
        grid_spec=grid_spec,
        out_shape=jax.ShapeDtypeStruct((nb, SUBLANES, n_keys), F32),
        compiler_params=_params("arbitrary"),
        name="sample_indexer_scores",
    )(page_table, qi8, w8, *([cache_ki_t] * (ns * n_pages)), ki_new_t)


def _sample_select_kernel(s_ref, o_ref, sc_ref, tie_ref, *, past, period):
    tq = s_ref.shape[1]
    n_keys = s_ref.shape[0]
    lane = lax.broadcasted_iota(jnp.int32, (1, tq), 1)
    qpos = past + (lane & (period - 1))
    nk = n_keys // PAGE_SIZE
    stats = _stats_init(tq)
    for c in range(nk):
        idx = c * PAGE_SIZE + lax.broadcasted_iota(jnp.int32, (PAGE_SIZE, tq), 0)
        s = jnp.where(idx <= qpos, s_ref[c * PAGE_SIZE:(c + 1) * PAGE_SIZE, :], -jnp.inf)
        sc_ref[c * PAGE_SIZE:(c + 1) * PAGE_SIZE, :] = s
        stats = _stats_update(stats, s)
    _select_to_bias(sc_ref, tie_ref, qpos, nk, PAGE_SIZE, stats)
    o_ref[...] = sc_ref[...]


def _sample_select(scores_t, past, period):
    n_keys, nq = scores_t.shape
    tq = Q_TILE
    blk = pl.BlockSpec((n_keys, tq), lambda i: (0, i))
    return pl.pallas_call(
        functools.partial(_sample_select_kernel, past=past, period=period),
        grid=(nq // tq,),
        in_specs=[blk],
        out_specs=blk,
        out_shape=jax.ShapeDtypeStruct((n_keys, nq), F32),
        scratch_shapes=[pltpu.VMEM((n_keys, tq), F32), pltpu.VMEM((n_keys, tq), jnp.int32)],
        compiler_params=_params("arbitrary"),
        name="sample_select",
    )(scores_t)


def _sample_attn_kernel(pt_ref, q_ref, bias_ref, *rest, n_pages):
    del pt_ref
    n_seq = q_ref.shape[0]
    kpages, vpages = rest[:n_seq * n_pages], rest[n_seq * n_pages:2 * n_seq * n_pages]
    knew_ref, vnew_ref, o_ref, kall_ref, vall_ref, s_ref, p_ref, l_ref = rest[2 * n_seq * n_pages:]
    n_keys = bias_ref.shape[2]
    fill = jnp.zeros((PAGE_SIZE - SUBLANES, LANES), F32)
    dyn0 = jnp.minimum(pl.program_id(0), 0)
    for q in range(n_seq):
        for j in range(n_pages):
            kall_ref[q + dyn0, :, :, j * PAGE_SIZE:(j + 1) * PAGE_SIZE] = kpages[q * n_pages + j][...].astype(BF16)
            vall_ref[q + dyn0, :, :, j * PAGE_SIZE:(j + 1) * PAGE_SIZE] = vpages[q * n_pages + j][...].astype(BF16)
        for g in range(N_KV_HEADS):
            for new_ref, all_ref in ((knew_ref, kall_ref), (vnew_ref, vall_ref)):
                page = jnp.concatenate([new_ref[q, g], fill], axis=0).T
                all_ref[q + dyn0, g, :, n_pages * PAGE_SIZE:(n_pages + 1) * PAGE_SIZE] = (
                    page[0:HEAD_DIM, :].astype(BF16))
    for q in range(n_seq):
        bias = jnp.broadcast_to(bias_ref[q][None], (HEADS_PER_KV, SUBLANES, n_keys)).reshape(
            HEADS_PER_KV * SUBLANES, n_keys)
        for g in range(N_KV_HEADS):
            s_ref[q, g] = _dot(q_ref[q, g], kall_ref[q + dyn0, g]) + bias
    for q in range(n_seq):
        for g in range(N_KV_HEADS):
            s = s_ref[q, g]
            m = s.max(axis=1, keepdims=True)
            p = jnp.exp(s - m)
            l_ref[q, g] = p.sum(axis=1, keepdims=True)
            p_ref[q, g] = p.astype(BF16)
    for q in range(n_seq):
        for g in range(N_KV_HEADS):
            o_ref[q, g] = _dot_nt(p_ref[q, g], vall_ref[q + dyn0, g]) / l_ref[q, g]


def _sample_attention(page_table, q8, bias8, cache_k_t, cache_v_t, k_new_t, v_new_t):
    nb, n_pages = page_table.shape
    n_keys = (n_pages + 1) * PAGE_SIZE
    rows = HEADS_PER_KV * SUBLANES
    ns = ATTN_SEQS
    page = [pl.BlockSpec((None, N_KV_HEADS, HEAD_DIM, PAGE_SIZE),
                         lambda b, pt, q=q, j=j: (pt[b * ns + q, j], 0, 0, 0))
            for q in range(ns) for j in range(n_pages)]
    new = pl.BlockSpec((ns, N_KV_HEADS, SUBLANES, LANES), lambda b, pt: (b, 0, 0, 0))
    qo = pl.BlockSpec((ns, N_KV_HEADS, rows, HEAD_DIM), lambda b, pt: (b, 0, 0, 0))
    grid_spec = pltpu.PrefetchScalarGridSpec(
        num_scalar_prefetch=1,
        grid=(nb // ns,),
        in_specs=[qo, pl.BlockSpec((ns, SUBLANES, n_keys), lambda b, pt: (b, 0, 0))]
        + page + page + [new, new],
        out_specs=qo,
        scratch_shapes=[pltpu.VMEM((ns, N_KV_HEADS, HEAD_DIM, n_keys), BF16),
                        pltpu.VMEM((ns, N_KV_HEADS, HEAD_DIM, n_keys), BF16),
                        pltpu.VMEM((ns, N_KV_HEADS, rows, n_keys), F32),
                        pltpu.VMEM((ns, N_KV_HEADS, rows, n_keys), BF16),
                        pltpu.VMEM((ns, N_KV_HEADS, rows, 1), F32)],
    )
    pages = ns * n_pages
    return pl.pallas_call(
        functools.partial(_sample_attn_kernel, n_pages=n_pages),
        grid_spec=grid_spec,
        out_shape=jax.ShapeDtypeStruct((nb, N_KV_HEADS, rows, HEAD_DIM), F32),
        compiler_params=_params("arbitrary"),
        name="sample_attention",
    )(page_table, q8, bias8, *([cache_k_t] * pages), *([cache_v_t] * pages), k_new_t, v_new_t)


def _ffn(h, wgu_ref, wd_ref, h_ref, acc_ref, dyn0):
    h_ref[dyn0] = h
    acc_ref[dyn0] = jnp.zeros(acc_ref.shape[1:], F32)
    for c in range(D_FF // FFN_CHUNK):
        gate = _dot(h_ref[dyn0], wgu_ref[:, c * FFN_CHUNK:(c + 1) * FFN_CHUNK])
        up = _dot(h_ref[dyn0], wgu_ref[:, D_FF + c * FFN_CHUNK:D_FF + (c + 1) * FFN_CHUNK])
        a = _silu(gate) * up
        acc_ref[dyn0] += _dot(a.astype(BF16), wd_ref[c * FFN_CHUNK:(c + 1) * FFN_CHUNK, :])
    return acc_ref[dyn0]


def _row_specs(nb, rows, per_row_mod):
    tm = min(ROW_TILE, rows)
    row = pl.BlockSpec((None, tm, D_MODEL), lambda b, j: (b, j, 0))
    if per_row_mod:
        mod = row
    else:
        mod = pl.BlockSpec((None, 1, D_MODEL), lambda b, j: (b, 0, 0))
    return tm, row, mod


def _attn_out_ffn_kernel(x_ref, o_ref, g1_ref, sc2_ref, sh2_ref, g2_ref, n2_ref,
                         wo_ref, wgu_ref, wd_ref, y_ref, h_ref, acc_ref):
    dyn0 = jnp.minimum(pl.program_id(1), 0)
    y1 = x_ref[...] + g1_ref[...] * _dot(o_ref[...], wo_ref[...])
    h2 = _norm_mod(y1, n2_ref[...], sc2_ref[...], sh2_ref[...]).astype(BF16)
    y_ref[...] = y1 + g2_ref[...] * _ffn(h2, wgu_ref, wd_ref, h_ref, acc_ref, dyn0)


def _layer_spec(stacked, layer):
    _, rows, cols = stacked.shape
    return pl.BlockSpec((None, rows, cols), lambda *_: (layer, 0, 0), pipeline_mode=pl.Buffered(1))


def _attn_out_ffn(x, o, g1, sc2, sh2, g2, n2, wo, wgu, wd, layer):
    nb, rows, _ = x.shape
    tm, row, mod = _row_specs(nb, rows, g1.shape[1] == rows)
    return pl.pallas_call(
        _attn_out_ffn_kernel,
        grid=(nb, rows // tm),
        in_specs=[row, row, mod, mod, mod, mod, _const_spec((1, D_MODEL)),
                  _const_spec(wo.shape), _layer_spec(wgu, layer), _layer_spec(wd, layer)],
        out_specs=row,
        out_shape=jax.ShapeDtypeStruct(x.shape, F32),
        scratch_shapes=[pltpu.VMEM((1, tm, D_MODEL), BF16), pltpu.VMEM((1, tm, D_MODEL), F32)],
        compiler_params=_params("arbitrary", "arbitrary"),
        name="attn_out_ffn",
    )(x, o, g1, sc2, sh2, g2, n2, wo, wgu, wd)


def _conv_layer_kernel(x_ref, sc1_ref, sh1_ref, g1_ref, sc2_ref, sh2_ref, g2_ref, p1_ref, p2_ref,
                       n1_ref, n2_ref, nf_ref, win_ref, ck_ref, wout_ref, wgu_ref, wd_ref,
                       y_ref, tail_ref, ubuf_ref, z_ref, h_ref, acc_ref, *, seg):
    j = pl.program_id(1)
    tm = x_ref.shape[0]
    x = x_ref[...]
    h = _norm_mod(x, n1_ref[...], sc1_ref[...], sh1_ref[...]).astype(BF16)

    @pl.when(j == 0)
    def _():
        ubuf_ref[0:SUBLANES, :] = jnp.zeros((SUBLANES, D_MODEL), F32)

    @pl.when(j > 0)
    def _():
        ubuf_ref[0:SUBLANES, :] = ubuf_ref[tm:tm + SUBLANES, :]

    t = (j * tm + lax.broadcasted_iota(jnp.int32, (tm, 1), 0)) & (seg - 1)
    ck = ck_ref[...]
    cw = FFN_CHUNK
    for c in range(D_MODEL // cw):
        cols = slice(c * cw, (c + 1) * cw)
        bg = _dot(h, win_ref[:, c * cw:(c + 1) * cw])
        cg = _dot(h, win_ref[:, D_MODEL + c * cw:D_MODEL + (c + 1) * cw])
        hv = _dot(h, win_ref[:, 2 * D_MODEL + c * cw:2 * D_MODEL + (c + 1) * cw])
        u = cg * hv
        ubuf_ref[SUBLANES:SUBLANES + tm, cols] = u
        um1 = jnp.where(t >= 1, ubuf_ref[SUBLANES - 1:SUBLANES - 1 + tm, cols], p1_ref[:, cols])
        um2 = jnp.where(t >= 2, ubuf_ref[SUBLANES - 2:SUBLANES - 2 + tm, cols], p2_ref[:, cols])
        conv = ck[0:1, cols] * um2 + ck[1:2, cols] * um1 + ck[2:3, cols] * u
        z_ref[:, cols] = (bg * conv).astype(BF16)
    r = tail_ref.shape[0]
    tail_ref[...] = ubuf_ref[SUBLANES + tm - r:SUBLANES + tm, :]

    y1 = x + g1_ref[...] * _dot(z_ref[...], wout_ref[...])
    h2 = _norm_mod(y1, n2_ref[...], sc2_ref[...], sh2_ref[...]).astype(BF16)
    y2 = y1 + g2_ref[...] * _ffn(h2, wgu_ref, wd_ref, h_ref, acc_ref, jnp.minimum(j, 0))
    ms = jnp.mean(y2 * y2, axis=-1, keepdims=True)
    y_ref[...] = (y2 * lax.rsqrt(ms + EPS)) * nf_ref[...]


def _conv_layer(x, mods, p1, p2, n1, n2, nf, win, ck, wout, wgu, wd, *, layer, seg, full_tail):
    nb, rows, _ = x.shape
    per_row = mods[0].shape[1] == rows
    tm, row, mod = _row_specs(nb, rows, per_row)
    if per_row:
        prefix = row
    else:
        prefix = pl.BlockSpec((None, 1, D_MODEL), lambda b, j: (0, 0, 0))
    if full_tail:
        tail_spec, tail_rows = row, rows
    else:
        tail_spec, tail_rows = pl.BlockSpec((None, SUBLANES, D_MODEL), lambda b, j: (b, 0, 0)), SUBLANES
    vec = _const_spec((1, D_MODEL))
    return pl.pallas_call(
        functools.partial(_conv_layer_kernel, seg=seg),
        grid=(nb, rows // tm),
        in_specs=[row] + [mod] * 6 + [prefix, prefix, vec, vec, vec,
                                      _const_spec(win.shape), _const_spec(ck.shape), _const_spec(wout.shape),
                                      _layer_spec(wgu, layer), _layer_spec(wd, layer)],
        out_specs=(row, tail_spec),
        out_shape=(jax.ShapeDtypeStruct(x.shape, F32),
                   jax.ShapeDtypeStruct((nb, tail_rows, D_MODEL), F32)),
        scratch_shapes=[pltpu.VMEM((tm + SUBLANES, D_MODEL), F32), pltpu.VMEM((tm, D_MODEL), BF16),
                        pltpu.VMEM((1, tm, D_MODEL), BF16), pltpu.VMEM((1, tm, D_MODEL), F32)],
        compiler_params=_params("arbitrary", "arbitrary"),
        name="conv_layer",
    )(x, *mods, p1, p2, n1, n2, nf, win, ck, wout, wgu, wd)


def _rope_angles(pos):
    inv = ROPE_THETA ** (-jnp.arange(0, ROT_DIM, 2, dtype=F32) / ROT_DIM)
    ang = pos[:, None] * inv[None, :]
    return jnp.cos(ang), jnp.sin(ang)


def _rope_tables(pos):
    half = ROT_DIM // 2
    cos, sin = _rope_angles(pos)
    n = pos.shape[0]
    rest = HEAD_DIM - ROT_DIM
    c = jnp.concatenate([cos, cos, jnp.ones((n, rest), F32)], axis=1)
    s1 = jnp.concatenate([-sin, jnp.zeros((n, half + rest), F32)], axis=1)
    s2 = jnp.concatenate([jnp.zeros((n, half), F32), sin, jnp.zeros((n, rest), F32)], axis=1)
    return tuple(jnp.tile(a, (1, LANES // HEAD_DIM)) for a in (c, s1, s2))


def kernel(x_prompt, x_sample, c_prompt, c_sample, cache_k, cache_v, cache_kidx, state_conv, page_table,
           ada_w, ada_b, norm1_g, norm2_g, final_g, attn_w_in, attn_w_out, conv_w_in, conv_k, conv_w_out,
           ffn_w_gu, ffn_w_down):
    nb, t, _ = x_prompt.shape
    ns, tn, _ = x_sample.shape
    n_pages = page_table.shape[1]
    past = n_pages * PAGE_SIZE
    rows_s = ns * tn

    pad = (-(nb + ns)) % SUBLANES
    c_all = jnp.concatenate([c_prompt, c_sample, jnp.zeros((pad, D_MODEL), F32)], axis=0)
    mod = _modulation(c_all, ada_w, ada_b)

    def mods(layer):
        cols = [mod[layer, :, i * D_MODEL:(i + 1) * D_MODEL] for i in range(6)]
        prompt = [m[:nb, None, :] for m in cols]
        sample = [jnp.repeat(m[nb:nb + ns], tn, axis=0)[None] for m in cols]
        return prompt, sample

    vec = lambda a: a.reshape(1, D_MODEL)
    w_in = jnp.pad(attn_w_in[0], ((0, 0), (0, ATTN_IN_PAD - ATTN_IN))).astype(BF16)
    w_o = attn_w_out[0].astype(BF16)
    wgu = ffn_w_gu.astype(BF16)
    wd = ffn_w_down.astype(BF16)

    (sh1p, sc1p, g1p, sh2p, sc2p, g2p), (sh1s, sc1s, g1s, sh2s, sc2s, g2s) = mods(0)
    pos_p = jnp.arange(t, dtype=F32)
    rope_p = _rope_tables(pos_p)
    rope_pt = tuple(a.T for a in _rope_angles(pos_p))
    k_p, v_p, ki_p, qt, qit, kg, vt, kib, wit = _proj_prompt(
        x_prompt, sc1p, sh1p, vec(norm1_g[0]), w_in, rope_p, rope_pt)
    o_p = _prompt_attention(qt, qit, wit, kg, vt, kib)
    y_p = _attn_out_ffn(x_prompt, o_p, g1p, sc2p, sh2p, g2p, vec(norm2_g[0]), w_o, wgu, wd, 0)

    pos_s = jnp.tile(past + jnp.arange(tn, dtype=F32), ns)
    rope_s = _rope_tables(pos_s)
    xs = x_sample.reshape(rows_s, D_MODEL)
    proj = _proj_sample(xs, sc1s[0], sh1s[0], vec(norm1_g[0]), w_in, rope_s)
    o = 0
    q_s = proj[:, o:o + Q_COLS].reshape(ns, tn, N_KV_HEADS, HEADS_PER_KV, HEAD_DIM); o += Q_COLS
    k_s = proj[:, o:o + KV_COLS].reshape(ns, tn, KV_COLS); o += KV_COLS
    v_s = proj[:, o:o + KV_COLS].reshape(ns, tn, KV_COLS); o += KV_COLS
    qi_s = proj[:, o:o + IQ_COLS].reshape(ns, tn, IDX_HEADS, IDX_DIM); o += IQ_COLS
    ki_s = proj[:, o:o + IDX_DIM].reshape(ns, tn, IDX_DIM); o += IDX_DIM
    wi_s = proj[:, o:o + IDX_HEADS].reshape(ns, tn, IDX_HEADS)

    qpad = ((0, 0), (0, SUBLANES - tn))
    qi8 = jnp.pad(qi_s, qpad + ((0, 0), (0, 0))).transpose(0, 2, 1, 3)
    qi8 = qi8.reshape(ns, IDX_HEADS * SUBLANES, IDX_DIM).astype(BF16)
    w8 = jnp.pad(wi_s, qpad + ((0, 0),)).transpose(0, 2, 1).reshape(ns, IDX_HEADS * SUBLANES, 1)
    slot_pad = (0, PAGE_SIZE - tn)
    ki_new_t = jnp.pad(ki_s.transpose(0, 2, 1), ((0, 0), (0, 0), slot_pad))
    scores = _sample_scores(page_table, qi8, w8, cache_kidx[0].transpose(0, 2, 1), ki_new_t)
    n_keys = scores.shape[2]
    scores_t = scores[:, :tn, :].transpose(2, 0, 1).reshape(n_keys, rows_s)
    bias_t = _sample_select(scores_t, past, tn)
    bias8 = jnp.pad(bias_t.reshape(n_keys, ns, tn).transpose(1, 2, 0), qpad + ((0, 0),))

    q8 = jnp.pad(q_s * HEAD_DIM ** -0.5, qpad + ((0, 0), (0, 0), (0, 0))).transpose(0, 2, 3, 1, 4)
    q8 = q8.reshape(ns, N_KV_HEADS, HEADS_PER_KV * SUBLANES, HEAD_DIM).astype(BF16)
    new_t = lambda a: jnp.pad(a.reshape(ns, tn, N_KV_HEADS, HEAD_DIM).transpose(0, 2, 1, 3),
                              ((0, 0), (0, 0), (0, SUBLANES - tn), (0, LANES - HEAD_DIM)))
    o_g = _sample_attention(page_table, q8, bias8,
                            cache_k[0].transpose(0, 2, 3, 1), cache_v[0].transpose(0, 2, 3, 1),
                            new_t(k_s), new_t(v_s))
    o_g = o_g.reshape(ns, N_KV_HEADS, HEADS_PER_KV, SUBLANES, HEAD_DIM)[:, :, :, :tn]
    o_s = o_g.transpose(0, 3, 1, 2, 4).reshape(1, rows_s, Q_COLS).astype(BF16)
    y_s = _attn_out_ffn(xs[None], o_s, g1s, sc2s, sh2s, g2s, vec(norm2_g[0]), w_o, wgu, wd, 0)

    (sh1p, sc1p, g1p, sh2p, sc2p, g2p), (sh1s, sc1s, g1s, sh2s, sc2s, g2s) = mods(1)
    win = conv_w_in[0].astype(BF16)
    wout = conv_w_out[0].astype(BF16)
    zero_prefix = jnp.zeros((1, 1, D_MODEL), F32)
    out_p, tail_p = _conv_layer(
        y_p, (sc1p, sh1p, g1p, sc2p, sh2p, g2p), zero_prefix, zero_prefix,
        vec(norm1_g[1]), vec(norm2_g[1]), vec(final_g), win, conv_k[0], wout, wgu, wd,
        layer=1, seg=t, full_tail=False)
    st = state_conv[0]
    zeros_row = jnp.zeros((ns, 1, D_MODEL), F32)
    p1 = jnp.concatenate([st[:, 1:2], zeros_row, zeros_row, zeros_row], axis=1).reshape(1, rows_s, D_MODEL)
    p2 = jnp.concatenate([st[:, 0:1], st[:, 1:2], zeros_row, zeros_row], axis=1).reshape(1, rows_s, D_MODEL)
    out_s, u_s = _conv_layer(
        y_s, (sc1s, sh1s, g1s, sc2s, sh2s, g2s), p1, p2,
        vec(norm1_g[1]), vec(norm2_g[1]), vec(final_g), win, conv_k[0], wout, wgu, wd,
        layer=1, seg=tn, full_tail=True)

    keep = CONV_WIDTH - 1
    return (
        out_p,
        out_s.reshape(ns, tn, D_MODEL),
        k_p.reshape(1, nb, t, N_KV_HEADS, HEAD_DIM),
        v_p.reshape(1, nb, t, N_KV_HEADS, HEAD_DIM),
        ki_p.transpose(0, 2, 1)[None],
        tail_p[None, :, SUBLANES - keep:, :],
        k_s.reshape(1, ns, tn, N_KV_HEADS, HEAD_DIM),
        v_s.reshape(1, ns, tn, N_KV_HEADS, HEAD_DIM),
        ki_s[None],
        u_s.reshape(ns, tn, D_MODEL)[None, :, tn - keep:, :],
    )
```

```python
import functools

import jax
import jax.numpy as jnp
from jax import lax
from jax.experimental import pallas as pl
from jax.experimental.pallas import tpu as pltpu

F32 = jnp.float32
BF16 = jnp.bfloat16

D_MODEL = 1024
N_HEADS = 16
HEAD_DIM = 64
N_KV_HEADS = 4
HEADS_PER_KV = N_HEADS // N_KV_HEADS
ROT_DIM = 16
ROPE_THETA = 500000.0
IDX_HEADS = 8
IDX_DIM = 64
TOPK = 256
PAGE_SIZE = 128
CONV_WIDTH = 3
D_FF = 2816
EPS = 1e-6
Q_COLS = N_HEADS * HEAD_DIM
KV_COLS = N_KV_HEADS * HEAD_DIM
IQ_COLS = IDX_HEADS * IDX_DIM
ATTN_IN = Q_COLS + 2 * KV_COLS + IQ_COLS + IDX_DIM + IDX_HEADS
LANES = 128
SUBLANES = 8
BF16_SUBLANES = 16
ATTN_IN_PAD = -(-ATTN_IN // LANES) * LANES
MASK_BIAS = -1e30
F32_MAX = 3.4028234663852886e38
LOG2_E = 1.4426950408889634
VMEM_LIMIT = 60 * 1024 * 1024

ROW_TILE = 512
Q_TILE = 256
SCORE_CHUNK = 512
SCORE_SLAB = 128
ATTN_CHUNK = 256
V_ROWS = HEAD_DIM + BF16_SUBLANES
FFN_CHUNK = 256
MOD_TILE = 1024
P_SLOTS = 4
SCORE_SEQS = 4
ATTN_SEQS = 2


def _dot(a, b):
    return jnp.dot(a, b, preferred_element_type=F32)


def _dot_nt(a, b):
    return lax.dot_general(a, b, (((1,), (1,)), ((), ())), preferred_element_type=F32)


def _silu(x):
    return x / (1.0 + jnp.exp(-x))


def _norm_mod(x, g, sc, sh):
    ms = jnp.mean(x * x, axis=-1, keepdims=True)
    y = x * lax.rsqrt(ms + EPS)
    return (y * g) * (1.0 + sc) + sh


def _params(*sem):
    return pltpu.CompilerParams(dimension_semantics=sem, vmem_limit_bytes=VMEM_LIMIT)


def _const_spec(shape):
    nd = len(shape)
    return pl.BlockSpec(shape, lambda *_: (0,) * nd, pipeline_mode=pl.Buffered(1))


def _mod_kernel(c_ref, w_ref, b_ref, o_ref):
    s = _silu(c_ref[...]).astype(BF16)
    o_ref[...] = _dot(s, w_ref[...].astype(BF16)) + b_ref[...]


def _modulation(c_all, ada_w, ada_b):
    depth, _, n = ada_w.shape
    rows = c_all.shape[0]
    tn = MOD_TILE
    return pl.pallas_call(
        _mod_kernel,
        grid=(depth, n // tn),
        in_specs=[
            pl.BlockSpec((rows, D_MODEL), lambda l, j: (0, 0)),
            pl.BlockSpec((None, D_MODEL, tn), lambda l, j: (l, 0, j)),
            pl.BlockSpec((None, 1, tn), lambda l, j: (l, 0, j)),
        ],
        out_specs=pl.BlockSpec((None, rows, tn), lambda l, j: (l, 0, j)),
        out_shape=jax.ShapeDtypeStruct((depth, rows, n), F32),
        compiler_params=_params("arbitrary", "arbitrary"),
        name="adaln_modulation",
    )(c_all, ada_w, ada_b.reshape(depth, 1, n))


def _rope_block(x, c, s1, s2):
    return x * c + pltpu.roll(x, LANES - ROT_DIM // 2, 1) * s1 + pltpu.roll(x, ROT_DIM // 2, 1) * s2


def _rope_cols(p, c, s1, s2):
    nblk = p.shape[1] // LANES
    return [_rope_block(p[:, j * LANES:(j + 1) * LANES], c, s1, s2) for j in range(nblk)]


def _last_block_tables(c, s1, s2):
    lane = lax.broadcasted_iota(jnp.int32, c.shape, 1)
    is_key = lane < IDX_DIM
    return jnp.where(is_key, c, 1.0), jnp.where(is_key, s1, 0.0), jnp.where(is_key, s2, 0.0)


def _rope_rows(xt, cos_t, sin_t):
    half = ROT_DIM // 2
    out = []
    for base in range(0, LANES, HEAD_DIM):
        x1, x2 = xt[base:base + half], xt[base + half:base + ROT_DIM]
        out += [x1 * cos_t - x2 * sin_t, x2 * cos_t + x1 * sin_t, xt[base + ROT_DIM:base + HEAD_DIM]]
    return jnp.concatenate(out, axis=0)


def _proj_prompt_kernel(x_ref, sc_ref, sh_ref, g_ref, w_ref, c_ref, s1_ref, s2_ref, ct_ref, st_ref,
                        k_ref, v_ref, kit_ref, qt_ref, qit_ref, kg_ref, vt_ref, kib_ref, wit_ref):
    h = _norm_mod(x_ref[...], g_ref[...], sc_ref[...], sh_ref[...]).astype(BF16)
    c, s1, s2 = c_ref[...], s1_ref[...], s2_ref[...]
    cos_t, sin_t = ct_ref[...], st_ref[...]

    pq = _dot(h, w_ref[:, 0:Q_COLS])
    for j in range(Q_COLS // LANES):
        qt = _rope_rows(pq[:, j * LANES:(j + 1) * LANES].T, cos_t, sin_t)
        qt_ref[j * LANES:(j + 1) * LANES, :] = (qt * (HEAD_DIM ** -0.5 * LOG2_E)).astype(BF16)

    pk = _dot(h, w_ref[:, Q_COLS:Q_COLS + KV_COLS])
    k = jnp.concatenate(_rope_cols(pk, c, s1, s2), axis=1)
    k_ref[...] = k
    for g in range(N_KV_HEADS):
        kg_ref[g] = k[:, g * HEAD_DIM:(g + 1) * HEAD_DIM].astype(BF16)

    pv = _dot(h, w_ref[:, Q_COLS + KV_COLS:Q_COLS + 2 * KV_COLS])
    v_ref[...] = pv
    ones = jnp.ones((V_ROWS - HEAD_DIM, ATTN_CHUNK), BF16)
    for cc in range(pv.shape[0] // ATTN_CHUNK):
        vt = pv[cc * ATTN_CHUNK:(cc + 1) * ATTN_CHUNK, :].astype(BF16).T
        for g in range(N_KV_HEADS):
            vt_ref[cc, g * V_ROWS:g * V_ROWS + HEAD_DIM, :] = vt[g * HEAD_DIM:(g + 1) * HEAD_DIM, :]
            vt_ref[cc, g * V_ROWS + HEAD_DIM:(g + 1) * V_ROWS, :] = ones

    o = Q_COLS + 2 * KV_COLS
    pqi = _dot(h, w_ref[:, o:o + IQ_COLS])
    for j in range(IQ_COLS // LANES):
        qit = _rope_rows(pqi[:, j * LANES:(j + 1) * LANES].T, cos_t, sin_t)
        qit_ref[j * LANES:(j + 1) * LANES, :] = qit.astype(BF16)

    pl_ = _dot(h, w_ref[:, o + IQ_COLS:ATTN_IN_PAD])
    last = _rope_block(pl_, *_last_block_tables(c, s1, s2))
    last_t = last.T
    kit_ref[...] = last_t[0:IDX_DIM, :]
    kib_ref[...] = last[:, 0:IDX_DIM].astype(BF16)
    wit_ref[...] = last_t[IDX_DIM:IDX_DIM + IDX_HEADS, :]


def _proj_prompt(x, sc, sh, g, w, rope, rope_t):
    nb, t, _ = x.shape
    tm = ROW_TILE
    row = lambda width: pl.BlockSpec((None, tm, width), lambda b, j: (b, j, 0))
    mod = pl.BlockSpec((None, 1, D_MODEL), lambda b, j: (b, 0, 0))
    tab = pl.BlockSpec((tm, LANES), lambda b, j: (j, 0))
    tab_t = pl.BlockSpec((ROT_DIM // 2, tm), lambda b, j: (0, j))
    col = lambda height: pl.BlockSpec((None, height, tm), lambda b, j: (b, 0, j))
    out_shape = (
        jax.ShapeDtypeStruct((nb, t, KV_COLS), F32),
        jax.ShapeDtypeStruct((nb, t, KV_COLS), F32),
        jax.ShapeDtypeStruct((nb, IDX_DIM, t), F32),
        jax.ShapeDtypeStruct((nb, Q_COLS, t), BF16),
        jax.ShapeDtypeStruct((nb, IQ_COLS, t), BF16),
        jax.ShapeDtypeStruct((nb, N_KV_HEADS, t, HEAD_DIM), BF16),
        jax.ShapeDtypeStruct((nb, t // ATTN_CHUNK, N_KV_HEADS * V_ROWS, ATTN_CHUNK), BF16),
        jax.ShapeDtypeStruct((nb, t, IDX_DIM), BF16),
        jax.ShapeDtypeStruct((nb, IDX_HEADS, t), F32),
    )
    out_specs = (
        row(KV_COLS), row(KV_COLS), col(IDX_DIM), col(Q_COLS), col(IQ_COLS),
        pl.BlockSpec((None, N_KV_HEADS, tm, HEAD_DIM), lambda b, j: (b, 0, j, 0)),
        pl.BlockSpec((None, tm // ATTN_CHUNK, N_KV_HEADS * V_ROWS, ATTN_CHUNK), lambda b, j: (b, j, 0, 0)),
        row(IDX_DIM), col(IDX_HEADS),
    )
    return pl.pallas_call(
        _proj_prompt_kernel,
        grid=(nb, t // tm),
        in_specs=[row(D_MODEL), mod, mod, _const_spec((1, D_MODEL)),
                  _const_spec((D_MODEL, ATTN_IN_PAD)), tab, tab, tab, tab_t, tab_t],
        out_specs=out_specs,
        out_shape=out_shape,
        compiler_params=_params("arbitrary", "arbitrary"),
        name="attn_proj_prompt",
    )(x, sc, sh, g, w, *rope, *rope_t)


def _proj_sample_kernel(x_ref, sc_ref, sh_ref, g_ref, w_ref, c_ref, s1_ref, s2_ref, p_ref):
    h = _norm_mod(x_ref[...], g_ref[...], sc_ref[...], sh_ref[...]).astype(BF16)
    c, s1, s2 = c_ref[...], s1_ref[...], s2_ref[...]
    p = _dot(h, w_ref[...])
    v_lo = (Q_COLS + KV_COLS) // LANES
    v_hi = (Q_COLS + 2 * KV_COLS) // LANES
    nblk = ATTN_IN_PAD // LANES
    for j in range(nblk):
        blk = p[:, j * LANES:(j + 1) * LANES]
        if j == nblk - 1:
            blk = _rope_block(blk, *_last_block_tables(c, s1, s2))
        elif not (v_lo <= j < v_hi):
            blk = _rope_block(blk, c, s1, s2)
        p_ref[:, j * LANES:(j + 1) * LANES] = blk


def _proj_sample(x, sc, sh, g, w, rope):
    rows = x.shape[0]
    full = lambda width: pl.BlockSpec((rows, width), lambda i: (0, 0))
    return pl.pallas_call(
        _proj_sample_kernel,
        grid=(1,),
        in_specs=[full(D_MODEL), full(D_MODEL), full(D_MODEL),
                  pl.BlockSpec((1, D_MODEL), lambda i: (0, 0)),
                  pl.BlockSpec((D_MODEL, ATTN_IN_PAD), lambda i: (0, 0)),
                  full(LANES), full(LANES), full(LANES)],
        out_specs=full(ATTN_IN_PAD),
        out_shape=jax.ShapeDtypeStruct((rows, ATTN_IN_PAD), F32),
        compiler_params=_params("arbitrary"),
        name="attn_proj_sample",
    )(x, sc, sh, g, w, *rope)


FOLD_ROWS = 32
BISECT_STEPS_FIRST = 14
BISECT_STEPS_PER_TEST = 3


def _fold(x, op):
    rows, tq = x.shape
    x3 = x.reshape(rows // FOLD_ROWS, FOLD_ROWS, tq)
    return {"sum": x3.sum, "max": x3.max, "min": x3.min}[op](axis=0)


def _stats_init(tq):
    zeros = jnp.zeros((FOLD_ROWS, tq), F32)
    return (jnp.full((FOLD_ROWS, tq), -jnp.inf, F32), jnp.full((FOLD_ROWS, tq), jnp.inf, F32),
            zeros, zeros, zeros)


def _stats_update(stats, s):
    mx, mn, n_above, n_ge0, n_gt0 = stats
    ones_where = lambda m: _fold(jnp.where(m, 1.0, 0.0), "sum")
    above = s > -jnp.inf
    return (jnp.maximum(mx, _fold(s, "max")),
            jnp.minimum(mn, _fold(jnp.where(above, s, jnp.inf), "min")),
            n_above + ones_where(above), n_ge0 + ones_where(s >= 0.0), n_gt0 + ones_where(s > 0.0))


def _select_to_bias(sc_ref, tie_ref, qpos, nk, kc, stats):
    tq = sc_ref.shape[1]
    topk = float(TOPK)

    slab = 4 * FOLD_ROWS

    def for_slabs(fn, init, slab=slab):
        def body(c, carry):
            for j in range(kc // slab):
                start = pl.multiple_of(c * kc + j * slab, slab)
                idx = start + lax.broadcasted_iota(jnp.int32, (slab, tq), 0)
                carry = fn(pl.ds(start, slab), idx, carry)
            return carry
        return lax.fori_loop(0, nk, body, init)

    def ones_where(m):
        return _fold(jnp.where(m, 1.0, 0.0), "sum")

    def count(pred):
        acc = for_slabs(lambda rows, idx, acc: acc + ones_where(pred(rows)), jnp.zeros((FOLD_ROWS, tq), F32))
        return acc.sum(axis=0, keepdims=True)

    def count_ge(t):
        return count(lambda rows: sc_ref[rows, :] >= t)

    mx, mn, n_above, n_ge0, n_gt0 = stats
    mx = mx.max(axis=0, keepdims=True)
    mn = mn.min(axis=0, keepdims=True)
    n_above = n_above.sum(axis=0, keepdims=True)
    n_ge0 = n_ge0.sum(axis=0, keepdims=True)
    n_gt0 = n_gt0.sum(axis=0, keepdims=True)
    n_max = count_ge(mx)

    few = n_above < topk
    flat = n_max >= topk
    zero = (n_gt0 < topk) & (n_ge0 >= topk)
    pos = n_gt0 >= topk
    lo = jnp.where(few, -jnp.inf, jnp.where(flat, mx, jnp.where(zero | pos, 0.0, mn)))
    hi = jnp.where(few, mn, jnp.where(flat, jnp.inf, jnp.where(pos, mx, 0.0)))
    n_lo = jnp.where(few, 2.0 * topk, jnp.where(flat, n_max, jnp.where(zero | pos, n_ge0, n_above)))
    n_hi = jnp.where(few, n_above, jnp.where(flat, 0.0, jnp.where(zero, n_gt0, jnp.where(pos, n_max, n_ge0))))
    done = jnp.where(few | flat | zero | (n_lo == topk), 1.0, 0.0)

    def not_finished(d):
        return (jnp.min(d) < 0.5).astype(jnp.int32)

    def bisect_step(state):
        lo, hi, n_lo, n_hi, done = state
        mid = jnp.clip(0.5 * lo + 0.5 * hi, -F32_MAX, F32_MAX)
        stuck = (mid <= lo) | (mid >= hi)
        n_mid = count_ge(mid)
        move = (done < 0.5) & jnp.logical_not(stuck)
        up = move & (n_mid >= topk)
        down = move & (n_mid < topk)
        lo = jnp.where(up, mid, lo)
        n_lo = jnp.where(up, n_mid, n_lo)
        hi = jnp.where(down, mid, hi)
        n_hi = jnp.where(down, n_mid, n_hi)
        done = jnp.where(stuck | (n_lo == topk), 1.0, done)
        return lo, hi, n_lo, n_hi, done

    def bisect(carry):
        state = lax.fori_loop(0, carry[6], lambda _, st: bisect_step(st), carry[:5])
        return (*state, not_finished(state[4]), jnp.int32(BISECT_STEPS_PER_TEST))

    lo, hi, n_lo, n_hi, done, _, _ = lax.while_loop(
        lambda carry: carry[5] > 0, bisect,
        (lo, hi, n_lo, n_hi, done, not_finished(done), jnp.int32(BISECT_STEPS_FIRST)))

    thr = lo
    exact = n_lo == topk
    need = topk - n_hi
    all_keys = jnp.full((1, tq), 2 ** 30, jnp.int32)

    def tie_break():
        def mark(rows, idx, _):
            tie_ref[rows, :] = jnp.where(sc_ref[rows, :] == thr, idx, 2 ** 30)
            return 0
        for_slabs(mark, 0)

        n_bits = (sc_ref.shape[0] - 1).bit_length()

        def step(i, j):
            cand = j + lax.shift_right_logical(jnp.int32(1 << (n_bits - 1)), i)
            n_before = count(lambda rows: tie_ref[rows, :] < cand)
            return jnp.where(n_before < need, cand, j)
        return lax.fori_loop(0, n_bits, step, jnp.zeros((1, tq), jnp.int32))

    any_tie = jnp.min(jnp.where(exact, 1.0, 0.0)) < 0.5

    def write(keep_fn):
        def body(rows, idx, _):
            sc_ref[rows, :] = jnp.where(keep_fn(sc_ref[rows, :], idx), 0.0, MASK_BIAS)
            return 0
        return for_slabs(body, 0, slab=kc)

    def write_with_ties():
        last_tied = jnp.minimum(jnp.where(exact, all_keys, tie_break()), qpos)
        return write(lambda s, idx: (s > thr) | ((s == thr) & (idx <= last_tied)))

    lax.cond(any_tie, write_with_ties, lambda: write(lambda s, idx: s >= thr))


def _prompt_attn_kernel(qt_ref, qit_ref, wit_ref, kg_ref, vt_ref, kib_ref, o_ref,
                        sc_ref, tie_ref, acc_ref, m_ref, mx_ref, s_ref, p_ref):
    i = pl.program_id(1)
    tq = qt_ref.shape[1]
    qpos = i * tq + lax.broadcasted_iota(jnp.int32, (1, tq), 1)
    n_keys = (i + 1) * tq

    n_score = (n_keys + SCORE_CHUNK - 1) // SCORE_CHUNK

    def score_chunk(c, stats):
        for j in range(SCORE_CHUNK // SCORE_SLAB):
            start = pl.multiple_of(c * SCORE_CHUNK + j * SCORE_SLAB, SCORE_SLAB)
            kc = kib_ref[pl.ds(start, SCORE_SLAB), :]
            acc = jnp.zeros((SCORE_SLAB, tq), F32)
            for h in range(IDX_HEADS):
                d = _dot(kc, qit_ref[h * IDX_DIM:(h + 1) * IDX_DIM, :])
                acc = acc + jnp.maximum(d, 0.0) * wit_ref[h:h + 1, :]
            idx = start + lax.broadcasted_iota(jnp.int32, (SCORE_SLAB, tq), 0)
            s = jnp.where(idx <= qpos, acc, -jnp.inf)
            sc_ref[pl.ds(start, SCORE_SLAB), :] = s
            stats = _stats_update(stats, s)
        return stats

    stats = lax.fori_loop(0, n_score, score_chunk, _stats_init(tq))

    _select_to_bias(sc_ref, tie_ref, qpos, n_score, SCORE_CHUNK, stats)

    m_ref[...] = jnp.full(m_ref.shape, MASK_BIAS, F32)
    acc_ref[...] = jnp.zeros(acc_ref.shape, F32)

    dyn0 = jnp.minimum(i, 0)

    def attn_chunk(c, _):
        start = pl.multiple_of(c * ATTN_CHUNK, ATTN_CHUNK)
        for h in range(N_HEADS):
            s = _dot(kg_ref[h // HEADS_PER_KV, pl.ds(start, ATTN_CHUNK), :],
                     qt_ref[h * HEAD_DIM:(h + 1) * HEAD_DIM, :])
            s = s + sc_ref[pl.ds(start, ATTN_CHUNK), :]
            s_ref[h + dyn0] = s
            mx_ref[h] = s.max(axis=0, keepdims=True)
        for h in range(N_HEADS):
            g = h // HEADS_PER_KV
            m_old = m_ref[h]
            m_new = jnp.maximum(m_old, mx_ref[h])
            alpha = jnp.exp2(m_old - m_new)
            p_ref[h % P_SLOTS + dyn0] = jnp.exp2(s_ref[h + dyn0] - m_new).astype(BF16)
            vt = vt_ref[c, g * V_ROWS:(g + 1) * V_ROWS, :]
            acc_ref[h] = alpha * acc_ref[h] + _dot(vt, p_ref[h % P_SLOTS + dyn0])
            m_ref[h] = m_new
        return 0

    n_chunks = (i + 1) * (tq // ATTN_CHUNK)

    def attn_pair(k, _):
        attn_chunk(2 * k, 0)
        attn_chunk(2 * k + 1, 0)
        return 0

    lax.fori_loop(0, n_chunks // 2, attn_pair, 0)

    @pl.when(n_chunks % 2 == 1)
    def _():
        attn_chunk(n_chunks - 1, 0)

    for pair in range(N_HEADS // 2):
        two = []
        for h in (2 * pair, 2 * pair + 1):
            a = acc_ref[h]
            two.append(a[0:HEAD_DIM, :] / a[HEAD_DIM:HEAD_DIM + 1, :])
        col = 2 * pair * HEAD_DIM
        o_ref[:, col:col + 2 * HEAD_DIM] = jnp.concatenate(two, axis=0).T.astype(BF16)


def _prompt_attention(qt, qit, wit, kg, vt, kib):
    nb, _, t = qt.shape
    tq = Q_TILE
    col = lambda height: pl.BlockSpec((None, height, tq), lambda b, i: (b, 0, i))
    return pl.pallas_call(
        _prompt_attn_kernel,
        grid=(nb, t // tq),
        in_specs=[
            col(Q_COLS), col(IQ_COLS), col(IDX_HEADS),
            pl.BlockSpec((None, N_KV_HEADS, t, HEAD_DIM), lambda b, i: (b, 0, 0, 0)),
            pl.BlockSpec((None, t // ATTN_CHUNK, N_KV_HEADS * V_ROWS, ATTN_CHUNK), lambda b, i: (b, 0, 0, 0)),
            pl.BlockSpec((None, t, IDX_DIM), lambda b, i: (b, 0, 0)),
        ],
        out_specs=pl.BlockSpec((None, tq, Q_COLS), lambda b, i: (b, i, 0)),
        out_shape=jax.ShapeDtypeStruct((nb, t, Q_COLS), BF16),
        scratch_shapes=[
            pltpu.VMEM((t, tq), F32),
            pltpu.VMEM((t, tq), jnp.int32),
            pltpu.VMEM((N_HEADS, V_ROWS, tq), F32),
            pltpu.VMEM((N_HEADS, 1, tq), F32),
            pltpu.VMEM((N_HEADS, 1, tq), F32),
            pltpu.VMEM((N_HEADS, ATTN_CHUNK, tq), F32),
            pltpu.VMEM((P_SLOTS, ATTN_CHUNK, tq), BF16),
        ],
        compiler_params=_params("arbitrary", "arbitrary"),
        name="prompt_attention",
    )(qt, qit, wit, kg, vt, kib)


def _sample_score_kernel(pt_ref, qi_ref, w_ref, *rest, n_pages):
    del pt_ref
    n_seq = qi_ref.shape[0]
    pages, (new_ref, o_ref, kall_ref) = rest[:n_seq * n_pages], rest[n_seq * n_pages:]
    for q in range(n_seq):
        for j in range(n_pages):
            kall_ref[q, :, j * PAGE_SIZE:(j + 1) * PAGE_SIZE] = pages[q * n_pages + j][...].astype(BF16)
        kall_ref[q, :, n_pages * PAGE_SIZE:(n_pages + 1) * PAGE_SIZE] = new_ref[q].astype(BF16)
    for q in range(n_seq):
        d = _dot(qi_ref[q], kall_ref[q])
        r = jnp.maximum(d, 0.0) * w_ref[q]
        o_ref[q] = r.reshape(IDX_HEADS, SUBLANES, r.shape[1]).sum(axis=0)


def _sample_scores(page_table, qi8, w8, cache_ki_t, ki_new_t):
    nb, n_pages = page_table.shape
    n_keys = (n_pages + 1) * PAGE_SIZE
    rows = IDX_HEADS * SUBLANES
    ns = SCORE_SEQS
    per_step = lambda r, w: pl.BlockSpec((ns, r, w), lambda b, pt: (b, 0, 0))
    grid_spec = pltpu.PrefetchScalarGridSpec(
        num_scalar_prefetch=1,
        grid=(nb // ns,),
        in_specs=[per_step(rows, IDX_DIM), per_step(rows, 1)]
        + [pl.BlockSpec((None, IDX_DIM, PAGE_SIZE), lambda b, pt, q=q, j=j: (pt[b * ns + q, j], 0, 0))
           for q in range(ns) for j in range(n_pages)]
        + [per_step(IDX_DIM, PAGE_SIZE)],
        out_specs=per_step(SUBLANES, n_keys),
        scratch_shapes=[pltpu.VMEM((ns, IDX_DIM, n_keys), BF16)],
    )
    return pl.pallas_call(
        functools.partial(_sample_score_kernel, n_pages=n_pages),
        grid_spec=grid_spec,
        out_shape=jax.ShapeDtypeStruct((nb, SUBLANES, n_keys), F32),
        compiler_params=_params("arbitrary"),
        name="sample_indexer_scores",
    )(page_table, qi8, w8, *([cache_ki_t] * (ns * n_pages)), ki_new_t)


def _sample_select_kernel(s_ref, o_ref, sc_ref, tie_ref, *, past, period):
    tq = s_ref.shape[1]
    n_keys = s_ref.shape[0]
    lane = lax.broadcasted_iota(jnp.int32, (1, tq), 1)
    qpos = past + (lane & (period - 1))
    nk = n_keys // PAGE_SIZE
    stats = _stats_init(tq)
    for c in range(nk):
        idx = c * PAGE_SIZE + lax.broadcasted_iota(jnp.int32, (PAGE_SIZE, tq), 0)
        s = jnp.where(idx <= qpos, s_ref[c * PAGE_SIZE:(c + 1) * PAGE_SIZE, :], -jnp.inf)
        sc_ref[c * PAGE_SIZE:(c + 1) * PAGE_SIZE, :] = s
        stats = _stats_update(stats, s)
    _select_to_bias(sc_ref, tie_ref, qpos, nk, PAGE_SIZE, stats)
    o_ref[...] = sc_ref[...]


def _sample_select(scores_t, past, period):
    n_keys, nq = scores_t.shape
    tq = Q_TILE
    blk = pl.BlockSpec((n_keys, tq), lambda i: (0, i))
    return pl.pallas_call(
        functools.partial(_sample_select_kernel, past=past, period=period),
        grid=(nq // tq,),
        in_specs=[blk],
        out_specs=blk,
        out_shape=jax.ShapeDtypeStruct((n_keys, nq), F32),
        scratch_shapes=[pltpu.VMEM((n_keys, tq), F32), pltpu.VMEM((n_keys, tq), jnp.int32)],
        compiler_params=_params("arbitrary"),
        name="sample_select",
    )(scores_t)


def _sample_attn_kernel(pt_ref, q_ref, bias_ref, *rest, n_pages):
    del pt_ref
    n_seq = q_ref.shape[0]
    kpages, vpages = rest[:n_seq * n_pages], rest[n_seq * n_pages:2 * n_seq * n_pages]
    knew_ref, vnew_ref, o_ref, kall_ref, vall_ref, s_ref, p_ref, l_ref = rest[2 * n_seq * n_pages:]
    n_keys = bias_ref.shape[2]
    fill = jnp.zeros((PAGE_SIZE - SUBLANES, LANES), F32)
    for q in range(n_seq):
        for j in range(n_pages):
            kall_ref[q, :, :, j * PAGE_SIZE:(j + 1) * PAGE_SIZE] = kpages[q * n_pages + j][...].astype(BF16)
            vall_ref[q, :, :, j * PAGE_SIZE:(j + 1) * PAGE_SIZE] = vpages[q * n_pages + j][...].astype(BF16)
        for g in range(N_KV_HEADS):
            for new_ref, all_ref in ((knew_ref, kall_ref), (vnew_ref, vall_ref)):
                page = jnp.concatenate([new_ref[q, g], fill], axis=0).T
                all_ref[q, g, :, n_pages * PAGE_SIZE:(n_pages + 1) * PAGE_SIZE] = (
                    page[0:HEAD_DIM, :].astype(BF16))
    for q in range(n_seq):
        bias = jnp.broadcast_to(bias_ref[q][None], (HEADS_PER_KV, SUBLANES, n_keys)).reshape(
            HEADS_PER_KV * SUBLANES, n_keys)
        for g in range(N_KV_HEADS):
            s_ref[q, g] = _dot(q_ref[q, g], kall_ref[q, g]) + bias
    for q in range(n_seq):
        for g in range(N_KV_HEADS):
            s = s_ref[q, g]
            m = s.max(axis=1, keepdims=True)
            p = jnp.exp(s - m)
            l_ref[q, g] = p.sum(axis=1, keepdims=True)
            p_ref[q, g] = p.astype(BF16)
    for q in range(n_seq):
        for g in range(N_KV_HEADS):
            o_ref[q, g] = _dot_nt(p_ref[q, g], vall_ref[q, g]) / l_ref[q, g]


def _sample_attention(page_table, q8, bias8, cache_k_t, cache_v_t, k_new_t, v_new_t):
    nb, n_pages = page_table.shape
    n_keys = (n_pages + 1) * PAGE_SIZE
    rows = HEADS_PER_KV * SUBLANES
    ns = ATTN_SEQS
    page = [pl.BlockSpec((None, N_KV_HEADS, HEAD_DIM, PAGE_SIZE),
                         lambda b, pt, q=q, j=j: (pt[b * ns + q, j], 0, 0, 0))
            for q in range(ns) for j in range(n_pages)]
    new = pl.BlockSpec((ns, N_KV_HEADS, SUBLANES, LANES), lambda b, pt: (b, 0, 0, 0))
    qo = pl.BlockSpec((ns, N_KV_HEADS, rows, HEAD_DIM), lambda b, pt: (b, 0, 0, 0))
    grid_spec = pltpu.PrefetchScalarGridSpec(
        num_scalar_prefetch=1,
        grid=(nb // ns,),
        in_specs=[qo, pl.BlockSpec((ns, SUBLANES, n_keys), lambda b, pt: (b, 0, 0))]
        + page + page + [new, new],
        out_specs=qo,
        scratch_shapes=[pltpu.VMEM((ns, N_KV_HEADS, HEAD_DIM, n_keys), BF16),
                        pltpu.VMEM((ns, N_KV_HEADS, HEAD_DIM, n_keys), BF16),
                        pltpu.VMEM((ns, N_KV_HEADS, rows, n_keys), F32),
                        pltpu.VMEM((ns, N_KV_HEADS, rows, n_keys), BF16),
                        pltpu.VMEM((ns, N_KV_HEADS, rows, 1), F32)],
    )
    pages = ns * n_pages
    return pl.pallas_call(
        functools.partial(_sample_attn_kernel, n_pages=n_pages),
        grid_spec=grid_spec,
        out_shape=jax.ShapeDtypeStruct((nb, N_KV_HEADS, rows, HEAD_DIM), F32),
        compiler_params=_params("arbitrary"),
        name="sample_attention",
    )(page_table, q8, bias8, *([cache_k_t] * pages), *([cache_v_t] * pages), k_new_t, v_new_t)


def _ffn(h, wgu_ref, wd_ref, h_ref, acc_ref, dyn0):
    h_ref[dyn0] = h
    acc_ref[dyn0] = jnp.zeros(acc_ref.shape[1:], F32)
    for c in range(D_FF // FFN_CHUNK):
        gate = _dot(h_ref[dyn0], wgu_ref[:, c * FFN_CHUNK:(c + 1) * FFN_CHUNK])
        up = _dot(h_ref[dyn0], wgu_ref[:, D_FF + c * FFN_CHUNK:D_FF + (c + 1) * FFN_CHUNK])
        a = _silu(gate) * up
        acc_ref[dyn0] += _dot(a.astype(BF16), wd_ref[c * FFN_CHUNK:(c + 1) * FFN_CHUNK, :])
    return acc_ref[dyn0]


def _row_specs(nb, rows, per_row_mod):
    tm = min(ROW_TILE, rows)
    row = pl.BlockSpec((None, tm, D_MODEL), lambda b, j: (b, j, 0))
    if per_row_mod:
        mod = row
    else:
        mod = pl.BlockSpec((None, 1, D_MODEL), lambda b, j: (b, 0, 0))
    return tm, row, mod


def _attn_out_ffn_kernel(x_ref, o_ref, g1_ref, sc2_ref, sh2_ref, g2_ref, n2_ref,
                         wo_ref, wgu_ref, wd_ref, y_ref, h_ref, acc_ref):
    dyn0 = jnp.minimum(pl.program_id(1), 0)
    y1 = x_ref[...] + g1_ref[...] * _dot(o_ref[...], wo_ref[...])
    h2 = _norm_mod(y1, n2_ref[...], sc2_ref[...], sh2_ref[...]).astype(BF16)
    y_ref[...] = y1 + g2_ref[...] * _ffn(h2, wgu_ref, wd_ref, h_ref, acc_ref, dyn0)


def _layer_spec(stacked, layer):
    _, rows, cols = stacked.shape
    return pl.BlockSpec((None, rows, cols), lambda *_: (layer, 0, 0), pipeline_mode=pl.Buffered(1))


def _attn_out_ffn(x, o, g1, sc2, sh2, g2, n2, wo, wgu, wd, layer):
    nb, rows, _ = x.shape
    tm, row, mod = _row_specs(nb, rows, g1.shape[1] == rows)
    return pl.pallas_call(
        _attn_out_ffn_kernel,
        grid=(nb, rows // tm),
        in_specs=[row, row, mod, mod, mod, mod, _const_spec((1, D_MODEL)),
                  _const_spec(wo.shape), _layer_spec(wgu, layer), _layer_spec(wd, layer)],
        out_specs=row,
        out_shape=jax.ShapeDtypeStruct(x.shape, F32),
        scratch_shapes=[pltpu.VMEM((1, tm, D_MODEL), BF16), pltpu.VMEM((1, tm, D_MODEL), F32)],
        compiler_params=_params("arbitrary", "arbitrary"),
        name="attn_out_ffn",
    )(x, o, g1, sc2, sh2, g2, n2, wo, wgu, wd)


def _conv_layer_kernel(x_ref, sc1_ref, sh1_ref, g1_ref, sc2_ref, sh2_ref, g2_ref, p1_ref, p2_ref,
                       n1_ref, n2_ref, nf_ref, win_ref, ck_ref, wout_ref, wgu_ref, wd_ref,
                       y_ref, tail_ref, ubuf_ref, z_ref, h_ref, acc_ref, *, seg):
    j = pl.program_id(1)
    tm = x_ref.shape[0]
    x = x_ref[...]
    h = _norm_mod(x, n1_ref[...], sc1_ref[...], sh1_ref[...]).astype(BF16)

    @pl.when(j == 0)
    def _():
        ubuf_ref[0:SUBLANES, :] = jnp.zeros((SUBLANES, D_MODEL), F32)

    @pl.when(j > 0)
    def _():
        ubuf_ref[0:SUBLANES, :] = ubuf_ref[tm:tm + SUBLANES, :]

    t = (j * tm + lax.broadcasted_iota(jnp.int32, (tm, 1), 0)) & (seg - 1)
    ck = ck_ref[...]
    cw = FFN_CHUNK
    for c in range(D_MODEL // cw):
        cols = slice(c * cw, (c + 1) * cw)
        bg = _dot(h, win_ref[:, c * cw:(c + 1) * cw])
        cg = _dot(h, win_ref[:, D_MODEL + c * cw:D_MODEL + (c + 1) * cw])
        hv = _dot(h, win_ref[:, 2 * D_MODEL + c * cw:2 * D_MODEL + (c + 1) * cw])
        u = cg * hv
        ubuf_ref[SUBLANES:SUBLANES + tm, cols] = u
        um1 = jnp.where(t >= 1, ubuf_ref[SUBLANES - 1:SUBLANES - 1 + tm, cols], p1_ref[:, cols])
        um2 = jnp.where(t >= 2, ubuf_ref[SUBLANES - 2:SUBLANES - 2 + tm, cols], p2_ref[:, cols])
        conv = ck[0:1, cols] * um2 + ck[1:2, cols] * um1 + ck[2:3, cols] * u
        z_ref[:, cols] = (bg * conv).astype(BF16)
    r = tail_ref.shape[0]
    tail_ref[...] = ubuf_ref[SUBLANES + tm - r:SUBLANES + tm, :]

    y1 = x + g1_ref[...] * _dot(z_ref[...], wout_ref[...])
    h2 = _norm_mod(y1, n2_ref[...], sc2_ref[...], sh2_ref[...]).astype(BF16)
    y2 = y1 + g2_ref[...] * _ffn(h2, wgu_ref, wd_ref, h_ref, acc_ref, jnp.minimum(j, 0))
    ms = jnp.mean(y2 * y2, axis=-1, keepdims=True)
    y_ref[...] = (y2 * lax.rsqrt(ms + EPS)) * nf_ref[...]


def _conv_layer(x, mods, p1, p2, n1, n2, nf, win, ck, wout, wgu, wd, *, layer, seg, full_tail):
    nb, rows, _ = x.shape
    per_row = mods[0].shape[1] == rows
    tm, row, mod = _row_specs(nb, rows, per_row)
    if per_row:
        prefix = row
    else:
        prefix = pl.BlockSpec((None, 1, D_MODEL), lambda b, j: (0, 0, 0))
    if full_tail:
        tail_spec, tail_rows = row, rows
    else:
        tail_spec, tail_rows = pl.BlockSpec((None, SUBLANES, D_MODEL), lambda b, j: (b, 0, 0)), SUBLANES
    vec = _const_spec((1, D_MODEL))
    return pl.pallas_call(
        functools.partial(_conv_layer_kernel, seg=seg),
        grid=(nb, rows // tm),
        in_specs=[row] + [mod] * 6 + [prefix, prefix, vec, vec, vec,
                                      _const_spec(win.shape), _const_spec(ck.shape), _const_spec(wout.shape),
                                      _layer_spec(wgu, layer), _layer_spec(wd, layer)],
        out_specs=(row, tail_spec),
        out_shape=(jax.ShapeDtypeStruct(x.shape, F32),
                   jax.ShapeDtypeStruct((nb, tail_rows, D_MODEL), F32)),
        scratch_shapes=[pltpu.VMEM((tm + SUBLANES, D_MODEL), F32), pltpu.VMEM((tm, D_MODEL), BF16),
                        pltpu.VMEM((1, tm, D_MODEL), BF16), pltpu.VMEM((1, tm, D_MODEL), F32)],
        compiler_params=_params("arbitrary", "arbitrary"),
        name="conv_layer",
    )(x, *mods, p1, p2, n1, n2, nf, win, ck, wout, wgu, wd)


def _rope_angles(pos):
    inv = ROPE_THETA ** (-jnp.arange(0, ROT_DIM, 2, dtype=F32) / ROT_DIM)
    ang = pos[:, None] * inv[None, :]
    return jnp.cos(ang), jnp.sin(ang)


def _rope_tables(pos):
    half = ROT_DIM // 2
    cos, sin = _rope_angles(pos)
    n = pos.shape[0]
    rest = HEAD_DIM - ROT_DIM
    c = jnp.concatenate([cos, cos, jnp.ones((n, rest), F32)], axis=1)
    s1 = jnp.concatenate([-sin, jnp.zeros((n, half + rest), F32)], axis=1)
    s2 = jnp.concatenate([jnp.zeros((n, half), F32), sin, jnp.zeros((n, rest), F32)], axis=1)
    return tuple(jnp.tile(a, (1, LANES // HEAD_DIM)) for a in (c, s1, s2))


def kernel(x_prompt, x_sample, c_prompt, c_sample, cache_k, cache_v, cache_kidx, state_conv, page_table,
           ada_w, ada_b, norm1_g, norm2_g, final_g, attn_w_in, attn_w_out, conv_w_in, conv_k, conv_w_out,
           ffn_w_gu, ffn_w_down):
    nb, t, _ = x_prompt.shape
    ns, tn, _ = x_sample.shape
    n_pages = page_table.shape[1]
    past = n_pages * PAGE_SIZE
    rows_s = ns * tn

    pad = (-(nb + ns)) % SUBLANES
    c_all = jnp.concatenate([c_prompt, c_sample, jnp.zeros((pad, D_MODEL), F32)], axis=0)
    mod = _modulation(c_all, ada_w, ada_b)

    def mods(layer):
        cols = [mod[layer, :, i * D_MODEL:(i + 1) * D_MODEL] for i in range(6)]
        prompt = [m[:nb, None, :] for m in cols]
        sample = [jnp.repeat(m[nb:nb + ns], tn, axis=0)[None] for m in cols]
        return prompt, sample

    vec = lambda a: a.reshape(1, D_MODEL)
    w_in = jnp.pad(attn_w_in[0], ((0, 0), (0, ATTN_IN_PAD - ATTN_IN))).astype(BF16)
    w_o = attn_w_out[0].astype(BF16)
    wgu = ffn_w_gu.astype(BF16)
    wd = ffn_w_down.astype(BF16)

    (sh1p, sc1p, g1p, sh2p, sc2p, g2p), (sh1s, sc1s, g1s, sh2s, sc2s, g2s) = mods(0)
    pos_p = jnp.arange(t, dtype=F32)
    rope_p = _rope_tables(pos_p)
    rope_pt = tuple(a.T for a in _rope_angles(pos_p))
    k_p, v_p, ki_p, qt, qit, kg, vt, kib, wit = _proj_prompt(
        x_prompt, sc1p, sh1p, vec(norm1_g[0]), w_in, rope_p, rope_pt)
    o_p = _prompt_attention(qt, qit, wit, kg, vt, kib)
    y_p = _attn_out_ffn(x_prompt, o_p, g1p, sc2p, sh2p, g2p, vec(norm2_g[0]), w_o, wgu, wd, 0)

    pos_s = jnp.tile(past + jnp.arange(tn, dtype=F32), ns)
    rope_s = _rope_tables(pos_s)
    xs = x_sample.reshape(rows_s, D_MODEL)
    proj = _proj_sample(xs, sc1s[0], sh1s[0], vec(norm1_g[0]), w_in, rope_s)
    o = 0
    q_s = proj[:, o:o + Q_COLS].reshape(ns, tn, N_KV_HEADS, HEADS_PER_KV, HEAD_DIM); o += Q_COLS
    k_s = proj[:, o:o + KV_COLS].reshape(ns, tn, KV_COLS); o += KV_COLS
    v_s = proj[:, o:o + KV_COLS].reshape(ns, tn, KV_COLS); o += KV_COLS
    qi_s = proj[:, o:o + IQ_COLS].reshape(ns, tn, IDX_HEADS, IDX_DIM); o += IQ_COLS
    ki_s = proj[:, o:o + IDX_DIM].reshape(ns, tn, IDX_DIM); o += IDX_DIM
    wi_s = proj[:, o:o + IDX_HEADS].reshape(ns, tn, IDX_HEADS)

    qpad = ((0, 0), (0, SUBLANES - tn))
    qi8 = jnp.pad(qi_s, qpad + ((0, 0), (0, 0))).transpose(0, 2, 1, 3)
    qi8 = qi8.reshape(ns, IDX_HEADS * SUBLANES, IDX_DIM).astype(BF16)
    w8 = jnp.pad(wi_s, qpad + ((0, 0),)).transpose(0, 2, 1).reshape(ns, IDX_HEADS * SUBLANES, 1)
    slot_pad = (0, PAGE_SIZE - tn)
    ki_new_t = jnp.pad(ki_s.transpose(0, 2, 1), ((0, 0), (0, 0), slot_pad))
    scores = _sample_scores(page_table, qi8, w8, cache_kidx[0].transpose(0, 2, 1), ki_new_t)
    n_keys = scores.shape[2]
    scores_t = scores[:, :tn, :].transpose(2, 0, 1).reshape(n_keys, rows_s)
    bias_t = _sample_select(scores_t, past, tn)
    bias8 = jnp.pad(bias_t.reshape(n_keys, ns, tn).transpose(1, 2, 0), qpad + ((0, 0),))

    q8 = jnp.pad(q_s * HEAD_DIM ** -0.5, qpad + ((0, 0), (0, 0), (0, 0))).transpose(0, 2, 3, 1, 4)
    q8 = q8.reshape(ns, N_KV_HEADS, HEADS_PER_KV * SUBLANES, HEAD_DIM).astype(BF16)
    new_t = lambda a: jnp.pad(a.reshape(ns, tn, N_KV_HEADS, HEAD_DIM).transpose(0, 2, 1, 3),
                              ((0, 0), (0, 0), (0, SUBLANES - tn), (0, LANES - HEAD_DIM)))
    o_g = _sample_attention(page_table, q8, bias8,
                            cache_k[0].transpose(0, 2, 3, 1), cache_v[0].transpose(0, 2, 3, 1),
                            new_t(k_s), new_t(v_s))
    o_g = o_g.reshape(ns, N_KV_HEADS, HEADS_PER_KV, SUBLANES, HEAD_DIM)[:, :, :, :tn]
    o_s = o_g.transpose(0, 3, 1, 2, 4).reshape(1, rows_s, Q_COLS).astype(BF16)
    y_s = _attn_out_ffn(xs[None], o_s, g1s, sc2s, sh2s, g2s, vec(norm2_g[0]), w_o, wgu, wd, 0)

    (sh1p, sc1p, g1p, sh2p, sc2p, g2p), (sh1s, sc1s, g1s, sh2s, sc2s, g2s) = mods(1)
    win = conv_w_in[0].astype(BF16)
    wout = conv_w_out[0].astype(BF16)
    zero_prefix = jnp.zeros((1, 1, D_MODEL), F32)
    out_p, tail_p = _conv_layer(
        y_p, (sc1p, sh1p, g1p, sc2p, sh2p, g2p), zero_prefix, zero_prefix,
        vec(norm1_g[1]), vec(norm2_g[1]), vec(final_g), win, conv_k[0], wout, wgu, wd,
        layer=1, seg=t, full_tail=False)
    st = state_conv[0]
    zeros_row = jnp.zeros((ns, 1, D_MODEL), F32)
    p1 = jnp.concatenate([st[:, 1:2], zeros_row, zeros_row, zeros_row], axis=1).reshape(1, rows_s, D_MODEL)
    p2 = jnp.concatenate([st[:, 0:1], st[:, 1:2], zeros_row, zeros_row], axis=1).reshape(1, rows_s, D_MODEL)
    out_s, u_s = _conv_layer(
        y_s, (sc1s, sh1s, g1s, sc2s, sh2s, g2s), p1, p2,
        vec(norm1_g[1]), vec(norm2_g[1]), vec(final_g), win, conv_k[0], wout, wgu, wd,
        layer=1, seg=tn, full_tail=True)

    keep = CONV_WIDTH - 1
    return (
        out_p,
        out_s.reshape(ns, tn, D_MODEL),
        k_p.reshape(1, nb, t, N_KV_HEADS, HEAD_DIM),
        v_p.reshape(1, nb, t, N_KV_HEADS, HEAD_DIM),
        ki_p.transpose(0, 2, 1)[None],
        tail_p[None, :, SUBLANES - keep:, :],
        k_s.reshape(1, ns, tn, N_KV_HEADS, HEAD_DIM),
        v_s.reshape(1, ns, tn, N_KV_HEADS, HEAD_DIM),
        ki_s[None],
        u_s.reshape(ns, tn, D_MODEL)[None, :, tn - keep:, :],
    )
```
